```python
import jax, jax.numpy as jnp
from jax import lax
import numpy as np

D_MODEL = 1024
BATCH = 2
SEQ = 8192
DEPTH = 1

D_MIX = D_MODEL
D_MLSTM = D_MIX // 2
N_MLSTM_HEADS = 4
MLSTM_HEAD_DIM = D_MLSTM // N_MLSTM_HEADS
D_NA = D_MIX - D_MLSTM
N_NA_HEADS = 8
NA_HEAD_DIM = D_NA // N_NA_HEADS
GRID_W = 64
WIN_H_MAX = 8
WIN_W = 16
CHUNK = 128
CONV_W = 3
N_GATE = 4 * N_MLSTM_HEADS
D_FF = 4 * D_MODEL
D_PLE = 256
RMS_EPS = 1e-6
SPLITS = (D_MLSTM, D_MLSTM, D_MLSTM, D_MLSTM, N_GATE, D_NA, D_NA, D_NA)
D_IN_PROJ = sum(SPLITS)

kernel_name = "hybrid_mlstm_natten_block"


def rms_norm(x, g):
    xf = x.astype(jnp.float32)
    y = xf * lax.rsqrt(jnp.mean(xf * xf, axis=-1, keepdims=True) + RMS_EPS)
    return (y * g.astype(jnp.float32)).astype(x.dtype)


def centred_depthwise_conv(x, w, b):
    c = x.shape[-1]
    y = lax.conv_general_dilated(x, w[:, None, :].astype(x.dtype), window_strides=(1,), padding='SAME',
                                 dimension_numbers=('NWC', 'WIO', 'NWC'), feature_group_count=c)
    return y + b.astype(x.dtype)


def mlstm_chunkwise(q, k, v, log_i, log_f):
    B, H, S, d = q.shape
    nc = S // CHUNK

    def to_chunks(a):
        a = a.reshape(B, H, nc, CHUNK, *a.shape[3:])
        return jnp.moveaxis(a, 2, 0)

    f32 = jnp.float32
    xs = (to_chunks(q.astype(f32)), to_chunks(k.astype(f32)), to_chunks(v.astype(f32)),
          to_chunks(log_i), to_chunks(log_f))
    tri = jnp.tril(jnp.ones((CHUNK, CHUNK), dtype=bool))

    def step(carry, inp):
        C, n, m = carry
        qb, kb, vb, ib, fb = inp
        b = jnp.cumsum(fb, axis=-1)
        g = b + m[..., None]
        dmat = jnp.where(tri, b[..., :, None] - b[..., None, :] + ib[..., None, :], -jnp.inf)
        m_t = jnp.maximum(g, jnp.max(dmat, axis=-1))
        w_inter = jnp.exp(g - m_t)
        s = jnp.einsum('bhld,bhsd->bhls', qb, kb) * jnp.exp(dmat - m_t[..., None])
        num = w_inter[..., None] * jnp.einsum('bhld,bhde->bhle', qb, C) + jnp.einsum('bhls,bhse->bhle', s, vb)
        den = w_inter * jnp.einsum('bhld,bhd->bhl', qb, n) + jnp.sum(s, axis=-1)
        h = num / jnp.maximum(jnp.abs(den), jnp.exp(-m_t))[..., None]
        b_last = b[..., -1]
        a_prev = m + b_last
        a_j = ib + b_last[..., None] - b
        m_new = jnp.maximum(a_prev, jnp.max(a_j, axis=-1))
        w_prev = jnp.exp(a_prev - m_new)
        w_j = jnp.exp(a_j - m_new[..., None])
        C_new = w_prev[..., None, None] * C + jnp.einsum('bhs,bhsd,bhse->bhde', w_j, kb, vb)
        n_new = w_prev[..., None] * n + jnp.einsum('bhs,bhsd->bhd', w_j, kb)
        return (C_new, n_new, m_new), h

    init = (jnp.zeros((B, H, d, d), f32), jnp.zeros((B, H, d), f32), jnp.zeros((B, H), f32))
    _, hs = lax.scan(step, init, xs)
    return jnp.moveaxis(hs, 0, 2).reshape(B, H, S, d).astype(q.dtype)


def mlstm_mixer(q_raw, k_raw, v, o_pre, gate_pre, conv_w, conv_b, gate_b, head_norm_g):
    B, S, _ = v.shape
    qk = jax.nn.silu(centred_depthwise_conv(jnp.concatenate([q_raw, k_raw], axis=-1), conv_w, conv_b))
    q, k = jnp.split(qk, 2, axis=-1)

    def heads(a):
        return a.reshape(B, S, N_MLSTM_HEADS, MLSTM_HEAD_DIM).transpose(0, 2, 1, 3)

    q, k, vh = heads(q), heads(k) * (MLSTM_HEAD_DIM ** -0.5), heads(v)
    gates = (gate_pre.astype(jnp.float32) + gate_b.astype(jnp.float32))
    gates = gates.reshape(B, S, 4, N_MLSTM_HEADS).transpose(2, 0, 3, 1)
    i_fwd, f_fwd, i_bwd, f_bwd = gates[0], gates[1], gates[2], gates[3]
    h_fwd = mlstm_chunkwise(q, k, vh, i_fwd, jax.nn.log_sigmoid(f_fwd))
    flip = lambda a: jnp.flip(a, axis=2)
    h_bwd = flip(mlstm_chunkwise(flip(q), flip(k), flip(vh), flip(i_bwd), flip(jax.nn.log_sigmoid(f_bwd))))
    h = (h_fwd + h_bwd).transpose(0, 2, 1, 3)
    h = rms_norm(h, head_norm_g).reshape(B, S, D_MLSTM)
    return h * jax.nn.sigmoid(o_pre)


def neighbourhood_attention(q, k, v, q_norm_g, k_norm_g, rpb):
    B, S, _ = q.shape
    rows = S // GRID_W
    win_h = min(WIN_H_MAX, rows)

    def grid(a):
        return a.reshape(B, rows, GRID_W, N_NA_HEADS, NA_HEAD_DIM).transpose(0, 3, 1, 2, 4)

    split_heads = lambda a: a.reshape(B, S, N_NA_HEADS, NA_HEAD_DIM)
    qg = grid(rms_norm(split_heads(q), q_norm_g) * (NA_HEAD_DIM ** -0.5))
    kg = grid(rms_norm(split_heads(k), k_norm_g))
    vg = grid(split_heads(v))

    cols = jnp.arange(GRID_W)
    col_start = jnp.clip(cols - WIN_W // 2, 0, GRID_W - WIN_W)
    col_idx = col_start[:, None] + jnp.arange(WIN_W)
    col_off = col_idx - cols[:, None] + (WIN_W - 1)

    def one_row(r):
        rs = jnp.clip(r - win_h // 2, 0, rows - win_h)
        k_win = lax.dynamic_slice_in_dim(kg, rs, win_h, axis=2)[:, :, :, col_idx]
        v_win = lax.dynamic_slice_in_dim(vg, rs, win_h, axis=2)[:, :, :, col_idx]
        q_row = lax.dynamic_index_in_dim(qg, r, axis=2, keepdims=False)
        s = jnp.einsum('bhcd,bhrcwd->bhcrw', q_row, k_win).astype(jnp.float32)
        row_off = rs + jnp.arange(win_h) - r + (WIN_H_MAX - 1)
        bias = rpb[:, row_off][:, :, col_off].astype(jnp.float32)
        s = s + bias.transpose(0, 2, 1, 3)[None]
        prob = jax.nn.softmax(s.reshape(B, N_NA_HEADS, GRID_W, win_h * WIN_W), axis=-1)
        prob = prob.reshape(B, N_NA_HEADS, GRID_W, win_h, WIN_W).astype(v.dtype)
        return jnp.einsum('bhcrw,bhrcwd->bhcd', prob, v_win)

    out = lax.map(one_row, jnp.arange(rows))
    return out.transpose(1, 0, 3, 2, 4).reshape(B, S, D_NA)


def setup_inputs(seed: int = 0) -> dict:
    key = jax.random.key(seed)
    ks = jax.random.split(key, 20)
    nrm = lambda k, shape: jax.random.normal(k, shape, dtype=jnp.float32)
    H = N_MLSTM_HEADS
    f_bias = jnp.tile(jnp.linspace(3.0, 6.0, H, dtype=jnp.float32), 2)
    gate_base = jnp.stack([jnp.zeros((H,), jnp.float32), f_bias[:H], jnp.zeros((H,), jnp.float32), f_bias[H:]]).reshape(-1)
    return {
        "x": nrm(ks[0], (BATCH, SEQ, D_MODEL)),
        "p": nrm(ks[1], (DEPTH, BATCH, SEQ, D_PLE)),
        "norm1_g": 1.0 + 0.01 * nrm(ks[2], (DEPTH, D_MODEL)),
        "w_in": nrm(ks[3], (DEPTH, D_MODEL, D_IN_PROJ)) * D_MODEL ** -0.5,
        "conv_w": nrm(ks[4], (DEPTH, CONV_W, 2 * D_MLSTM)) * CONV_W ** -0.5,
        "conv_b": 0.01 * nrm(ks[5], (DEPTH, 2 * D_MLSTM)),
        "gate_b": gate_base[None] + 0.1 * nrm(ks[6], (DEPTH, N_GATE)),
        "mlstm_norm_g": 1.0 + 0.01 * nrm(ks[7], (DEPTH, N_MLSTM_HEADS, MLSTM_HEAD_DIM)),
        "q_norm_g": 1.0 + 0.01 * nrm(ks[8], (DEPTH, N_NA_HEADS, NA_HEAD_DIM)),
        "k_norm_g": 1.0 + 0.01 * nrm(ks[9], (DEPTH, N_NA_HEADS, NA_HEAD_DIM)),
        "rpb": 0.02 * nrm(ks[10], (DEPTH, N_NA_HEADS, 2 * WIN_H_MAX - 1, 2 * WIN_W - 1)),
        "w_out": nrm(ks[11], (DEPTH, D_MIX, D_MODEL)) * D_MIX ** -0.5,
        "norm2_g": 1.0 + 0.01 * nrm(ks[12], (DEPTH, D_MODEL)),
        "w_ff1": nrm(ks[13], (DEPTH, D_MODEL, D_FF)) * D_MODEL ** -0.5,
        "w_ff2": nrm(ks[14], (DEPTH, D_FF, D_MODEL)) * D_FF ** -0.5,
        "ple_norm_g": 1.0 + 0.01 * nrm(ks[15], (DEPTH, D_MODEL)),
        "w_ple_gate": nrm(ks[16], (DEPTH, D_MODEL, D_MODEL)) * D_MODEL ** -0.5,
        "w_ple_up": nrm(ks[17], (DEPTH, D_PLE, D_MODEL)) * D_PLE ** -0.5,
    }


def reference(x, p, norm1_g, w_in, conv_w, conv_b, gate_b, mlstm_norm_g, q_norm_g, k_norm_g, rpb,
              w_out, norm2_g, w_ff1, w_ff2, ple_norm_g, w_ple_gate, w_ple_up):
    h = x
    offsets = list(np.cumsum(SPLITS)[:-1])
    for i in range(DEPTH):
        u = rms_norm(h, norm1_g[i])
        proj = jnp.einsum('bsd,de->bse', u, w_in[i])
        mq, mk, mv, mo, mg, nq, nk, nv = jnp.split(proj, offsets, axis=-1)
        y_a = mlstm_mixer(mq, mk, mv, mo, mg, conv_w[i], conv_b[i], gate_b[i], mlstm_norm_g[i])
        y_b = neighbourhood_attention(nq, nk, nv, q_norm_g[i], k_norm_g[i], rpb[i])
        h = h + jnp.einsum('bse,ed->bsd', jnp.concatenate([y_a, y_b], axis=-1), w_out[i])
        z = jnp.einsum('bsd,df->bsf', rms_norm(h, norm2_g[i]), w_ff1[i])
        h = h + jnp.einsum('bsf,fd->bsd', jnp.square(jax.nn.relu(z)), w_ff2[i])
        gate = jax.nn.sigmoid(jnp.einsum('bsd,de->bse', rms_norm(h, ple_norm_g[i]), w_ple_gate[i]))
        h = h + gate * jnp.einsum('bsk,kd->bsd', p[i], w_ple_up[i])
    return h
```

```python
import functools

import jax
import jax.numpy as jnp
from jax import lax
from jax.experimental import pallas as pl
from jax.experimental.pallas import tpu as pltpu

F32 = jnp.float32
BF16 = jnp.bfloat16

N_MLSTM_HEADS = 4
MLSTM_HEAD_DIM = 128
N_NA_HEADS = 8
NA_HEAD_DIM = 64
GRID_W = 64
WIN_H = 8
WIN_W = 16
CHUNK = 128
N_GATE = 4 * N_MLSTM_HEADS
RMS_EPS = 1e-6
NEG_BIG = -1e30

LANES = 128
GATE_PAD = LANES
MIB = 1024 * 1024

HIGHEST = lax.Precision.HIGHEST
NT_DIMS = (((1,), (1,)), ((), ()))
TN_DIMS = (((0,), (0,)), ((), ()))


def _const_spec(shape):
    return pl.BlockSpec(shape, lambda *_: (0,) * len(shape), pipeline_mode=pl.Buffered(1))


def _rms_scale(x):
    return lax.rsqrt(jnp.mean(x * x, axis=-1, keepdims=True) + RMS_EPS)


def _inproj_body(x_ref, g_ref, w_ref, gb_ref, bd_ref, qg_ref, kg_ref,
                 qk_ref, v_ref, o_ref, nq_ref, nk_ref, nv_ref, gate_ref, *, d_m, d_n):
    x = x_ref[...]
    u = (x * _rms_scale(x) * g_ref[...]).astype(BF16)

    def proj(lo, hi):
        return jnp.dot(u, w_ref[:, lo:hi], preferred_element_type=F32)

    def head_norm(y, gain_ref):
        sq = y * y
        hi = sq.astype(BF16)
        lo = (sq - hi.astype(F32)).astype(BF16)
        ss = (jnp.dot(hi, bd_ref[...], preferred_element_type=F32)
              + jnp.dot(lo, bd_ref[...], preferred_element_type=F32))
        return y * lax.rsqrt(ss * (1.0 / NA_HEAD_DIM) + RMS_EPS) * gain_ref[...]

    qk_ref[...] = proj(0, 2 * d_m).astype(BF16)
    v_ref[...] = proj(2 * d_m, 3 * d_m).astype(BF16)
    o_ref[...] = proj(3 * d_m, 4 * d_m).astype(BF16)
    base = 4 * d_m
    nq_ref[...] = head_norm(proj(base, base + d_n), qg_ref).astype(BF16)
    nk_ref[...] = head_norm(proj(base + d_n, base + 2 * d_n), kg_ref).astype(BF16)
    nv_ref[...] = proj(base + 2 * d_n, base + 3 * d_n).astype(BF16)
    gates = proj(base + 3 * d_n, base + 3 * d_n + GATE_PAD)
    gate_ref[...] = gates[:, :N_GATE] + gb_ref[...]


def _inproj(x2, g, w_cat, gate_b, bd, qg, kg, *, d_m, d_n, tm):
    t, d = x2.shape
    n_cat = w_cat.shape[1]
    row = lambda width: pl.BlockSpec((tm, width), lambda i: (i, 0))
    out_shapes = (
        jax.ShapeDtypeStruct((t, 2 * d_m), BF16),
        jax.ShapeDtypeStruct((t, d_m), BF16),
        jax.ShapeDtypeStruct((t, d_m), BF16),
        jax.ShapeDtypeStruct((t, d_n), BF16),
        jax.ShapeDtypeStruct((t, d_n), BF16),
        jax.ShapeDtypeStruct((t, d_n), BF16),
        jax.ShapeDtypeStruct((t, N_GATE), F32),
    )
    return pl.pallas_call(
        functools.partial(_inproj_body, d_m=d_m, d_n=d_n),
        grid=(t // tm,),
        in_specs=[row(d), _const_spec((1, d)), _const_spec((d, n_cat)), _const_spec((1, N_GATE)),
                  _const_spec((d_n, d_n)), _const_spec((1, d_n)), _const_spec((1, d_n))],
        out_specs=(row(2 * d_m), row(d_m), row(d_m), row(d_n), row(d_n), row(d_n), row(N_GATE)),
        out_shape=out_shapes,
        compiler_params=pltpu.CompilerParams(dimension_semantics=("arbitrary",),
                                             vmem_limit_bytes=44 * MIB),
        name="inproj",
    )(x2, g, w_cat, gate_b, bd, qg, kg)


def _log_sigmoid(x):
    return jnp.minimum(x, 0.0) - jnp.log1p(jnp.exp(-jnp.abs(x)))


def _sigmoid(x):
    return 1.0 / (1.0 + jnp.exp(-x))


def _mlstm_body(q_ref, k_ref, v_ref, o_ref, grow_ref, gcol_ref, cw_ref, cb_ref, ng_ref, tri_ref,
                out_ref, qs_ref, ks_ref, hbuf_ref, c_ref, *, seq):
    L = CHUNK
    d = MLSTM_HEAD_DIM
    nc = seq // L

    row_id = lax.broadcasted_iota(jnp.int32, (L, d), 0)
    lane_id = lax.broadcasted_iota(jnp.int32, (L, d), 1)
    ones_col = (lane_id == 0).astype(BF16)

    def conv_silu(src_ref, c, s0, w, b):
        x = src_ref[pl.ds(s0, L), :].astype(F32)
        p0 = pl.multiple_of(jnp.maximum(s0 - 16, 0), 16)
        n0 = pl.multiple_of(jnp.minimum(s0 + L, seq - 16), 16)
        prev_row = src_ref[pl.ds(p0, 16), :][15:16, :].astype(F32)
        next_row = src_ref[pl.ds(n0, 16), :][0:1, :].astype(F32)
        prev_row = jnp.where(c > 0, prev_row, 0.0)
        next_row = jnp.where(c < nc - 1, next_row, 0.0)
        x_prev = jnp.where(row_id == 0, prev_row, pltpu.roll(x, 1, 0))
        x_next = jnp.where(row_id == L - 1, next_row, pltpu.roll(x, L - 1, 0))
        y = w[0:1, :] * x_prev + w[1:2, :] * x + w[2:3, :] * x_next + b
        return y * _sigmoid(y)

    def conv_step(c, carry):
        s0 = pl.multiple_of(c * L, L)
        qs_ref[pl.ds(s0, L), :] = conv_silu(q_ref, c, s0, cw_ref[0], cb_ref[0]).astype(BF16)
        kk = conv_silu(k_ref, c, s0, cw_ref[1], cb_ref[1]) * (d ** -0.5)
        ks_ref[pl.ds(s0, L), :] = kk.astype(BF16)
        return carry

    lax.fori_loop(0, nc, conv_step, 0)

    c_ref[...] = jnp.zeros_like(c_ref)

    def chunk_dir(c, dirn, m):
        s0 = pl.multiple_of(c * L, L)
        q = qs_ref[pl.ds(s0, L), :]
        k = ks_ref[pl.ds(s0, L), :]
        vaug = jnp.concatenate([v_ref[pl.ds(s0, L), :], ones_col], axis=1)
        gr = grow_ref[:, pl.ds(s0, L)]
        gc = gcol_ref[pl.ds(s0, L), :]
        i_r = gr[2 * dirn:2 * dirn + 1, :]
        f_r = _log_sigmoid(gr[2 * dirn + 1:2 * dirn + 2, :])
        i_c = gc[:, 2 * dirn:2 * dirn + 1]
        f_c = _log_sigmoid(gc[:, 2 * dirn + 1:2 * dirn + 2])
        cum_col = tri_ref[dirn]
        cum_row = tri_ref[1 - dirn]
        bcol = jnp.dot(cum_col, jnp.broadcast_to(f_c, (L, L)), precision=HIGHEST,
                       preferred_element_type=F32)
        brow = jnp.dot(jnp.broadcast_to(f_r, (8, L)), cum_row, precision=HIGHEST,
                       preferred_element_type=F32)[0:1, :]
        dmat = jnp.where(cum_col > 0.5, bcol - brow + i_r, -jnp.inf)
        g = bcol[:, 0:1] + m
        m_t = jnp.maximum(g, jnp.max(dmat, axis=1, keepdims=True))
        w_inter = jnp.exp(g - m_t)
        s = lax.dot_general(q, k, NT_DIMS, preferred_element_type=F32) * jnp.exp(dmat - m_t)
        nd = (w_inter * jnp.dot(q, c_ref[dirn].astype(BF16), preferred_element_type=F32)
              + jnp.dot(s.astype(BF16), vaug, preferred_element_type=F32))
        den = nd[:, d:d + 1]
        h = nd[:, :d] / jnp.maximum(jnp.abs(den), jnp.exp(-m_t))
        b_last = brow[:, L - 1:L] if dirn == 0 else brow[:, 0:1]
        a_prev = m + b_last
        a_row = i_r + b_last - brow
        a_col = i_c + b_last - bcol[:, 0:1]
        m_new = jnp.maximum(a_prev, jnp.max(a_row, axis=1, keepdims=True))
        w_prev = jnp.exp(a_prev - m_new)
        kw = (k.astype(F32) * jnp.exp(a_col - m_new)).astype(BF16)
        c_ref[dirn] = w_prev * c_ref[dirn] + lax.dot_general(kw, vaug, TN_DIMS,
                                                             preferred_element_type=F32)
        return h, m_new

    def finalize(c, hsum):
        s0 = pl.multiple_of(c * L, L)
        y = hsum * _rms_scale(hsum) * ng_ref[...]
        y = y * _sigmoid(o_ref[pl.ds(s0, L), :].astype(F32))
        out_ref[pl.ds(s0, L), :] = y.astype(out_ref.dtype)

    def first_half(i, carry):
        m_f, m_b = carry
        j = nc - 1 - i
        h_f, m_f = chunk_dir(i, 0, m_f)
        hbuf_ref[pl.ds(pl.multiple_of(i * L, L), L), :] = h_f
        h_b, m_b = chunk_dir(j, 1, m_b)
        hbuf_ref[pl.ds(pl.multiple_of(j * L, L), L), :] = h_b
        return m_f, m_b

    def second_half(i, carry):
        m_f, m_b = carry
        j = nc - 1 - i
        h_f, m_f = chunk_dir(i, 0, m_f)
        finalize(i, h_f + hbuf_ref[pl.ds(pl.multiple_of(i * L, L), L), :])
        h_b, m_b = chunk_dir(j, 1, m_b)
        finalize(j, h_b + hbuf_ref[pl.ds(pl.multiple_of(j * L, L), L), :])
        return m_f, m_b

    m0 = jnp.zeros((1, 1), F32)
    carry = lax.fori_loop(0, nc // 2, first_half, (m0, m0))
    lax.fori_loop(nc // 2, nc, second_half, carry)


def _mlstm(qk, v, o, grow, gcol, conv_w, conv_b, norm_g, tri):
    b, seq, _ = v.shape
    H, d = N_MLSTM_HEADS, MLSTM_HEAD_DIM
    assert seq % (2 * CHUNK) == 0
    col = lambda off: pl.BlockSpec((None, seq, d), lambda bi, hi: (bi, 0, hi + off))
    return pl.pallas_call(
        functools.partial(_mlstm_body, seq=seq),
        grid=(b, H),
        in_specs=[
            col(0), col(H), col(0), col(0),
            pl.BlockSpec((None, None, 4, seq), lambda bi, hi: (bi, hi, 0, 0)),
            pl.BlockSpec((None, None, seq, 4), lambda bi, hi: (bi, hi, 0, 0)),
            pl.BlockSpec((None, 2, 3, d), lambda bi, hi: (hi, 0, 0, 0)),
            pl.BlockSpec((None, 2, 1, d), lambda bi, hi: (hi, 0, 0, 0)),
            pl.BlockSpec((None, 1, d), lambda bi, hi: (hi, 0, 0)),
            _const_spec((2, CHUNK, CHUNK)),
        ],
        out_specs=pl.BlockSpec((None, seq, d), lambda bi, hi: (bi, 0, hi)),
        out_shape=jax.ShapeDtypeStruct((b, seq, H * d), BF16),
        scratch_shapes=[
            pltpu.VMEM((seq, d), BF16),
            pltpu.VMEM((seq, d), BF16),
            pltpu.VMEM((seq, d), F32),
            pltpu.VMEM((2, d, 2 * d), F32),
        ],
        compiler_params=pltpu.CompilerParams(dimension_semantics=("arbitrary", "arbitrary"),
                                             vmem_limit_bytes=52 * MIB),
        name="mlstm",
    )(qk, qk, v, o, grow, gcol, conv_w, conv_b, norm_g, tri)


def _natten_body(q_ref, k_ref, v_ref, bias_ref, out_ref, *, rows):
    hd = NA_HEAD_DIM
    lane = lax.broadcasted_iota(jnp.int32, (GRID_W, 2 * hd), 1)
    first = lane < hd

    def row_step(r, carry):
        rs = jnp.clip(r - WIN_H // 2, 0, rows - WIN_H)
        slab = rs - r + (WIN_H - 1)
        q0 = pl.multiple_of(r * GRID_W, GRID_W)
        k0 = pl.multiple_of(rs * GRID_W, GRID_W)
        q = q_ref[pl.ds(q0, GRID_W), :]
        zero = jnp.zeros_like(q)
        qs = jnp.concatenate([jnp.where(first, q, zero), jnp.where(first, zero, q)], axis=0)
        kwin = k_ref[pl.ds(k0, WIN_H * GRID_W), :]
        vwin = v_ref[pl.ds(k0, WIN_H * GRID_W), :]
        s = lax.dot_general(qs, kwin, NT_DIMS, preferred_element_type=F32) + bias_ref[slab]
        m = jnp.max(s, axis=1, keepdims=True)
        p = jnp.exp(s - m)
        l = jnp.sum(p, axis=1, keepdims=True)
        o = jnp.dot(p.astype(BF16), vwin, preferred_element_type=F32) / l
        out = jnp.where(first, o[:GRID_W], o[GRID_W:])
        out_ref[pl.ds(q0, GRID_W), :] = out.astype(out_ref.dtype)
        return carry

    lax.fori_loop(0, rows, row_step, 0)


def _natten(nq, nk, nv, bias_tab):
    b, seq, d_n = nq.shape
    pairs = N_NA_HEADS // 2
    width = 2 * NA_HEAD_DIM
    rows = seq // GRID_W
    col = pl.BlockSpec((None, seq, width), lambda bi, pi: (bi, 0, pi))
    return pl.pallas_call(
        functools.partial(_natten_body, rows=rows),
        grid=(b, pairs),
        in_specs=[col, col, col,
                  pl.BlockSpec((None, WIN_H, 2 * GRID_W, WIN_H * GRID_W), lambda bi, pi: (pi, 0, 0, 0))],
        out_specs=col,
        out_shape=jax.ShapeDtypeStruct((b, seq, d_n), BF16),
        compiler_params=pltpu.CompilerParams(dimension_semantics=("arbitrary", "arbitrary"),
                                             vmem_limit_bytes=40 * MIB),
        name="natten",
    )(nq, nk, nv, bias_tab)


def _natten_bias_table(rpb, rows):
    win_h = min(WIN_H, rows)
    c = jnp.arange(GRID_W)
    cs = jnp.clip(c - WIN_W // 2, 0, GRID_W - WIN_W)
    cp = jnp.arange(GRID_W)
    valid = (cp[None, :] >= cs[:, None]) & (cp[None, :] < cs[:, None] + WIN_W)
    coff = jnp.clip(cp[None, :] - c[:, None] + (WIN_W - 1), 0, 2 * WIN_W - 2)
    dense = jnp.where(valid[None, None], rpb.astype(F32)[:, :, coff], NEG_BIG)
    ro = jnp.arange(win_h)[:, None] + jnp.arange(win_h)[None, :]
    slabs = dense[:, ro]
    slabs = slabs.transpose(0, 1, 3, 2, 4).reshape(N_NA_HEADS, win_h, GRID_W, win_h * GRID_W)
    slabs = slabs.reshape(N_NA_HEADS // 2, 2, win_h, GRID_W, win_h * GRID_W).transpose(0, 2, 1, 3, 4)
    return slabs.reshape(N_NA_HEADS // 2, win_h, 2 * GRID_W, win_h * GRID_W)


def _tail_body(x_ref, ya_ref, yb_ref, p_ref, woa_ref, wob_ref, g2_ref, w1_ref, w2_ref,
               g3_ref, wg_ref, wu_ref, out_ref, *, ff_chunk):
    d_ff = w1_ref.shape[1]
    h = (x_ref[...]
         + jnp.dot(ya_ref[...], woa_ref[...], preferred_element_type=F32)
         + jnp.dot(yb_ref[...], wob_ref[...], preferred_element_type=F32))
    u = (h * _rms_scale(h) * g2_ref[...]).astype(BF16)
    out_ref[...] = h
    for j in range(d_ff // ff_chunk):
        z = jnp.dot(u, w1_ref[:, j * ff_chunk:(j + 1) * ff_chunk], preferred_element_type=F32)
        z = jnp.maximum(z, 0.0)
        out_ref[...] += jnp.dot((z * z).astype(BF16), w2_ref[j * ff_chunk:(j + 1) * ff_chunk, :],
                                preferred_element_type=F32)
    h = out_ref[...]
    u = (h * _rms_scale(h) * g3_ref[...]).astype(BF16)
    gate = _sigmoid(jnp.dot(u, wg_ref[...], preferred_element_type=F32))
    up = jnp.dot(p_ref[...].astype(BF16), wu_ref[...], preferred_element_type=F32)
    out_ref[...] = h + gate * up


def _tail(x2, ya, yb, p2, woa, wob, g2, w1, w2, g3, wg, wu, *, tm, ff_chunk):
    t, d = x2.shape
    row = lambda width: pl.BlockSpec((tm, width), lambda i: (i, 0))
    consts = [woa, wob, g2, w1, w2, g3, wg, wu]
    return pl.pallas_call(
        functools.partial(_tail_body, ff_chunk=ff_chunk),
        grid=(t // tm,),
        in_specs=[row(d), row(ya.shape[1]), row(yb.shape[1]), row(p2.shape[1])]
                 + [_const_spec(c.shape) for c in consts],
        out_specs=row(d),
        out_shape=jax.ShapeDtypeStruct((t, d), F32),
        compiler_params=pltpu.CompilerParams(dimension_semantics=("arbitrary",),
                                             vmem_limit_bytes=52 * MIB),
        name="tail",
    )(x2, ya, yb, p2, *consts)


def kernel(x, p, norm1_g, w_in, conv_w, conv_b, gate_b, mlstm_norm_g, q_norm_g, k_norm_g, rpb,
           w_out, norm2_g, w_ff1, w_ff2, ple_norm_g, w_ple_gate, w_ple_up):
    b, seq, d = x.shape
    depth = w_in.shape[0]
    H, hd = N_MLSTM_HEADS, MLSTM_HEAD_DIM
    d_m = H * hd
    d_n = N_NA_HEADS * NA_HEAD_DIM
    t = b * seq
    rows = seq // GRID_W
    assert rows >= WIN_H

    bd = jnp.kron(jnp.eye(N_NA_HEADS, dtype=F32), jnp.ones((NA_HEAD_DIM, NA_HEAD_DIM), F32)).astype(BF16)
    ri = lax.broadcasted_iota(jnp.int32, (CHUNK, CHUNK), 0)
    ci = lax.broadcasted_iota(jnp.int32, (CHUNK, CHUNK), 1)
    tri = jnp.stack([(ci <= ri).astype(F32), (ci >= ri).astype(F32)])

    h = x.reshape(t, d)
    for i in range(depth):
        wi = w_in[i]
        w_cat = jnp.concatenate(
            [wi[:, :4 * d_m], wi[:, 4 * d_m + N_GATE:], wi[:, 4 * d_m:4 * d_m + N_GATE],
             jnp.zeros((d, GATE_PAD - N_GATE), wi.dtype)], axis=1).astype(BF16)
        qg = (q_norm_g[i].reshape(1, d_n) * (NA_HEAD_DIM ** -0.5)).astype(F32)
        kg = k_norm_g[i].reshape(1, d_n).astype(F32)
        qk, mv, mo, nq, nk, nv, gates = _inproj(
            h, norm1_g[i].reshape(1, d), w_cat, gate_b[i].reshape(1, N_GATE), bd, qg, kg,
            d_m=d_m, d_n=d_n, tm=512)

        gates = gates.reshape(b, seq, 4, H)
        grow = gates.transpose(0, 3, 2, 1)
        gcol = gates.transpose(0, 3, 1, 2)
        cw = conv_w[i].reshape(3, 2, H, hd).transpose(2, 1, 0, 3)
        cb = conv_b[i].reshape(2, H, 1, hd).transpose(1, 0, 2, 3)
        y_a = _mlstm(qk.reshape(b, seq, 2 * d_m), mv.reshape(b, seq, d_m), mo.reshape(b, seq, d_m),
                     grow, gcol, cw, cb, mlstm_norm_g[i].reshape(H, 1, hd), tri)

        bias_tab = _natten_bias_table(rpb[i], rows)
        y_b = _natten(nq.reshape(b, seq, d_n), nk.reshape(b, seq, d_n), nv.reshape(b, seq, d_n), bias_tab)

        wo = w_out[i].astype(BF16)
        h = _tail(h, y_a.reshape(t, d_m), y_b.reshape(t, d_n), p[i].reshape(t, -1),
                  wo[:d_m], wo[d_m:], norm2_g[i].reshape(1, d), w_ff1[i].astype(BF16),
                  w_ff2[i].astype(BF16), ple_norm_g[i].reshape(1, d), w_ple_gate[i].astype(BF16),
                  w_ple_up[i].astype(BF16), tm=256, ff_chunk=1024)
    return h.reshape(b, seq, d)
```

```python
import functools

import jax
import jax.numpy as jnp
from jax import lax
from jax.experimental import pallas as pl
from jax.experimental.pallas import tpu as pltpu

F32 = jnp.float32
BF16 = jnp.bfloat16

N_MLSTM_HEADS = 4
MLSTM_HEAD_DIM = 128
N_NA_HEADS = 8
NA_HEAD_DIM = 64
GRID_W = 64
WIN_H = 8
WIN_W = 16
CHUNK = 128
N_GATE = 4 * N_MLSTM_HEADS
RMS_EPS = 1e-6
NEG_BIG = -1e30
NA_ROW_UNROLL = 4
MLSTM_UNROLL = 2

LANES = 128
GATE_PAD = LANES
MIB = 1024 * 1024

NT_DIMS = (((1,), (1,)), ((), ()))


def _const_spec(shape):
    return pl.BlockSpec(shape, lambda *_: (0,) * len(shape), pipeline_mode=pl.Buffered(1))


def _rms_scale(x):
    return lax.rsqrt(jnp.mean(x * x, axis=-1, keepdims=True) + RMS_EPS)


def _split3(x):
    hi = x.astype(BF16)
    r1 = x - hi.astype(F32)
    mid = r1.astype(BF16)
    lo = (r1 - mid.astype(F32)).astype(BF16)
    return hi, mid, lo


def _inproj_body(x_ref, g_ref, wm_ref, wn_ref, wg_ref, gb_ref, bd_ref, qg_ref, kg_ref,
                 qk_ref, v_ref, o_ref, nq_ref, nk_ref, nv_ref, gate_ref, *, d_m, d_n):
    x = x_ref[...]
    u = (x * _rms_scale(x) * g_ref[...]).astype(BF16)

    def proj(w_ref, lo, hi):
        return jnp.dot(u, w_ref[:, lo:hi], preferred_element_type=F32)

    def head_norm(y, gain_ref):
        sq = y * y
        hi = sq.astype(BF16)
        lo = (sq - hi.astype(F32)).astype(BF16)
        ss = (jnp.dot(hi, bd_ref[...], preferred_element_type=F32)
              + jnp.dot(lo, bd_ref[...], preferred_element_type=F32))
        return y * lax.rsqrt(ss * (1.0 / NA_HEAD_DIM) + RMS_EPS) * gain_ref[...]

    qk_ref[...] = proj(wm_ref, 0, 2 * d_m).astype(BF16)
    v_ref[...] = proj(wm_ref, 2 * d_m, 3 * d_m).astype(BF16)
    o_ref[...] = proj(wm_ref, 3 * d_m, 4 * d_m).astype(BF16)
    nq_ref[...] = head_norm(proj(wn_ref, 0, d_n), qg_ref).astype(BF16)
    nk_ref[...] = head_norm(proj(wn_ref, d_n, 2 * d_n), kg_ref).astype(BF16)
    nv_ref[...] = proj(wn_ref, 2 * d_n, 3 * d_n).astype(BF16)
    gates_t = jnp.dot(u, wg_ref[...], preferred_element_type=F32).T
    gate_ref[...] = gates_t[:N_GATE, :] + gb_ref[...]


def _inproj(x2, g, w_m, w_n, w_g, gate_b, bd, qg, kg, *, d_m, d_n, tm):
    t, d = x2.shape
    row = lambda width: pl.BlockSpec((tm, width), lambda i: (i, 0))
    out_shapes = (
        jax.ShapeDtypeStruct((t, 2 * d_m), BF16),
        jax.ShapeDtypeStruct((t, d_m), BF16),
        jax.ShapeDtypeStruct((t, d_m), BF16),
        jax.ShapeDtypeStruct((t, d_n), BF16),
        jax.ShapeDtypeStruct((t, d_n), BF16),
        jax.ShapeDtypeStruct((t, d_n), BF16),
        jax.ShapeDtypeStruct((N_GATE, t), F32),
    )
    return pl.pallas_call(
        functools.partial(_inproj_body, d_m=d_m, d_n=d_n),
        grid=(t // tm,),
        in_specs=[row(d), _const_spec((1, d)), _const_spec(w_m.shape), _const_spec(w_n.shape),
                  _const_spec(w_g.shape), _const_spec((N_GATE, 1)),
                  _const_spec((d_n, d_n)), _const_spec((1, d_n)), _const_spec((1, d_n))],
        out_specs=(row(2 * d_m), row(d_m), row(d_m), row(d_n), row(d_n), row(d_n),
                   pl.BlockSpec((N_GATE, tm), lambda i: (0, i))),
        out_shape=out_shapes,
        compiler_params=pltpu.CompilerParams(dimension_semantics=("arbitrary",),
                                             vmem_limit_bytes=44 * MIB),
        name="inproj",
    )(x2, g, w_m, w_n, w_g, gate_b, bd, qg, kg)


def _log_sigmoid(x):
    return jnp.minimum(x, 0.0) - jnp.log1p(jnp.exp(-jnp.abs(x)))


def _sigmoid(x):
    return 1.0 / (1.0 + jnp.exp(-x))


def _mlstm_body(q_ref, k_ref, v_ref, o_ref, gate_ref, cw_ref, cb_ref, ng_ref, tri3_ref, eye3_ref,
                out_ref, qs_ref, kt_ref, cs_ref, ccur_ref, brow_ref, crow_ref, wt_ref, bl_ref,
                ml_ref, ms_ref, *, seq):
    L = CHUNK
    d = MLSTM_HEAD_DIM
    nc = seq // L

    row_id = lax.broadcasted_iota(jnp.int32, (L, d), 0)
    col_id = lax.broadcasted_iota(jnp.int32, (L, d), 1)
    ones_blk = jnp.ones((L, d), BF16)

    def conv_silu(src_ref, c, s0, w, b):
        x = src_ref[pl.ds(s0, L), :].astype(F32)
        p0 = pl.multiple_of(jnp.maximum(s0 - 16, 0), 16)
        n0 = pl.multiple_of(jnp.minimum(s0 + L, seq - 16), 16)
        prev_row = src_ref[pl.ds(p0, 16), :][15:16, :].astype(F32)
        next_row = src_ref[pl.ds(n0, 16), :][0:1, :].astype(F32)
        prev_row = jnp.where(c > 0, prev_row, 0.0)
        next_row = jnp.where(c < nc - 1, next_row, 0.0)
        x_prev = jnp.where(row_id == 0, prev_row, pltpu.roll(x, 1, 0))
        x_next = jnp.where(row_id == L - 1, next_row, pltpu.roll(x, L - 1, 0))
        y = w[0:1, :] * x_prev + w[1:2, :] * x + w[2:3, :] * x_next + b
        return y * _sigmoid(y)

    def conv_step(c, carry):
        s0 = pl.multiple_of(c * L, L)
        qs_ref[pl.ds(s0, L), :] = conv_silu(q_ref, c, s0, cw_ref[0], cb_ref[0]).astype(BF16)
        kk = conv_silu(k_ref, c, s0, cw_ref[1], cb_ref[1]) * (d ** -0.5)
        kt_ref[:, pl.ds(s0, L)] = kk.T.astype(BF16)
        return carry

    lax.fori_loop(0, nc, conv_step, 0)

    for dirn in (0, 1):
        i_g = gate_ref[2 * dirn]
        f_log = _log_sigmoid(gate_ref[2 * dirn + 1])
        f_cat = jnp.concatenate(_split3(f_log), axis=1)
        brow = jnp.dot(f_cat, tri3_ref[1 - dirn], preferred_element_type=F32)
        b_last = brow[:, L - 1:L] if dirn == 0 else brow[:, 0:1]
        a_row = i_g + b_last - brow
        a_max = jnp.max(a_row, axis=1, keepdims=True)
        brow_ref[dirn] = brow
        crow_ref[dirn] = i_g - brow
        wt_ref[dirn] = jnp.exp(a_row - a_max)
        bl_ref[dirn] = jnp.broadcast_to(b_last, (nc, L))
        ml_ref[dirn] = jnp.broadcast_to(a_max, (nc, L))

    def v_aug(s0):
        return jnp.concatenate([v_ref[pl.ds(s0, L), :], ones_blk], axis=1)

    ccur_ref[...] = jnp.zeros_like(ccur_ref)

    def scan_step(i, carry):
        new = []
        for dirn, c, m in ((0, i, carry[0]), (1, nc - 1 - i, carry[1])):
            s0 = pl.multiple_of(c * L, L)
            kw = (kt_ref[:, pl.ds(s0, L)].astype(F32) * wt_ref[dirn, pl.ds(c, 1), :]).astype(BF16)
            k_loc = jnp.dot(kw, v_aug(s0), preferred_element_type=F32)
            state = ccur_ref[dirn]
            cs_ref[dirn, c] = state.astype(BF16)
            ms_ref[dirn, pl.ds(c, 1), :] = m
            a_prev = m + bl_ref[dirn, pl.ds(c, 1), :]
            a_max = ml_ref[dirn, pl.ds(c, 1), :]
            m_new = jnp.maximum(a_prev, a_max)
            w_prev = jnp.exp(a_prev - m_new)[:, 0:1]
            w_loc = jnp.exp(a_max - m_new)[:, 0:1]
            ccur_ref[dirn] = w_prev * state + w_loc * k_loc
            new.append(m_new)
        return tuple(new)

    m0 = jnp.zeros((1, L), F32)
    lax.fori_loop(0, nc, scan_step, (m0, m0), unroll=MLSTM_UNROLL)

    lower = col_id <= row_id
    upper = col_id >= row_id

    def out_step(c, carry):
        s0 = pl.multiple_of(c * L, L)
        q = qs_ref[pl.ds(s0, L), :]
        qk = jnp.dot(q, kt_ref[:, pl.ds(s0, L)], preferred_element_type=F32)
        vaug = v_aug(s0)
        hsum = None
        for dirn, mask in ((0, lower), (1, upper)):
            b_cat = jnp.concatenate(_split3(brow_ref[dirn, pl.ds(c, 1), :]), axis=1)
            bcol = lax.dot_general(eye3_ref[...], jnp.broadcast_to(b_cat, (L, 3 * L)), NT_DIMS,
                                   preferred_element_type=F32)
            dmat = jnp.where(mask, bcol + crow_ref[dirn, pl.ds(c, 1), :], -jnp.inf)
            g = bcol + ms_ref[dirn, pl.ds(c, 1), :]
            m_t = jnp.maximum(g, jnp.max(dmat, axis=1, keepdims=True))
            s = qk * jnp.exp(dmat - m_t)
            intra = jnp.dot(s.astype(BF16), vaug, preferred_element_type=F32)
            inter = jnp.dot(q, cs_ref[dirn, c], preferred_element_type=F32)
            w_inter = jnp.exp(g - m_t)
            num = w_inter * inter[:, :d] + intra[:, :d]
            den = w_inter * inter[:, d:] + intra[:, d:]
            h = num / jnp.maximum(jnp.abs(den), jnp.exp(-m_t))
            hsum = h if hsum is None else hsum + h
        y = hsum * _rms_scale(hsum) * ng_ref[...]
        y = y * _sigmoid(o_ref[pl.ds(s0, L), :].astype(F32))
        out_ref[pl.ds(s0, L), :] = y.astype(out_ref.dtype)
        return carry

    lax.fori_loop(0, nc, out_step, 0, unroll=MLSTM_UNROLL)


def _mlstm(qk, v, o, gates, conv_w, conv_b, norm_g, tri3, eye3):
    b, seq, _ = v.shape
    H, d, L = N_MLSTM_HEADS, MLSTM_HEAD_DIM, CHUNK
    nc = seq // L
    assert seq % (MLSTM_UNROLL * L) == 0 and L == LANES
    col = lambda off: pl.BlockSpec((None, seq, d), lambda bi, hi: (bi, 0, hi + off))
    stat = pltpu.VMEM((2, nc, L), F32)
    return pl.pallas_call(
        functools.partial(_mlstm_body, seq=seq),
        grid=(b, H),
        in_specs=[
            col(0), col(H), col(0), col(0),
            pl.BlockSpec((4, None, None, nc, L), lambda bi, hi: (0, hi, bi, 0, 0)),
            pl.BlockSpec((None, 2, 3, d), lambda bi, hi: (hi, 0, 0, 0)),
            pl.BlockSpec((None, 2, 1, d), lambda bi, hi: (hi, 0, 0, 0)),
            pl.BlockSpec((None, 1, d), lambda bi, hi: (hi, 0, 0)),
            _const_spec(tri3.shape), _const_spec(eye3.shape),
        ],
        out_specs=pl.BlockSpec((None, seq, d), lambda bi, hi: (bi, 0, hi)),
        out_shape=jax.ShapeDtypeStruct((b, seq, H * d), BF16),
        scratch_shapes=[
            pltpu.VMEM((seq, d), BF16),
            pltpu.VMEM((d, seq), BF16),
            pltpu.VMEM((2, nc, d, 2 * d), BF16),
            pltpu.VMEM((2, d, 2 * d), F32),
            stat, stat, stat, stat, stat,
            stat,
        ],
        compiler_params=pltpu.CompilerParams(dimension_semantics=("arbitrary", "arbitrary"),
                                             vmem_limit_bytes=52 * MIB),
        name="mlstm",
    )(qk, qk, v, o, gates, conv_w, conv_b, norm_g, tri3, eye3)


def _natten_body(q_ref, k_ref, v_ref, bias_ref, out_ref, *, rows):
    hd = NA_HEAD_DIM
    lane = lax.broadcasted_iota(jnp.int32, (GRID_W, 2 * hd), 1)
    first = lane < hd

    def row_step(r, carry):
        rs = jnp.clip(r - WIN_H // 2, 0, rows - WIN_H)
        slab = rs - r + (WIN_H - 1)
        q0 = pl.multiple_of(r * GRID_W, GRID_W)
        k0 = pl.multiple_of(rs * GRID_W, GRID_W)
        q = q_ref[pl.ds(q0, GRID_W), :]
        zero = jnp.zeros_like(q)
        qs = jnp.concatenate([jnp.where(first, q, zero), jnp.where(first, zero, q)], axis=0)
        kwin = k_ref[pl.ds(k0, WIN_H * GRID_W), :]
        vwin = v_ref[pl.ds(k0, WIN_H * GRID_W), :]
        s = lax.dot_general(qs, kwin, NT_DIMS, preferred_element_type=F32) + bias_ref[slab]
        m = jnp.max(s, axis=1, keepdims=True)
        p = jnp.exp(s - m)
        l = jnp.sum(p, axis=1, keepdims=True)
        o = jnp.dot(p.astype(BF16), vwin, preferred_element_type=F32) / l
        out = jnp.where(first, o[:GRID_W], o[GRID_W:])
        out_ref[pl.ds(q0, GRID_W), :] = out.astype(out_ref.dtype)
        return carry

    lax.fori_loop(0, rows, row_step, 0, unroll=NA_ROW_UNROLL)


def _natten(nq, nk, nv, bias_tab):
    b, seq, d_n = nq.shape
    pairs = N_NA_HEADS // 2
    width = 2 * NA_HEAD_DIM
    rows = seq // GRID_W
    assert rows % NA_ROW_UNROLL == 0
    col = pl.BlockSpec((None, seq, width), lambda bi, pi: (bi, 0, pi))
    return pl.pallas_call(
        functools.partial(_natten_body, rows=rows),
        grid=(b, pairs),
        in_specs=[col, col, col,
                  pl.BlockSpec((None, WIN_H, 2 * GRID_W, WIN_H * GRID_W), lambda bi, pi: (pi, 0, 0, 0))],
        out_specs=col,
        out_shape=jax.ShapeDtypeStruct((b, seq, d_n), BF16),
        compiler_params=pltpu.CompilerParams(dimension_semantics=("arbitrary", "arbitrary"),
                                             vmem_limit_bytes=40 * MIB),
        name="natten",
    )(nq, nk, nv, bias_tab)


def _natten_bias_table(rpb, rows):
    win_h = min(WIN_H, rows)
    c = jnp.arange(GRID_W)
    cs = jnp.clip(c - WIN_W // 2, 0, GRID_W - WIN_W)
    cp = jnp.arange(GRID_W)
    valid = (cp[None, :] >= cs[:, None]) & (cp[None, :] < cs[:, None] + WIN_W)
    coff = jnp.clip(cp[None, :] - c[:, None] + (WIN_W - 1), 0, 2 * WIN_W - 2)
    dense = jnp.where(valid[None, None], rpb.astype(F32)[:, :, coff], NEG_BIG)
    ro = jnp.arange(win_h)[:, None] + jnp.arange(win_h)[None, :]
    slabs = dense[:, ro]
    slabs = slabs.transpose(0, 1, 3, 2, 4).reshape(N_NA_HEADS, win_h, GRID_W, win_h * GRID_W)
    slabs = slabs.reshape(N_NA_HEADS // 2, 2, win_h, GRID_W, win_h * GRID_W).transpose(0, 2, 1, 3, 4)
    return slabs.reshape(N_NA_HEADS // 2, win_h, 2 * GRID_W, win_h * GRID_W)


def _tail_body(x_ref, ya_ref, yb_ref, p_ref, woa_ref, wob_ref, g2_ref, w1_ref, w2_ref,
               g3_ref, wg_ref, wu_ref, out_ref, *, ff_chunk):
    d_ff = w1_ref.shape[1]
    h = (x_ref[...]
         + jnp.dot(ya_ref[...], woa_ref[...], preferred_element_type=F32)
         + jnp.dot(yb_ref[...], wob_ref[...], preferred_element_type=F32))
    u = (h * _rms_scale(h) * g2_ref[...]).astype(BF16)
    out_ref[...] = h
    for j in range(d_ff // ff_chunk):
        z = jnp.dot(u, w1_ref[:, j * ff_chunk:(j + 1) * ff_chunk], preferred_element_type=F32)
        z = jnp.maximum(z, 0.0)
        out_ref[...] += jnp.dot((z * z).astype(BF16), w2_ref[j * ff_chunk:(j + 1) * ff_chunk, :],
                                preferred_element_type=F32)
    h = out_ref[...]
    u = (h * _rms_scale(h) * g3_ref[...]).astype(BF16)
    gate = _sigmoid(jnp.dot(u, wg_ref[...], preferred_element_type=F32))
    up = jnp.dot(p_ref[...].astype(BF16), wu_ref[...], preferred_element_type=F32)
    out_ref[...] = h + gate * up


def _tail(x2, ya, yb, p2, woa, wob, g2, w1, w2, g3, wg, wu, *, tm, ff_chunk):
    t, d = x2.shape
    row = lambda width: pl.BlockSpec((tm, width), lambda i: (i, 0))
    consts = [woa, wob, g2, w1, w2, g3, wg, wu]
    return pl.pallas_call(
        functools.partial(_tail_body, ff_chunk=ff_chunk),
        grid=(t // tm,),
        in_specs=[row(d), row(ya.shape[1]), row(yb.shape[1]), row(p2.shape[1])]
                 + [_const_spec(c.shape) for c in consts],
        out_specs=row(d),
        out_shape=jax.ShapeDtypeStruct((t, d), F32),
        compiler_params=pltpu.CompilerParams(dimension_semantics=("arbitrary",),
                                             vmem_limit_bytes=52 * MIB),
        name="tail",
    )(x2, ya, yb, p2, *consts)


def kernel(x, p, norm1_g, w_in, conv_w, conv_b, gate_b, mlstm_norm_g, q_norm_g, k_norm_g, rpb,
           w_out, norm2_g, w_ff1, w_ff2, ple_norm_g, w_ple_gate, w_ple_up):
    b, seq, d = x.shape
    depth = w_in.shape[0]
    H, hd, L = N_MLSTM_HEADS, MLSTM_HEAD_DIM, CHUNK
    d_m = H * hd
    d_n = N_NA_HEADS * NA_HEAD_DIM
    t = b * seq
    rows = seq // GRID_W
    assert rows >= WIN_H

    bd = jnp.kron(jnp.eye(N_NA_HEADS, dtype=F32), jnp.ones((NA_HEAD_DIM, NA_HEAD_DIM), F32)).astype(BF16)
    ri = lax.broadcasted_iota(jnp.int32, (L, L), 0)
    ci = lax.broadcasted_iota(jnp.int32, (L, L), 1)
    tri = jnp.stack([(ci <= ri), (ci >= ri)]).astype(BF16)
    tri3 = jnp.concatenate([tri, tri, tri], axis=1)
    eye = (ci == ri).astype(BF16)
    eye3 = jnp.concatenate([eye, eye, eye], axis=1)

    h = x.reshape(t, d)
    for i in range(depth):
        wi = w_in[i]
        w_m = wi[:, :4 * d_m].astype(BF16)
        w_n = wi[:, 4 * d_m + N_GATE:].astype(BF16)
        w_g = jnp.pad(wi[:, 4 * d_m:4 * d_m + N_GATE], ((0, 0), (0, GATE_PAD - N_GATE))).astype(BF16)
        qg = (q_norm_g[i].reshape(1, d_n) * (NA_HEAD_DIM ** -0.5)).astype(F32)
        kg = k_norm_g[i].reshape(1, d_n).astype(F32)
        qk, mv, mo, nq, nk, nv, gates = _inproj(
            h, norm1_g[i].reshape(1, d), w_m, w_n, w_g, gate_b[i].reshape(N_GATE, 1), bd, qg, kg,
            d_m=d_m, d_n=d_n, tm=512)

        cw = conv_w[i].reshape(3, 2, H, hd).transpose(2, 1, 0, 3)
        cb = conv_b[i].reshape(2, H, 1, hd).transpose(1, 0, 2, 3)
        y_a = _mlstm(qk.reshape(b, seq, 2 * d_m), mv.reshape(b, seq, d_m), mo.reshape(b, seq, d_m),
                     gates.reshape(4, H, b, seq // L, L), cw, cb,
                     mlstm_norm_g[i].reshape(H, 1, hd), tri3, eye3)

        bias_tab = _natten_bias_table(rpb[i], rows)
        y_b = _natten(nq.reshape(b, seq, d_n), nk.reshape(b, seq, d_n), nv.reshape(b, seq, d_n), bias_tab)

        wo = w_out[i].astype(BF16)
        h = _tail(h, y_a.reshape(t, d_m), y_b.reshape(t, d_n), p[i].reshape(t, -1),
                  wo[:d_m], wo[d_m:], norm2_g[i].reshape(1, d), w_ff1[i].astype(BF16),
                  w_ff2[i].astype(BF16), ple_norm_g[i].reshape(1, d), w_ple_gate[i].astype(BF16),
                  w_ple_up[i].astype(BF16), tm=256, ff_chunk=1024)
    return h.reshape(b, seq, d)
```

```python
import functools

import jax
import jax.numpy as jnp
from jax import lax
from jax.experimental import pallas as pl
from jax.experimental.pallas import tpu as pltpu

F32 = jnp.float32
BF16 = jnp.bfloat16

N_MLSTM_HEADS = 4
MLSTM_HEAD_DIM = 128
N_NA_HEADS = 8
NA_HEAD_DIM = 64
GRID_W = 64
WIN_H = 8
WIN_W = 16
CHUNK = 128
N_GATE = 4 * N_MLSTM_HEADS
RMS_EPS = 1e-6
NEG_BIG = -1e30
NA_ROW_UNROLL = 4
MLSTM_SCAN_UNROLL = 4

LANES = 128
GATE_PAD = LANES
MIB = 1024 * 1024

NT_DIMS = (((1,), (1,)), ((), ()))


def _const_spec(shape):
    return pl.BlockSpec(shape, lambda *_: (0,) * len(shape), pipeline_mode=pl.Buffered(1))


def _rms_scale(x):
    return lax.rsqrt(jnp.mean(x * x, axis=-1, keepdims=True) + RMS_EPS)


def _split3(x):
    hi = x.astype(BF16)
    r1 = x - hi.astype(F32)
    mid = r1.astype(BF16)
    lo = (r1 - mid.astype(F32)).astype(BF16)
    return hi, mid, lo


def _inproj_body(x_ref, g_ref, wm_ref, wn_ref, wg_ref, gb_ref, bd_ref, qg_ref, kg_ref,
                 qk_ref, v_ref, o_ref, nq_ref, nk_ref, nv_ref, gate_ref, *, d_m, d_n):
    x = x_ref[...]
    u = (x * _rms_scale(x) * g_ref[...]).astype(BF16)

    def proj(w_ref, lo, hi):
        return jnp.dot(u, w_ref[:, lo:hi], preferred_element_type=F32)

    def head_norm(y, gain_ref):
        sq = y * y
        hi = sq.astype(BF16)
        lo = (sq - hi.astype(F32)).astype(BF16)
        ss = (jnp.dot(hi, bd_ref[...], preferred_element_type=F32)
              + jnp.dot(lo, bd_ref[...], preferred_element_type=F32))
        return y * lax.rsqrt(ss * (1.0 / NA_HEAD_DIM) + RMS_EPS) * gain_ref[...]

    qk_ref[...] = proj(wm_ref, 0, 2 * d_m).astype(BF16)
    v_ref[...] = proj(wm_ref, 2 * d_m, 3 * d_m).astype(BF16)
    o_ref[...] = proj(wm_ref, 3 * d_m, 4 * d_m).astype(BF16)
    nq_ref[...] = head_norm(proj(wn_ref, 0, d_n), qg_ref).astype(BF16)
    nk_ref[...] = head_norm(proj(wn_ref, d_n, 2 * d_n), kg_ref).astype(BF16)
    nv_ref[...] = proj(wn_ref, 2 * d_n, 3 * d_n).astype(BF16)
    gates_t = jnp.dot(u, wg_ref[...], preferred_element_type=F32).T
    gate_ref[...] = gates_t[:N_GATE, :] + gb_ref[...]


def _inproj(x2, g, w_m, w_n, w_g, gate_b, bd, qg, kg, *, d_m, d_n, tm):
    t, d = x2.shape
    row = lambda width: pl.BlockSpec((tm, width), lambda i: (i, 0))
    out_shapes = (
        jax.ShapeDtypeStruct((t, 2 * d_m), BF16),
        jax.ShapeDtypeStruct((t, d_m), BF16),
        jax.ShapeDtypeStruct((t, d_m), BF16),
        jax.ShapeDtypeStruct((t, d_n), BF16),
        jax.ShapeDtypeStruct((t, d_n), BF16),
        jax.ShapeDtypeStruct((t, d_n), BF16),
        jax.ShapeDtypeStruct((N_GATE, t), F32),
    )
    return pl.pallas_call(
        functools.partial(_inproj_body, d_m=d_m, d_n=d_n),
        grid=(t // tm,),
        in_specs=[row(d), _const_spec((1, d)), _const_spec(w_m.shape), _const_spec(w_n.shape),
                  _const_spec(w_g.shape), _const_spec((N_GATE, 1)),
                  _const_spec((d_n, d_n)), _const_spec((1, d_n)), _const_spec((1, d_n))],
        out_specs=(row(2 * d_m), row(d_m), row(d_m), row(d_n), row(d_n), row(d_n),
                   pl.BlockSpec((N_GATE, tm), lambda i: (0, i))),
        out_shape=out_shapes,
        compiler_params=pltpu.CompilerParams(dimension_semantics=("arbitrary",),
                                             vmem_limit_bytes=44 * MIB),
        name="inproj",
    )(x2, g, w_m, w_n, w_g, gate_b, bd, qg, kg)


def _log_sigmoid(x):
    return jnp.minimum(x, 0.0) - jnp.log1p(jnp.exp(-jnp.abs(x)))


def _sigmoid(x):
    return 1.0 / (1.0 + jnp.exp(-x))


def _mlstm_body(q_ref, k_ref, v_ref, o_ref, gate_ref, cw_ref, cb_ref, ng_ref, tri3_ref, eye2_ref,
                out_ref, qs_ref, kt_ref, cs_ref, ccur_ref, brow_ref, crow_ref, cmax_ref, wt_ref, bl_ref,
                ml_ref, ms_ref, s_ref, wint_ref, einv_ref, *, seq):
    L = CHUNK
    d = MLSTM_HEAD_DIM
    nc = seq // L

    row_id = lax.broadcasted_iota(jnp.int32, (L, d), 0)
    col_id = lax.broadcasted_iota(jnp.int32, (L, d), 1)
    pos_id = lax.broadcasted_iota(jnp.int32, (nc, L), 1)
    ones_blk = jnp.ones((L, d), BF16)

    def conv_silu(src_ref, c, s0, w, b):
        x = src_ref[pl.ds(s0, L), :].astype(F32)
        p0 = pl.multiple_of(jnp.maximum(s0 - 16, 0), 16)
        n0 = pl.multiple_of(jnp.minimum(s0 + L, seq - 16), 16)
        prev_row = src_ref[pl.ds(p0, 16), :][15:16, :].astype(F32)
        next_row = src_ref[pl.ds(n0, 16), :][0:1, :].astype(F32)
        prev_row = jnp.where(c > 0, prev_row, 0.0)
        next_row = jnp.where(c < nc - 1, next_row, 0.0)
        x_prev = jnp.where(row_id == 0, prev_row, pltpu.roll(x, 1, 0))
        x_next = jnp.where(row_id == L - 1, next_row, pltpu.roll(x, L - 1, 0))
        y = w[0:1, :] * x_prev + w[1:2, :] * x + w[2:3, :] * x_next + b
        return y * _sigmoid(y)

    def conv_step(c, carry):
        s0 = pl.multiple_of(c * L, L)
        qs_ref[pl.ds(s0, L), :] = conv_silu(q_ref, c, s0, cw_ref[0], cb_ref[0]).astype(BF16)
        kk = conv_silu(k_ref, c, s0, cw_ref[1], cb_ref[1]) * (d ** -0.5)
        kt_ref[:, pl.ds(s0, L)] = kk.T.astype(BF16)
        return carry

    lax.fori_loop(0, nc, conv_step, 0)

    for dirn in (0, 1):
        i_g = gate_ref[2 * dirn]
        f_log = _log_sigmoid(gate_ref[2 * dirn + 1])
        f_cat = jnp.concatenate(_split3(f_log), axis=1)
        brow = jnp.dot(f_cat, tri3_ref[1 - dirn], preferred_element_type=F32)
        b_last = brow[:, L - 1:L] if dirn == 0 else brow[:, 0:1]
        a_row = i_g + b_last - brow
        a_max = jnp.max(a_row, axis=1, keepdims=True)
        crow = i_g - brow
        cmax = crow
        for sh in [1 << e for e in range(L.bit_length() - 1)]:
            if dirn == 0:
                cmax = jnp.maximum(cmax, jnp.where(pos_id >= sh, pltpu.roll(cmax, sh, 1), -jnp.inf))
            else:
                cmax = jnp.maximum(cmax, jnp.where(pos_id < L - sh, pltpu.roll(cmax, L - sh, 1), -jnp.inf))
        brow_ref[dirn] = brow
        crow_ref[dirn] = crow
        cmax_ref[dirn] = cmax
        wt_ref[dirn] = jnp.exp(a_row - a_max)
        bl_ref[dirn] = jnp.broadcast_to(b_last, (nc, L))
        ml_ref[dirn] = jnp.broadcast_to(a_max, (nc, L))

    def v_aug(s0):
        return jnp.concatenate([v_ref[pl.ds(s0, L), :], ones_blk], axis=1)

    ccur_ref[...] = jnp.zeros_like(ccur_ref)

    def scan_step(i, carry):
        new = []
        for dirn, c, m in ((0, i, carry[0]), (1, nc - 1 - i, carry[1])):
            s0 = pl.multiple_of(c * L, L)
            kw = (kt_ref[:, pl.ds(s0, L)].astype(F32) * wt_ref[dirn, pl.ds(c, 1), :]).astype(BF16)
            k_loc = jnp.dot(kw, v_aug(s0), preferred_element_type=F32)
            state = ccur_ref[dirn]
            cs_ref[dirn, c] = state.astype(BF16)
            ms_ref[dirn, pl.ds(c, 1), :] = m
            a_prev = m + bl_ref[dirn, pl.ds(c, 1), :]
            a_max = ml_ref[dirn, pl.ds(c, 1), :]
            m_new = jnp.maximum(a_prev, a_max)
            w_prev = jnp.exp(a_prev - m_new)[:, 0:1]
            w_loc = jnp.exp(a_max - m_new)[:, 0:1]
            ccur_ref[dirn] = w_prev * state + w_loc * k_loc
            new.append(m_new)
        return tuple(new)

    m0 = jnp.zeros((1, L), F32)
    lax.fori_loop(0, nc, scan_step, (m0, m0), unroll=MLSTM_SCAN_UNROLL)

    for dirn in (0, 1):
        cmax_ref[dirn] = jnp.maximum(ms_ref[dirn], cmax_ref[dirn])
    lower = col_id <= row_id
    upper = col_id >= row_id

    def weights_stage(c, slot):
        s0 = pl.multiple_of(c * L, L)
        qk = jnp.dot(qs_ref[pl.ds(s0, L), :], kt_ref[:, pl.ds(s0, L)], preferred_element_type=F32)
        for dirn, mask in ((0, lower), (1, upper)):
            rows = []
            for stat_ref in (cmax_ref, brow_ref):
                hi, mid, _ = _split3(stat_ref[dirn, pl.ds(c, 1), :])
                rows.append(jnp.broadcast_to(jnp.concatenate([hi, mid], axis=1), (L, 2 * L)))
            col = lax.dot_general(eye2_ref[...], jnp.concatenate(rows, axis=0), NT_DIMS,
                                  preferred_element_type=F32)
            mu, bcol = col[:, :L], col[:, L:]
            p = jnp.exp(jnp.where(mask, crow_ref[dirn, pl.ds(c, 1), :] - mu, -jnp.inf))
            s_ref[slot, dirn] = (qk * p).astype(BF16)
            wint_ref[slot, dirn] = jnp.exp(ms_ref[dirn, pl.ds(c, 1), :] - mu)
            einv_ref[slot, dirn] = jnp.exp(-(bcol + mu))

    def output_stage(c, slot):
        s0 = pl.multiple_of(c * L, L)
        q = qs_ref[pl.ds(s0, L), :]
        vaug = v_aug(s0)
        hsum = None
        for dirn in (0, 1):
            intra = jnp.dot(s_ref[slot, dirn], vaug, preferred_element_type=F32)
            inter = jnp.dot(q, cs_ref[dirn, c], preferred_element_type=F32)
            w_inter = wint_ref[slot, dirn]
            num = w_inter * inter[:, :d] + intra[:, :d]
            den = w_inter * inter[:, d:] + intra[:, d:]
            h = num / jnp.maximum(jnp.abs(den), einv_ref[slot, dirn])
            hsum = h if hsum is None else hsum + h
        y = hsum * _rms_scale(hsum) * ng_ref[...]
        y = y * _sigmoid(o_ref[pl.ds(s0, L), :].astype(F32))
        out_ref[pl.ds(s0, L), :] = y.astype(out_ref.dtype)

    weights_stage(0, 0)

    def out_step(j, carry):
        c = 2 * j
        weights_stage(c + 1, 1)
        output_stage(c, 0)
        weights_stage(jnp.minimum(c + 2, nc - 1), 0)
        output_stage(c + 1, 1)
        return carry

    lax.fori_loop(0, nc // 2, out_step, 0)


def _mlstm(qk, v, o, gates, conv_w, conv_b, norm_g, tri3, eye2):
    b, seq, _ = v.shape
    H, d, L = N_MLSTM_HEADS, MLSTM_HEAD_DIM, CHUNK
    nc = seq // L
    assert nc % MLSTM_SCAN_UNROLL == 0 and nc % 2 == 0 and L == LANES
    col = lambda off: pl.BlockSpec((None, seq, d), lambda bi, hi: (bi, 0, hi + off))
    stat = pltpu.VMEM((2, nc, L), F32)
    return pl.pallas_call(
        functools.partial(_mlstm_body, seq=seq),
        grid=(b, H),
        in_specs=[
            col(0), col(H), col(0), col(0),
            pl.BlockSpec((4, None, None, nc, L), lambda bi, hi: (0, hi, bi, 0, 0)),
            pl.BlockSpec((None, 2, 3, d), lambda bi, hi: (hi, 0, 0, 0)),
            pl.BlockSpec((None, 2, 1, d), lambda bi, hi: (hi, 0, 0, 0)),
            pl.BlockSpec((None, 1, d), lambda bi, hi: (hi, 0, 0)),
            _const_spec(tri3.shape), _const_spec(eye2.shape),
        ],
        out_specs=pl.BlockSpec((None, seq, d), lambda bi, hi: (bi, 0, hi)),
        out_shape=jax.ShapeDtypeStruct((b, seq, H * d), BF16),
        scratch_shapes=[
            pltpu.VMEM((seq, d), BF16),
            pltpu.VMEM((d, seq), BF16),
            pltpu.VMEM((2, nc, d, 2 * d), BF16),
            pltpu.VMEM((2, d, 2 * d), F32),
            stat, stat, stat,
            stat, stat, stat,
            stat,
            pltpu.VMEM((2, 2, L, L), BF16),
            pltpu.VMEM((2, 2, L, d), F32),
            pltpu.VMEM((2, 2, L, d), F32),
        ],
        compiler_params=pltpu.CompilerParams(dimension_semantics=("arbitrary", "arbitrary"),
                                             vmem_limit_bytes=52 * MIB),
        name="mlstm",
    )(qk, qk, v, o, gates, conv_w, conv_b, norm_g, tri3, eye2)


def _natten_body(q_ref, k_ref, v_ref, bias_ref, out_ref, *, rows):
    hd = NA_HEAD_DIM
    lane = lax.broadcasted_iota(jnp.int32, (GRID_W, 2 * hd), 1)
    first = lane < hd

    def row_step(r, carry):
        rs = jnp.clip(r - WIN_H // 2, 0, rows - WIN_H)
        slab = rs - r + (WIN_H - 1)
        q0 = pl.multiple_of(r * GRID_W, GRID_W)
        k0 = pl.multiple_of(rs * GRID_W, GRID_W)
        q = q_ref[pl.ds(q0, GRID_W), :]
        zero = jnp.zeros_like(q)
        qs = jnp.concatenate([jnp.where(first, q, zero), jnp.where(first, zero, q)], axis=0)
        kwin = k_ref[pl.ds(k0, WIN_H * GRID_W), :]
        vwin = v_ref[pl.ds(k0, WIN_H * GRID_W), :]
        s = lax.dot_general(qs, kwin, NT_DIMS, preferred_element_type=F32) + bias_ref[slab]
        m = jnp.max(s, axis=1, keepdims=True)
        p = jnp.exp(s - m)
        l = jnp.sum(p, axis=1, keepdims=True)
        o = jnp.dot(p.astype(BF16), vwin, preferred_element_type=F32) / l
        out = jnp.where(first, o[:GRID_W], o[GRID_W:])
        out_ref[pl.ds(q0, GRID_W), :] = out.astype(out_ref.dtype)
        return carry

    lax.fori_loop(0, rows, row_step, 0, unroll=NA_ROW_UNROLL)


def _natten(nq, nk, nv, bias_tab):
    b, seq, d_n = nq.shape
    pairs = N_NA_HEADS // 2
    width = 2 * NA_HEAD_DIM
    rows = seq // GRID_W
    assert rows % NA_ROW_UNROLL == 0
    col = pl.BlockSpec((None, seq, width), lambda bi, pi: (bi, 0, pi))
    return pl.pallas_call(
        functools.partial(_natten_body, rows=rows),
        grid=(b, pairs),
        in_specs=[col, col, col,
                  pl.BlockSpec((None, WIN_H, 2 * GRID_W, WIN_H * GRID_W), lambda bi, pi: (pi, 0, 0, 0))],
        out_specs=col,
        out_shape=jax.ShapeDtypeStruct((b, seq, d_n), BF16),
        compiler_params=pltpu.CompilerParams(dimension_semantics=("arbitrary", "arbitrary"),
                                             vmem_limit_bytes=40 * MIB),
        name="natten",
    )(nq, nk, nv, bias_tab)


def _natten_bias_table(rpb, rows):
    win_h = min(WIN_H, rows)
    c = jnp.arange(GRID_W)
    cs = jnp.clip(c - WIN_W // 2, 0, GRID_W - WIN_W)
    cp = jnp.arange(GRID_W)
    valid = (cp[None, :] >= cs[:, None]) & (cp[None, :] < cs[:, None] + WIN_W)
    coff = jnp.clip(cp[None, :] - c[:, None] + (WIN_W - 1), 0, 2 * WIN_W - 2)
    dense = jnp.where(valid[None, None], rpb.astype(F32)[:, :, coff], NEG_BIG)
    ro = jnp.arange(win_h)[:, None] + jnp.arange(win_h)[None, :]
    slabs = dense[:, ro]
    slabs = slabs.transpose(0, 1, 3, 2, 4).reshape(N_NA_HEADS, win_h, GRID_W, win_h * GRID_W)
    slabs = slabs.reshape(N_NA_HEADS // 2, 2, win_h, GRID_W, win_h * GRID_W).transpose(0, 2, 1, 3, 4)
    return slabs.reshape(N_NA_HEADS // 2, win_h, 2 * GRID_W, win_h * GRID_W)


def _tail_body(x_ref, ya_ref, yb_ref, p_ref, woa_ref, wob_ref, g2_ref, w1_ref, w2_ref,
               g3_ref, wg_ref, wu_ref, out_ref, *, ff_chunk):
    d_ff = w1_ref.shape[1]
    h = (x_ref[...]
         + jnp.dot(ya_ref[...], woa_ref[...], preferred_element_type=F32)
         + jnp.dot(yb_ref[...], wob_ref[...], preferred_element_type=F32))
    u = (h * _rms_scale(h) * g2_ref[...]).astype(BF16)
    out_ref[...] = h
    for j in range(d_ff // ff_chunk):
        z = jnp.dot(u, w1_ref[:, j * ff_chunk:(j + 1) * ff_chunk], preferred_element_type=F32)
        z = jnp.maximum(z, 0.0)
        out_ref[...] += jnp.dot((z * z).astype(BF16), w2_ref[j * ff_chunk:(j + 1) * ff_chunk, :],
                                preferred_element_type=F32)
    h = out_ref[...]
    u = (h * _rms_scale(h) * g3_ref[...]).astype(BF16)
    gate = _sigmoid(jnp.dot(u, wg_ref[...], preferred_element_type=F32))
    up = jnp.dot(p_ref[...].astype(BF16), wu_ref[...], preferred_element_type=F32)
    out_ref[...] = h + gate * up


def _tail(x2, ya, yb, p2, woa, wob, g2, w1, w2, g3, wg, wu, *, tm, ff_chunk):
    t, d = x2.shape
    row = lambda width: pl.BlockSpec((tm, width), lambda i: (i, 0))
    consts = [woa, wob, g2, w1, w2, g3, wg, wu]
    return pl.pallas_call(
        functools.partial(_tail_body, ff_chunk=ff_chunk),
        grid=(t // tm,),
        in_specs=[row(d), row(ya.shape[1]), row(yb.shape[1]), row(p2.shape[1])]
                 + [_const_spec(c.shape) for c in consts],
        out_specs=row(d),
        out_shape=jax.ShapeDtypeStruct((t, d), F32),
        compiler_params=pltpu.CompilerParams(dimension_semantics=("arbitrary",),
                                             vmem_limit_bytes=52 * MIB),
        name="tail",
    )(x2, ya, yb, p2, *consts)


def kernel(x, p, norm1_g, w_in, conv_w, conv_b, gate_b, mlstm_norm_g, q_norm_g, k_norm_g, rpb,
           w_out, norm2_g, w_ff1, w_ff2, ple_norm_g, w_ple_gate, w_ple_up):
    b, seq, d = x.shape
    depth = w_in.shape[0]
    H, hd, L = N_MLSTM_HEADS, MLSTM_HEAD_DIM, CHUNK
    d_m = H * hd
    d_n = N_NA_HEADS * NA_HEAD_DIM
    t = b * seq
    rows = seq // GRID_W
    assert rows >= WIN_H

    bd = jnp.kron(jnp.eye(N_NA_HEADS, dtype=F32), jnp.ones((NA_HEAD_DIM, NA_HEAD_DIM), F32)).astype(BF16)
    ri = lax.broadcasted_iota(jnp.int32, (L, L), 0)
    ci = lax.broadcasted_iota(jnp.int32, (L, L), 1)
    tri = jnp.stack([(ci <= ri), (ci >= ri)]).astype(BF16)
    tri3 = jnp.concatenate([tri, tri, tri], axis=1)
    eye = (ci == ri).astype(BF16)
    eye2 = jnp.concatenate([eye, eye], axis=1)

    h = x.reshape(t, d)
    for i in range(depth):
        wi = w_in[i]
        w_m = wi[:, :4 * d_m].astype(BF16)
        w_n = wi[:, 4 * d_m + N_GATE:].astype(BF16)
        w_g = jnp.pad(wi[:, 4 * d_m:4 * d_m + N_GATE], ((0, 0), (0, GATE_PAD - N_GATE))).astype(BF16)
        qg = (q_norm_g[i].reshape(1, d_n) * (NA_HEAD_DIM ** -0.5)).astype(F32)
        kg = k_norm_g[i].reshape(1, d_n).astype(F32)
        qk, mv, mo, nq, nk, nv, gates = _inproj(
            h, norm1_g[i].reshape(1, d), w_m, w_n, w_g, gate_b[i].reshape(N_GATE, 1), bd, qg, kg,
            d_m=d_m, d_n=d_n, tm=512)

        cw = conv_w[i].reshape(3, 2, H, hd).transpose(2, 1, 0, 3)
        cb = conv_b[i].reshape(2, H, 1, hd).transpose(1, 0, 2, 3)
        y_a = _mlstm(qk.reshape(b, seq, 2 * d_m), mv.reshape(b, seq, d_m), mo.reshape(b, seq, d_m),
                     gates.reshape(4, H, b, seq // L, L), cw, cb,
                     mlstm_norm_g[i].reshape(H, 1, hd), tri3, eye2)

        bias_tab = _natten_bias_table(rpb[i], rows)
        y_b = _natten(nq.reshape(b, seq, d_n), nk.reshape(b, seq, d_n), nv.reshape(b, seq, d_n), bias_tab)

        wo = w_out[i].astype(BF16)
        h = _tail(h, y_a.reshape(t, d_m), y_b.reshape(t, d_n), p[i].reshape(t, -1),
                  wo[:d_m], wo[d_m:], norm2_g[i].reshape(1, d), w_ff1[i].astype(BF16),
                  w_ff2[i].astype(BF16), ple_norm_g[i].reshape(1, d), w_ple_gate[i].astype(BF16),
                  w_ple_up[i].astype(BF16), tm=256, ff_chunk=1024)
    return h.reshape(b, seq, d)
```

```python
import functools

import jax
import jax.numpy as jnp
from jax import lax
from jax.experimental import pallas as pl
from jax.experimental.pallas import tpu as pltpu

F32 = jnp.float32
BF16 = jnp.bfloat16

N_MLSTM_HEADS = 4
MLSTM_HEAD_DIM = 128
N_NA_HEADS = 8
NA_HEAD_DIM = 64
GRID_W = 64
WIN_H = 8
WIN_W = 16
CHUNK = 128
N_GATE = 4 * N_MLSTM_HEADS
RMS_EPS = 1e-6
NEG_BIG = -1e30
NA_ROWS_PER_STEP = 8
MLSTM_CHUNKS_PER_STEP = 4
MLSTM_CONV_UNROLL = 4
MLSTM_SCAN_UNROLL = 4

LANES = 128
GATE_PAD = LANES
MIB = 1024 * 1024

NT_DIMS = (((1,), (1,)), ((), ()))


def _const_spec(shape):
    return pl.BlockSpec(shape, lambda *_: (0,) * len(shape), pipeline_mode=pl.Buffered(1))


def _rms_scale(x):
    return lax.rsqrt(jnp.mean(x * x, axis=-1, keepdims=True) + RMS_EPS)


def _split3(x):
    hi = x.astype(BF16)
    r1 = x - hi.astype(F32)
    mid = r1.astype(BF16)
    lo = (r1 - mid.astype(F32)).astype(BF16)
    return hi, mid, lo


def _inproj_body(x_ref, g_ref, wm_ref, wn_ref, wg_ref, gb_ref, bd_ref, qg_ref, kg_ref,
                 qk_ref, v_ref, o_ref, nq_ref, nk_ref, nv_ref, gate_ref, *, d_m, d_n):
    x = x_ref[...]
    u = (x * _rms_scale(x) * g_ref[...]).astype(BF16)

    def proj(w_ref, lo, hi):
        return jnp.dot(u, w_ref[:, lo:hi], preferred_element_type=F32)

    def head_norm(y, gain_ref):
        sq = y * y
        hi = sq.astype(BF16)
        lo = (sq - hi.astype(F32)).astype(BF16)
        ss = (jnp.dot(hi, bd_ref[...], preferred_element_type=F32)
              + jnp.dot(lo, bd_ref[...], preferred_element_type=F32))
        return y * lax.rsqrt(ss * (1.0 / NA_HEAD_DIM) + RMS_EPS) * gain_ref[...]

    qk_ref[...] = proj(wm_ref, 0, 2 * d_m).astype(BF16)
    v_ref[...] = proj(wm_ref, 2 * d_m, 3 * d_m).astype(BF16)
    o_ref[...] = proj(wm_ref, 3 * d_m, 4 * d_m).astype(BF16)
    nq_ref[...] = head_norm(proj(wn_ref, 0, d_n), qg_ref).astype(BF16)
    nk_ref[...] = head_norm(proj(wn_ref, d_n, 2 * d_n), kg_ref).astype(BF16)
    nv_ref[...] = proj(wn_ref, 2 * d_n, 3 * d_n).astype(BF16)
    gates_t = jnp.dot(u, wg_ref[...], preferred_element_type=F32).T
    gate_ref[...] = gates_t[:N_GATE, :] + gb_ref[...]


def _inproj(x2, g, w_m, w_n, w_g, gate_b, bd, qg, kg, *, d_m, d_n, tm):
    t, d = x2.shape
    row = lambda width: pl.BlockSpec((tm, width), lambda i: (i, 0))
    out_shapes = (
        jax.ShapeDtypeStruct((t, 2 * d_m), BF16),
        jax.ShapeDtypeStruct((t, d_m), BF16),
        jax.ShapeDtypeStruct((t, d_m), BF16),
        jax.ShapeDtypeStruct((t, d_n), BF16),
        jax.ShapeDtypeStruct((t, d_n), BF16),
        jax.ShapeDtypeStruct((t, d_n), BF16),
        jax.ShapeDtypeStruct((N_GATE, t), F32),
    )
    return pl.pallas_call(
        functools.partial(_inproj_body, d_m=d_m, d_n=d_n),
        grid=(t // tm,),
        in_specs=[row(d), _const_spec((1, d)), _const_spec(w_m.shape), _const_spec(w_n.shape),
                  _const_spec(w_g.shape), _const_spec((N_GATE, 1)),
                  _const_spec((d_n, d_n)), _const_spec((1, d_n)), _const_spec((1, d_n))],
        out_specs=(row(2 * d_m), row(d_m), row(d_m), row(d_n), row(d_n), row(d_n),
                   pl.BlockSpec((N_GATE, tm), lambda i: (0, i))),
        out_shape=out_shapes,
        compiler_params=pltpu.CompilerParams(dimension_semantics=("arbitrary",),
                                             vmem_limit_bytes=44 * MIB),
        name="inproj",
    )(x2, g, w_m, w_n, w_g, gate_b, bd, qg, kg)


def _log_sigmoid(x):
    return jnp.minimum(x, 0.0) - jnp.log1p(jnp.exp(-jnp.abs(x)))


def _sigmoid(x):
    return 1.0 / (1.0 + jnp.exp(-x))


def _mlstm_body(q_ref, k_ref, v_ref, o_ref, gate_ref, cw_ref, cb_ref, ng_ref, tri3_ref, eye2_ref,
                out_ref, qs_ref, kt_ref, cs_ref, ccur_ref, brow_ref, crow_ref, cmax_ref, wt_ref, bl_ref,
                ml_ref, ms_ref, s_ref, wint_ref, einv_ref, *, seq):
    L = CHUNK
    d = MLSTM_HEAD_DIM
    nc = seq // L

    row_id = lax.broadcasted_iota(jnp.int32, (L, d), 0)
    col_id = lax.broadcasted_iota(jnp.int32, (L, d), 1)
    pos_id = lax.broadcasted_iota(jnp.int32, (nc, L), 1)
    ones_blk = jnp.ones((L, d), BF16)

    def conv_silu(src_ref, c, s0, w, b):
        x = src_ref[pl.ds(s0, L), :].astype(F32)
        p0 = pl.multiple_of(jnp.maximum(s0 - 16, 0), 16)
        n0 = pl.multiple_of(jnp.minimum(s0 + L, seq - 16), 16)
        prev_row = src_ref[pl.ds(p0, 16), :][15:16, :].astype(F32)
        next_row = src_ref[pl.ds(n0, 16), :][0:1, :].astype(F32)
        prev_row = jnp.where(c > 0, prev_row, 0.0)
        next_row = jnp.where(c < nc - 1, next_row, 0.0)
        x_prev = jnp.where(row_id == 0, prev_row, pltpu.roll(x, 1, 0))
        x_next = jnp.where(row_id == L - 1, next_row, pltpu.roll(x, L - 1, 0))
        y = w[0:1, :] * x_prev + w[1:2, :] * x + w[2:3, :] * x_next + b
        return y * _sigmoid(y)

    def conv_step(c, carry):
        s0 = pl.multiple_of(c * L, L)
        qs_ref[pl.ds(s0, L), :] = conv_silu(q_ref, c, s0, cw_ref[0], cb_ref[0]).astype(BF16)
        kk = conv_silu(k_ref, c, s0, cw_ref[1], cb_ref[1]) * (d ** -0.5)
        kt_ref[:, pl.ds(s0, L)] = kk.T.astype(BF16)
        return carry

    lax.fori_loop(0, nc, conv_step, 0, unroll=MLSTM_CONV_UNROLL)

    for dirn in (0, 1):
        i_g = gate_ref[2 * dirn]
        f_log = _log_sigmoid(gate_ref[2 * dirn + 1])
        f_cat = jnp.concatenate(_split3(f_log), axis=1)
        brow = jnp.dot(f_cat, tri3_ref[1 - dirn], preferred_element_type=F32)
        b_last = brow[:, L - 1:L] if dirn == 0 else brow[:, 0:1]
        a_row = i_g + b_last - brow
        a_max = jnp.max(a_row, axis=1, keepdims=True)
        crow = i_g - brow
        cmax = crow
        for sh in [1 << e for e in range(L.bit_length() - 1)]:
            if dirn == 0:
                cmax = jnp.maximum(cmax, jnp.where(pos_id >= sh, pltpu.roll(cmax, sh, 1), -jnp.inf))
            else:
                cmax = jnp.maximum(cmax, jnp.where(pos_id < L - sh, pltpu.roll(cmax, L - sh, 1), -jnp.inf))
        brow_ref[dirn] = brow
        crow_ref[dirn] = crow
        cmax_ref[dirn] = cmax
        wt_ref[dirn] = jnp.exp(a_row - a_max)
        bl_ref[dirn] = jnp.broadcast_to(b_last, (nc, L))
        ml_ref[dirn] = jnp.broadcast_to(a_max, (nc, L))

    def v_aug(s0):
        return jnp.concatenate([v_ref[pl.ds(s0, L), :], ones_blk], axis=1)

    ccur_ref[...] = jnp.zeros_like(ccur_ref)

    def scan_step(i, carry):
        new = []
        for dirn, c, m in ((0, i, carry[0]), (1, nc - 1 - i, carry[1])):
            s0 = pl.multiple_of(c * L, L)
            kw = (kt_ref[:, pl.ds(s0, L)].astype(F32) * wt_ref[dirn, pl.ds(c, 1), :]).astype(BF16)
            k_loc = jnp.dot(kw, v_aug(s0), preferred_element_type=F32)
            state = ccur_ref[dirn]
            cs_ref[dirn, c] = state.astype(BF16)
            ms_ref[dirn, pl.ds(c, 1), :] = m
            a_prev = m + bl_ref[dirn, pl.ds(c, 1), :]
            a_max = ml_ref[dirn, pl.ds(c, 1), :]
            m_new = jnp.maximum(a_prev, a_max)
            w_prev = jnp.exp(a_prev - m_new)[:, 0:1]
            w_loc = jnp.exp(a_max - m_new)[:, 0:1]
            ccur_ref[dirn] = w_prev * state + w_loc * k_loc
            new.append(m_new)
        return tuple(new)

    m0 = jnp.zeros((1, L), F32)
    lax.fori_loop(0, nc, scan_step, (m0, m0), unroll=MLSTM_SCAN_UNROLL)

    for dirn in (0, 1):
        cmax_ref[dirn] = jnp.maximum(ms_ref[dirn], cmax_ref[dirn])
    lower = col_id <= row_id
    upper = col_id >= row_id

    def weights_stage(c, slot):
        s0 = pl.multiple_of(c * L, L)
        qk = jnp.dot(qs_ref[pl.ds(s0, L), :], kt_ref[:, pl.ds(s0, L)], preferred_element_type=F32)
        for dirn, mask in ((0, lower), (1, upper)):
            rows = []
            for stat_ref in (cmax_ref, brow_ref):
                hi, mid, _ = _split3(stat_ref[dirn, pl.ds(c, 1), :])
                rows.append(jnp.broadcast_to(jnp.concatenate([hi, mid], axis=1), (L, 2 * L)))
            col = lax.dot_general(eye2_ref[...], jnp.concatenate(rows, axis=0), NT_DIMS,
                                  preferred_element_type=F32)
            mu, bcol = col[:, :L], col[:, L:]
            p = jnp.exp(jnp.where(mask, crow_ref[dirn, pl.ds(c, 1), :] - mu, -jnp.inf))
            s_ref[slot, dirn] = (qk * p).astype(BF16)
            wint_ref[slot, dirn] = jnp.exp(ms_ref[dirn, pl.ds(c, 1), :] - mu)
            einv_ref[slot, dirn] = jnp.exp(-(bcol + mu))

    def output_stage(c, slot):
        s0 = pl.multiple_of(c * L, L)
        q = qs_ref[pl.ds(s0, L), :]
        vaug = v_aug(s0)
        hsum = None
        for dirn in (0, 1):
            intra = jnp.dot(s_ref[slot, dirn], vaug, preferred_element_type=F32)
            inter = jnp.dot(q, cs_ref[dirn, c], preferred_element_type=F32)
            w_inter = wint_ref[slot, dirn]
            num = w_inter * inter[:, :d] + intra[:, :d]
            den = w_inter * inter[:, d:] + intra[:, d:]
            h = num / jnp.maximum(jnp.abs(den), einv_ref[slot, dirn])
            hsum = h if hsum is None else hsum + h
        y = hsum * _rms_scale(hsum) * ng_ref[...]
        y = y * _sigmoid(o_ref[pl.ds(s0, L), :].astype(F32))
        out_ref[pl.ds(s0, L), :] = y.astype(out_ref.dtype)

    G = MLSTM_CHUNKS_PER_STEP
    for i in range(G):
        weights_stage(i, i)

    def out_step(j, carry):
        cur = (j % 2) * G
        nxt = G - cur
        for i in range(G):
            output_stage(j * G + i, cur + i)
        for i in range(G):
            weights_stage(jnp.minimum((j + 1) * G + i, nc - 1), nxt + i)
        return carry

    lax.fori_loop(0, nc // G, out_step, 0)


def _mlstm(qk, v, o, gates, conv_w, conv_b, norm_g, tri3, eye2):
    b, seq, _ = v.shape
    H, d, L = N_MLSTM_HEADS, MLSTM_HEAD_DIM, CHUNK
    nc = seq // L
    assert nc % MLSTM_SCAN_UNROLL == 0 and nc % MLSTM_CONV_UNROLL == 0 and L == LANES
    assert nc % MLSTM_CHUNKS_PER_STEP == 0
    col = lambda off: pl.BlockSpec((None, seq, d), lambda bi, hi: (bi, 0, hi + off))
    stat = pltpu.VMEM((2, nc, L), F32)
    slots = 2 * MLSTM_CHUNKS_PER_STEP
    return pl.pallas_call(
        functools.partial(_mlstm_body, seq=seq),
        grid=(b, H),
        in_specs=[
            col(0), col(H), col(0), col(0),
            pl.BlockSpec((4, None, None, nc, L), lambda bi, hi: (0, hi, bi, 0, 0)),
            pl.BlockSpec((None, 2, 3, d), lambda bi, hi: (hi, 0, 0, 0)),
            pl.BlockSpec((None, 2, 1, d), lambda bi, hi: (hi, 0, 0, 0)),
            pl.BlockSpec((None, 1, d), lambda bi, hi: (hi, 0, 0)),
            _const_spec(tri3.shape), _const_spec(eye2.shape),
        ],
        out_specs=pl.BlockSpec((None, seq, d), lambda bi, hi: (bi, 0, hi)),
        out_shape=jax.ShapeDtypeStruct((b, seq, H * d), BF16),
        scratch_shapes=[
            pltpu.VMEM((seq, d), BF16),
            pltpu.VMEM((d, seq), BF16),
            pltpu.VMEM((2, nc, d, 2 * d), BF16),
            pltpu.VMEM((2, d, 2 * d), F32),
            stat, stat, stat,
            stat, stat, stat,
            stat,
            pltpu.VMEM((slots, 2, L, L), BF16),
            pltpu.VMEM((slots, 2, L, d), F32),
            pltpu.VMEM((slots, 2, L, d), F32),
        ],
        compiler_params=pltpu.CompilerParams(dimension_semantics=("arbitrary", "arbitrary"),
                                             vmem_limit_bytes=52 * MIB),
        name="mlstm",
    )(qk, qk, v, o, gates, conv_w, conv_b, norm_g, tri3, eye2)


def _natten_body(q_ref, k_ref, v_ref, bias_ref, out_ref, p_ref, *, rows):
    hd = NA_HEAD_DIM
    win = WIN_H * GRID_W
    lane = lax.broadcasted_iota(jnp.int32, (GRID_W, 2 * hd), 1)
    first = lane < hd
    ones_blk = jnp.ones((win, 2 * hd), BF16)

    def window_start(r):
        return jnp.clip(r - WIN_H // 2, 0, rows - WIN_H)

    def prob_stage(r, slot):
        rs = window_start(r)
        q = q_ref[pl.ds(pl.multiple_of(r * GRID_W, GRID_W), GRID_W), :]
        zero = jnp.zeros_like(q)
        qs = jnp.concatenate([jnp.where(first, q, zero), jnp.where(first, zero, q)], axis=0)
        kwin = k_ref[pl.ds(pl.multiple_of(rs * GRID_W, GRID_W), win), :]
        s = lax.dot_general(qs, kwin, NT_DIMS, preferred_element_type=F32) + bias_ref[rs - r + (WIN_H - 1)]
        p_ref[slot] = jnp.exp(s - jnp.max(s, axis=1, keepdims=True)).astype(BF16)

    def output_stage(r, slot):
        rs = window_start(r)
        vwin = v_ref[pl.ds(pl.multiple_of(rs * GRID_W, GRID_W), win), :]
        o = jnp.dot(p_ref[slot], jnp.concatenate([vwin, ones_blk], axis=1), preferred_element_type=F32)
        o = o[:, :2 * hd] / o[:, 2 * hd:]
        out = jnp.where(first, o[:GRID_W], o[GRID_W:])
        out_ref[pl.ds(pl.multiple_of(r * GRID_W, GRID_W), GRID_W), :] = out.astype(out_ref.dtype)

    n_it = rows // NA_ROWS_PER_STEP
    for i in range(NA_ROWS_PER_STEP):
        prob_stage(i, i)

    def row_group(j, carry):
        cur = (j % 2) * NA_ROWS_PER_STEP
        nxt = NA_ROWS_PER_STEP - cur
        for i in range(NA_ROWS_PER_STEP):
            output_stage(j * NA_ROWS_PER_STEP + i, cur + i)
        for i in range(NA_ROWS_PER_STEP):
            prob_stage(jnp.minimum((j + 1) * NA_ROWS_PER_STEP + i, rows - 1), nxt + i)
        return carry

    lax.fori_loop(0, n_it, row_group, 0)


def _natten(nq, nk, nv, bias_tab):
    b, seq, d_n = nq.shape
    pairs = N_NA_HEADS // 2
    width = 2 * NA_HEAD_DIM
    rows = seq // GRID_W
    assert rows % NA_ROWS_PER_STEP == 0
    col = pl.BlockSpec((None, seq, width), lambda bi, pi: (bi, 0, pi))
    return pl.pallas_call(
        functools.partial(_natten_body, rows=rows),
        grid=(b, pairs),
        in_specs=[col, col, col,
                  pl.BlockSpec((None, WIN_H, 2 * GRID_W, WIN_H * GRID_W), lambda bi, pi: (pi, 0, 0, 0))],
        out_specs=col,
        out_shape=jax.ShapeDtypeStruct((b, seq, d_n), BF16),
        scratch_shapes=[pltpu.VMEM((2 * NA_ROWS_PER_STEP, 2 * GRID_W, WIN_H * GRID_W), BF16)],
        compiler_params=pltpu.CompilerParams(dimension_semantics=("arbitrary", "arbitrary"),
                                             vmem_limit_bytes=40 * MIB),
        name="natten",
    )(nq, nk, nv, bias_tab)


def _natten_bias_table(rpb, rows):
    win_h = min(WIN_H, rows)
    c = jnp.arange(GRID_W)
    cs = jnp.clip(c - WIN_W // 2, 0, GRID_W - WIN_W)
    cp = jnp.arange(GRID_W)
    valid = (cp[None, :] >= cs[:, None]) & (cp[None, :] < cs[:, None] + WIN_W)
    coff = jnp.clip(cp[None, :] - c[:, None] + (WIN_W - 1), 0, 2 * WIN_W - 2)
    dense = jnp.where(valid[None, None], rpb.astype(F32)[:, :, coff], NEG_BIG)
    ro = jnp.arange(win_h)[:, None] + jnp.arange(win_h)[None, :]
    slabs = dense[:, ro]
    slabs = slabs.transpose(0, 1, 3, 2, 4).reshape(N_NA_HEADS, win_h, GRID_W, win_h * GRID_W)
    slabs = slabs.reshape(N_NA_HEADS // 2, 2, win_h, GRID_W, win_h * GRID_W).transpose(0, 2, 1, 3, 4)
    return slabs.reshape(N_NA_HEADS // 2, win_h, 2 * GRID_W, win_h * GRID_W)


def _tail_body(x_ref, ya_ref, yb_ref, p_ref, woa_ref, wob_ref, g2_ref, w1_ref, w2_ref,
               g3_ref, wg_ref, wu_ref, out_ref, *, ff_chunk):
    d_ff = w1_ref.shape[1]
    h = (x_ref[...]
         + jnp.dot(ya_ref[...], woa_ref[...], preferred_element_type=F32)
         + jnp.dot(yb_ref[...], wob_ref[...], preferred_element_type=F32))
    u = (h * _rms_scale(h) * g2_ref[...]).astype(BF16)
    out_ref[...] = h
    for j in range(d_ff // ff_chunk):
        z = jnp.dot(u, w1_ref[:, j * ff_chunk:(j + 1) * ff_chunk], preferred_element_type=F32)
        z = jnp.maximum(z, 0.0)
        out_ref[...] += jnp.dot((z * z).astype(BF16), w2_ref[j * ff_chunk:(j + 1) * ff_chunk, :],
                                preferred_element_type=F32)
    h = out_ref[...]
    u = (h * _rms_scale(h) * g3_ref[...]).astype(BF16)
    gate = _sigmoid(jnp.dot(u, wg_ref[...], preferred_element_type=F32))
    up = jnp.dot(p_ref[...].astype(BF16), wu_ref[...], preferred_element_type=F32)
    out_ref[...] = h + gate * up


def _tail(x2, ya, yb, p2, woa, wob, g2, w1, w2, g3, wg, wu, *, tm, ff_chunk):
    t, d = x2.shape
    row = lambda width: pl.BlockSpec((tm, width), lambda i: (i, 0))
    consts = [woa, wob, g2, w1, w2, g3, wg, wu]
    return pl.pallas_call(
        functools.partial(_tail_body, ff_chunk=ff_chunk),
        grid=(t // tm,),
        in_specs=[row(d), row(ya.shape[1]), row(yb.shape[1]), row(p2.shape[1])]
                 + [_const_spec(c.shape) for c in consts],
        out_specs=row(d),
        out_shape=jax.ShapeDtypeStruct((t, d), F32),
        compiler_params=pltpu.CompilerParams(dimension_semantics=("arbitrary",),
                                             vmem_limit_bytes=52 * MIB),
        name="tail",
    )(x2, ya, yb, p2, *consts)


def kernel(x, p, norm1_g, w_in, conv_w, conv_b, gate_b, mlstm_norm_g, q_norm_g, k_norm_g, rpb,
           w_out, norm2_g, w_ff1, w_ff2, ple_norm_g, w_ple_gate, w_ple_up):
    b, seq, d = x.shape
    depth = w_in.shape[0]
    H, hd, L = N_MLSTM_HEADS, MLSTM_HEAD_DIM, CHUNK
    d_m = H * hd
    d_n = N_NA_HEADS * NA_HEAD_DIM
    t = b * seq
    rows = seq // GRID_W
    assert rows >= WIN_H

    bd = jnp.kron(jnp.eye(N_NA_HEADS, dtype=F32), jnp.ones((NA_HEAD_DIM, NA_HEAD_DIM), F32)).astype(BF16)
    ri = lax.broadcasted_iota(jnp.int32, (L, L), 0)
    ci = lax.broadcasted_iota(jnp.int32, (L, L), 1)
    tri = jnp.stack([(ci <= ri), (ci >= ri)]).astype(BF16)
    tri3 = jnp.concatenate([tri, tri, tri], axis=1)
    eye = (ci == ri).astype(BF16)
    eye2 = jnp.concatenate([eye, eye], axis=1)

    h = x.reshape(t, d)
    for i in range(depth):
        wi = w_in[i]
        w_m = wi[:, :4 * d_m].astype(BF16)
        w_n = wi[:, 4 * d_m + N_GATE:].astype(BF16)
        w_g = jnp.pad(wi[:, 4 * d_m:4 * d_m + N_GATE], ((0, 0), (0, GATE_PAD - N_GATE))).astype(BF16)
        qg = (q_norm_g[i].reshape(1, d_n) * (NA_HEAD_DIM ** -0.5)).astype(F32)
        kg = k_norm_g[i].reshape(1, d_n).astype(F32)
        qk, mv, mo, nq, nk, nv, gates = _inproj(
            h, norm1_g[i].reshape(1, d), w_m, w_n, w_g, gate_b[i].reshape(N_GATE, 1), bd, qg, kg,
            d_m=d_m, d_n=d_n, tm=512)

        cw = conv_w[i].reshape(3, 2, H, hd).transpose(2, 1, 0, 3)
        cb = conv_b[i].reshape(2, H, 1, hd).transpose(1, 0, 2, 3)
        y_a = _mlstm(qk.reshape(b, seq, 2 * d_m), mv.reshape(b, seq, d_m), mo.reshape(b, seq, d_m),
                     gates.reshape(4, H, b, seq // L, L), cw, cb,
                     mlstm_norm_g[i].reshape(H, 1, hd), tri3, eye2)

        bias_tab = _natten_bias_table(rpb[i], rows)
        y_b = _natten(nq.reshape(b, seq, d_n), nk.reshape(b, seq, d_n), nv.reshape(b, seq, d_n), bias_tab)

        wo = w_out[i].astype(BF16)
        h = _tail(h, y_a.reshape(t, d_m), y_b.reshape(t, d_n), p[i].reshape(t, -1),
                  wo[:d_m], wo[d_m:], norm2_g[i].reshape(1, d), w_ff1[i].astype(BF16),
                  w_ff2[i].astype(BF16), ple_norm_g[i].reshape(1, d), w_ple_gate[i].astype(BF16),
                  w_ple_up[i].astype(BF16), tm=256, ff_chunk=1024)
    return h.reshape(b, seq, d)
```

```python
import functools

import jax
import jax.numpy as jnp
from jax import lax
from jax.experimental import pallas as pl
from jax.experimental.pallas import tpu as pltpu

F32 = jnp.float32
BF16 = jnp.bfloat16

N_MLSTM_HEADS = 4
MLSTM_HEAD_DIM = 128
N_NA_HEADS = 8
NA_HEAD_DIM = 64
GRID_W = 64
WIN_H = 8
WIN_W = 16
CHUNK = 128
N_GATE = 4 * N_MLSTM_HEADS
RMS_EPS = 1e-6
NEG_BIG = -1e30
NA_BIAS_LANES = (2 * WIN_H - 2) * GRID_W
NA_ROWS_PER_STEP = 8
MLSTM_CHUNKS_PER_STEP = 4
MLSTM_CONV_UNROLL = 4
MLSTM_SCAN_UNROLL = 4

LANES = 128
GATE_PAD = LANES
MIB = 1024 * 1024

NT_DIMS = (((1,), (1,)), ((), ()))


def _const_spec(shape):
    return pl.BlockSpec(shape, lambda *_: (0,) * len(shape), pipeline_mode=pl.Buffered(1))


def _rms_scale(x):
    return lax.rsqrt(jnp.mean(x * x, axis=-1, keepdims=True) + RMS_EPS)


def _split3(x):
    hi = x.astype(BF16)
    r1 = x - hi.astype(F32)
    mid = r1.astype(BF16)
    lo = (r1 - mid.astype(F32)).astype(BF16)
    return hi, mid, lo


def _inproj_body(x_ref, g_ref, wm_ref, wn_ref, wg_ref, gb_ref, bd_ref, qg_ref, kg_ref,
                 qk_ref, v_ref, o_ref, nq_ref, nk_ref, nv_ref, gate_ref, *, d_m, d_n):
    x = x_ref[...]
    u = (x * _rms_scale(x) * g_ref[...]).astype(BF16)

    def proj(w_ref, lo, hi):
        return jnp.dot(u, w_ref[:, lo:hi], preferred_element_type=F32)

    def head_norm(y, gain_ref):
        ss = jnp.dot((y * y).astype(BF16), bd_ref[...], preferred_element_type=F32)
        return y * lax.rsqrt(ss * (1.0 / NA_HEAD_DIM) + RMS_EPS) * gain_ref[...]

    qk_ref[...] = proj(wm_ref, 0, 2 * d_m).astype(BF16)
    v_ref[...] = proj(wm_ref, 2 * d_m, 3 * d_m).astype(BF16)
    o_ref[...] = proj(wm_ref, 3 * d_m, 4 * d_m).astype(BF16)
    nq_ref[...] = head_norm(proj(wn_ref, 0, d_n), qg_ref).astype(BF16)
    nk_ref[...] = head_norm(proj(wn_ref, d_n, 2 * d_n), kg_ref).astype(BF16)
    nv_ref[...] = proj(wn_ref, 2 * d_n, 3 * d_n).astype(BF16)
    gates_t = jnp.dot(u, wg_ref[...], preferred_element_type=F32).T
    gate_ref[...] = gates_t[:N_GATE, :] + gb_ref[...]


def _inproj(x2, g, w_m, w_n, w_g, gate_b, bd, qg, kg, *, d_m, d_n, tm):
    t, d = x2.shape
    row = lambda width: pl.BlockSpec((tm, width), lambda i: (i, 0))
    out_shapes = (
        jax.ShapeDtypeStruct((t, 2 * d_m), BF16),
        jax.ShapeDtypeStruct((t, d_m), BF16),
        jax.ShapeDtypeStruct((t, d_m), BF16),
        jax.ShapeDtypeStruct((t, d_n), BF16),
        jax.ShapeDtypeStruct((t, d_n), BF16),
        jax.ShapeDtypeStruct((t, d_n), BF16),
        jax.ShapeDtypeStruct((N_GATE, t), F32),
    )
    return pl.pallas_call(
        functools.partial(_inproj_body, d_m=d_m, d_n=d_n),
        grid=(t // tm,),
        in_specs=[row(d), _const_spec((1, d)), _const_spec(w_m.shape), _const_spec(w_n.shape),
                  _const_spec(w_g.shape), _const_spec((N_GATE, 1)),
                  _const_spec((d_n, d_n)), _const_spec((1, d_n)), _const_spec((1, d_n))],
        out_specs=(row(2 * d_m), row(d_m), row(d_m), row(d_n), row(d_n), row(d_n),
                   pl.BlockSpec((N_GATE, tm), lambda i: (0, i))),
        out_shape=out_shapes,
        compiler_params=pltpu.CompilerParams(dimension_semantics=("arbitrary",),
                                             vmem_limit_bytes=44 * MIB),
        name="inproj",
    )(x2, g, w_m, w_n, w_g, gate_b, bd, qg, kg)


def _log_sigmoid(x):
    return jnp.minimum(x, 0.0) - jnp.log1p(jnp.exp(-jnp.abs(x)))


def _sigmoid(x):
    return 1.0 / (1.0 + jnp.exp(-x))


def _mlstm_body(q_ref, k_ref, v_ref, o_ref, gate_ref, cw_ref, cb_ref, ng_ref, tri3_ref, eye2_ref,
                out_ref, qs_ref, kt_ref, cs_ref, ccur_ref, brow_ref, crow_ref, cmax_ref, wt_ref, bl_ref,
                ml_ref, ms_ref, s_ref, wint_ref, einv_ref, *, seq):
    L = CHUNK
    d = MLSTM_HEAD_DIM
    nc = seq // L

    row_id = lax.broadcasted_iota(jnp.int32, (L, d), 0)
    col_id = lax.broadcasted_iota(jnp.int32, (L, d), 1)
    pos_id = lax.broadcasted_iota(jnp.int32, (nc, L), 1)
    ones_blk = jnp.ones((L, d), BF16)

    def conv_silu(src_ref, c, s0, w, b):
        x = src_ref[pl.ds(s0, L), :].astype(F32)
        p0 = pl.multiple_of(jnp.maximum(s0 - 16, 0), 16)
        n0 = pl.multiple_of(jnp.minimum(s0 + L, seq - 16), 16)
        prev_row = src_ref[pl.ds(p0, 16), :][15:16, :].astype(F32)
        next_row = src_ref[pl.ds(n0, 16), :][0:1, :].astype(F32)
        prev_row = jnp.where(c > 0, prev_row, 0.0)
        next_row = jnp.where(c < nc - 1, next_row, 0.0)
        x_prev = jnp.where(row_id == 0, prev_row, pltpu.roll(x, 1, 0))
        x_next = jnp.where(row_id == L - 1, next_row, pltpu.roll(x, L - 1, 0))
        y = w[0:1, :] * x_prev + w[1:2, :] * x + w[2:3, :] * x_next + b
        return y * _sigmoid(y)

    def conv_step(c, carry):
        s0 = pl.multiple_of(c * L, L)
        qs_ref[pl.ds(s0, L), :] = conv_silu(q_ref, c, s0, cw_ref[0], cb_ref[0]).astype(BF16)
        kk = conv_silu(k_ref, c, s0, cw_ref[1], cb_ref[1]) * (d ** -0.5)
        kt_ref[:, pl.ds(s0, L)] = kk.T.astype(BF16)
        return carry

    lax.fori_loop(0, nc, conv_step, 0, unroll=MLSTM_CONV_UNROLL)

    for dirn in (0, 1):
        i_g = gate_ref[2 * dirn]
        f_log = _log_sigmoid(gate_ref[2 * dirn + 1])
        f_cat = jnp.concatenate(_split3(f_log), axis=1)
        brow = jnp.dot(f_cat, tri3_ref[1 - dirn], preferred_element_type=F32)
        b_last = brow[:, L - 1:L] if dirn == 0 else brow[:, 0:1]
        a_row = i_g + b_last - brow
        a_max = jnp.max(a_row, axis=1, keepdims=True)
        crow = i_g - brow
        cmax = crow
        for sh in [1 << e for e in range(L.bit_length() - 1)]:
            if dirn == 0:
                cmax = jnp.maximum(cmax, jnp.where(pos_id >= sh, pltpu.roll(cmax, sh, 1), -jnp.inf))
            else:
                cmax = jnp.maximum(cmax, jnp.where(pos_id < L - sh, pltpu.roll(cmax, L - sh, 1), -jnp.inf))
        brow_ref[dirn] = brow
        crow_ref[dirn] = crow
        cmax_ref[dirn] = cmax
        wt_ref[dirn] = jnp.exp(a_row - a_max)
        bl_ref[dirn] = jnp.broadcast_to(b_last, (nc, L))
        ml_ref[dirn] = jnp.broadcast_to(a_max, (nc, L))

    def v_aug(s0):
        return jnp.concatenate([v_ref[pl.ds(s0, L), :], ones_blk], axis=1)

    ccur_ref[...] = jnp.zeros_like(ccur_ref)

    def scan_step(i, carry):
        new = []
        for dirn, c, m in ((0, i, carry[0]), (1, nc - 1 - i, carry[1])):
            s0 = pl.multiple_of(c * L, L)
            kw = (kt_ref[:, pl.ds(s0, L)].astype(F32) * wt_ref[dirn, pl.ds(c, 1), :]).astype(BF16)
            k_loc = jnp.dot(kw, v_aug(s0), preferred_element_type=F32)
            state = ccur_ref[dirn]
            cs_ref[dirn, c] = state.astype(BF16)
            ms_ref[dirn, pl.ds(c, 1), :] = m
            a_prev = m + bl_ref[dirn, pl.ds(c, 1), :]
            a_max = ml_ref[dirn, pl.ds(c, 1), :]
            m_new = jnp.maximum(a_prev, a_max)
            w_prev = jnp.exp(a_prev - m_new)[:, 0:1]
            w_loc = jnp.exp(a_max - m_new)[:, 0:1]
            ccur_ref[dirn] = w_prev * state + w_loc * k_loc
            new.append(m_new)
        return tuple(new)

    m0 = jnp.zeros((1, L), F32)
    lax.fori_loop(0, nc, scan_step, (m0, m0), unroll=MLSTM_SCAN_UNROLL)

    for dirn in (0, 1):
        cmax_ref[dirn] = jnp.maximum(ms_ref[dirn], cmax_ref[dirn])
    lower = col_id <= row_id
    upper = col_id >= row_id

    def weights_stage(c, slot):
        s0 = pl.multiple_of(c * L, L)
        qk = jnp.dot(qs_ref[pl.ds(s0, L), :], kt_ref[:, pl.ds(s0, L)], preferred_element_type=F32)
        for dirn, mask in ((0, lower), (1, upper)):
            rows = []
            for stat_ref in (cmax_ref, brow_ref):
                hi, mid, _ = _split3(stat_ref[dirn, pl.ds(c, 1), :])
                rows.append(jnp.broadcast_to(jnp.concatenate([hi, mid], axis=1), (L, 2 * L)))
            col = lax.dot_general(eye2_ref[...], jnp.concatenate(rows, axis=0), NT_DIMS,
                                  preferred_element_type=F32)
            mu, bcol = col[:, :L], col[:, L:]
            p = jnp.exp(jnp.where(mask, crow_ref[dirn, pl.ds(c, 1), :] - mu, -jnp.inf))
            s_ref[slot, dirn] = (qk * p).astype(BF16)
            wint_ref[slot, dirn] = jnp.exp(ms_ref[dirn, pl.ds(c, 1), :] - mu)
            einv_ref[slot, dirn] = jnp.exp(-(bcol + mu))

    def output_stage(c, slot):
        s0 = pl.multiple_of(c * L, L)
        q = qs_ref[pl.ds(s0, L), :]
        vaug = v_aug(s0)
        hsum = None
        for dirn in (0, 1):
            intra = jnp.dot(s_ref[slot, dirn], vaug, preferred_element_type=F32)
            inter = jnp.dot(q, cs_ref[dirn, c], preferred_element_type=F32)
            w_inter = wint_ref[slot, dirn]
            num = w_inter * inter[:, :d] + intra[:, :d]
            den = w_inter * inter[:, d:] + intra[:, d:]
            h = num / jnp.maximum(jnp.abs(den), einv_ref[slot, dirn])
            hsum = h if hsum is None else hsum + h
        y = hsum * _rms_scale(hsum) * ng_ref[...]
        y = y * _sigmoid(o_ref[pl.ds(s0, L), :].astype(F32))
        out_ref[pl.ds(s0, L), :] = y.astype(out_ref.dtype)

    G = MLSTM_CHUNKS_PER_STEP
    for i in range(G):
        weights_stage(i, i)

    def out_step(j, carry):
        cur = (j % 2) * G
        nxt = G - cur
        for i in range(G):
            output_stage(j * G + i, cur + i)
        for i in range(G):
            weights_stage(jnp.minimum((j + 1) * G + i, nc - 1), nxt + i)
        return carry

    lax.fori_loop(0, nc // G, out_step, 0)


def _mlstm(qk, v, o, gates, conv_w, conv_b, norm_g, tri3, eye2):
    b, seq, _ = v.shape
    H, d, L = N_MLSTM_HEADS, MLSTM_HEAD_DIM, CHUNK
    nc = seq // L
    assert nc % MLSTM_SCAN_UNROLL == 0 and nc % MLSTM_CONV_UNROLL == 0 and L == LANES
    assert nc % MLSTM_CHUNKS_PER_STEP == 0
    col = lambda off: pl.BlockSpec((None, seq, d), lambda bi, hi: (bi, 0, hi + off))
    stat = pltpu.VMEM((2, nc, L), F32)
    slots = 2 * MLSTM_CHUNKS_PER_STEP
    return pl.pallas_call(
        functools.partial(_mlstm_body, seq=seq),
        grid=(b, H),
        in_specs=[
            col(0), col(H), col(0), col(0),
            pl.BlockSpec((4, None, None, nc, L), lambda bi, hi: (0, hi, bi, 0, 0)),
            pl.BlockSpec((None, 2, 3, d), lambda bi, hi: (hi, 0, 0, 0)),
            pl.BlockSpec((None, 2, 1, d), lambda bi, hi: (hi, 0, 0, 0)),
            pl.BlockSpec((None, 1, d), lambda bi, hi: (hi, 0, 0)),
            _const_spec(tri3.shape), _const_spec(eye2.shape),
        ],
        out_specs=pl.BlockSpec((None, seq, d), lambda bi, hi: (bi, 0, hi)),
        out_shape=jax.ShapeDtypeStruct((b, seq, H * d), BF16),
        scratch_shapes=[
            pltpu.VMEM((seq, d), BF16),
            pltpu.VMEM((d, seq), BF16),
            pltpu.VMEM((2, nc, d, 2 * d), BF16),
            pltpu.VMEM((2, d, 2 * d), F32),
            stat, stat, stat,
            stat, stat, stat,
            stat,
            pltpu.VMEM((slots, 2, L, L), BF16),
            pltpu.VMEM((slots, 2, L, d), F32),
            pltpu.VMEM((slots, 2, L, d), F32),
        ],
        compiler_params=pltpu.CompilerParams(dimension_semantics=("arbitrary", "arbitrary"),
                                             vmem_limit_bytes=52 * MIB),
        name="mlstm",
    )(qk, qk, v, o, gates, conv_w, conv_b, norm_g, tri3, eye2)


def _natten_body(q_ref, k_ref, v_ref, bias_ref, out_ref, p_ref, *, rows):
    hd = NA_HEAD_DIM
    win = WIN_H * GRID_W
    lane = lax.broadcasted_iota(jnp.int32, (GRID_W, 2 * hd), 1)
    first = lane < hd
    ones_blk = jnp.ones((win, 2 * hd), BF16)

    def window_start(r):
        return jnp.clip(r - WIN_H // 2, 0, rows - WIN_H)

    def prob_stage(r, slot):
        rs = window_start(r)
        q = q_ref[pl.ds(pl.multiple_of(r * GRID_W, GRID_W), GRID_W), :]
        zero = jnp.zeros_like(q)
        qs = jnp.concatenate([jnp.where(first, q, zero), jnp.where(first, zero, q)], axis=0)
        kwin = k_ref[pl.ds(pl.multiple_of(rs * GRID_W, GRID_W), win), :]
        off = rs - r + (WIN_H - 1)
        bias = bias_ref[off & 1, :, pl.ds(pl.multiple_of((off >> 1) * LANES, LANES), win)]
        s = lax.dot_general(qs, kwin, NT_DIMS, preferred_element_type=F32) + bias
        p_ref[slot] = jnp.exp(s - jnp.max(s, axis=1, keepdims=True)).astype(BF16)

    def output_stage(r, slot):
        rs = window_start(r)
        vwin = v_ref[pl.ds(pl.multiple_of(rs * GRID_W, GRID_W), win), :]
        o = jnp.dot(p_ref[slot], jnp.concatenate([vwin, ones_blk], axis=1), preferred_element_type=F32)
        o = o[:, :2 * hd] / o[:, 2 * hd:]
        out = jnp.where(first, o[:GRID_W], o[GRID_W:])
        out_ref[pl.ds(pl.multiple_of(r * GRID_W, GRID_W), GRID_W), :] = out.astype(out_ref.dtype)

    n_it = rows // NA_ROWS_PER_STEP
    for i in range(NA_ROWS_PER_STEP):
        prob_stage(i, i)

    def row_group(j, carry):
        cur = (j % 2) * NA_ROWS_PER_STEP
        nxt = NA_ROWS_PER_STEP - cur
        for i in range(NA_ROWS_PER_STEP):
            output_stage(j * NA_ROWS_PER_STEP + i, cur + i)
        for i in range(NA_ROWS_PER_STEP):
            prob_stage(jnp.minimum((j + 1) * NA_ROWS_PER_STEP + i, rows - 1), nxt + i)
        return carry

    lax.fori_loop(0, n_it, row_group, 0)


def _natten(nq, nk, nv, bias_tab):
    b, seq, d_n = nq.shape
    pairs = N_NA_HEADS // 2
    width = 2 * NA_HEAD_DIM
    rows = seq // GRID_W
    assert rows % NA_ROWS_PER_STEP == 0
    col = pl.BlockSpec((None, seq, width), lambda bi, pi: (bi, 0, pi))
    return pl.pallas_call(
        functools.partial(_natten_body, rows=rows),
        grid=(b, pairs),
        in_specs=[col, col, col,
                  pl.BlockSpec((None, 2, 2 * GRID_W, NA_BIAS_LANES), lambda bi, pi: (pi, 0, 0, 0))],
        out_specs=col,
        out_shape=jax.ShapeDtypeStruct((b, seq, d_n), BF16),
        scratch_shapes=[pltpu.VMEM((2 * NA_ROWS_PER_STEP, 2 * GRID_W, WIN_H * GRID_W), BF16)],
        compiler_params=pltpu.CompilerParams(dimension_semantics=("arbitrary", "arbitrary"),
                                             vmem_limit_bytes=40 * MIB),
        name="natten",
    )(nq, nk, nv, bias_tab)


def _natten_bias_table(rpb):
    c = jnp.arange(GRID_W)
    cs = jnp.clip(c - WIN_W // 2, 0, GRID_W - WIN_W)
    cp = jnp.arange(GRID_W)
    valid = (cp[None, :] >= cs[:, None]) & (cp[None, :] < cs[:, None] + WIN_W)
    coff = jnp.clip(cp[None, :] - c[:, None] + (WIN_W - 1), 0, 2 * WIN_W - 2)
    dense = jnp.where(valid[None, None], rpb.astype(F32)[:, :, coff], NEG_BIG)
    n_rel = 2 * WIN_H - 1
    tab = dense.reshape(N_NA_HEADS // 2, 2, n_rel, GRID_W, GRID_W).transpose(0, 1, 3, 2, 4)
    tab = tab.reshape(N_NA_HEADS // 2, 2 * GRID_W, n_rel * GRID_W)
    even = tab[:, :, :NA_BIAS_LANES]
    odd = tab[:, :, GRID_W:GRID_W + NA_BIAS_LANES]
    return jnp.stack([even, odd], axis=1)


def _tail_body(x_ref, ya_ref, yb_ref, p_ref, woa_ref, wob_ref, g2_ref, w1_ref, w2_ref,
               g3_ref, wg_ref, wu_ref, out_ref, *, ff_chunk):
    d_ff = w1_ref.shape[1]
    h = (x_ref[...]
         + jnp.dot(ya_ref[...], woa_ref[...], preferred_element_type=F32)
         + jnp.dot(yb_ref[...], wob_ref[...], preferred_element_type=F32))
    u = (h * _rms_scale(h) * g2_ref[...]).astype(BF16)
    out_ref[...] = h
    for j in range(d_ff // ff_chunk):
        z = jnp.dot(u, w1_ref[:, j * ff_chunk:(j + 1) * ff_chunk], preferred_element_type=F32)
        z = jnp.maximum(z, 0.0)
        out_ref[...] += jnp.dot((z * z).astype(BF16), w2_ref[j * ff_chunk:(j + 1) * ff_chunk, :],
                                preferred_element_type=F32)
    h = out_ref[...]
    u = (h * _rms_scale(h) * g3_ref[...]).astype(BF16)
    gate = _sigmoid(jnp.dot(u, wg_ref[...], preferred_element_type=F32))
    up = jnp.dot(p_ref[...].astype(BF16), wu_ref[...], preferred_element_type=F32)
    out_ref[...] = h + gate * up


def _tail(x2, ya, yb, p2, woa, wob, g2, w1, w2, g3, wg, wu, *, tm, ff_chunk):
    t, d = x2.shape
    row = lambda width: pl.BlockSpec((tm, width), lambda i: (i, 0))
    consts = [woa, wob, g2, w1, w2, g3, wg, wu]
    return pl.pallas_call(
        functools.partial(_tail_body, ff_chunk=ff_chunk),
        grid=(t // tm,),
        in_specs=[row(d), row(ya.shape[1]), row(yb.shape[1]), row(p2.shape[1])]
                 + [_const_spec(c.shape) for c in consts],
        out_specs=row(d),
        out_shape=jax.ShapeDtypeStruct((t, d), F32),
        compiler_params=pltpu.CompilerParams(dimension_semantics=("arbitrary",),
                                             vmem_limit_bytes=52 * MIB),
        name="tail",
    )(x2, ya, yb, p2, *consts)


def kernel(x, p, norm1_g, w_in, conv_w, conv_b, gate_b, mlstm_norm_g, q_norm_g, k_norm_g, rpb,
           w_out, norm2_g, w_ff1, w_ff2, ple_norm_g, w_ple_gate, w_ple_up):
    b, seq, d = x.shape
    depth = w_in.shape[0]
    H, hd, L = N_MLSTM_HEADS, MLSTM_HEAD_DIM, CHUNK
    d_m = H * hd
    d_n = N_NA_HEADS * NA_HEAD_DIM
    t = b * seq
    rows = seq // GRID_W
    assert rows >= WIN_H

    bd = jnp.kron(jnp.eye(N_NA_HEADS, dtype=F32), jnp.ones((NA_HEAD_DIM, NA_HEAD_DIM), F32)).astype(BF16)
    ri = lax.broadcasted_iota(jnp.int32, (L, L), 0)
    ci = lax.broadcasted_iota(jnp.int32, (L, L), 1)
    tri = jnp.stack([(ci <= ri), (ci >= ri)]).astype(BF16)
    tri3 = jnp.concatenate([tri, tri, tri], axis=1)
    eye = (ci == ri).astype(BF16)
    eye2 = jnp.concatenate([eye, eye], axis=1)

    h = x.reshape(t, d)
    for i in range(depth):
        wi = w_in[i]
        w_m = wi[:, :4 * d_m].astype(BF16)
        w_n = wi[:, 4 * d_m + N_GATE:].astype(BF16)
        w_g = jnp.pad(wi[:, 4 * d_m:4 * d_m + N_GATE], ((0, 0), (0, GATE_PAD - N_GATE))).astype(BF16)
        qg = (q_norm_g[i].reshape(1, d_n) * (NA_HEAD_DIM ** -0.5)).astype(F32)
        kg = k_norm_g[i].reshape(1, d_n).astype(F32)
        qk, mv, mo, nq, nk, nv, gates = _inproj(
            h, norm1_g[i].reshape(1, d), w_m, w_n, w_g, gate_b[i].reshape(N_GATE, 1), bd, qg, kg,
            d_m=d_m, d_n=d_n, tm=512)

        cw = conv_w[i].reshape(3, 2, H, hd).transpose(2, 1, 0, 3)
        cb = conv_b[i].reshape(2, H, 1, hd).transpose(1, 0, 2, 3)
        y_a = _mlstm(qk.reshape(b, seq, 2 * d_m), mv.reshape(b, seq, d_m), mo.reshape(b, seq, d_m),
                     gates.reshape(4, H, b, seq // L, L), cw, cb,
                     mlstm_norm_g[i].reshape(H, 1, hd), tri3, eye2)

        bias_tab = _natten_bias_table(rpb[i])
        y_b = _natten(nq.reshape(b, seq, d_n), nk.reshape(b, seq, d_n), nv.reshape(b, seq, d_n), bias_tab)

        wo = w_out[i].astype(BF16)
        h = _tail(h, y_a.reshape(t, d_m), y_b.reshape(t, d_n), p[i].reshape(t, -1),
                  wo[:d_m], wo[d_m:], norm2_g[i].reshape(1, d), w_ff1[i].astype(BF16),
                  w_ff2[i].astype(BF16), ple_norm_g[i].reshape(1, d), w_ple_gate[i].astype(BF16),
                  w_ple_up[i].astype(BF16), tm=512, ff_chunk=1024)
    return h.reshape(b, seq, d)
```

```python
import functools

import jax
import jax.numpy as jnp
from jax import lax
from jax.experimental import pallas as pl
from jax.experimental.pallas import tpu as pltpu

F32 = jnp.float32
BF16 = jnp.bfloat16

N_MLSTM_HEADS = 4
MLSTM_HEAD_DIM = 128
N_NA_HEADS = 8
NA_HEAD_DIM = 64
GRID_W = 64
WIN_H = 8
WIN_W = 16
CHUNK = 128
N_GATE = 4 * N_MLSTM_HEADS
RMS_EPS = 1e-6
NEG_BIG = -1e30
NA_BIAS_LANES = (2 * WIN_H - 2) * GRID_W
NA_ROWS_PER_STEP = 8
MLSTM_CHUNKS_PER_STEP = 4
MLSTM_CONV_UNROLL = 4
MLSTM_SCAN_UNROLL = 4

LANES = 128
GATE_PAD = LANES
MIB = 1024 * 1024

NT_DIMS = (((1,), (1,)), ((), ()))


def _const_spec(shape):
    return pl.BlockSpec(shape, lambda *_: (0,) * len(shape), pipeline_mode=pl.Buffered(1))


def _rms_scale(x):
    return lax.rsqrt(jnp.mean(x * x, axis=-1, keepdims=True) + RMS_EPS)


def _split3(x):
    hi = x.astype(BF16)
    r1 = x - hi.astype(F32)
    mid = r1.astype(BF16)
    lo = (r1 - mid.astype(F32)).astype(BF16)
    return hi, mid, lo


def _inproj_body(x_ref, g_ref, wm_ref, wn_ref, wg_ref, gb_ref, bd_ref, qg_ref, kg_ref,
                 qk_ref, v_ref, o_ref, nq_ref, nk_ref, nv_ref, gate_ref, *, d_m, d_n):
    x = x_ref[...]
    u = (x * _rms_scale(x) * g_ref[...]).astype(BF16)

    def proj(w_ref, lo, hi):
        return jnp.dot(u, w_ref[:, lo:hi], preferred_element_type=F32)

    def head_norm(y, gain_ref):
        ss = jnp.dot((y * y).astype(BF16), bd_ref[...], preferred_element_type=F32)
        return y * lax.rsqrt(ss * (1.0 / NA_HEAD_DIM) + RMS_EPS) * gain_ref[...]

    qk_ref[...] = proj(wm_ref, 0, 2 * d_m).astype(BF16)
    v_ref[...] = proj(wm_ref, 2 * d_m, 3 * d_m).astype(BF16)
    o_ref[...] = proj(wm_ref, 3 * d_m, 4 * d_m).astype(BF16)
    nq_ref[...] = head_norm(proj(wn_ref, 0, d_n), qg_ref).astype(BF16)
    nk_ref[...] = head_norm(proj(wn_ref, d_n, 2 * d_n), kg_ref).astype(BF16)
    nv_ref[...] = proj(wn_ref, 2 * d_n, 3 * d_n).astype(BF16)
    gates_t = jnp.dot(u, wg_ref[...], preferred_element_type=F32).T
    gate_ref[...] = gates_t[:N_GATE, :] + gb_ref[...]


def _inproj(x2, g, w_m, w_n, w_g, gate_b, bd, qg, kg, *, d_m, d_n, tm):
    t, d = x2.shape
    row = lambda width: pl.BlockSpec((tm, width), lambda i: (i, 0))
    out_shapes = (
        jax.ShapeDtypeStruct((t, 2 * d_m), BF16),
        jax.ShapeDtypeStruct((t, d_m), BF16),
        jax.ShapeDtypeStruct((t, d_m), BF16),
        jax.ShapeDtypeStruct((t, d_n), BF16),
        jax.ShapeDtypeStruct((t, d_n), BF16),
        jax.ShapeDtypeStruct((t, d_n), BF16),
        jax.ShapeDtypeStruct((N_GATE, t), F32),
    )
    return pl.pallas_call(
        functools.partial(_inproj_body, d_m=d_m, d_n=d_n),
        grid=(t // tm,),
        in_specs=[row(d), _const_spec((1, d)), _const_spec(w_m.shape), _const_spec(w_n.shape),
                  _const_spec(w_g.shape), _const_spec((N_GATE, 1)),
                  _const_spec((d_n, d_n)), _const_spec((1, d_n)), _const_spec((1, d_n))],
        out_specs=(row(2 * d_m), row(d_m), row(d_m), row(d_n), row(d_n), row(d_n),
                   pl.BlockSpec((N_GATE, tm), lambda i: (0, i))),
        out_shape=out_shapes,
        compiler_params=pltpu.CompilerParams(dimension_semantics=("arbitrary",),
                                             vmem_limit_bytes=44 * MIB),
        name="inproj",
    )(x2, g, w_m, w_n, w_g, gate_b, bd, qg, kg)


def _log_sigmoid(x):
    return jnp.minimum(x, 0.0) - jnp.log1p(jnp.exp(-jnp.abs(x)))


def _sigmoid(x):
    return 1.0 / (1.0 + jnp.exp(-x))


def _mlstm_body(q_ref, k_ref, v_ref, o_ref, gate_ref, cw_ref, cb_ref, ng_ref, tri3_ref, eye2_ref,
                out_ref, qs_ref, kt_ref, cs_ref, ccur_ref, brow_ref, crow_ref, cmax_ref, wt_ref, bl_ref,
                ml_ref, ms_ref, s_ref, wint_ref, einv_ref, *, seq):
    L = CHUNK
    d = MLSTM_HEAD_DIM
    nc = seq // L

    row_id = lax.broadcasted_iota(jnp.int32, (L, d), 0)
    col_id = lax.broadcasted_iota(jnp.int32, (L, d), 1)
    pos_id = lax.broadcasted_iota(jnp.int32, (nc, L), 1)
    ones_blk = jnp.ones((L, d), BF16)

    def conv_silu(src_ref, c, s0, w, b):
        x = src_ref[pl.ds(s0, L), :].astype(F32)
        p0 = pl.multiple_of(jnp.maximum(s0 - 16, 0), 16)
        n0 = pl.multiple_of(jnp.minimum(s0 + L, seq - 16), 16)
        prev_row = src_ref[pl.ds(p0, 16), :][15:16, :].astype(F32)
        next_row = src_ref[pl.ds(n0, 16), :][0:1, :].astype(F32)
        prev_row = jnp.where(c > 0, prev_row, 0.0)
        next_row = jnp.where(c < nc - 1, next_row, 0.0)
        x_prev = jnp.where(row_id == 0, prev_row, pltpu.roll(x, 1, 0))
        x_next = jnp.where(row_id == L - 1, next_row, pltpu.roll(x, L - 1, 0))
        y = w[0:1, :] * x_prev + w[1:2, :] * x + w[2:3, :] * x_next + b
        return y * _sigmoid(y)

    def conv_step(c, carry):
        s0 = pl.multiple_of(c * L, L)
        qs_ref[pl.ds(s0, L), :] = conv_silu(q_ref, c, s0, cw_ref[0], cb_ref[0]).astype(BF16)
        kk = conv_silu(k_ref, c, s0, cw_ref[1], cb_ref[1]) * (d ** -0.5)
        kt_ref[:, pl.ds(s0, L)] = kk.T.astype(BF16)
        return carry

    lax.fori_loop(0, nc, conv_step, 0, unroll=MLSTM_CONV_UNROLL)

    for dirn in (0, 1):
        i_g = gate_ref[2 * dirn]
        f_log = _log_sigmoid(gate_ref[2 * dirn + 1])
        f_cat = jnp.concatenate(_split3(f_log), axis=1)
        brow = jnp.dot(f_cat, tri3_ref[1 - dirn], preferred_element_type=F32)
        b_last = brow[:, L - 1:L] if dirn == 0 else brow[:, 0:1]
        a_row = i_g + b_last - brow
        a_max = jnp.max(a_row, axis=1, keepdims=True)
        crow = i_g - brow
        cmax = crow
        for sh in [1 << e for e in range(L.bit_length() - 1)]:
            if dirn == 0:
                cmax = jnp.maximum(cmax, jnp.where(pos_id >= sh, pltpu.roll(cmax, sh, 1), -jnp.inf))
            else:
                cmax = jnp.maximum(cmax, jnp.where(pos_id < L - sh, pltpu.roll(cmax, L - sh, 1), -jnp.inf))
        brow_ref[dirn] = brow
        crow_ref[dirn] = crow
        cmax_ref[dirn] = cmax
        wt_ref[dirn] = jnp.exp(a_row - a_max)
        bl_ref[dirn] = jnp.broadcast_to(b_last, (nc, L))
        ml_ref[dirn] = jnp.broadcast_to(a_max, (nc, L))

    def v_aug(s0):
        return jnp.concatenate([v_ref[pl.ds(s0, L), :], ones_blk], axis=1)

    ccur_ref[...] = jnp.zeros_like(ccur_ref)

    def scan_step(i, carry):
        new = []
        for dirn, c, m in ((0, i, carry[0]), (1, nc - 1 - i, carry[1])):
            s0 = pl.multiple_of(c * L, L)
            kw = (kt_ref[:, pl.ds(s0, L)].astype(F32) * wt_ref[dirn, pl.ds(c, 1), :]).astype(BF16)
            k_loc = jnp.dot(kw, v_aug(s0), preferred_element_type=F32)
            state = ccur_ref[dirn]
            cs_ref[dirn, c] = state.astype(BF16)
            ms_ref[dirn, pl.ds(c, 1), :] = m
            a_prev = m + bl_ref[dirn, pl.ds(c, 1), :]
            a_max = ml_ref[dirn, pl.ds(c, 1), :]
            m_new = jnp.maximum(a_prev, a_max)
            w_prev = jnp.exp(a_prev - m_new)[:, 0:1]
            w_loc = jnp.exp(a_max - m_new)[:, 0:1]
            ccur_ref[dirn] = w_prev * state + w_loc * k_loc
            new.append(m_new)
        return tuple(new)

    m0 = jnp.zeros((1, L), F32)
    lax.fori_loop(0, nc, scan_step, (m0, m0), unroll=MLSTM_SCAN_UNROLL)

    for dirn in (0, 1):
        cmax_ref[dirn] = jnp.maximum(ms_ref[dirn], cmax_ref[dirn])
    lower = col_id <= row_id
    upper = col_id >= row_id

    def weights_stage(c, slot):
        s0 = pl.multiple_of(c * L, L)
        qk = jnp.dot(qs_ref[pl.ds(s0, L), :], kt_ref[:, pl.ds(s0, L)], preferred_element_type=F32)
        for dirn, mask in ((0, lower), (1, upper)):
            rows = []
            for stat_ref in (cmax_ref, brow_ref):
                hi, mid, _ = _split3(stat_ref[dirn, pl.ds(c, 1), :])
                rows.append(jnp.broadcast_to(jnp.concatenate([hi, mid], axis=1), (L, 2 * L)))
            col = lax.dot_general(eye2_ref[...], jnp.concatenate(rows, axis=0), NT_DIMS,
                                  preferred_element_type=F32)
            mu, bcol = col[:, :L], col[:, L:]
            p = jnp.exp(jnp.where(mask, crow_ref[dirn, pl.ds(c, 1), :] - mu, -jnp.inf))
            s_ref[slot, dirn] = (qk * p).astype(BF16)
            wint_ref[slot, dirn] = jnp.exp(ms_ref[dirn, pl.ds(c, 1), :] - mu)
            einv_ref[slot, dirn] = jnp.exp(-(bcol + mu))

    def output_stage(c, slot):
        s0 = pl.multiple_of(c * L, L)
        q = qs_ref[pl.ds(s0, L), :]
        vaug = v_aug(s0)
        hsum = None
        for dirn in (0, 1):
            intra = jnp.dot(s_ref[slot, dirn], vaug, preferred_element_type=F32)
            inter = jnp.dot(q, cs_ref[dirn, c], preferred_element_type=F32)
            w_inter = wint_ref[slot, dirn]
            num = w_inter * inter[:, :d] + intra[:, :d]
            den = w_inter * inter[:, d:] + intra[:, d:]
            h = num / jnp.maximum(jnp.abs(den), einv_ref[slot, dirn])
            hsum = h if hsum is None else hsum + h
        y = hsum * _rms_scale(hsum) * ng_ref[...]
        y = y * _sigmoid(o_ref[pl.ds(s0, L), :].astype(F32))
        out_ref[pl.ds(s0, L), :] = y.astype(out_ref.dtype)

    G = MLSTM_CHUNKS_PER_STEP
    for i in range(G):
        weights_stage(i, i)

    def out_step(j, carry):
        cur = (j % 2) * G
        nxt = G - cur
        for i in range(G):
            output_stage(j * G + i, cur + i)
        for i in range(G):
            weights_stage(jnp.minimum((j + 1) * G + i, nc - 1), nxt + i)
        return carry

    lax.fori_loop(0, nc // G, out_step, 0)


def _mlstm(qk, v, o, gates, conv_w, conv_b, norm_g, tri3, eye2):
    b, seq, _ = v.shape
    H, d, L = N_MLSTM_HEADS, MLSTM_HEAD_DIM, CHUNK
    nc = seq // L
    assert nc % MLSTM_SCAN_UNROLL == 0 and nc % MLSTM_CONV_UNROLL == 0 and L == LANES
    assert nc % MLSTM_CHUNKS_PER_STEP == 0
    col = lambda off: pl.BlockSpec((None, seq, d), lambda bi, hi: (bi, 0, hi + off))
    stat = pltpu.VMEM((2, nc, L), F32)
    slots = 2 * MLSTM_CHUNKS_PER_STEP
    return pl.pallas_call(
        functools.partial(_mlstm_body, seq=seq),
        grid=(b, H),
        in_specs=[
            col(0), col(H), col(0), col(0),
            pl.BlockSpec((4, None, None, nc, L), lambda bi, hi: (0, hi, bi, 0, 0)),
            pl.BlockSpec((None, 2, 3, d), lambda bi, hi: (hi, 0, 0, 0)),
            pl.BlockSpec((None, 2, 1, d), lambda bi, hi: (hi, 0, 0, 0)),
            pl.BlockSpec((None, 1, d), lambda bi, hi: (hi, 0, 0)),
            _const_spec(tri3.shape), _const_spec(eye2.shape),
        ],
        out_specs=pl.BlockSpec((None, seq, d), lambda bi, hi: (bi, 0, hi)),
        out_shape=jax.ShapeDtypeStruct((b, seq, H * d), BF16),
        scratch_shapes=[
            pltpu.VMEM((seq, d), BF16),
            pltpu.VMEM((d, seq), BF16),
            pltpu.VMEM((2, nc, d, 2 * d), BF16),
            pltpu.VMEM((2, d, 2 * d), F32),
            stat, stat, stat,
            stat, stat, stat,
            stat,
            pltpu.VMEM((slots, 2, L, L), BF16),
            pltpu.VMEM((slots, 2, L, d), F32),
            pltpu.VMEM((slots, 2, L, d), F32),
        ],
        compiler_params=pltpu.CompilerParams(dimension_semantics=("arbitrary", "arbitrary"),
                                             vmem_limit_bytes=52 * MIB),
        name="mlstm",
    )(qk, qk, v, o, gates, conv_w, conv_b, norm_g, tri3, eye2)


def _natten_body(q_ref, k_ref, v_ref, bias_ref, out_ref, p_ref, *, rows):
    hd = NA_HEAD_DIM
    win = WIN_H * GRID_W
    lane = lax.broadcasted_iota(jnp.int32, (GRID_W, 2 * hd), 1)
    first = lane < hd
    ones_blk = jnp.ones((win, 2 * hd), BF16)

    def window_start(r):
        return jnp.clip(r - WIN_H // 2, 0, rows - WIN_H)

    def prob_stage(r, slot):
        rs = window_start(r)
        q = q_ref[pl.ds(pl.multiple_of(r * GRID_W, GRID_W), GRID_W), :]
        zero = jnp.zeros_like(q)
        qs = jnp.concatenate([jnp.where(first, q, zero), jnp.where(first, zero, q)], axis=0)
        kwin = k_ref[pl.ds(pl.multiple_of(rs * GRID_W, GRID_W), win), :]
        off = rs - r + (WIN_H - 1)
        bias = bias_ref[off & 1, :, pl.ds(pl.multiple_of((off >> 1) * LANES, LANES), win)]
        s = lax.dot_general(qs, kwin, NT_DIMS, preferred_element_type=F32) + bias
        p_ref[slot] = jnp.exp(s - jnp.max(s, axis=1, keepdims=True)).astype(BF16)

    def output_stage(r, slot):
        rs = window_start(r)
        vwin = v_ref[pl.ds(pl.multiple_of(rs * GRID_W, GRID_W), win), :]
        o = jnp.dot(p_ref[slot], jnp.concatenate([vwin, ones_blk], axis=1), preferred_element_type=F32)
        o = o[:, :2 * hd] / o[:, 2 * hd:]
        out = jnp.where(first, o[:GRID_W], o[GRID_W:])
        out_ref[pl.ds(pl.multiple_of(r * GRID_W, GRID_W), GRID_W), :] = out.astype(out_ref.dtype)

    n_it = rows // NA_ROWS_PER_STEP
    for i in range(NA_ROWS_PER_STEP):
        prob_stage(i, i)

    def row_group(j, carry):
        cur = (j % 2) * NA_ROWS_PER_STEP
        nxt = NA_ROWS_PER_STEP - cur
        for i in range(NA_ROWS_PER_STEP):
            output_stage(j * NA_ROWS_PER_STEP + i, cur + i)
        for i in range(NA_ROWS_PER_STEP):
            prob_stage(jnp.minimum((j + 1) * NA_ROWS_PER_STEP + i, rows - 1), nxt + i)
        return carry

    lax.fori_loop(0, n_it, row_group, 0)


def _natten(nq, nk, nv, bias_tab):
    b, seq, d_n = nq.shape
    pairs = N_NA_HEADS // 2
    width = 2 * NA_HEAD_DIM
    rows = seq // GRID_W
    assert rows % NA_ROWS_PER_STEP == 0
    col = pl.BlockSpec((None, seq, width), lambda bi, pi: (bi, 0, pi))
    return pl.pallas_call(
        functools.partial(_natten_body, rows=rows),
        grid=(b, pairs),
        in_specs=[col, col, col,
                  pl.BlockSpec((None, 2, 2 * GRID_W, NA_BIAS_LANES), lambda bi, pi: (pi, 0, 0, 0))],
        out_specs=col,
        out_shape=jax.ShapeDtypeStruct((b, seq, d_n), BF16),
        scratch_shapes=[pltpu.VMEM((2 * NA_ROWS_PER_STEP, 2 * GRID_W, WIN_H * GRID_W), BF16)],
        compiler_params=pltpu.CompilerParams(dimension_semantics=("arbitrary", "arbitrary"),
                                             vmem_limit_bytes=40 * MIB),
        name="natten",
    )(nq, nk, nv, bias_tab)


def _natten_bias_table(rpb):
    c = jnp.arange(GRID_W)
    cs = jnp.clip(c - WIN_W // 2, 0, GRID_W - WIN_W)
    cp = jnp.arange(GRID_W)
    valid = (cp[None, :] >= cs[:, None]) & (cp[None, :] < cs[:, None] + WIN_W)
    rel = cp[None, None, :] - c[None, :, None] + (WIN_W - 1)
    onehot = (rel == jnp.arange(2 * WIN_W - 1)[:, None, None]).astype(F32)
    dense = jnp.einsum('hrd,dcp->hrcp', rpb.astype(F32), onehot, precision=lax.Precision.HIGHEST)
    dense = jnp.where(valid[None, None], dense, NEG_BIG)
    n_rel = 2 * WIN_H - 1
    tab = dense.reshape(N_NA_HEADS // 2, 2, n_rel, GRID_W, GRID_W).transpose(0, 1, 3, 2, 4)
    tab = tab.reshape(N_NA_HEADS // 2, 2 * GRID_W, n_rel * GRID_W)
    even = tab[:, :, :NA_BIAS_LANES]
    odd = tab[:, :, GRID_W:GRID_W + NA_BIAS_LANES]
    return jnp.stack([even, odd], axis=1)


def _tail_body(x_ref, ya_ref, yb_ref, p_ref, woa_ref, wob_ref, g2_ref, w1_ref, w2_ref,
               g3_ref, wg_ref, wu_ref, out_ref, *, ff_chunk):
    d_ff = w1_ref.shape[1]
    h = (x_ref[...]
         + jnp.dot(ya_ref[...], woa_ref[...], preferred_element_type=F32)
         + jnp.dot(yb_ref[...], wob_ref[...], preferred_element_type=F32))
    u = (h * _rms_scale(h) * g2_ref[...]).astype(BF16)
    out_ref[...] = h
    for j in range(d_ff // ff_chunk):
        z = jnp.dot(u, w1_ref[:, j * ff_chunk:(j + 1) * ff_chunk], preferred_element_type=F32)
        z = jnp.maximum(z, 0.0)
        out_ref[...] += jnp.dot((z * z).astype(BF16), w2_ref[j * ff_chunk:(j + 1) * ff_chunk, :],
                                preferred_element_type=F32)
    h = out_ref[...]
    u = (h * _rms_scale(h) * g3_ref[...]).astype(BF16)
    gate = _sigmoid(jnp.dot(u, wg_ref[...], preferred_element_type=F32))
    up = jnp.dot(p_ref[...].astype(BF16), wu_ref[...], preferred_element_type=F32)
    out_ref[...] = h + gate * up


def _tail(x2, ya, yb, p2, woa, wob, g2, w1, w2, g3, wg, wu, *, tm, ff_chunk):
    t, d = x2.shape
    row = lambda width: pl.BlockSpec((tm, width), lambda i: (i, 0))
    consts = [woa, wob, g2, w1, w2, g3, wg, wu]
    return pl.pallas_call(
        functools.partial(_tail_body, ff_chunk=ff_chunk),
        grid=(t // tm,),
        in_specs=[row(d), row(ya.shape[1]), row(yb.shape[1]), row(p2.shape[1])]
                 + [_const_spec(c.shape) for c in consts],
        out_specs=row(d),
        out_shape=jax.ShapeDtypeStruct((t, d), F32),
        compiler_params=pltpu.CompilerParams(dimension_semantics=("arbitrary",),
                                             vmem_limit_bytes=52 * MIB),
        name="tail",
    )(x2, ya, yb, p2, *consts)


def kernel(x, p, norm1_g, w_in, conv_w, conv_b, gate_b, mlstm_norm_g, q_norm_g, k_norm_g, rpb,
           w_out, norm2_g, w_ff1, w_ff2, ple_norm_g, w_ple_gate, w_ple_up):
    b, seq, d = x.shape
    depth = w_in.shape[0]
    H, hd, L = N_MLSTM_HEADS, MLSTM_HEAD_DIM, CHUNK
    d_m = H * hd
    d_n = N_NA_HEADS * NA_HEAD_DIM
    t = b * seq
    rows = seq // GRID_W
    assert rows >= WIN_H

    bd = jnp.kron(jnp.eye(N_NA_HEADS, dtype=F32), jnp.ones((NA_HEAD_DIM, NA_HEAD_DIM), F32)).astype(BF16)
    ri = lax.broadcasted_iota(jnp.int32, (L, L), 0)
    ci = lax.broadcasted_iota(jnp.int32, (L, L), 1)
    tri = jnp.stack([(ci <= ri), (ci >= ri)]).astype(BF16)
    tri3 = jnp.concatenate([tri, tri, tri], axis=1)
    eye = (ci == ri).astype(BF16)
    eye2 = jnp.concatenate([eye, eye], axis=1)

    h = x.reshape(t, d)
    for i in range(depth):
        wi = w_in[i]
        w_m = wi[:, :4 * d_m].astype(BF16)
        w_n = wi[:, 4 * d_m + N_GATE:].astype(BF16)
        w_g = jnp.pad(wi[:, 4 * d_m:4 * d_m + N_GATE], ((0, 0), (0, GATE_PAD - N_GATE))).astype(BF16)
        qg = (q_norm_g[i].reshape(1, d_n) * (NA_HEAD_DIM ** -0.5)).astype(F32)
        kg = k_norm_g[i].reshape(1, d_n).astype(F32)
        qk, mv, mo, nq, nk, nv, gates = _inproj(
            h, norm1_g[i].reshape(1, d), w_m, w_n, w_g, gate_b[i].reshape(N_GATE, 1), bd, qg, kg,
            d_m=d_m, d_n=d_n, tm=512)

        cw = conv_w[i].reshape(3, 2, H, hd).transpose(2, 1, 0, 3)
        cb = conv_b[i].reshape(2, H, 1, hd).transpose(1, 0, 2, 3)
        y_a = _mlstm(qk.reshape(b, seq, 2 * d_m), mv.reshape(b, seq, d_m), mo.reshape(b, seq, d_m),
                     gates.reshape(4, H, b, seq // L, L), cw, cb,
                     mlstm_norm_g[i].reshape(H, 1, hd), tri3, eye2)

        bias_tab = _natten_bias_table(rpb[i])
        y_b = _natten(nq.reshape(b, seq, d_n), nk.reshape(b, seq, d_n), nv.reshape(b, seq, d_n), bias_tab)

        wo = w_out[i].astype(BF16)
        h = _tail(h, y_a.reshape(t, d_m), y_b.reshape(t, d_n), p[i].reshape(t, -1),
                  wo[:d_m], wo[d_m:], norm2_g[i].reshape(1, d), w_ff1[i].astype(BF16),
                  w_ff2[i].astype(BF16), ple_norm_g[i].reshape(1, d), w_ple_gate[i].astype(BF16),
                  w_ple_up[i].astype(BF16), tm=512, ff_chunk=1024)
    return h.reshape(b, seq, d)
```

```python
import functools

import jax
import jax.numpy as jnp
import numpy as np
from jax import lax
from jax.experimental import pallas as pl
from jax.experimental.pallas import tpu as pltpu

F32 = jnp.float32
BF16 = jnp.bfloat16

N_MLSTM_HEADS = 4
MLSTM_HEAD_DIM = 128
N_NA_HEADS = 8
NA_HEAD_DIM = 64
GRID_W = 64
WIN_H = 8
WIN_W = 16
CHUNK = 128
N_GATE = 4 * N_MLSTM_HEADS
RMS_EPS = 1e-6
NEG_BIG = -1e30
F32_BIG = 3e38
NA_BIAS_LANES = (2 * WIN_H - 2) * GRID_W
NA_ROWS_PER_STEP = 8
MLSTM_CHUNKS_PER_STEP = 8
MLSTM_CONV_UNROLL = 4
MLSTM_SCAN_UNROLL = 8

LANES = 128
GATE_PAD = LANES
MIB = 1024 * 1024

NT_DIMS = (((1,), (1,)), ((), ()))


def _const_spec(shape):
    return pl.BlockSpec(shape, lambda *_: (0,) * len(shape), pipeline_mode=pl.Buffered(1))


def _rms_scale(x):
    return lax.rsqrt(jnp.mean(x * x, axis=-1, keepdims=True) + RMS_EPS)


def _split3(x):
    hi = x.astype(BF16)
    r1 = x - hi.astype(F32)
    mid = r1.astype(BF16)
    lo = (r1 - mid.astype(F32)).astype(BF16)
    return hi, mid, lo


def _inproj_body(x_ref, g_ref, wm_ref, wn_ref, wg_ref, gb_ref, bd_ref, qg_ref, kg_ref,
                 qk_ref, v_ref, o_ref, nq_ref, nk_ref, nv_ref, gate_ref, *, d_m, d_n):
    x = x_ref[...]
    u = (x * _rms_scale(x) * g_ref[...]).astype(BF16)

    def proj(w_ref, lo, hi):
        return jnp.dot(u, w_ref[:, lo:hi], preferred_element_type=F32)

    def head_norm(y, gain_ref):
        ss = jnp.dot((y * y).astype(BF16), bd_ref[...], preferred_element_type=F32)
        return y * lax.rsqrt(ss * (1.0 / NA_HEAD_DIM) + RMS_EPS) * gain_ref[...]

    qk_ref[...] = proj(wm_ref, 0, 2 * d_m).astype(BF16)
    v_ref[...] = proj(wm_ref, 2 * d_m, 3 * d_m).astype(BF16)
    o_ref[...] = proj(wm_ref, 3 * d_m, 4 * d_m).astype(BF16)
    nq_ref[...] = head_norm(proj(wn_ref, 0, d_n), qg_ref).astype(BF16)
    nk_ref[...] = head_norm(proj(wn_ref, d_n, 2 * d_n), kg_ref).astype(BF16)
    nv_ref[...] = proj(wn_ref, 2 * d_n, 3 * d_n).astype(BF16)
    gates_t = jnp.dot(u, wg_ref[...], preferred_element_type=F32).T
    gate_ref[...] = gates_t[:N_GATE, :] + gb_ref[...]


def _inproj(x2, g, w_m, w_n, w_g, gate_b, bd, qg, kg, *, d_m, d_n, tm):
    t, d = x2.shape
    row = lambda width: pl.BlockSpec((tm, width), lambda i: (i, 0))
    out_shapes = (
        jax.ShapeDtypeStruct((t, 2 * d_m), BF16),
        jax.ShapeDtypeStruct((t, d_m), BF16),
        jax.ShapeDtypeStruct((t, d_m), BF16),
        jax.ShapeDtypeStruct((t, d_n), BF16),
        jax.ShapeDtypeStruct((t, d_n), BF16),
        jax.ShapeDtypeStruct((t, d_n), BF16),
        jax.ShapeDtypeStruct((N_GATE, t), F32),
    )
    return pl.pallas_call(
        functools.partial(_inproj_body, d_m=d_m, d_n=d_n),
        grid=(t // tm,),
        in_specs=[row(d), _const_spec((1, d)), _const_spec(w_m.shape), _const_spec(w_n.shape),
                  _const_spec(w_g.shape), _const_spec((N_GATE, 1)),
                  _const_spec((d_n, d_n)), _const_spec((1, d_n)), _const_spec((1, d_n))],
        out_specs=(row(2 * d_m), row(d_m), row(d_m), row(d_n), row(d_n), row(d_n),
                   pl.BlockSpec((N_GATE, tm), lambda i: (0, i))),
        out_shape=out_shapes,
        compiler_params=pltpu.CompilerParams(dimension_semantics=("arbitrary",),
                                             vmem_limit_bytes=44 * MIB),
        name="inproj",
    )(x2, g, w_m, w_n, w_g, gate_b, bd, qg, kg)


def _log_sigmoid(x):
    return jnp.minimum(x, 0.0) - jnp.log1p(jnp.exp(-jnp.abs(x)))


def _sigmoid(x):
    return 1.0 / (1.0 + jnp.exp(-x))


def _mlstm_body(q_ref, k_ref, v_ref, o_ref, gate_ref, cw_ref, cb_ref, ng_ref, tri3_ref, eye2_ref,
                out_ref, qs_ref, kt_ref, cs_ref, ccur_ref, brow_ref, crow_ref, cmax_ref, wt_ref, bl_ref,
                ml_ref, ms_ref, s_ref, wint_ref, einv_ref, *, seq):
    L = CHUNK
    d = MLSTM_HEAD_DIM
    nc = seq // L

    row_id = lax.broadcasted_iota(jnp.int32, (L, d), 0)
    col_id = lax.broadcasted_iota(jnp.int32, (L, d), 1)
    pos_id = lax.broadcasted_iota(jnp.int32, (nc, L), 1)
    ones_blk = jnp.ones((L, d), BF16)

    def conv_silu(src_ref, c, s0, w, b):
        x = src_ref[pl.ds(s0, L), :].astype(F32)
        p0 = pl.multiple_of(jnp.maximum(s0 - 16, 0), 16)
        n0 = pl.multiple_of(jnp.minimum(s0 + L, seq - 16), 16)
        prev_row = src_ref[pl.ds(p0, 16), :][15:16, :].astype(F32)
        next_row = src_ref[pl.ds(n0, 16), :][0:1, :].astype(F32)
        prev_row = jnp.where(c > 0, prev_row, 0.0)
        next_row = jnp.where(c < nc - 1, next_row, 0.0)
        x_prev = jnp.where(row_id == 0, prev_row, pltpu.roll(x, 1, 0))
        x_next = jnp.where(row_id == L - 1, next_row, pltpu.roll(x, L - 1, 0))
        y = w[0:1, :] * x_prev + w[1:2, :] * x + w[2:3, :] * x_next + b
        return y * _sigmoid(y)

    def conv_step(c, carry):
        s0 = pl.multiple_of(c * L, L)
        qs_ref[pl.ds(s0, L), :] = conv_silu(q_ref, c, s0, cw_ref[0], cb_ref[0]).astype(BF16)
        kk = conv_silu(k_ref, c, s0, cw_ref[1], cb_ref[1]) * (d ** -0.5)
        kt_ref[:, pl.ds(s0, L)] = kk.T.astype(BF16)
        return carry

    lax.fori_loop(0, nc, conv_step, 0, unroll=MLSTM_CONV_UNROLL)

    for dirn in (0, 1):
        i_g = gate_ref[2 * dirn]
        f_log = _log_sigmoid(gate_ref[2 * dirn + 1])
        f_cat = jnp.concatenate(_split3(f_log), axis=1)
        brow = jnp.dot(f_cat, tri3_ref[1 - dirn], preferred_element_type=F32)
        b_last = brow[:, L - 1:L] if dirn == 0 else brow[:, 0:1]
        a_row = i_g + b_last - brow
        a_max = jnp.max(a_row, axis=1, keepdims=True)
        crow = i_g - brow
        cmax = crow
        for sh in [1 << e for e in range(L.bit_length() - 1)]:
            if dirn == 0:
                cmax = jnp.maximum(cmax, jnp.where(pos_id >= sh, pltpu.roll(cmax, sh, 1), -jnp.inf))
            else:
                cmax = jnp.maximum(cmax, jnp.where(pos_id < L - sh, pltpu.roll(cmax, L - sh, 1), -jnp.inf))
        brow_ref[dirn] = brow
        crow_ref[dirn] = crow
        cmax_ref[dirn] = cmax
        wt_ref[dirn] = jnp.exp(a_row - a_max)
        bl_ref[dirn] = jnp.broadcast_to(b_last, (nc, L))
        ml_ref[dirn] = jnp.broadcast_to(a_max, (nc, L))

    def v_aug(s0):
        return jnp.concatenate([v_ref[pl.ds(s0, L), :], ones_blk], axis=1)

    ccur_ref[...] = jnp.zeros_like(ccur_ref)

    def scan_step(i, carry):
        new = []
        for dirn, c, m in ((0, i, carry[0]), (1, nc - 1 - i, carry[1])):
            s0 = pl.multiple_of(c * L, L)
            kw = (kt_ref[:, pl.ds(s0, L)].astype(F32) * wt_ref[dirn, pl.ds(c, 1), :]).astype(BF16)
            k_loc = jnp.dot(kw, v_aug(s0), preferred_element_type=F32)
            state = ccur_ref[dirn]
            cs_ref[dirn, c] = state.astype(BF16)
            ms_ref[dirn, pl.ds(c, 1), :] = m
            a_prev = m + bl_ref[dirn, pl.ds(c, 1), :]
            a_max = ml_ref[dirn, pl.ds(c, 1), :]
            m_new = jnp.maximum(a_prev, a_max)
            w_prev = jnp.exp(a_prev - m_new)[:, 0:1]
            w_loc = jnp.exp(a_max - m_new)[:, 0:1]
            ccur_ref[dirn] = w_prev * state + w_loc * k_loc
            new.append(m_new)
        return tuple(new)

    m0 = jnp.zeros((1, L), F32)
    lax.fori_loop(0, nc, scan_step, (m0, m0), unroll=MLSTM_SCAN_UNROLL)

    for dirn in (0, 1):
        mu_all = jnp.maximum(ms_ref[dirn], cmax_ref[dirn])
        cmax_ref[dirn] = mu_all
        brow_ref[dirn] = jnp.minimum(jnp.exp(-(brow_ref[dirn] + mu_all)), F32_BIG)
    lower = col_id <= row_id
    upper = col_id >= row_id

    def weights_stage(c, slot):
        s0 = pl.multiple_of(c * L, L)
        qk = jnp.dot(qs_ref[pl.ds(s0, L), :], kt_ref[:, pl.ds(s0, L)], preferred_element_type=F32)
        for dirn, mask in ((0, lower), (1, upper)):
            rows = []
            for stat_ref in (cmax_ref, brow_ref):
                hi, mid, _ = _split3(stat_ref[dirn, pl.ds(c, 1), :])
                rows.append(jnp.broadcast_to(jnp.concatenate([hi, mid], axis=1), (L, 2 * L)))
            col = lax.dot_general(eye2_ref[...], jnp.concatenate(rows, axis=0), NT_DIMS,
                                  preferred_element_type=F32)
            mu = col[:, :L]
            p = jnp.exp(jnp.where(mask, crow_ref[dirn, pl.ds(c, 1), :] - mu, -jnp.inf))
            s_ref[slot, dirn] = (qk * p).astype(BF16)
            wint_ref[slot, dirn] = jnp.exp(ms_ref[dirn, pl.ds(c, 1), :] - mu)
            einv_ref[slot, dirn] = col[:, L:]

    def output_stage(c, slot):
        s0 = pl.multiple_of(c * L, L)
        q = qs_ref[pl.ds(s0, L), :]
        vaug = v_aug(s0)
        hsum = None
        for dirn in (0, 1):
            intra = jnp.dot(s_ref[slot, dirn], vaug, preferred_element_type=F32)
            inter = jnp.dot(q, cs_ref[dirn, c], preferred_element_type=F32)
            w_inter = wint_ref[slot, dirn]
            num = w_inter * inter[:, :d] + intra[:, :d]
            den = w_inter * inter[:, d:] + intra[:, d:]
            h = num / jnp.maximum(jnp.abs(den), einv_ref[slot, dirn])
            hsum = h if hsum is None else hsum + h
        y = hsum * _rms_scale(hsum) * ng_ref[...]
        y = y * _sigmoid(o_ref[pl.ds(s0, L), :].astype(F32))
        out_ref[pl.ds(s0, L), :] = y.astype(out_ref.dtype)

    G = MLSTM_CHUNKS_PER_STEP
    for i in range(G):
        weights_stage(i, i)

    def out_step(j, carry):
        cur = (j % 2) * G
        nxt = G - cur
        for i in range(G):
            output_stage(j * G + i, cur + i)
        for i in range(G):
            weights_stage(jnp.minimum((j + 1) * G + i, nc - 1), nxt + i)
        return carry

    lax.fori_loop(0, nc // G, out_step, 0)


def _mlstm(qk, v, o, gates, conv_w, conv_b, norm_g, tri3, eye2):
    b, seq, _ = v.shape
    H, d, L = N_MLSTM_HEADS, MLSTM_HEAD_DIM, CHUNK
    nc = seq // L
    assert nc % MLSTM_SCAN_UNROLL == 0 and nc % MLSTM_CONV_UNROLL == 0 and L == LANES
    assert nc % MLSTM_CHUNKS_PER_STEP == 0
    col = lambda off: pl.BlockSpec((None, seq, d), lambda bi, hi: (bi, 0, hi + off))
    stat = pltpu.VMEM((2, nc, L), F32)
    slots = 2 * MLSTM_CHUNKS_PER_STEP
    return pl.pallas_call(
        functools.partial(_mlstm_body, seq=seq),
        grid=(b, H),
        in_specs=[
            col(0), col(H), col(0), col(0),
            pl.BlockSpec((4, None, None, nc, L), lambda bi, hi: (0, hi, bi, 0, 0)),
            pl.BlockSpec((None, 2, 3, d), lambda bi, hi: (hi, 0, 0, 0)),
            pl.BlockSpec((None, 2, 1, d), lambda bi, hi: (hi, 0, 0, 0)),
            pl.BlockSpec((None, 1, d), lambda bi, hi: (hi, 0, 0)),
            _const_spec(tri3.shape), _const_spec(eye2.shape),
        ],
        out_specs=pl.BlockSpec((None, seq, d), lambda bi, hi: (bi, 0, hi)),
        out_shape=jax.ShapeDtypeStruct((b, seq, H * d), BF16),
        scratch_shapes=[
            pltpu.VMEM((seq, d), BF16),
            pltpu.VMEM((d, seq), BF16),
            pltpu.VMEM((2, nc, d, 2 * d), BF16),
            pltpu.VMEM((2, d, 2 * d), F32),
            stat, stat, stat,
            stat, stat, stat,
            stat,
            pltpu.VMEM((slots, 2, L, L), BF16),
            pltpu.VMEM((slots, 2, L, d), F32),
            pltpu.VMEM((slots, 2, L, d), F32),
        ],
        compiler_params=pltpu.CompilerParams(dimension_semantics=("arbitrary", "arbitrary"),
                                             vmem_limit_bytes=52 * MIB),
        name="mlstm",
    )(qk, qk, v, o, gates, conv_w, conv_b, norm_g, tri3, eye2)


def _natten_body(q_ref, k_ref, v_ref, bias_ref, out_ref, p_ref, *, rows):
    hd = NA_HEAD_DIM
    win = WIN_H * GRID_W
    lane = lax.broadcasted_iota(jnp.int32, (GRID_W, 2 * hd), 1)
    first = lane < hd
    ones_blk = jnp.ones((win, 2 * hd), BF16)

    def window_start(r):
        return jnp.clip(r - WIN_H // 2, 0, rows - WIN_H)

    def prob_stage(r, slot):
        rs = window_start(r)
        q = q_ref[pl.ds(pl.multiple_of(r * GRID_W, GRID_W), GRID_W), :]
        zero = jnp.zeros_like(q)
        qs = jnp.concatenate([jnp.where(first, q, zero), jnp.where(first, zero, q)], axis=0)
        kwin = k_ref[pl.ds(pl.multiple_of(rs * GRID_W, GRID_W), win), :]
        off = rs - r + (WIN_H - 1)
        bias = bias_ref[off & 1, :, pl.ds(pl.multiple_of((off >> 1) * LANES, LANES), win)]
        s = lax.dot_general(qs, kwin, NT_DIMS, preferred_element_type=F32) + bias
        p_ref[slot] = jnp.exp(s - jnp.max(s, axis=1, keepdims=True)).astype(BF16)

    def output_stage(r, slot):
        rs = window_start(r)
        vwin = v_ref[pl.ds(pl.multiple_of(rs * GRID_W, GRID_W), win), :]
        o = jnp.dot(p_ref[slot], jnp.concatenate([vwin, ones_blk], axis=1), preferred_element_type=F32)
        o = o[:, :2 * hd] / o[:, 2 * hd:]
        out = jnp.where(first, o[:GRID_W], o[GRID_W:])
        out_ref[pl.ds(pl.multiple_of(r * GRID_W, GRID_W), GRID_W), :] = out.astype(out_ref.dtype)

    n_it = rows // NA_ROWS_PER_STEP
    for i in range(NA_ROWS_PER_STEP):
        prob_stage(i, i)

    def row_group(j, carry):
        cur = (j % 2) * NA_ROWS_PER_STEP
        nxt = NA_ROWS_PER_STEP - cur
        for i in range(NA_ROWS_PER_STEP):
            output_stage(j * NA_ROWS_PER_STEP + i, cur + i)
        for i in range(NA_ROWS_PER_STEP):
            prob_stage(jnp.minimum((j + 1) * NA_ROWS_PER_STEP + i, rows - 1), nxt + i)
        return carry

    lax.fori_loop(0, n_it, row_group, 0)


def _natten(nq, nk, nv, bias_tab):
    b, seq, d_n = nq.shape
    pairs = N_NA_HEADS // 2
    width = 2 * NA_HEAD_DIM
    rows = seq // GRID_W
    assert rows % NA_ROWS_PER_STEP == 0
    col = pl.BlockSpec((None, seq, width), lambda bi, pi: (bi, 0, pi))
    return pl.pallas_call(
        functools.partial(_natten_body, rows=rows),
        grid=(b, pairs),
        in_specs=[col, col, col,
                  pl.BlockSpec((None, 2, 2 * GRID_W, NA_BIAS_LANES), lambda bi, pi: (pi, 0, 0, 0))],
        out_specs=col,
        out_shape=jax.ShapeDtypeStruct((b, seq, d_n), BF16),
        scratch_shapes=[pltpu.VMEM((2 * NA_ROWS_PER_STEP, 2 * GRID_W, WIN_H * GRID_W), BF16)],
        compiler_params=pltpu.CompilerParams(dimension_semantics=("arbitrary", "arbitrary"),
                                             vmem_limit_bytes=40 * MIB),
        name="natten",
    )(nq, nk, nv, bias_tab)


def _natten_bias_table(rpb):
    c = np.arange(GRID_W)
    cs = np.clip(c - WIN_W // 2, 0, GRID_W - WIN_W)
    cp = np.arange(GRID_W)
    valid = (cp[None, :] >= cs[:, None]) & (cp[None, :] < cs[:, None] + WIN_W)
    rel = cp[None, None, :] - c[None, :, None] + (WIN_W - 1)
    onehot = (rel == np.arange(2 * WIN_W - 1)[:, None, None]).astype(np.float32)
    dense = jnp.einsum('hrd,dcp->hrcp', rpb.astype(F32), jnp.asarray(onehot),
                       precision=lax.Precision.HIGHEST)
    dense = jnp.where(jnp.asarray(valid)[None, None], dense, NEG_BIG)
    n_rel = 2 * WIN_H - 1
    tab = dense.reshape(N_NA_HEADS // 2, 2, n_rel, GRID_W, GRID_W).transpose(0, 1, 3, 2, 4)
    tab = tab.reshape(N_NA_HEADS // 2, 2 * GRID_W, n_rel * GRID_W)
    even = tab[:, :, :NA_BIAS_LANES]
    odd = tab[:, :, GRID_W:GRID_W + NA_BIAS_LANES]
    return jnp.stack([even, odd], axis=1)


def _tail_body(x_ref, ya_ref, yb_ref, p_ref, woa_ref, wob_ref, g2_ref, w1_ref, w2_ref,
               g3_ref, wg_ref, wu_ref, out_ref, *, ff_chunk):
    d_ff = w1_ref.shape[1]
    h = (x_ref[...]
         + jnp.dot(ya_ref[...], woa_ref[...], preferred_element_type=F32)
         + jnp.dot(yb_ref[...], wob_ref[...], preferred_element_type=F32))
    u = (h * _rms_scale(h) * g2_ref[...]).astype(BF16)
    out_ref[...] = h
    for j in range(d_ff // ff_chunk):
        z = jnp.dot(u, w1_ref[:, j * ff_chunk:(j + 1) * ff_chunk], preferred_element_type=F32)
        z = jnp.maximum(z, 0.0)
        out_ref[...] += jnp.dot((z * z).astype(BF16), w2_ref[j * ff_chunk:(j + 1) * ff_chunk, :],
                                preferred_element_type=F32)
    h = out_ref[...]
    u = (h * _rms_scale(h) * g3_ref[...]).astype(BF16)
    gate = _sigmoid(jnp.dot(u, wg_ref[...], preferred_element_type=F32))
    up = jnp.dot(p_ref[...].astype(BF16), wu_ref[...], preferred_element_type=F32)
    out_ref[...] = h + gate * up


def _tail(x2, ya, yb, p2, woa, wob, g2, w1, w2, g3, wg, wu, *, tm, ff_chunk):
    t, d = x2.shape
    row = lambda width: pl.BlockSpec((tm, width), lambda i: (i, 0))
    consts = [woa, wob, g2, w1, w2, g3, wg, wu]
    return pl.pallas_call(
        functools.partial(_tail_body, ff_chunk=ff_chunk),
        grid=(t // tm,),
        in_specs=[row(d), row(ya.shape[1]), row(yb.shape[1]), row(p2.shape[1])]
                 + [_const_spec(c.shape) for c in consts],
        out_specs=row(d),
        out_shape=jax.ShapeDtypeStruct((t, d), F32),
        compiler_params=pltpu.CompilerParams(dimension_semantics=("arbitrary",),
                                             vmem_limit_bytes=52 * MIB),
        name="tail",
    )(x2, ya, yb, p2, *consts)


def kernel(x, p, norm1_g, w_in, conv_w, conv_b, gate_b, mlstm_norm_g, q_norm_g, k_norm_g, rpb,
           w_out, norm2_g, w_ff1, w_ff2, ple_norm_g, w_ple_gate, w_ple_up):
    b, seq, d = x.shape
    depth = w_in.shape[0]
    H, hd, L = N_MLSTM_HEADS, MLSTM_HEAD_DIM, CHUNK
    d_m = H * hd
    d_n = N_NA_HEADS * NA_HEAD_DIM
    t = b * seq
    rows = seq // GRID_W
    assert rows >= WIN_H

    bd = jnp.asarray(np.kron(np.eye(N_NA_HEADS), np.ones((NA_HEAD_DIM, NA_HEAD_DIM))), BF16)
    ri, ci = np.indices((L, L))
    tri = np.stack([ci <= ri, ci >= ri])
    tri3 = jnp.asarray(np.concatenate([tri, tri, tri], axis=1), BF16)
    eye2 = jnp.asarray(np.concatenate([ci == ri, ci == ri], axis=1), BF16)

    h = x.reshape(t, d)
    for i in range(depth):
        wi = w_in[i]
        w_m = wi[:, :4 * d_m].astype(BF16)
        w_n = wi[:, 4 * d_m + N_GATE:].astype(BF16)
        w_g = jnp.pad(wi[:, 4 * d_m:4 * d_m + N_GATE], ((0, 0), (0, GATE_PAD - N_GATE))).astype(BF16)
        qg = (q_norm_g[i].reshape(1, d_n) * (NA_HEAD_DIM ** -0.5)).astype(F32)
        kg = k_norm_g[i].reshape(1, d_n).astype(F32)
        qk, mv, mo, nq, nk, nv, gates = _inproj(
            h, norm1_g[i].reshape(1, d), w_m, w_n, w_g, gate_b[i].reshape(N_GATE, 1), bd, qg, kg,
            d_m=d_m, d_n=d_n, tm=1024)

        cw = conv_w[i].reshape(3, 2, H, hd).transpose(2, 1, 0, 3)
        cb = conv_b[i].reshape(2, H, 1, hd).transpose(1, 0, 2, 3)
        y_a = _mlstm(qk.reshape(b, seq, 2 * d_m), mv.reshape(b, seq, d_m), mo.reshape(b, seq, d_m),
                     gates.reshape(4, H, b, seq // L, L), cw, cb,
                     mlstm_norm_g[i].reshape(H, 1, hd), tri3, eye2)

        bias_tab = _natten_bias_table(rpb[i])
        y_b = _natten(nq.reshape(b, seq, d_n), nk.reshape(b, seq, d_n), nv.reshape(b, seq, d_n), bias_tab)

        wo = w_out[i].astype(BF16)
        h = _tail(h, y_a.reshape(t, d_m), y_b.reshape(t, d_n), p[i].reshape(t, -1),
                  wo[:d_m], wo[d_m:], norm2_g[i].reshape(1, d), w_ff1[i].astype(BF16),
                  w_ff2[i].astype(BF16), ple_norm_g[i].reshape(1, d), w_ple_gate[i].astype(BF16),
                  w_ple_up[i].astype(BF16), tm=512, ff_chunk=1024)
    return h.reshape(b, seq, d)
```

```python
import functools

import jax
import jax.numpy as jnp
import numpy as np
from jax import lax
from jax.experimental import pallas as pl
from jax.experimental.pallas import tpu as pltpu

F32 = jnp.float32
BF16 = jnp.bfloat16

N_MLSTM_HEADS = 4
MLSTM_HEAD_DIM = 128
N_NA_HEADS = 8
NA_HEAD_DIM = 64
GRID_W = 64
WIN_H = 8
WIN_W = 16
CHUNK = 128
N_GATE = 4 * N_MLSTM_HEADS
RMS_EPS = 1e-6
NEG_BIG = -1e30
F32_BIG = 3e38
NA_BIAS_LANES = (2 * WIN_H - 2) * GRID_W
NA_ROWS_PER_STEP = 8
MLSTM_CHUNKS_PER_STEP = 8
MLSTM_CONV_UNROLL = 4
MLSTM_SCAN_UNROLL = 8

LANES = 128
GATE_PAD = LANES
MIB = 1024 * 1024

NT_DIMS = (((1,), (1,)), ((), ()))


def _const_spec(shape):
    return pl.BlockSpec(shape, lambda *_: (0,) * len(shape), pipeline_mode=pl.Buffered(1))


def _rms_scale(x):
    return lax.rsqrt(jnp.mean(x * x, axis=-1, keepdims=True) + RMS_EPS)


def _split3(x):
    hi = x.astype(BF16)
    r1 = x - hi.astype(F32)
    mid = r1.astype(BF16)
    lo = (r1 - mid.astype(F32)).astype(BF16)
    return hi, mid, lo


def _inproj_body(x_ref, g_ref, wm_ref, wn_ref, wg_ref, gb_ref, bd_ref, qg_ref, kg_ref,
                 qk_ref, v_ref, o_ref, nq_ref, nk_ref, nv_ref, gate_ref, *, d_m, d_n):
    x = x_ref[...]
    u = (x * _rms_scale(x) * g_ref[...]).astype(BF16)

    def proj(w_ref, lo, hi):
        return jnp.dot(u, w_ref[:, lo:hi], preferred_element_type=F32)

    def head_norm(y, gain_ref):
        ss = jnp.dot((y * y).astype(BF16), bd_ref[...], preferred_element_type=F32)
        return y * lax.rsqrt(ss * (1.0 / NA_HEAD_DIM) + RMS_EPS) * gain_ref[...]

    qk_ref[...] = proj(wm_ref, 0, 2 * d_m).astype(BF16)
    v_ref[...] = proj(wm_ref, 2 * d_m, 3 * d_m).astype(BF16)
    o_ref[...] = proj(wm_ref, 3 * d_m, 4 * d_m).astype(BF16)
    nq_ref[...] = head_norm(proj(wn_ref, 0, d_n), qg_ref).astype(BF16)
    nk_ref[...] = head_norm(proj(wn_ref, d_n, 2 * d_n), kg_ref).astype(BF16)
    nv_ref[...] = proj(wn_ref, 2 * d_n, 3 * d_n).astype(BF16)
    gates_t = jnp.dot(u, wg_ref[...], preferred_element_type=F32).T
    gate_ref[...] = gates_t[:N_GATE, :] + gb_ref[...]


def _inproj(x2, g, w_all, w_n, gate_b, bd, qg, kg, *, d_m, d_n, tm):
    t, d = x2.shape
    assert (4 * d_m) % GATE_PAD == 0
    fixed = functools.partial(pl.BlockSpec, pipeline_mode=pl.Buffered(1))
    row = lambda width: pl.BlockSpec((tm, width), lambda i: (i, 0))
    out_shapes = (
        jax.ShapeDtypeStruct((t, 2 * d_m), BF16),
        jax.ShapeDtypeStruct((t, d_m), BF16),
        jax.ShapeDtypeStruct((t, d_m), BF16),
        jax.ShapeDtypeStruct((t, d_n), BF16),
        jax.ShapeDtypeStruct((t, d_n), BF16),
        jax.ShapeDtypeStruct((t, d_n), BF16),
        jax.ShapeDtypeStruct((N_GATE, t), F32),
    )
    return pl.pallas_call(
        functools.partial(_inproj_body, d_m=d_m, d_n=d_n),
        grid=(t // tm,),
        in_specs=[row(d), _const_spec((1, d)), fixed((d, 4 * d_m), lambda i: (0, 0)), _const_spec(w_n.shape),
                  fixed((d, GATE_PAD), lambda i: (0, 4 * d_m // GATE_PAD)), _const_spec((N_GATE, 1)),
                  _const_spec((d_n, d_n)), _const_spec((1, d_n)), _const_spec((1, d_n))],
        out_specs=(row(2 * d_m), row(d_m), row(d_m), row(d_n), row(d_n), row(d_n),
                   pl.BlockSpec((N_GATE, tm), lambda i: (0, i))),
        out_shape=out_shapes,
        compiler_params=pltpu.CompilerParams(dimension_semantics=("arbitrary",),
                                             vmem_limit_bytes=44 * MIB),
        name="inproj",
    )(x2, g, w_all, w_n, w_all, gate_b, bd, qg, kg)


def _log_sigmoid(x):
    return jnp.minimum(x, 0.0) - jnp.log1p(jnp.exp(-jnp.abs(x)))


def _sigmoid(x):
    return 1.0 / (1.0 + jnp.exp(-x))


def _mlstm_body(q_ref, k_ref, v_ref, o_ref, gate_ref, cw_ref, cb_ref, ng_ref, tri3_ref, eye2_ref,
                out_ref, qs_ref, kt_ref, cs_ref, ccur_ref, brow_ref, crow_ref, cmax_ref, wt_ref, bl_ref,
                ml_ref, ms_ref, s_ref, wint_ref, einv_ref, *, seq):
    L = CHUNK
    d = MLSTM_HEAD_DIM
    nc = seq // L

    row_id = lax.broadcasted_iota(jnp.int32, (L, d), 0)
    col_id = lax.broadcasted_iota(jnp.int32, (L, d), 1)
    pos_id = lax.broadcasted_iota(jnp.int32, (nc, L), 1)
    ones_blk = jnp.ones((L, d), BF16)

    def conv_silu(src_ref, c, s0, w, b):
        x = src_ref[pl.ds(s0, L), :].astype(F32)
        p0 = pl.multiple_of(jnp.maximum(s0 - 16, 0), 16)
        n0 = pl.multiple_of(jnp.minimum(s0 + L, seq - 16), 16)
        prev_row = src_ref[pl.ds(p0, 16), :][15:16, :].astype(F32)
        next_row = src_ref[pl.ds(n0, 16), :][0:1, :].astype(F32)
        prev_row = jnp.where(c > 0, prev_row, 0.0)
        next_row = jnp.where(c < nc - 1, next_row, 0.0)
        x_prev = jnp.where(row_id == 0, prev_row, pltpu.roll(x, 1, 0))
        x_next = jnp.where(row_id == L - 1, next_row, pltpu.roll(x, L - 1, 0))
        y = w[0:1, :] * x_prev + w[1:2, :] * x + w[2:3, :] * x_next + b
        return y * _sigmoid(y)

    def conv_step(c, carry):
        s0 = pl.multiple_of(c * L, L)
        qs_ref[pl.ds(s0, L), :] = conv_silu(q_ref, c, s0, cw_ref[0], cb_ref[0]).astype(BF16)
        kk = conv_silu(k_ref, c, s0, cw_ref[1], cb_ref[1]) * (d ** -0.5)
        kt_ref[:, pl.ds(s0, L)] = kk.T.astype(BF16)
        return carry

    lax.fori_loop(0, nc, conv_step, 0, unroll=MLSTM_CONV_UNROLL)

    for dirn in (0, 1):
        i_g = gate_ref[2 * dirn]
        f_log = _log_sigmoid(gate_ref[2 * dirn + 1])
        f_cat = jnp.concatenate(_split3(f_log), axis=1)
        brow = jnp.dot(f_cat, tri3_ref[1 - dirn], preferred_element_type=F32)
        b_last = brow[:, L - 1:L] if dirn == 0 else brow[:, 0:1]
        a_row = i_g + b_last - brow
        a_max = jnp.max(a_row, axis=1, keepdims=True)
        crow = i_g - brow
        cmax = crow
        for sh in [1 << e for e in range(L.bit_length() - 1)]:
            if dirn == 0:
                cmax = jnp.maximum(cmax, jnp.where(pos_id >= sh, pltpu.roll(cmax, sh, 1), -jnp.inf))
            else:
                cmax = jnp.maximum(cmax, jnp.where(pos_id < L - sh, pltpu.roll(cmax, L - sh, 1), -jnp.inf))
        brow_ref[dirn] = brow
        crow_ref[dirn] = crow
        cmax_ref[dirn] = cmax
        wt_ref[dirn] = jnp.exp(a_row - a_max)
        bl_ref[dirn] = jnp.broadcast_to(b_last, (nc, L))
        ml_ref[dirn] = jnp.broadcast_to(a_max, (nc, L))

    def v_aug(s0):
        return jnp.concatenate([v_ref[pl.ds(s0, L), :], ones_blk], axis=1)

    ccur_ref[...] = jnp.zeros_like(ccur_ref)

    def scan_step(i, carry):
        new = []
        for dirn, c, m in ((0, i, carry[0]), (1, nc - 1 - i, carry[1])):
            s0 = pl.multiple_of(c * L, L)
            kw = (kt_ref[:, pl.ds(s0, L)].astype(F32) * wt_ref[dirn, pl.ds(c, 1), :]).astype(BF16)
            k_loc = jnp.dot(kw, v_aug(s0), preferred_element_type=F32)
            state = ccur_ref[dirn]
            cs_ref[dirn, c] = state.astype(BF16)
            ms_ref[dirn, pl.ds(c, 1), :] = m
            a_prev = m + bl_ref[dirn, pl.ds(c, 1), :]
            a_max = ml_ref[dirn, pl.ds(c, 1), :]
            m_new = jnp.maximum(a_prev, a_max)
            w_prev = jnp.exp(a_prev - m_new)[:, 0:1]
            w_loc = jnp.exp(a_max - m_new)[:, 0:1]
            ccur_ref[dirn] = w_prev * state + w_loc * k_loc
            new.append(m_new)
        return tuple(new)

    m0 = jnp.zeros((1, L), F32)
    lax.fori_loop(0, nc, scan_step, (m0, m0), unroll=MLSTM_SCAN_UNROLL)

    for dirn in (0, 1):
        mu_all = jnp.maximum(ms_ref[dirn], cmax_ref[dirn])
        cmax_ref[dirn] = mu_all
        brow_ref[dirn] = jnp.minimum(jnp.exp(-(brow_ref[dirn] + mu_all)), F32_BIG)
    lower = col_id <= row_id
    upper = col_id >= row_id

    def weights_stage(c, slot):
        s0 = pl.multiple_of(c * L, L)
        qk = jnp.dot(qs_ref[pl.ds(s0, L), :], kt_ref[:, pl.ds(s0, L)], preferred_element_type=F32)
        for dirn, mask in ((0, lower), (1, upper)):
            rows = []
            for stat_ref in (cmax_ref, brow_ref):
                hi, mid, _ = _split3(stat_ref[dirn, pl.ds(c, 1), :])
                rows.append(jnp.broadcast_to(jnp.concatenate([hi, mid], axis=1), (L, 2 * L)))
            col = lax.dot_general(eye2_ref[...], jnp.concatenate(rows, axis=0), NT_DIMS,
                                  preferred_element_type=F32)
            mu = col[:, :L]
            p = jnp.exp(jnp.where(mask, crow_ref[dirn, pl.ds(c, 1), :] - mu, -jnp.inf))
            s_ref[slot, dirn] = (qk * p).astype(BF16)
            wint_ref[slot, dirn] = jnp.exp(ms_ref[dirn, pl.ds(c, 1), :] - mu)
            einv_ref[slot, dirn] = col[:, L:]

    def output_stage(c, slot):
        s0 = pl.multiple_of(c * L, L)
        q = qs_ref[pl.ds(s0, L), :]
        vaug = v_aug(s0)
        hsum = None
        for dirn in (0, 1):
            intra = jnp.dot(s_ref[slot, dirn], vaug, preferred_element_type=F32)
            inter = jnp.dot(q, cs_ref[dirn, c], preferred_element_type=F32)
            w_inter = wint_ref[slot, dirn]
            num = w_inter * inter[:, :d] + intra[:, :d]
            den = w_inter * inter[:, d:] + intra[:, d:]
            h = num / jnp.maximum(jnp.abs(den), einv_ref[slot, dirn])
            hsum = h if hsum is None else hsum + h
        y = hsum * _rms_scale(hsum) * ng_ref[...]
        y = y * _sigmoid(o_ref[pl.ds(s0, L), :].astype(F32))
        out_ref[pl.ds(s0, L), :] = y.astype(out_ref.dtype)

    G = MLSTM_CHUNKS_PER_STEP
    for i in range(G):
        weights_stage(i, i)

    def out_step(j, carry):
        cur = (j % 2) * G
        nxt = G - cur
        for i in range(G):
            output_stage(j * G + i, cur + i)
        for i in range(G):
            weights_stage(jnp.minimum((j + 1) * G + i, nc - 1), nxt + i)
        return carry

    lax.fori_loop(0, nc // G, out_step, 0)


def _mlstm(qk, v, o, gates, conv_w, conv_b, norm_g, tri3, eye2):
    b, seq, _ = v.shape
    H, d, L = N_MLSTM_HEADS, MLSTM_HEAD_DIM, CHUNK
    nc = seq // L
    assert nc % MLSTM_SCAN_UNROLL == 0 and nc % MLSTM_CONV_UNROLL == 0 and L == LANES
    assert nc % MLSTM_CHUNKS_PER_STEP == 0
    col = lambda off: pl.BlockSpec((None, seq, d), lambda bi, hi: (bi, 0, hi + off))
    stat = pltpu.VMEM((2, nc, L), F32)
    slots = 2 * MLSTM_CHUNKS_PER_STEP
    return pl.pallas_call(
        functools.partial(_mlstm_body, seq=seq),
        grid=(b, H),
        in_specs=[
            col(0), col(H), col(0), col(0),
            pl.BlockSpec((4, None, None, nc, L), lambda bi, hi: (0, hi, bi, 0, 0)),
            pl.BlockSpec((None, 2, 3, d), lambda bi, hi: (hi, 0, 0, 0)),
            pl.BlockSpec((None, 2, 1, d), lambda bi, hi: (hi, 0, 0, 0)),
            pl.BlockSpec((None, 1, d), lambda bi, hi: (hi, 0, 0)),
            _const_spec(tri3.shape), _const_spec(eye2.shape),
        ],
        out_specs=pl.BlockSpec((None, seq, d), lambda bi, hi: (bi, 0, hi)),
        out_shape=jax.ShapeDtypeStruct((b, seq, H * d), BF16),
        scratch_shapes=[
            pltpu.VMEM((seq, d), BF16),
            pltpu.VMEM((d, seq), BF16),
            pltpu.VMEM((2, nc, d, 2 * d), BF16),
            pltpu.VMEM((2, d, 2 * d), F32),
            stat, stat, stat,
            stat, stat, stat,
            stat,
            pltpu.VMEM((slots, 2, L, L), BF16),
            pltpu.VMEM((slots, 2, L, d), F32),
            pltpu.VMEM((slots, 2, L, d), F32),
        ],
        compiler_params=pltpu.CompilerParams(dimension_semantics=("arbitrary", "arbitrary"),
                                             vmem_limit_bytes=52 * MIB),
        name="mlstm",
    )(qk, qk, v, o, gates, conv_w, conv_b, norm_g, tri3, eye2)


def _natten_body(q_ref, k_ref, v_ref, bias_ref, out_ref, p_ref, *, rows):
    hd = NA_HEAD_DIM
    win = WIN_H * GRID_W
    lane = lax.broadcasted_iota(jnp.int32, (GRID_W, 2 * hd), 1)
    first = lane < hd
    ones_blk = jnp.ones((win, 2 * hd), BF16)

    def window_start(r):
        return jnp.clip(r - WIN_H // 2, 0, rows - WIN_H)

    def prob_stage(r, slot):
        rs = window_start(r)
        q = q_ref[pl.ds(pl.multiple_of(r * GRID_W, GRID_W), GRID_W), :]
        zero = jnp.zeros_like(q)
        qs = jnp.concatenate([jnp.where(first, q, zero), jnp.where(first, zero, q)], axis=0)
        kwin = k_ref[pl.ds(pl.multiple_of(rs * GRID_W, GRID_W), win), :]
        off = rs - r + (WIN_H - 1)
        bias = bias_ref[off & 1, :, pl.ds(pl.multiple_of((off >> 1) * LANES, LANES), win)]
        s = lax.dot_general(qs, kwin, NT_DIMS, preferred_element_type=F32) + bias
        p_ref[slot] = jnp.exp(s - jnp.max(s, axis=1, keepdims=True)).astype(BF16)

    def output_stage(r, slot):
        rs = window_start(r)
        vwin = v_ref[pl.ds(pl.multiple_of(rs * GRID_W, GRID_W), win), :]
        o = jnp.dot(p_ref[slot], jnp.concatenate([vwin, ones_blk], axis=1), preferred_element_type=F32)
        o = o[:, :2 * hd] / o[:, 2 * hd:]
        out = jnp.where(first, o[:GRID_W], o[GRID_W:])
        out_ref[pl.ds(pl.multiple_of(r * GRID_W, GRID_W), GRID_W), :] = out.astype(out_ref.dtype)

    n_it = rows // NA_ROWS_PER_STEP
    for i in range(NA_ROWS_PER_STEP):
        prob_stage(i, i)

    def row_group(j, carry):
        cur = (j % 2) * NA_ROWS_PER_STEP
        nxt = NA_ROWS_PER_STEP - cur
        for i in range(NA_ROWS_PER_STEP):
            output_stage(j * NA_ROWS_PER_STEP + i, cur + i)
        for i in range(NA_ROWS_PER_STEP):
            prob_stage(jnp.minimum((j + 1) * NA_ROWS_PER_STEP + i, rows - 1), nxt + i)
        return carry

    lax.fori_loop(0, n_it, row_group, 0)


def _natten(nq, nk, nv, bias_tab):
    b, seq, d_n = nq.shape
    pairs = N_NA_HEADS // 2
    width = 2 * NA_HEAD_DIM
    rows = seq // GRID_W
    assert rows % NA_ROWS_PER_STEP == 0
    col = pl.BlockSpec((None, seq, width), lambda bi, pi: (bi, 0, pi))
    return pl.pallas_call(
        functools.partial(_natten_body, rows=rows),
        grid=(b, pairs),
        in_specs=[col, col, col,
                  pl.BlockSpec((None, 2, 2 * GRID_W, NA_BIAS_LANES), lambda bi, pi: (pi, 0, 0, 0))],
        out_specs=col,
        out_shape=jax.ShapeDtypeStruct((b, seq, d_n), BF16),
        scratch_shapes=[pltpu.VMEM((2 * NA_ROWS_PER_STEP, 2 * GRID_W, WIN_H * GRID_W), BF16)],
        compiler_params=pltpu.CompilerParams(dimension_semantics=("arbitrary", "arbitrary"),
                                             vmem_limit_bytes=40 * MIB),
        name="natten",
    )(nq, nk, nv, bias_tab)


def _natten_bias_table(rpb):
    c = np.arange(GRID_W)
    cs = np.clip(c - WIN_W // 2, 0, GRID_W - WIN_W)
    cp = np.arange(GRID_W)
    valid = (cp[None, :] >= cs[:, None]) & (cp[None, :] < cs[:, None] + WIN_W)
    rel = cp[None, None, :] - c[None, :, None] + (WIN_W - 1)
    onehot = (rel == np.arange(2 * WIN_W - 1)[:, None, None]).astype(np.float32)
    n_rel = 2 * WIN_H - 1
    tab = jnp.einsum('phrd,dcq->phcrq', rpb.astype(F32).reshape(N_NA_HEADS // 2, 2, n_rel, 2 * WIN_W - 1),
                     jnp.asarray(onehot), precision=lax.Precision.HIGHEST)
    tab = jnp.where(jnp.asarray(valid)[None, None, :, None, :], tab, NEG_BIG)
    tab = tab.reshape(N_NA_HEADS // 2, 2 * GRID_W, n_rel * GRID_W)
    even = tab[:, :, :NA_BIAS_LANES]
    odd = tab[:, :, GRID_W:GRID_W + NA_BIAS_LANES]
    return jnp.stack([even, odd], axis=1)


def _tail_body(x_ref, ya_ref, yb_ref, p_ref, woa_ref, wob_ref, g2_ref, w1_ref, w2_ref,
               g3_ref, wg_ref, wu_ref, out_ref, *, ff_chunk):
    d_ff = w1_ref.shape[1]
    h = (x_ref[...]
         + jnp.dot(ya_ref[...], woa_ref[...], preferred_element_type=F32)
         + jnp.dot(yb_ref[...], wob_ref[...], preferred_element_type=F32))
    u = (h * _rms_scale(h) * g2_ref[...]).astype(BF16)
    out_ref[...] = h
    for j in range(d_ff // ff_chunk):
        z = jnp.dot(u, w1_ref[:, j * ff_chunk:(j + 1) * ff_chunk], preferred_element_type=F32)
        z = jnp.maximum(z, 0.0)
        out_ref[...] += jnp.dot((z * z).astype(BF16), w2_ref[j * ff_chunk:(j + 1) * ff_chunk, :],
                                preferred_element_type=F32)
    h = out_ref[...]
    u = (h * _rms_scale(h) * g3_ref[...]).astype(BF16)
    gate = _sigmoid(jnp.dot(u, wg_ref[...], preferred_element_type=F32))
    up = jnp.dot(p_ref[...].astype(BF16), wu_ref[...], preferred_element_type=F32)
    out_ref[...] = h + gate * up


def _tail(x2, ya, yb, p2, woa, wob, g2, w1, w2, g3, wg, wu, *, tm, ff_chunk):
    t, d = x2.shape
    row = lambda width: pl.BlockSpec((tm, width), lambda i: (i, 0))
    consts = [woa, wob, g2, w1, w2, g3, wg, wu]
    return pl.pallas_call(
        functools.partial(_tail_body, ff_chunk=ff_chunk),
        grid=(t // tm,),
        in_specs=[row(d), row(ya.shape[1]), row(yb.shape[1]), row(p2.shape[1])]
                 + [_const_spec(c.shape) for c in consts],
        out_specs=row(d),
        out_shape=jax.ShapeDtypeStruct((t, d), F32),
        compiler_params=pltpu.CompilerParams(dimension_semantics=("arbitrary",),
                                             vmem_limit_bytes=52 * MIB),
        name="tail",
    )(x2, ya, yb, p2, *consts)


def kernel(x, p, norm1_g, w_in, conv_w, conv_b, gate_b, mlstm_norm_g, q_norm_g, k_norm_g, rpb,
           w_out, norm2_g, w_ff1, w_ff2, ple_norm_g, w_ple_gate, w_ple_up):
    b, seq, d = x.shape
    depth = w_in.shape[0]
    H, hd, L = N_MLSTM_HEADS, MLSTM_HEAD_DIM, CHUNK
    d_m = H * hd
    d_n = N_NA_HEADS * NA_HEAD_DIM
    t = b * seq
    rows = seq // GRID_W
    assert rows >= WIN_H

    bd = jnp.asarray(np.kron(np.eye(N_NA_HEADS), np.ones((NA_HEAD_DIM, NA_HEAD_DIM))), BF16)
    ri, ci = np.indices((L, L))
    tri = np.stack([ci <= ri, ci >= ri])
    tri3 = jnp.asarray(np.concatenate([tri, tri, tri], axis=1), BF16)
    eye2 = jnp.asarray(np.concatenate([ci == ri, ci == ri], axis=1), BF16)

    h = x.reshape(t, d)
    for i in range(depth):
        wi = w_in[i]
        wi = wi.astype(BF16)
        w_n = wi[:, 4 * d_m + N_GATE:]
        qg = (q_norm_g[i].reshape(1, d_n) * (NA_HEAD_DIM ** -0.5)).astype(F32)
        kg = k_norm_g[i].reshape(1, d_n).astype(F32)
        qk, mv, mo, nq, nk, nv, gates = _inproj(
            h, norm1_g[i].reshape(1, d), wi, w_n, gate_b[i].reshape(N_GATE, 1), bd, qg, kg,
            d_m=d_m, d_n=d_n, tm=1024)

        cw = conv_w[i].reshape(3, 2, H, hd).transpose(2, 1, 0, 3)
        cb = conv_b[i].reshape(2, H, 1, hd).transpose(1, 0, 2, 3)
        y_a = _mlstm(qk.reshape(b, seq, 2 * d_m), mv.reshape(b, seq, d_m), mo.reshape(b, seq, d_m),
                     gates.reshape(4, H, b, seq // L, L), cw, cb,
                     mlstm_norm_g[i].reshape(H, 1, hd), tri3, eye2)

        bias_tab = _natten_bias_table(rpb[i])
        y_b = _natten(nq.reshape(b, seq, d_n), nk.reshape(b, seq, d_n), nv.reshape(b, seq, d_n), bias_tab)

        wo = w_out[i].astype(BF16)
        h = _tail(h, y_a.reshape(t, d_m), y_b.reshape(t, d_n), p[i].reshape(t, -1),
                  wo[:d_m], wo[d_m:], norm2_g[i].reshape(1, d), w_ff1[i].astype(BF16),
                  w_ff2[i].astype(BF16), ple_norm_g[i].reshape(1, d), w_ple_gate[i].astype(BF16),
                  w_ple_up[i].astype(BF16), tm=1024, ff_chunk=1024)
    return h.reshape(b, seq, d)
```

```python
import functools

import jax
import jax.numpy as jnp
import numpy as np
from jax import lax
from jax.experimental import pallas as pl
from jax.experimental.pallas import tpu as pltpu

F32 = jnp.float32
BF16 = jnp.bfloat16

N_MLSTM_HEADS = 4
MLSTM_HEAD_DIM = 128
N_NA_HEADS = 8
NA_HEAD_DIM = 64
GRID_W = 64
WIN_H = 8
WIN_W = 16
CHUNK = 128
N_GATE = 4 * N_MLSTM_HEADS
RMS_EPS = 1e-6
NEG_BIG = -1e30
F32_BIG = 3e38
NA_BIAS_LANES = (2 * WIN_H - 2) * GRID_W
NA_ROWS_PER_STEP = 8
MLSTM_CHUNKS_PER_STEP = 8
MLSTM_CONV_UNROLL = 4
MLSTM_SCAN_UNROLL = 8

LANES = 128
GATE_PAD = LANES
MIB = 1024 * 1024

NT_DIMS = (((1,), (1,)), ((), ()))


def _const_spec(shape):
    return pl.BlockSpec(shape, lambda *_: (0,) * len(shape), pipeline_mode=pl.Buffered(1))


def _rms_scale(x):
    return lax.rsqrt(jnp.mean(x * x, axis=-1, keepdims=True) + RMS_EPS)


def _split3(x):
    hi = x.astype(BF16)
    r1 = x - hi.astype(F32)
    mid = r1.astype(BF16)
    lo = (r1 - mid.astype(F32)).astype(BF16)
    return hi, mid, lo


def _inproj_body(x_ref, g_ref, wm_ref, wn_ref, wg_ref, gb_ref, bd_ref, qg_ref, kg_ref,
                 qk_ref, v_ref, o_ref, nq_ref, nk_ref, nv_ref, gate_ref, *, d_m, d_n):
    x = x_ref[...]
    u = (x * _rms_scale(x) * g_ref[...]).astype(BF16)

    def proj(w_ref, lo, hi):
        return jnp.dot(u, w_ref[:, lo:hi], preferred_element_type=F32)

    def head_norm(y, gain_ref):
        ss = jnp.dot((y * y).astype(BF16), bd_ref[...], preferred_element_type=F32)
        return y * lax.rsqrt(ss * (1.0 / NA_HEAD_DIM) + RMS_EPS) * gain_ref[...]

    qk_ref[...] = proj(wm_ref, 0, 2 * d_m).astype(BF16)
    v_ref[...] = proj(wm_ref, 2 * d_m, 3 * d_m).astype(BF16)
    o_ref[...] = proj(wm_ref, 3 * d_m, 4 * d_m).astype(BF16)
    nq_ref[...] = head_norm(proj(wn_ref, 0, d_n), qg_ref).astype(BF16)
    nk_ref[...] = head_norm(proj(wn_ref, d_n, 2 * d_n), kg_ref).astype(BF16)
    nv_ref[...] = proj(wn_ref, 2 * d_n, 3 * d_n).astype(BF16)
    gates_t = jnp.dot(u, wg_ref[...], preferred_element_type=F32).T
    gate_ref[...] = gates_t[:N_GATE, :] + gb_ref[...]


def _inproj(x2, g, w_all, w_n, gate_b, bd, qg, kg, *, d_m, d_n, tm):
    t, d = x2.shape
    assert (4 * d_m) % GATE_PAD == 0
    fixed = functools.partial(pl.BlockSpec, pipeline_mode=pl.Buffered(1))
    row = lambda width: pl.BlockSpec((tm, width), lambda i: (i, 0))
    out_shapes = (
        jax.ShapeDtypeStruct((t, 2 * d_m), BF16),
        jax.ShapeDtypeStruct((t, d_m), BF16),
        jax.ShapeDtypeStruct((t, d_m), BF16),
        jax.ShapeDtypeStruct((t, d_n), BF16),
        jax.ShapeDtypeStruct((t, d_n), BF16),
        jax.ShapeDtypeStruct((t, d_n), BF16),
        jax.ShapeDtypeStruct((N_GATE, t), F32),
    )
    return pl.pallas_call(
        functools.partial(_inproj_body, d_m=d_m, d_n=d_n),
        grid=(t // tm,),
        in_specs=[row(d), _const_spec((1, d)), fixed((d, 4 * d_m), lambda i: (0, 0)), _const_spec(w_n.shape),
                  fixed((d, GATE_PAD), lambda i: (0, 4 * d_m // GATE_PAD)), _const_spec((N_GATE, 1)),
                  _const_spec((d_n, d_n)), _const_spec((1, d_n)), _const_spec((1, d_n))],
        out_specs=(row(2 * d_m), row(d_m), row(d_m), row(d_n), row(d_n), row(d_n),
                   pl.BlockSpec((N_GATE, tm), lambda i: (0, i))),
        out_shape=out_shapes,
        compiler_params=pltpu.CompilerParams(dimension_semantics=("arbitrary",),
                                             vmem_limit_bytes=44 * MIB),
        name="inproj",
    )(x2, g, w_all, w_n, w_all, gate_b, bd, qg, kg)


def _log_sigmoid(x):
    return jnp.minimum(x, 0.0) - jnp.log1p(jnp.exp(-jnp.abs(x)))


def _sigmoid(x):
    return 1.0 / (1.0 + jnp.exp(-x))


def _mlstm_body(q_ref, k_ref, v_ref, o_ref, gate_ref, cw_ref, cb_ref, ng_ref, tri3_ref, eye2_ref,
                out_ref, qs_ref, kt_ref, cs_ref, ccur_ref, brow_ref, crow_ref, cmax_ref, wt_ref, bl_ref,
                ml_ref, ms_ref, s_ref, einv_ref, *, seq):
    L = CHUNK
    d = MLSTM_HEAD_DIM
    nc = seq // L

    row_id = lax.broadcasted_iota(jnp.int32, (L, d), 0)
    col_id = lax.broadcasted_iota(jnp.int32, (L, d), 1)
    pos_id = lax.broadcasted_iota(jnp.int32, (nc, L), 1)
    ones_blk = jnp.ones((L, d), BF16)

    def conv_silu(src_ref, c, s0, w, b):
        x = src_ref[pl.ds(s0, L), :].astype(F32)
        p0 = pl.multiple_of(jnp.maximum(s0 - 16, 0), 16)
        n0 = pl.multiple_of(jnp.minimum(s0 + L, seq - 16), 16)
        prev_row = src_ref[pl.ds(p0, 16), :][15:16, :].astype(F32)
        next_row = src_ref[pl.ds(n0, 16), :][0:1, :].astype(F32)
        prev_row = jnp.where(c > 0, prev_row, 0.0)
        next_row = jnp.where(c < nc - 1, next_row, 0.0)
        x_prev = jnp.where(row_id == 0, prev_row, pltpu.roll(x, 1, 0))
        x_next = jnp.where(row_id == L - 1, next_row, pltpu.roll(x, L - 1, 0))
        y = w[0:1, :] * x_prev + w[1:2, :] * x + w[2:3, :] * x_next + b
        return y * _sigmoid(y)

    def conv_step(c, carry):
        s0 = pl.multiple_of(c * L, L)
        qs_ref[pl.ds(s0, L), :] = conv_silu(q_ref, c, s0, cw_ref[0], cb_ref[0]).astype(BF16)
        kk = conv_silu(k_ref, c, s0, cw_ref[1], cb_ref[1]) * (d ** -0.5)
        kt_ref[:, pl.ds(s0, L)] = kk.T.astype(BF16)
        return carry

    lax.fori_loop(0, nc, conv_step, 0, unroll=MLSTM_CONV_UNROLL)

    for dirn in (0, 1):
        i_g = gate_ref[2 * dirn]
        f_log = _log_sigmoid(gate_ref[2 * dirn + 1])
        f_cat = jnp.concatenate(_split3(f_log), axis=1)
        brow = jnp.dot(f_cat, tri3_ref[1 - dirn], preferred_element_type=F32)
        b_last = brow[:, L - 1:L] if dirn == 0 else brow[:, 0:1]
        a_row = i_g + b_last - brow
        a_max = jnp.max(a_row, axis=1, keepdims=True)
        crow = i_g - brow
        cmax = crow
        for sh in [1 << e for e in range(L.bit_length() - 1)]:
            if dirn == 0:
                cmax = jnp.maximum(cmax, jnp.where(pos_id >= sh, pltpu.roll(cmax, sh, 1), -jnp.inf))
            else:
                cmax = jnp.maximum(cmax, jnp.where(pos_id < L - sh, pltpu.roll(cmax, L - sh, 1), -jnp.inf))
        brow_ref[dirn] = brow
        crow_ref[dirn] = crow
        cmax_ref[dirn] = cmax
        wt_ref[dirn] = jnp.exp(a_row - a_max)
        bl_ref[dirn] = jnp.broadcast_to(b_last, (nc, L))
        ml_ref[dirn] = jnp.broadcast_to(a_max, (nc, L))

    def v_aug(s0):
        return jnp.concatenate([v_ref[pl.ds(s0, L), :], ones_blk], axis=1)

    ccur_ref[...] = jnp.zeros_like(ccur_ref)

    def scan_step(i, carry):
        new = []
        for dirn, c, m in ((0, i, carry[0]), (1, nc - 1 - i, carry[1])):
            s0 = pl.multiple_of(c * L, L)
            kw = (kt_ref[:, pl.ds(s0, L)].astype(F32) * wt_ref[dirn, pl.ds(c, 1), :]).astype(BF16)
            k_loc = jnp.dot(kw, v_aug(s0), preferred_element_type=F32)
            state = ccur_ref[dirn]
            cs_ref[dirn, c] = state.astype(BF16)
            ms_ref[dirn, pl.ds(c, 1), :] = m
            a_prev = m + bl_ref[dirn, pl.ds(c, 1), :]
            a_max = ml_ref[dirn, pl.ds(c, 1), :]
            m_new = jnp.maximum(a_prev, a_max)
            w_prev = jnp.exp(a_prev - m_new)[:, 0:1]
            w_loc = jnp.exp(a_max - m_new)[:, 0:1]
            ccur_ref[dirn] = w_prev * state + w_loc * k_loc
            new.append(m_new)
        return tuple(new)

    m0 = jnp.zeros((1, L), F32)
    lax.fori_loop(0, nc, scan_step, (m0, m0), unroll=MLSTM_SCAN_UNROLL)

    for dirn in (0, 1):
        mu_all = jnp.maximum(ms_ref[dirn], cmax_ref[dirn])
        cmax_ref[dirn] = mu_all
        brow_ref[dirn] = jnp.minimum(jnp.exp(-(brow_ref[dirn] + mu_all)), F32_BIG)
    lower = col_id <= row_id
    upper = col_id >= row_id

    def weights_stage(c, slot):
        s0 = pl.multiple_of(c * L, L)
        q = qs_ref[pl.ds(s0, L), :]
        qk = jnp.dot(q, kt_ref[:, pl.ds(s0, L)], preferred_element_type=F32)
        for dirn, mask in ((0, lower), (1, upper)):
            rows = []
            for stat_ref in (cmax_ref, brow_ref):
                hi, mid, _ = _split3(stat_ref[dirn, pl.ds(c, 1), :])
                rows.append(jnp.broadcast_to(jnp.concatenate([hi, mid], axis=1), (L, 2 * L)))
            col = lax.dot_general(eye2_ref[...], jnp.concatenate(rows, axis=0), NT_DIMS,
                                  preferred_element_type=F32)
            mu = col[:, :L]
            p = jnp.exp(jnp.where(mask, crow_ref[dirn, pl.ds(c, 1), :] - mu, -jnp.inf))
            q_inter = q.astype(F32) * jnp.exp(ms_ref[dirn, pl.ds(c, 1), :] - mu)
            s_ref[slot, dirn] = jnp.concatenate([qk * p, q_inter], axis=1).astype(BF16)
            einv_ref[slot, dirn] = col[:, L:]

    def output_stage(c, slot):
        s0 = pl.multiple_of(c * L, L)
        vaug = v_aug(s0)
        hsum = None
        for dirn in (0, 1):
            nd = jnp.dot(s_ref[slot, dirn], jnp.concatenate([vaug, cs_ref[dirn, c]], axis=0),
                         preferred_element_type=F32)
            h = nd[:, :d] / jnp.maximum(jnp.abs(nd[:, d:]), einv_ref[slot, dirn])
            hsum = h if hsum is None else hsum + h
        y = hsum * _rms_scale(hsum) * ng_ref[...]
        y = y * _sigmoid(o_ref[pl.ds(s0, L), :].astype(F32))
        out_ref[pl.ds(s0, L), :] = y.astype(out_ref.dtype)

    G = MLSTM_CHUNKS_PER_STEP
    for i in range(G):
        weights_stage(i, i)

    def out_step(j, carry):
        cur = (j % 2) * G
        nxt = G - cur
        for i in range(G):
            output_stage(j * G + i, cur + i)
        for i in range(G):
            weights_stage(jnp.minimum((j + 1) * G + i, nc - 1), nxt + i)
        return carry

    lax.fori_loop(0, nc // G, out_step, 0)


def _mlstm(qk, v, o, gates, conv_w, conv_b, norm_g, tri3, eye2):
    b, seq, _ = v.shape
    H, d, L = N_MLSTM_HEADS, MLSTM_HEAD_DIM, CHUNK
    nc = seq // L
    assert nc % MLSTM_SCAN_UNROLL == 0 and nc % MLSTM_CONV_UNROLL == 0 and L == LANES
    assert nc % MLSTM_CHUNKS_PER_STEP == 0
    col = lambda off: pl.BlockSpec((None, seq, d), lambda bi, hi: (bi, 0, hi + off))
    stat = pltpu.VMEM((2, nc, L), F32)
    slots = 2 * MLSTM_CHUNKS_PER_STEP
    return pl.pallas_call(
        functools.partial(_mlstm_body, seq=seq),
        grid=(b, H),
        in_specs=[
            col(0), col(H), col(0), col(0),
            pl.BlockSpec((4, None, None, nc, L), lambda bi, hi: (0, hi, bi, 0, 0)),
            pl.BlockSpec((None, 2, 3, d), lambda bi, hi: (hi, 0, 0, 0)),
            pl.BlockSpec((None, 2, 1, d), lambda bi, hi: (hi, 0, 0, 0)),
            pl.BlockSpec((None, 1, d), lambda bi, hi: (hi, 0, 0)),
            _const_spec(tri3.shape), _const_spec(eye2.shape),
        ],
        out_specs=pl.BlockSpec((None, seq, d), lambda bi, hi: (bi, 0, hi)),
        out_shape=jax.ShapeDtypeStruct((b, seq, H * d), BF16),
        scratch_shapes=[
            pltpu.VMEM((seq, d), BF16),
            pltpu.VMEM((d, seq), BF16),
            pltpu.VMEM((2, nc, d, 2 * d), BF16),
            pltpu.VMEM((2, d, 2 * d), F32),
            stat, stat, stat,
            stat, stat, stat,
            stat,
            pltpu.VMEM((slots, 2, L, L + d), BF16),
            pltpu.VMEM((slots, 2, L, d), F32),
        ],
        compiler_params=pltpu.CompilerParams(dimension_semantics=("arbitrary", "arbitrary"),
                                             vmem_limit_bytes=52 * MIB),
        name="mlstm",
    )(qk, qk, v, o, gates, conv_w, conv_b, norm_g, tri3, eye2)


def _natten_body(q_ref, k_ref, v_ref, bias_ref, out_ref, p_ref, *, rows):
    hd = NA_HEAD_DIM
    win = WIN_H * GRID_W
    lane = lax.broadcasted_iota(jnp.int32, (GRID_W, 2 * hd), 1)
    first = lane < hd
    ones_blk = jnp.ones((win, 2 * hd), BF16)

    def window_start(r):
        return jnp.clip(r - WIN_H // 2, 0, rows - WIN_H)

    def prob_stage(r, slot):
        rs = window_start(r)
        q = q_ref[pl.ds(pl.multiple_of(r * GRID_W, GRID_W), GRID_W), :]
        zero = jnp.zeros_like(q)
        qs = jnp.concatenate([jnp.where(first, q, zero), jnp.where(first, zero, q)], axis=0)
        kwin = k_ref[pl.ds(pl.multiple_of(rs * GRID_W, GRID_W), win), :]
        off = rs - r + (WIN_H - 1)
        bias = bias_ref[off & 1, :, pl.ds(pl.multiple_of((off >> 1) * LANES, LANES), win)]
        s = lax.dot_general(qs, kwin, NT_DIMS, preferred_element_type=F32) + bias
        p_ref[slot] = jnp.exp(s - jnp.max(s, axis=1, keepdims=True)).astype(BF16)

    def output_stage(r, slot):
        rs = window_start(r)
        vwin = v_ref[pl.ds(pl.multiple_of(rs * GRID_W, GRID_W), win), :]
        o = jnp.dot(p_ref[slot], jnp.concatenate([vwin, ones_blk], axis=1), preferred_element_type=F32)
        o = o[:, :2 * hd] / o[:, 2 * hd:]
        out = jnp.where(first, o[:GRID_W], o[GRID_W:])
        out_ref[pl.ds(pl.multiple_of(r * GRID_W, GRID_W), GRID_W), :] = out.astype(out_ref.dtype)

    n_it = rows // NA_ROWS_PER_STEP
    for i in range(NA_ROWS_PER_STEP):
        prob_stage(i, i)

    def row_group(j, carry):
        cur = (j % 2) * NA_ROWS_PER_STEP
        nxt = NA_ROWS_PER_STEP - cur
        for i in range(NA_ROWS_PER_STEP):
            output_stage(j * NA_ROWS_PER_STEP + i, cur + i)
        for i in range(NA_ROWS_PER_STEP):
            prob_stage(jnp.minimum((j + 1) * NA_ROWS_PER_STEP + i, rows - 1), nxt + i)
        return carry

    lax.fori_loop(0, n_it, row_group, 0)


def _natten(nq, nk, nv, bias_tab):
    b, seq, d_n = nq.shape
    pairs = N_NA_HEADS // 2
    width = 2 * NA_HEAD_DIM
    rows = seq // GRID_W
    assert rows % NA_ROWS_PER_STEP == 0
    col = pl.BlockSpec((None, seq, width), lambda bi, pi: (bi, 0, pi))
    return pl.pallas_call(
        functools.partial(_natten_body, rows=rows),
        grid=(b, pairs),
        in_specs=[col, col, col,
                  pl.BlockSpec((None, 2, 2 * GRID_W, NA_BIAS_LANES), lambda bi, pi: (pi, 0, 0, 0))],
        out_specs=col,
        out_shape=jax.ShapeDtypeStruct((b, seq, d_n), BF16),
        scratch_shapes=[pltpu.VMEM((2 * NA_ROWS_PER_STEP, 2 * GRID_W, WIN_H * GRID_W), BF16)],
        compiler_params=pltpu.CompilerParams(dimension_semantics=("arbitrary", "arbitrary"),
                                             vmem_limit_bytes=40 * MIB),
        name="natten",
    )(nq, nk, nv, bias_tab)


def _natten_bias_table(rpb):
    c = np.arange(GRID_W)
    cs = np.clip(c - WIN_W // 2, 0, GRID_W - WIN_W)
    cp = np.arange(GRID_W)
    valid = (cp[None, :] >= cs[:, None]) & (cp[None, :] < cs[:, None] + WIN_W)
    rel = cp[None, None, :] - c[None, :, None] + (WIN_W - 1)
    onehot = (rel == np.arange(2 * WIN_W - 1)[:, None, None]).astype(np.float32)
    n_rel = 2 * WIN_H - 1
    tab = jnp.einsum('phrd,dcq->phcrq', rpb.astype(F32).reshape(N_NA_HEADS // 2, 2, n_rel, 2 * WIN_W - 1),
                     jnp.asarray(onehot), precision=lax.Precision.HIGHEST)
    tab = jnp.where(jnp.asarray(valid)[None, None, :, None, :], tab, NEG_BIG)
    tab = tab.reshape(N_NA_HEADS // 2, 2 * GRID_W, n_rel * GRID_W)
    even = tab[:, :, :NA_BIAS_LANES]
    odd = tab[:, :, GRID_W:GRID_W + NA_BIAS_LANES]
    return jnp.stack([even, odd], axis=1)


def _tail_body(x_ref, ya_ref, yb_ref, p_ref, woa_ref, wob_ref, g2_ref, w1_ref, w2_ref,
               g3_ref, wg_ref, wu_ref, out_ref, *, ff_chunk):
    d_ff = w1_ref.shape[1]
    h = (x_ref[...]
         + jnp.dot(ya_ref[...], woa_ref[...], preferred_element_type=F32)
         + jnp.dot(yb_ref[...], wob_ref[...], preferred_element_type=F32))
    u = (h * _rms_scale(h) * g2_ref[...]).astype(BF16)
    out_ref[...] = h
    for j in range(d_ff // ff_chunk):
        z = jnp.dot(u, w1_ref[:, j * ff_chunk:(j + 1) * ff_chunk], preferred_element_type=F32)
        z = jnp.maximum(z, 0.0)
        out_ref[...] += jnp.dot((z * z).astype(BF16), w2_ref[j * ff_chunk:(j + 1) * ff_chunk, :],
                                preferred_element_type=F32)
    h = out_ref[...]
    u = (h * _rms_scale(h) * g3_ref[...]).astype(BF16)
    gate = _sigmoid(jnp.dot(u, wg_ref[...], preferred_element_type=F32))
    up = jnp.dot(p_ref[...].astype(BF16), wu_ref[...], preferred_element_type=F32)
    out_ref[...] = h + gate * up


def _tail(x2, ya, yb, p2, woa, wob, g2, w1, w2, g3, wg, wu, *, tm, ff_chunk):
    t, d = x2.shape
    row = lambda width: pl.BlockSpec((tm, width), lambda i: (i, 0))
    consts = [woa, wob, g2, w1, w2, g3, wg, wu]
    return pl.pallas_call(
        functools.partial(_tail_body, ff_chunk=ff_chunk),
        grid=(t // tm,),
        in_specs=[row(d), row(ya.shape[1]), row(yb.shape[1]), row(p2.shape[1])]
                 + [_const_spec(c.shape) for c in consts],
        out_specs=row(d),
        out_shape=jax.ShapeDtypeStruct((t, d), F32),
        compiler_params=pltpu.CompilerParams(dimension_semantics=("arbitrary",),
                                             vmem_limit_bytes=52 * MIB),
        name="tail",
    )(x2, ya, yb, p2, *consts)


def kernel(x, p, norm1_g, w_in, conv_w, conv_b, gate_b, mlstm_norm_g, q_norm_g, k_norm_g, rpb,
           w_out, norm2_g, w_ff1, w_ff2, ple_norm_g, w_ple_gate, w_ple_up):
    b, seq, d = x.shape
    depth = w_in.shape[0]
    H, hd, L = N_MLSTM_HEADS, MLSTM_HEAD_DIM, CHUNK
    d_m = H * hd
    d_n = N_NA_HEADS * NA_HEAD_DIM
    t = b * seq
    rows = seq // GRID_W
    assert rows >= WIN_H

    bd = jnp.asarray(np.kron(np.eye(N_NA_HEADS), np.ones((NA_HEAD_DIM, NA_HEAD_DIM))), BF16)
    ri, ci = np.indices((L, L))
    tri = np.stack([ci <= ri, ci >= ri])
    tri3 = jnp.asarray(np.concatenate([tri, tri, tri], axis=1), BF16)
    eye2 = jnp.asarray(np.concatenate([ci == ri, ci == ri], axis=1), BF16)

    h = x.reshape(t, d)
    for i in range(depth):
        wi = w_in[i]
        wi = wi.astype(BF16)
        w_n = wi[:, 4 * d_m + N_GATE:]
        qg = (q_norm_g[i].reshape(1, d_n) * (NA_HEAD_DIM ** -0.5)).astype(F32)
        kg = k_norm_g[i].reshape(1, d_n).astype(F32)
        qk, mv, mo, nq, nk, nv, gates = _inproj(
            h, norm1_g[i].reshape(1, d), wi, w_n, gate_b[i].reshape(N_GATE, 1), bd, qg, kg,
            d_m=d_m, d_n=d_n, tm=1024)

        cw = conv_w[i].reshape(3, 2, H, hd).transpose(2, 1, 0, 3)
        cb = conv_b[i].reshape(2, H, 1, hd).transpose(1, 0, 2, 3)
        y_a = _mlstm(qk.reshape(b, seq, 2 * d_m), mv.reshape(b, seq, d_m), mo.reshape(b, seq, d_m),
                     gates.reshape(4, H, b, seq // L, L), cw, cb,
                     mlstm_norm_g[i].reshape(H, 1, hd), tri3, eye2)

        bias_tab = _natten_bias_table(rpb[i])
        y_b = _natten(nq.reshape(b, seq, d_n), nk.reshape(b, seq, d_n), nv.reshape(b, seq, d_n), bias_tab)

        wo = w_out[i].astype(BF16)
        h = _tail(h, y_a.reshape(t, d_m), y_b.reshape(t, d_n), p[i].reshape(t, -1),
                  wo[:d_m], wo[d_m:], norm2_g[i].reshape(1, d), w_ff1[i].astype(BF16),
                  w_ff2[i].astype(BF16), ple_norm_g[i].reshape(1, d), w_ple_gate[i].astype(BF16),
                  w_ple_up[i].astype(BF16), tm=1024, ff_chunk=1024)
    return h.reshape(b, seq, d)
```

```python
import functools

import jax
import jax.numpy as jnp
import numpy as np
from jax import lax
from jax.experimental import pallas as pl
from jax.experimental.pallas import tpu as pltpu

F32 = jnp.float32
BF16 = jnp.bfloat16

N_MLSTM_HEADS = 4
MLSTM_HEAD_DIM = 128
N_NA_HEADS = 8
NA_HEAD_DIM = 64
GRID_W = 64
WIN_H = 8
WIN_W = 16
CHUNK = 128
N_GATE = 4 * N_MLSTM_HEADS
RMS_EPS = 1e-6
NEG_BIG = -1e30
F32_BIG = 3e38
NA_BIAS_LANES = (2 * WIN_H - 2) * GRID_W
NA_ROWS_PER_STEP = 8
MLSTM_CHUNKS_PER_STEP = 8
MLSTM_CONV_UNROLL = 4
MLSTM_SCAN_UNROLL = 8

LANES = 128
GATE_PAD = LANES
MIB = 1024 * 1024

NT_DIMS = (((1,), (1,)), ((), ()))


def _const_spec(shape):
    return pl.BlockSpec(shape, lambda *_: (0,) * len(shape), pipeline_mode=pl.Buffered(1))


def _rms_scale(x):
    return lax.rsqrt(jnp.mean(x * x, axis=-1, keepdims=True) + RMS_EPS)


def _split3(x):
    hi = x.astype(BF16)
    r1 = x - hi.astype(F32)
    mid = r1.astype(BF16)
    lo = (r1 - mid.astype(F32)).astype(BF16)
    return hi, mid, lo


def _inproj_body(x_ref, g_ref, wm_ref, wn_ref, wg_ref, gb_ref, bd_ref, qg_ref, kg_ref,
                 qk_ref, v_ref, o_ref, nq_ref, nk_ref, nv_ref, gate_ref, *, d_m, d_n):
    x = x_ref[...]
    u = (x * _rms_scale(x) * g_ref[...]).astype(BF16)

    def proj(w_ref, lo, hi):
        return jnp.dot(u, w_ref[:, lo:hi], preferred_element_type=F32)

    def head_norm(y, gain_ref):
        ss = jnp.dot((y * y).astype(BF16), bd_ref[...], preferred_element_type=F32)
        return y * lax.rsqrt(ss * (1.0 / NA_HEAD_DIM) + RMS_EPS) * gain_ref[...]

    qk_ref[...] = proj(wm_ref, 0, 2 * d_m).astype(BF16)
    v_ref[...] = proj(wm_ref, 2 * d_m, 3 * d_m).astype(BF16)
    o_ref[...] = proj(wm_ref, 3 * d_m, 4 * d_m).astype(BF16)
    nq_ref[...] = head_norm(proj(wn_ref, 0, d_n), qg_ref).astype(BF16)
    nk_ref[...] = head_norm(proj(wn_ref, d_n, 2 * d_n), kg_ref).astype(BF16)
    nv_ref[...] = proj(wn_ref, 2 * d_n, 3 * d_n).astype(BF16)
    gates_t = jnp.dot(u, wg_ref[...], preferred_element_type=F32).T
    gate_ref[...] = gates_t[:N_GATE, :] + gb_ref[...]


def _inproj(x2, g, w_all, w_n, gate_b, bd, qg, kg, *, d_m, d_n, tm):
    t, d = x2.shape
    assert (4 * d_m) % GATE_PAD == 0
    fixed = functools.partial(pl.BlockSpec, pipeline_mode=pl.Buffered(1))
    row = lambda width: pl.BlockSpec((tm, width), lambda i: (i, 0))
    out_shapes = (
        jax.ShapeDtypeStruct((t, 2 * d_m), BF16),
        jax.ShapeDtypeStruct((t, d_m), BF16),
        jax.ShapeDtypeStruct((t, d_m), BF16),
        jax.ShapeDtypeStruct((t, d_n), BF16),
        jax.ShapeDtypeStruct((t, d_n), BF16),
        jax.ShapeDtypeStruct((t, d_n), BF16),
        jax.ShapeDtypeStruct((N_GATE, t), F32),
    )
    return pl.pallas_call(
        functools.partial(_inproj_body, d_m=d_m, d_n=d_n),
        grid=(t // tm,),
        in_specs=[row(d), _const_spec((1, d)), fixed((d, 4 * d_m), lambda i: (0, 0)), _const_spec(w_n.shape),
                  fixed((d, GATE_PAD), lambda i: (0, 4 * d_m // GATE_PAD)), _const_spec((N_GATE, 1)),
                  _const_spec((d_n, d_n)), _const_spec((1, d_n)), _const_spec((1, d_n))],
        out_specs=(row(2 * d_m), row(d_m), row(d_m), row(d_n), row(d_n), row(d_n),
                   pl.BlockSpec((N_GATE, tm), lambda i: (0, i))),
        out_shape=out_shapes,
        compiler_params=pltpu.CompilerParams(dimension_semantics=("arbitrary",),
                                             vmem_limit_bytes=44 * MIB),
        name="inproj",
    )(x2, g, w_all, w_n, w_all, gate_b, bd, qg, kg)


def _log_sigmoid(x):
    return jnp.minimum(x, 0.0) - jnp.log1p(jnp.exp(-jnp.abs(x)))


def _sigmoid(x):
    return 1.0 / (1.0 + jnp.exp(-x))


def _mlstm_body(q_ref, k_ref, v_ref, o_ref, gate_ref, cw_ref, cb_ref, ng_ref, tri3_ref, eye2_ref,
                out_ref, qs_ref, kt_ref, cs_ref, ccur_ref, brow_ref, crow_ref, cmax_ref, wt_ref, bl_ref,
                ml_ref, ms_ref, s_ref, einv_ref, *, seq):
    L = CHUNK
    d = MLSTM_HEAD_DIM
    nc = seq // L

    row_id = lax.broadcasted_iota(jnp.int32, (L, d), 0)
    col_id = lax.broadcasted_iota(jnp.int32, (L, d), 1)
    pos_id = lax.broadcasted_iota(jnp.int32, (nc, L), 1)
    ones_blk = jnp.ones((L, d), BF16)

    def conv_silu(src_ref, c, s0, w, b):
        x = src_ref[pl.ds(s0, L), :].astype(F32)
        p0 = pl.multiple_of(jnp.maximum(s0 - 16, 0), 16)
        n0 = pl.multiple_of(jnp.minimum(s0 + L, seq - 16), 16)
        prev_row = src_ref[pl.ds(p0, 16), :][15:16, :].astype(F32)
        next_row = src_ref[pl.ds(n0, 16), :][0:1, :].astype(F32)
        prev_row = jnp.where(c > 0, prev_row, 0.0)
        next_row = jnp.where(c < nc - 1, next_row, 0.0)
        x_prev = jnp.where(row_id == 0, prev_row, pltpu.roll(x, 1, 0))
        x_next = jnp.where(row_id == L - 1, next_row, pltpu.roll(x, L - 1, 0))
        y = w[0:1, :] * x_prev + w[1:2, :] * x + w[2:3, :] * x_next + b
        return y * _sigmoid(y)

    def conv_step(c, carry):
        s0 = pl.multiple_of(c * L, L)
        qs_ref[pl.ds(s0, L), :] = conv_silu(q_ref, c, s0, cw_ref[0], cb_ref[0]).astype(BF16)
        kk = conv_silu(k_ref, c, s0, cw_ref[1], cb_ref[1]) * (d ** -0.5)
        kt_ref[:, pl.ds(s0, L)] = kk.T.astype(BF16)
        return carry

    lax.fori_loop(0, nc, conv_step, 0, unroll=MLSTM_CONV_UNROLL)

    for dirn in (0, 1):
        i_g = gate_ref[2 * dirn]
        f_log = _log_sigmoid(gate_ref[2 * dirn + 1])
        f_cat = jnp.concatenate(_split3(f_log), axis=1)
        brow = jnp.dot(f_cat, tri3_ref[1 - dirn], preferred_element_type=F32)
        b_last = brow[:, L - 1:L] if dirn == 0 else brow[:, 0:1]
        a_row = i_g + b_last - brow
        a_max = jnp.max(a_row, axis=1, keepdims=True)
        crow = i_g - brow
        cmax = crow
        for sh in [1 << e for e in range(L.bit_length() - 1)]:
            if dirn == 0:
                cmax = jnp.maximum(cmax, jnp.where(pos_id >= sh, pltpu.roll(cmax, sh, 1), -jnp.inf))
            else:
                cmax = jnp.maximum(cmax, jnp.where(pos_id < L - sh, pltpu.roll(cmax, L - sh, 1), -jnp.inf))
        brow_ref[dirn] = brow
        crow_ref[dirn] = crow
        cmax_ref[dirn] = cmax
        wt_ref[dirn] = jnp.exp(a_row - a_max)
        bl_ref[dirn] = jnp.broadcast_to(b_last, (nc, L))
        ml_ref[dirn] = jnp.broadcast_to(a_max, (nc, L))

    def v_aug(s0):
        return jnp.concatenate([v_ref[pl.ds(s0, L), :], ones_blk], axis=1)

    ccur_ref[...] = jnp.zeros_like(ccur_ref)

    def scan_step(i, carry):
        new = []
        for dirn, c, m in ((0, i, carry[0]), (1, nc - 1 - i, carry[1])):
            s0 = pl.multiple_of(c * L, L)
            kw = (kt_ref[:, pl.ds(s0, L)].astype(F32) * wt_ref[dirn, pl.ds(c, 1), :]).astype(BF16)
            k_loc = jnp.dot(kw, v_aug(s0), preferred_element_type=F32)
            state = ccur_ref[dirn]
            cs_ref[dirn, c] = state.astype(BF16)
            ms_ref[dirn, pl.ds(c, 1), :] = m
            a_prev = m + bl_ref[dirn, pl.ds(c, 1), :]
            a_max = ml_ref[dirn, pl.ds(c, 1), :]
            m_new = jnp.maximum(a_prev, a_max)
            w_prev = jnp.exp(a_prev - m_new)[:, 0:1]
            w_loc = jnp.exp(a_max - m_new)[:, 0:1]
            ccur_ref[dirn] = w_prev * state + w_loc * k_loc
            new.append(m_new)
        return tuple(new)

    m0 = jnp.zeros((1, L), F32)
    lax.fori_loop(0, nc, scan_step, (m0, m0), unroll=MLSTM_SCAN_UNROLL)

    for dirn in (0, 1):
        mu_all = jnp.maximum(ms_ref[dirn], cmax_ref[dirn])
        cmax_ref[dirn] = mu_all
        brow_ref[dirn] = jnp.minimum(jnp.exp(-(brow_ref[dirn] + mu_all)), F32_BIG)
    lower = col_id <= row_id
    upper = col_id >= row_id

    def weights_stage(c, slot):
        s0 = pl.multiple_of(c * L, L)
        q = qs_ref[pl.ds(s0, L), :]
        qk = jnp.dot(q, kt_ref[:, pl.ds(s0, L)], preferred_element_type=F32)
        for dirn, mask in ((0, lower), (1, upper)):
            rows = []
            for stat_ref in (cmax_ref, brow_ref):
                hi, mid, _ = _split3(stat_ref[dirn, pl.ds(c, 1), :])
                rows.append(jnp.broadcast_to(jnp.concatenate([hi, mid], axis=1), (L, 2 * L)))
            col = lax.dot_general(eye2_ref[...], jnp.concatenate(rows, axis=0), NT_DIMS,
                                  preferred_element_type=F32)
            mu = col[:, :L]
            p = jnp.exp(jnp.where(mask, crow_ref[dirn, pl.ds(c, 1), :] - mu, -jnp.inf))
            q_inter = q.astype(F32) * jnp.exp(ms_ref[dirn, pl.ds(c, 1), :] - mu)
            s_ref[slot, dirn] = jnp.concatenate([qk * p, q_inter], axis=1).astype(BF16)
            einv_ref[slot, dirn] = col[:, L:]

    def output_stage(c, slot):
        s0 = pl.multiple_of(c * L, L)
        vaug = v_aug(s0)
        hsum = None
        for dirn in (0, 1):
            nd = jnp.dot(s_ref[slot, dirn], jnp.concatenate([vaug, cs_ref[dirn, c]], axis=0),
                         preferred_element_type=F32)
            h = nd[:, :d] / jnp.maximum(jnp.abs(nd[:, d:]), einv_ref[slot, dirn])
            hsum = h if hsum is None else hsum + h
        y = hsum * _rms_scale(hsum) * ng_ref[...]
        y = y * _sigmoid(o_ref[pl.ds(s0, L), :].astype(F32))
        out_ref[pl.ds(s0, L), :] = y.astype(out_ref.dtype)

    G = MLSTM_CHUNKS_PER_STEP
    for i in range(G):
        weights_stage(i, i)

    def out_step(j, carry):
        cur = (j % 2) * G
        nxt = G - cur
        for i in range(G):
            output_stage(j * G + i, cur + i)
        for i in range(G):
            weights_stage(jnp.minimum((j + 1) * G + i, nc - 1), nxt + i)
        return carry

    lax.fori_loop(0, nc // G, out_step, 0)


def _mlstm(qk, v, o, gates, conv_w, conv_b, norm_g, tri3, eye2):
    b, seq, _ = v.shape
    H, d, L = N_MLSTM_HEADS, MLSTM_HEAD_DIM, CHUNK
    nc = seq // L
    assert nc % MLSTM_SCAN_UNROLL == 0 and nc % MLSTM_CONV_UNROLL == 0 and L == LANES
    assert nc % MLSTM_CHUNKS_PER_STEP == 0
    col = lambda off: pl.BlockSpec((None, seq, d), lambda bi, hi: (bi, 0, hi + off))
    stat = pltpu.VMEM((2, nc, L), F32)
    slots = 2 * MLSTM_CHUNKS_PER_STEP
    return pl.pallas_call(
        functools.partial(_mlstm_body, seq=seq),
        grid=(b, H),
        in_specs=[
            col(0), col(H), col(0), col(0),
            pl.BlockSpec((4, None, None, nc, L), lambda bi, hi: (0, hi, bi, 0, 0)),
            pl.BlockSpec((None, 2, 3, d), lambda bi, hi: (hi, 0, 0, 0)),
            pl.BlockSpec((None, 2, 1, d), lambda bi, hi: (hi, 0, 0, 0)),
            pl.BlockSpec((None, 1, d), lambda bi, hi: (hi, 0, 0)),
            _const_spec(tri3.shape), _const_spec(eye2.shape),
        ],
        out_specs=pl.BlockSpec((None, seq, d), lambda bi, hi: (bi, 0, hi)),
        out_shape=jax.ShapeDtypeStruct((b, seq, H * d), BF16),
        scratch_shapes=[
            pltpu.VMEM((seq, d), BF16),
            pltpu.VMEM((d, seq), BF16),
            pltpu.VMEM((2, nc, d, 2 * d), BF16),
            pltpu.VMEM((2, d, 2 * d), F32),
            stat, stat, stat,
            stat, stat, stat,
            stat,
            pltpu.VMEM((slots, 2, L, L + d), BF16),
            pltpu.VMEM((slots, 2, L, d), F32),
        ],
        compiler_params=pltpu.CompilerParams(dimension_semantics=("arbitrary", "arbitrary"),
                                             vmem_limit_bytes=52 * MIB),
        name="mlstm",
    )(qk, qk, v, o, gates, conv_w, conv_b, norm_g, tri3, eye2)


def _natten_body(q_ref, k_ref, v_ref, bias_ref, out_ref, pa_ref, pb_ref, *, rows):
    hd = NA_HEAD_DIM
    win = WIN_H * GRID_W
    lane = lax.broadcasted_iota(jnp.int32, (GRID_W, 2 * hd), 1)
    first = lane < hd
    ones_blk = jnp.ones((win, 2 * hd), BF16)

    def window_start(r):
        return jnp.clip(r - WIN_H // 2, 0, rows - WIN_H)

    def prob_stage(r, p_ref, slot):
        rs = window_start(r)
        q = q_ref[pl.ds(pl.multiple_of(r * GRID_W, GRID_W), GRID_W), :]
        zero = jnp.zeros_like(q)
        qs = jnp.concatenate([jnp.where(first, q, zero), jnp.where(first, zero, q)], axis=0)
        kwin = k_ref[pl.ds(pl.multiple_of(rs * GRID_W, GRID_W), win), :]
        off = rs - r + (WIN_H - 1)
        bias = bias_ref[off & 1, :, pl.ds(pl.multiple_of((off >> 1) * LANES, LANES), win)]
        s = lax.dot_general(qs, kwin, NT_DIMS, preferred_element_type=F32) + bias
        p_ref[slot] = jnp.exp(s - jnp.max(s, axis=1, keepdims=True)).astype(BF16)

    def output_stage(r, p_ref, slot):
        rs = window_start(r)
        vwin = v_ref[pl.ds(pl.multiple_of(rs * GRID_W, GRID_W), win), :]
        o = jnp.dot(p_ref[slot], jnp.concatenate([vwin, ones_blk], axis=1), preferred_element_type=F32)
        o = o[:, :2 * hd] / o[:, 2 * hd:]
        out = jnp.where(first, o[:GRID_W], o[GRID_W:])
        out_ref[pl.ds(pl.multiple_of(r * GRID_W, GRID_W), GRID_W), :] = out.astype(out_ref.dtype)

    R = NA_ROWS_PER_STEP
    n_groups = rows // R

    def produce(g, p_ref):
        for i in range(R):
            prob_stage(jnp.minimum(g * R + i, rows - 1), p_ref, i)

    def consume(g, p_ref):
        for i in range(R):
            output_stage(g * R + i, p_ref, i)

    produce(0, pa_ref)

    def group_pair(j, carry):
        g = 2 * j
        produce(g + 1, pb_ref)
        consume(g, pa_ref)
        produce(g + 2, pa_ref)
        consume(g + 1, pb_ref)
        return carry

    lax.fori_loop(0, n_groups // 2, group_pair, 0)


def _natten(nq, nk, nv, bias_tab):
    b, seq, d_n = nq.shape
    pairs = N_NA_HEADS // 2
    width = 2 * NA_HEAD_DIM
    rows = seq // GRID_W
    assert rows % (2 * NA_ROWS_PER_STEP) == 0
    col = pl.BlockSpec((None, seq, width), lambda bi, pi: (bi, 0, pi))
    return pl.pallas_call(
        functools.partial(_natten_body, rows=rows),
        grid=(b, pairs),
        in_specs=[col, col, col,
                  pl.BlockSpec((None, 2, 2 * GRID_W, NA_BIAS_LANES), lambda bi, pi: (pi, 0, 0, 0))],
        out_specs=col,
        out_shape=jax.ShapeDtypeStruct((b, seq, d_n), BF16),
        scratch_shapes=[pltpu.VMEM((NA_ROWS_PER_STEP, 2 * GRID_W, WIN_H * GRID_W), BF16)] * 2,
        compiler_params=pltpu.CompilerParams(dimension_semantics=("arbitrary", "arbitrary"),
                                             vmem_limit_bytes=40 * MIB),
        name="natten",
    )(nq, nk, nv, bias_tab)


def _natten_bias_table(rpb):
    c = np.arange(GRID_W)
    cs = np.clip(c - WIN_W // 2, 0, GRID_W - WIN_W)
    cp = np.arange(GRID_W)
    valid = (cp[None, :] >= cs[:, None]) & (cp[None, :] < cs[:, None] + WIN_W)
    rel = cp[None, None, :] - c[None, :, None] + (WIN_W - 1)
    onehot = (rel == np.arange(2 * WIN_W - 1)[:, None, None]).astype(np.float32)
    n_rel = 2 * WIN_H - 1
    tab = jnp.einsum('phrd,dcq->phcrq', rpb.astype(F32).reshape(N_NA_HEADS // 2, 2, n_rel, 2 * WIN_W - 1),
                     jnp.asarray(onehot), precision=lax.Precision.HIGHEST)
    tab = jnp.where(jnp.asarray(valid)[None, None, :, None, :], tab, NEG_BIG)
    tab = tab.reshape(N_NA_HEADS // 2, 2 * GRID_W, n_rel * GRID_W)
    even = tab[:, :, :NA_BIAS_LANES]
    odd = tab[:, :, GRID_W:GRID_W + NA_BIAS_LANES]
    return jnp.stack([even, odd], axis=1)


def _tail_body(x_ref, ya_ref, yb_ref, p_ref, woa_ref, wob_ref, g2_ref, w1_ref, w2_ref,
               g3_ref, wg_ref, wu_ref, out_ref, *, ff_chunk):
    d_ff = w1_ref.shape[1]
    h = (x_ref[...]
         + jnp.dot(ya_ref[...], woa_ref[...], preferred_element_type=F32)
         + jnp.dot(yb_ref[...], wob_ref[...], preferred_element_type=F32))
    u = (h * _rms_scale(h) * g2_ref[...]).astype(BF16)
    out_ref[...] = h
    for j in range(d_ff // ff_chunk):
        z = jnp.dot(u, w1_ref[:, j * ff_chunk:(j + 1) * ff_chunk], preferred_element_type=F32)
        z = jnp.maximum(z, 0.0)
        out_ref[...] += jnp.dot((z * z).astype(BF16), w2_ref[j * ff_chunk:(j + 1) * ff_chunk, :],
                                preferred_element_type=F32)
    h = out_ref[...]
    u = (h * _rms_scale(h) * g3_ref[...]).astype(BF16)
    gate = _sigmoid(jnp.dot(u, wg_ref[...], preferred_element_type=F32))
    up = jnp.dot(p_ref[...].astype(BF16), wu_ref[...], preferred_element_type=F32)
    out_ref[...] = h + gate * up


def _tail(x2, ya, yb, p2, woa, wob, g2, w1, w2, g3, wg, wu, *, tm, ff_chunk):
    t, d = x2.shape
    row = lambda width: pl.BlockSpec((tm, width), lambda i: (i, 0))
    consts = [woa, wob, g2, w1, w2, g3, wg, wu]
    return pl.pallas_call(
        functools.partial(_tail_body, ff_chunk=ff_chunk),
        grid=(t // tm,),
        in_specs=[row(d), row(ya.shape[1]), row(yb.shape[1]), row(p2.shape[1])]
                 + [_const_spec(c.shape) for c in consts],
        out_specs=row(d),
        out_shape=jax.ShapeDtypeStruct((t, d), F32),
        compiler_params=pltpu.CompilerParams(dimension_semantics=("arbitrary",),
                                             vmem_limit_bytes=52 * MIB),
        name="tail",
    )(x2, ya, yb, p2, *consts)


def kernel(x, p, norm1_g, w_in, conv_w, conv_b, gate_b, mlstm_norm_g, q_norm_g, k_norm_g, rpb,
           w_out, norm2_g, w_ff1, w_ff2, ple_norm_g, w_ple_gate, w_ple_up):
    b, seq, d = x.shape
    depth = w_in.shape[0]
    H, hd, L = N_MLSTM_HEADS, MLSTM_HEAD_DIM, CHUNK
    d_m = H * hd
    d_n = N_NA_HEADS * NA_HEAD_DIM
    t = b * seq
    rows = seq // GRID_W
    assert rows >= WIN_H

    bd = jnp.asarray(np.kron(np.eye(N_NA_HEADS), np.ones((NA_HEAD_DIM, NA_HEAD_DIM))), BF16)
    ri, ci = np.indices((L, L))
    tri = np.stack([ci <= ri, ci >= ri])
    tri3 = jnp.asarray(np.concatenate([tri, tri, tri], axis=1), BF16)
    eye2 = jnp.asarray(np.concatenate([ci == ri, ci == ri], axis=1), BF16)

    h = x.reshape(t, d)
    for i in range(depth):
        wi = w_in[i]
        wi = wi.astype(BF16)
        w_n = wi[:, 4 * d_m + N_GATE:]
        qg = (q_norm_g[i].reshape(1, d_n) * (NA_HEAD_DIM ** -0.5)).astype(F32)
        kg = k_norm_g[i].reshape(1, d_n).astype(F32)
        qk, mv, mo, nq, nk, nv, gates = _inproj(
            h, norm1_g[i].reshape(1, d), wi, w_n, gate_b[i].reshape(N_GATE, 1), bd, qg, kg,
            d_m=d_m, d_n=d_n, tm=1024)

        cw = conv_w[i].reshape(3, 2, H, hd).transpose(2, 1, 0, 3)
        cb = conv_b[i].reshape(2, H, 1, hd).transpose(1, 0, 2, 3)
        y_a = _mlstm(qk.reshape(b, seq, 2 * d_m), mv.reshape(b, seq, d_m), mo.reshape(b, seq, d_m),
                     gates.reshape(4, H, b, seq // L, L), cw, cb,
                     mlstm_norm_g[i].reshape(H, 1, hd), tri3, eye2)

        bias_tab = _natten_bias_table(rpb[i])
        y_b = _natten(nq.reshape(b, seq, d_n), nk.reshape(b, seq, d_n), nv.reshape(b, seq, d_n), bias_tab)

        wo = w_out[i].astype(BF16)
        h = _tail(h, y_a.reshape(t, d_m), y_b.reshape(t, d_n), p[i].reshape(t, -1),
                  wo[:d_m], wo[d_m:], norm2_g[i].reshape(1, d), w_ff1[i].astype(BF16),
                  w_ff2[i].astype(BF16), ple_norm_g[i].reshape(1, d), w_ple_gate[i].astype(BF16),
                  w_ple_up[i].astype(BF16), tm=1024, ff_chunk=1024)
    return h.reshape(b, seq, d)
```

```python
import functools

import jax
import jax.numpy as jnp
import numpy as np
from jax import lax
from jax.experimental import pallas as pl
from jax.experimental.pallas import tpu as pltpu

F32 = jnp.float32
BF16 = jnp.bfloat16

N_MLSTM_HEADS = 4
MLSTM_HEAD_DIM = 128
N_NA_HEADS = 8
NA_HEAD_DIM = 64
GRID_W = 64
WIN_H = 8
WIN_W = 16
CHUNK = 128
N_GATE = 4 * N_MLSTM_HEADS
RMS_EPS = 1e-6
NEG_BIG = -1e30
F32_BIG = 3e38
NA_BIAS_LANES = (2 * WIN_H - 2) * GRID_W
NA_ROWS_PER_STEP = 8
MLSTM_CHUNKS_PER_STEP = 8
MLSTM_CONV_UNROLL = 4
MLSTM_SCAN_UNROLL = 8

LANES = 128
GATE_PAD = LANES
MIB = 1024 * 1024
BF16_SUBLANES = 16
CAST_STEPS = 8

NT_DIMS = (((1,), (1,)), ((), ()))


def _const_spec(shape):
    return pl.BlockSpec(shape, lambda *_: (0,) * len(shape), pipeline_mode=pl.Buffered(1))


def _rms_scale(x):
    return lax.rsqrt(jnp.mean(x * x, axis=-1, keepdims=True) + RMS_EPS)


def _split3(x):
    hi = x.astype(BF16)
    r1 = x - hi.astype(F32)
    mid = r1.astype(BF16)
    lo = (r1 - mid.astype(F32)).astype(BF16)
    return hi, mid, lo


def _cast_body(*refs):
    n = len(refs) // 2
    for src, dst in zip(refs[:n], refs[n:]):
        dst[...] = src[...].astype(dst.dtype)


def _to_bf16(*weights):
    specs = [pl.BlockSpec((w.shape[0] // CAST_STEPS, w.shape[1]), lambda i: (i, 0)) for w in weights]
    assert all(w.shape[0] % (CAST_STEPS * BF16_SUBLANES) == 0 for w in weights)
    return pl.pallas_call(
        _cast_body,
        grid=(CAST_STEPS,),
        in_specs=specs,
        out_specs=specs,
        out_shape=[jax.ShapeDtypeStruct(w.shape, BF16) for w in weights],
        compiler_params=pltpu.CompilerParams(dimension_semantics=("arbitrary",),
                                             vmem_limit_bytes=40 * MIB),
        name="cast_weights",
    )(*weights)


def _inproj_body(x_ref, g_ref, wm_ref, wn_ref, wg_ref, gb_ref, bd_ref, qg_ref, kg_ref,
                 qk_ref, v_ref, o_ref, nq_ref, nk_ref, nv_ref, gate_ref, *, d_m, d_n):
    x = x_ref[...]
    u = (x * _rms_scale(x) * g_ref[...]).astype(BF16)

    def proj(w_ref, lo, hi):
        return jnp.dot(u, w_ref[:, lo:hi], preferred_element_type=F32)

    def head_norm(y, gain_ref):
        ss = jnp.dot((y * y).astype(BF16), bd_ref[...], preferred_element_type=F32)
        return y * lax.rsqrt(ss * (1.0 / NA_HEAD_DIM) + RMS_EPS) * gain_ref[...]

    qk_ref[...] = proj(wm_ref, 0, 2 * d_m).astype(BF16)
    v_ref[...] = proj(wm_ref, 2 * d_m, 3 * d_m).astype(BF16)
    o_ref[...] = proj(wm_ref, 3 * d_m, 4 * d_m).astype(BF16)
    nq_ref[...] = head_norm(proj(wn_ref, 0, d_n), qg_ref).astype(BF16)
    nk_ref[...] = head_norm(proj(wn_ref, d_n, 2 * d_n), kg_ref).astype(BF16)
    nv_ref[...] = proj(wn_ref, 2 * d_n, 3 * d_n).astype(BF16)
    gates_t = jnp.dot(u, wg_ref[...], preferred_element_type=F32).T
    gate_ref[...] = gates_t[:N_GATE, :] + gb_ref[...]


def _inproj(x2, g, w_all, w_n, gate_b, bd, qg, kg, *, d_m, d_n, tm):
    t, d = x2.shape
    assert (4 * d_m) % GATE_PAD == 0
    fixed = functools.partial(pl.BlockSpec, pipeline_mode=pl.Buffered(1))
    row = lambda width: pl.BlockSpec((tm, width), lambda i: (i, 0))
    out_shapes = (
        jax.ShapeDtypeStruct((t, 2 * d_m), BF16),
        jax.ShapeDtypeStruct((t, d_m), BF16),
        jax.ShapeDtypeStruct((t, d_m), BF16),
        jax.ShapeDtypeStruct((t, d_n), BF16),
        jax.ShapeDtypeStruct((t, d_n), BF16),
        jax.ShapeDtypeStruct((t, d_n), BF16),
        jax.ShapeDtypeStruct((N_GATE, t), F32),
    )
    return pl.pallas_call(
        functools.partial(_inproj_body, d_m=d_m, d_n=d_n),
        grid=(t // tm,),
        in_specs=[row(d), _const_spec((1, d)), fixed((d, 4 * d_m), lambda i: (0, 0)), _const_spec(w_n.shape),
                  fixed((d, GATE_PAD), lambda i: (0, 4 * d_m // GATE_PAD)), _const_spec((N_GATE, 1)),
                  _const_spec((d_n, d_n)), _const_spec((1, d_n)), _const_spec((1, d_n))],
        out_specs=(row(2 * d_m), row(d_m), row(d_m), row(d_n), row(d_n), row(d_n),
                   pl.BlockSpec((N_GATE, tm), lambda i: (0, i))),
        out_shape=out_shapes,
        compiler_params=pltpu.CompilerParams(dimension_semantics=("arbitrary",),
                                             vmem_limit_bytes=44 * MIB),
        name="inproj",
    )(x2, g, w_all, w_n, w_all, gate_b, bd, qg, kg)


def _log_sigmoid(x):
    return jnp.minimum(x, 0.0) - jnp.log1p(jnp.exp(-jnp.abs(x)))


def _sigmoid(x):
    return 1.0 / (1.0 + jnp.exp(-x))


def _mlstm_body(q_ref, k_ref, v_ref, o_ref, gate_ref, cw_ref, cb_ref, ng_ref, tri3_ref, eye2_ref,
                out_ref, qs_ref, kt_ref, cs_ref, ccur_ref, brow_ref, crow_ref, cmax_ref, wt_ref, bl_ref,
                ml_ref, ms_ref, s_ref, einv_ref, *, seq):
    L = CHUNK
    d = MLSTM_HEAD_DIM
    nc = seq // L

    row_id = lax.broadcasted_iota(jnp.int32, (L, d), 0)
    col_id = lax.broadcasted_iota(jnp.int32, (L, d), 1)
    pos_id = lax.broadcasted_iota(jnp.int32, (nc, L), 1)
    ones_blk = jnp.ones((L, d), BF16)

    def conv_silu(src_ref, c, s0, w, b):
        x = src_ref[pl.ds(s0, L), :].astype(F32)
        p0 = pl.multiple_of(jnp.maximum(s0 - 16, 0), 16)
        n0 = pl.multiple_of(jnp.minimum(s0 + L, seq - 16), 16)
        prev_row = src_ref[pl.ds(p0, 16), :][15:16, :].astype(F32)
        next_row = src_ref[pl.ds(n0, 16), :][0:1, :].astype(F32)
        prev_row = jnp.where(c > 0, prev_row, 0.0)
        next_row = jnp.where(c < nc - 1, next_row, 0.0)
        x_prev = jnp.where(row_id == 0, prev_row, pltpu.roll(x, 1, 0))
        x_next = jnp.where(row_id == L - 1, next_row, pltpu.roll(x, L - 1, 0))
        y = w[0:1, :] * x_prev + w[1:2, :] * x + w[2:3, :] * x_next + b
        return y * _sigmoid(y)

    def conv_step(c, carry):
        s0 = pl.multiple_of(c * L, L)
        qs_ref[pl.ds(s0, L), :] = conv_silu(q_ref, c, s0, cw_ref[0], cb_ref[0]).astype(BF16)
        kk = conv_silu(k_ref, c, s0, cw_ref[1], cb_ref[1]) * (d ** -0.5)
        kt_ref[:, pl.ds(s0, L)] = kk.T.astype(BF16)
        return carry

    lax.fori_loop(0, nc, conv_step, 0, unroll=MLSTM_CONV_UNROLL)

    for dirn in (0, 1):
        i_g = gate_ref[2 * dirn]
        f_log = _log_sigmoid(gate_ref[2 * dirn + 1])
        f_cat = jnp.concatenate(_split3(f_log), axis=1)
        brow = jnp.dot(f_cat, tri3_ref[1 - dirn], preferred_element_type=F32)
        b_last = brow[:, L - 1:L] if dirn == 0 else brow[:, 0:1]
        a_row = i_g + b_last - brow
        a_max = jnp.max(a_row, axis=1, keepdims=True)
        crow = i_g - brow
        cmax = crow
        for sh in [1 << e for e in range(L.bit_length() - 1)]:
            if dirn == 0:
                cmax = jnp.maximum(cmax, jnp.where(pos_id >= sh, pltpu.roll(cmax, sh, 1), -jnp.inf))
            else:
                cmax = jnp.maximum(cmax, jnp.where(pos_id < L - sh, pltpu.roll(cmax, L - sh, 1), -jnp.inf))
        brow_ref[dirn] = brow
        crow_ref[dirn] = crow
        cmax_ref[dirn] = cmax
        wt_ref[dirn] = jnp.exp(a_row - a_max)
        bl_ref[dirn] = jnp.broadcast_to(b_last, (nc, L))
        ml_ref[dirn] = jnp.broadcast_to(a_max, (nc, L))

    def v_aug(s0):
        return jnp.concatenate([v_ref[pl.ds(s0, L), :], ones_blk], axis=1)

    ccur_ref[...] = jnp.zeros_like(ccur_ref)

    def scan_step(i, carry):
        new = []
        for dirn, c, m in ((0, i, carry[0]), (1, nc - 1 - i, carry[1])):
            s0 = pl.multiple_of(c * L, L)
            kw = (kt_ref[:, pl.ds(s0, L)].astype(F32) * wt_ref[dirn, pl.ds(c, 1), :]).astype(BF16)
            k_loc = jnp.dot(kw, v_aug(s0), preferred_element_type=F32)
            state = ccur_ref[dirn]
            cs_ref[dirn, c] = state.astype(BF16)
            ms_ref[dirn, pl.ds(c, 1), :] = m
            a_prev = m + bl_ref[dirn, pl.ds(c, 1), :]
            a_max = ml_ref[dirn, pl.ds(c, 1), :]
            m_new = jnp.maximum(a_prev, a_max)
            w_prev = jnp.exp(a_prev - m_new)[:, 0:1]
            w_loc = jnp.exp(a_max - m_new)[:, 0:1]
            ccur_ref[dirn] = w_prev * state + w_loc * k_loc
            new.append(m_new)
        return tuple(new)

    m0 = jnp.zeros((1, L), F32)
    lax.fori_loop(0, nc, scan_step, (m0, m0), unroll=MLSTM_SCAN_UNROLL)

    for dirn in (0, 1):
        mu_all = jnp.maximum(ms_ref[dirn], cmax_ref[dirn])
        cmax_ref[dirn] = mu_all
        brow_ref[dirn] = jnp.minimum(jnp.exp(-(brow_ref[dirn] + mu_all)), F32_BIG)
    lower = col_id <= row_id
    upper = col_id >= row_id

    def weights_stage(c, slot):
        s0 = pl.multiple_of(c * L, L)
        q = qs_ref[pl.ds(s0, L), :]
        qk = jnp.dot(q, kt_ref[:, pl.ds(s0, L)], preferred_element_type=F32)
        for dirn, mask in ((0, lower), (1, upper)):
            rows = []
            for stat_ref in (cmax_ref, brow_ref):
                hi, mid, _ = _split3(stat_ref[dirn, pl.ds(c, 1), :])
                rows.append(jnp.broadcast_to(jnp.concatenate([hi, mid], axis=1), (L, 2 * L)))
            col = lax.dot_general(eye2_ref[...], jnp.concatenate(rows, axis=0), NT_DIMS,
                                  preferred_element_type=F32)
            mu = col[:, :L]
            p = jnp.exp(jnp.where(mask, crow_ref[dirn, pl.ds(c, 1), :] - mu, -jnp.inf))
            q_inter = q.astype(F32) * jnp.exp(ms_ref[dirn, pl.ds(c, 1), :] - mu)
            s_ref[slot, dirn] = jnp.concatenate([qk * p, q_inter], axis=1).astype(BF16)
            einv_ref[slot, dirn] = col[:, L:]

    def output_stage(c, slot):
        s0 = pl.multiple_of(c * L, L)
        vaug = v_aug(s0)
        hsum = None
        for dirn in (0, 1):
            nd = jnp.dot(s_ref[slot, dirn], jnp.concatenate([vaug, cs_ref[dirn, c]], axis=0),
                         preferred_element_type=F32)
            h = nd[:, :d] / jnp.maximum(jnp.abs(nd[:, d:]), einv_ref[slot, dirn])
            hsum = h if hsum is None else hsum + h
        y = hsum * _rms_scale(hsum) * ng_ref[...]
        y = y * _sigmoid(o_ref[pl.ds(s0, L), :].astype(F32))
        out_ref[pl.ds(s0, L), :] = y.astype(out_ref.dtype)

    G = MLSTM_CHUNKS_PER_STEP
    for i in range(G):
        weights_stage(i, i)

    def out_step(j, carry):
        cur = (j % 2) * G
        nxt = G - cur
        for i in range(G):
            output_stage(j * G + i, cur + i)
        for i in range(G):
            weights_stage(jnp.minimum((j + 1) * G + i, nc - 1), nxt + i)
        return carry

    lax.fori_loop(0, nc // G, out_step, 0)


def _mlstm(qk, v, o, gates, conv_w, conv_b, norm_g, tri3, eye2):
    b, seq, _ = v.shape
    H, d, L = N_MLSTM_HEADS, MLSTM_HEAD_DIM, CHUNK
    nc = seq // L
    assert nc % MLSTM_SCAN_UNROLL == 0 and nc % MLSTM_CONV_UNROLL == 0 and L == LANES
    assert nc % MLSTM_CHUNKS_PER_STEP == 0
    col = lambda off: pl.BlockSpec((None, seq, d), lambda bi, hi: (bi, 0, hi + off))
    stat = pltpu.VMEM((2, nc, L), F32)
    slots = 2 * MLSTM_CHUNKS_PER_STEP
    return pl.pallas_call(
        functools.partial(_mlstm_body, seq=seq),
        grid=(b, H),
        in_specs=[
            col(0), col(H), col(0), col(0),
            pl.BlockSpec((4, None, None, nc, L), lambda bi, hi: (0, hi, bi, 0, 0)),
            pl.BlockSpec((None, 2, 3, d), lambda bi, hi: (hi, 0, 0, 0)),
            pl.BlockSpec((None, 2, 1, d), lambda bi, hi: (hi, 0, 0, 0)),
            pl.BlockSpec((None, 1, d), lambda bi, hi: (hi, 0, 0)),
            _const_spec(tri3.shape), _const_spec(eye2.shape),
        ],
        out_specs=pl.BlockSpec((None, seq, d), lambda bi, hi: (bi, 0, hi)),
        out_shape=jax.ShapeDtypeStruct((b, seq, H * d), BF16),
        scratch_shapes=[
            pltpu.VMEM((seq, d), BF16),
            pltpu.VMEM((d, seq), BF16),
            pltpu.VMEM((2, nc, d, 2 * d), BF16),
            pltpu.VMEM((2, d, 2 * d), F32),
            stat, stat, stat,
            stat, stat, stat,
            stat,
            pltpu.VMEM((slots, 2, L, L + d), BF16),
            pltpu.VMEM((slots, 2, L, d), F32),
        ],
        compiler_params=pltpu.CompilerParams(dimension_semantics=("arbitrary", "arbitrary"),
                                             vmem_limit_bytes=52 * MIB),
        name="mlstm",
    )(qk, qk, v, o, gates, conv_w, conv_b, norm_g, tri3, eye2)


def _natten_body(q_ref, k_ref, v_ref, bias_ref, out_ref, pa_ref, pb_ref, *, rows):
    hd = NA_HEAD_DIM
    win = WIN_H * GRID_W
    lane = lax.broadcasted_iota(jnp.int32, (GRID_W, 2 * hd), 1)
    first = lane < hd
    ones_blk = jnp.ones((win, 2 * hd), BF16)

    def window_start(r):
        return jnp.clip(r - WIN_H // 2, 0, rows - WIN_H)

    def prob_stage(r, p_ref, slot):
        rs = window_start(r)
        q = q_ref[pl.ds(pl.multiple_of(r * GRID_W, GRID_W), GRID_W), :]
        zero = jnp.zeros_like(q)
        qs = jnp.concatenate([jnp.where(first, q, zero), jnp.where(first, zero, q)], axis=0)
        kwin = k_ref[pl.ds(pl.multiple_of(rs * GRID_W, GRID_W), win), :]
        off = rs - r + (WIN_H - 1)
        bias = bias_ref[off & 1, :, pl.ds(pl.multiple_of((off >> 1) * LANES, LANES), win)]
        s = lax.dot_general(qs, kwin, NT_DIMS, preferred_element_type=F32) + bias
        p_ref[slot] = jnp.exp(s - jnp.max(s, axis=1, keepdims=True)).astype(BF16)

    def output_stage(r, p_ref, slot):
        rs = window_start(r)
        vwin = v_ref[pl.ds(pl.multiple_of(rs * GRID_W, GRID_W), win), :]
        o = jnp.dot(p_ref[slot], jnp.concatenate([vwin, ones_blk], axis=1), preferred_element_type=F32)
        o = o[:, :2 * hd] / o[:, 2 * hd:]
        out = jnp.where(first, o[:GRID_W], o[GRID_W:])
        out_ref[pl.ds(pl.multiple_of(r * GRID_W, GRID_W), GRID_W), :] = out.astype(out_ref.dtype)

    R = NA_ROWS_PER_STEP
    n_groups = rows // R

    def produce(g, p_ref):
        for i in range(R):
            prob_stage(jnp.minimum(g * R + i, rows - 1), p_ref, i)

    def consume(g, p_ref):
        for i in range(R):
            output_stage(g * R + i, p_ref, i)

    produce(0, pa_ref)

    def group_pair(j, carry):
        g = 2 * j
        produce(g + 1, pb_ref)
        consume(g, pa_ref)
        produce(g + 2, pa_ref)
        consume(g + 1, pb_ref)
        return carry

    lax.fori_loop(0, n_groups // 2, group_pair, 0)


def _natten(nq, nk, nv, bias_tab):
    b, seq, d_n = nq.shape
    pairs = N_NA_HEADS // 2
    width = 2 * NA_HEAD_DIM
    rows = seq // GRID_W
    assert rows % (2 * NA_ROWS_PER_STEP) == 0
    col = pl.BlockSpec((None, seq, width), lambda bi, pi: (bi, 0, pi))
    return pl.pallas_call(
        functools.partial(_natten_body, rows=rows),
        grid=(b, pairs),
        in_specs=[col, col, col,
                  pl.BlockSpec((None, 2, 2 * GRID_W, NA_BIAS_LANES), lambda bi, pi: (pi, 0, 0, 0))],
        out_specs=col,
        out_shape=jax.ShapeDtypeStruct((b, seq, d_n), BF16),
        scratch_shapes=[pltpu.VMEM((NA_ROWS_PER_STEP, 2 * GRID_W, WIN_H * GRID_W), BF16)] * 2,
        compiler_params=pltpu.CompilerParams(dimension_semantics=("arbitrary", "arbitrary"),
                                             vmem_limit_bytes=40 * MIB),
        name="natten",
    )(nq, nk, nv, bias_tab)


def _natten_bias_table(rpb):
    c = np.arange(GRID_W)
    cs = np.clip(c - WIN_W // 2, 0, GRID_W - WIN_W)
    cp = np.arange(GRID_W)
    valid = (cp[None, :] >= cs[:, None]) & (cp[None, :] < cs[:, None] + WIN_W)
    rel = cp[None, None, :] - c[None, :, None] + (WIN_W - 1)
    onehot = (rel == np.arange(2 * WIN_W - 1)[:, None, None]).astype(np.float32)
    n_rel = 2 * WIN_H - 1
    rpb4 = rpb.astype(F32).reshape(N_NA_HEADS // 2, 2, n_rel, 2 * WIN_W - 1)
    rpb_par = jnp.stack([rpb4[:, :, :n_rel - 1], rpb4[:, :, 1:]], axis=1)
    tab = jnp.einsum('pshrd,dcq->pshcrq', rpb_par, jnp.asarray(onehot), precision=lax.Precision.HIGHEST)
    mask = np.where(valid, 0.0, NEG_BIG).astype(np.float32)[:, None, :]
    tab = tab + jnp.asarray(mask)
    return tab.reshape(N_NA_HEADS // 2, 2, 2 * GRID_W, NA_BIAS_LANES)


def _tail_body(x_ref, ya_ref, yb_ref, p_ref, woa_ref, wob_ref, g2_ref, w1_ref, w2_ref,
               g3_ref, wg_ref, wu_ref, out_ref, *, ff_chunk):
    d_ff = w1_ref.shape[1]
    h = (x_ref[...]
         + jnp.dot(ya_ref[...], woa_ref[...], preferred_element_type=F32)
         + jnp.dot(yb_ref[...], wob_ref[...], preferred_element_type=F32))
    u = (h * _rms_scale(h) * g2_ref[...]).astype(BF16)
    out_ref[...] = h
    for j in range(d_ff // ff_chunk):
        z = jnp.dot(u, w1_ref[:, j * ff_chunk:(j + 1) * ff_chunk], preferred_element_type=F32)
        z = jnp.maximum(z, 0.0)
        out_ref[...] += jnp.dot((z * z).astype(BF16), w2_ref[j * ff_chunk:(j + 1) * ff_chunk, :],
                                preferred_element_type=F32)
    h = out_ref[...]
    u = (h * _rms_scale(h) * g3_ref[...]).astype(BF16)
    gate = _sigmoid(jnp.dot(u, wg_ref[...], preferred_element_type=F32))
    up = jnp.dot(p_ref[...].astype(BF16), wu_ref[...], preferred_element_type=F32)
    out_ref[...] = h + gate * up


def _tail(x2, ya, yb, p2, wo, g2, w1, w2, g3, wg, wu, *, tm, ff_chunk):
    t, d = x2.shape
    d_mix = wo.shape[0]
    assert ya.shape[1] == yb.shape[1] == d_mix // 2 and t % tm == 0
    row = lambda width: pl.BlockSpec((tm, width), lambda i: (i, 0))
    half = lambda k: pl.BlockSpec((d_mix // 2, d), lambda i: (k, 0), pipeline_mode=pl.Buffered(1))
    consts = [g2, w1, w2, g3, wg, wu]
    return pl.pallas_call(
        functools.partial(_tail_body, ff_chunk=ff_chunk),
        grid=(t // tm,),
        in_specs=[row(d), row(ya.shape[1]), row(yb.shape[1]), row(p2.shape[1]), half(0), half(1)]
                 + [_const_spec(c.shape) for c in consts],
        out_specs=row(d),
        out_shape=jax.ShapeDtypeStruct((t, d), F32),
        compiler_params=pltpu.CompilerParams(dimension_semantics=("arbitrary",),
                                             vmem_limit_bytes=52 * MIB),
        name="tail",
    )(x2, ya, yb, p2, wo, wo, *consts)


def kernel(x, p, norm1_g, w_in, conv_w, conv_b, gate_b, mlstm_norm_g, q_norm_g, k_norm_g, rpb,
           w_out, norm2_g, w_ff1, w_ff2, ple_norm_g, w_ple_gate, w_ple_up):
    b, seq, d = x.shape
    depth = w_in.shape[0]
    H, hd, L = N_MLSTM_HEADS, MLSTM_HEAD_DIM, CHUNK
    d_m = H * hd
    d_n = N_NA_HEADS * NA_HEAD_DIM
    t = b * seq
    rows = seq // GRID_W
    assert rows >= WIN_H

    bd = jnp.asarray(np.kron(np.eye(N_NA_HEADS), np.ones((NA_HEAD_DIM, NA_HEAD_DIM))), BF16)
    ri, ci = np.indices((L, L))
    tri = np.stack([ci <= ri, ci >= ri])
    tri3 = jnp.asarray(np.concatenate([tri, tri, tri], axis=1), BF16)
    eye2 = jnp.asarray(np.concatenate([ci == ri, ci == ri], axis=1), BF16)

    h = x.reshape(t, d)
    for i in range(depth):
        wi, wo, w1, w2, wg, wu = _to_bf16(w_in[i], w_out[i], w_ff1[i], w_ff2[i], w_ple_gate[i], w_ple_up[i])
        w_n = wi[:, 4 * d_m + N_GATE:]
        qg = (q_norm_g[i].reshape(1, d_n) * (NA_HEAD_DIM ** -0.5)).astype(F32)
        kg = k_norm_g[i].reshape(1, d_n).astype(F32)
        qk, mv, mo, nq, nk, nv, gates = _inproj(
            h, norm1_g[i].reshape(1, d), wi, w_n, gate_b[i].reshape(N_GATE, 1), bd, qg, kg,
            d_m=d_m, d_n=d_n, tm=1024)

        cw = conv_w[i].reshape(3, 2, H, hd).transpose(2, 1, 0, 3)
        cb = conv_b[i].reshape(2, H, 1, hd).transpose(1, 0, 2, 3)
        y_a = _mlstm(qk.reshape(b, seq, 2 * d_m), mv.reshape(b, seq, d_m), mo.reshape(b, seq, d_m),
                     gates.reshape(4, H, b, seq // L, L), cw, cb,
                     mlstm_norm_g[i].reshape(H, 1, hd), tri3, eye2)

        bias_tab = _natten_bias_table(rpb[i])
        y_b = _natten(nq.reshape(b, seq, d_n), nk.reshape(b, seq, d_n), nv.reshape(b, seq, d_n), bias_tab)

        h = _tail(h, y_a.reshape(t, d_m), y_b.reshape(t, d_n), p[i].reshape(t, -1),
                  wo, norm2_g[i].reshape(1, d), w1, w2, ple_norm_g[i].reshape(1, d), wg, wu,
                  tm=1024, ff_chunk=1024)
    return h.reshape(b, seq, d)
```

```python
import functools

import jax
import jax.numpy as jnp
import numpy as np
from jax import lax
from jax.experimental import pallas as pl
from jax.experimental.pallas import tpu as pltpu

F32 = jnp.float32
BF16 = jnp.bfloat16

N_MLSTM_HEADS = 4
MLSTM_HEAD_DIM = 128
N_NA_HEADS = 8
NA_HEAD_DIM = 64
GRID_W = 64
WIN_H = 8
WIN_W = 16
CHUNK = 128
N_GATE = 4 * N_MLSTM_HEADS
RMS_EPS = 1e-6
NEG_BIG = -1e30
F32_BIG = 3e38
NA_BIAS_LANES = (2 * WIN_H - 2) * GRID_W
NA_ROWS_PER_STEP = 8
MLSTM_CHUNKS_PER_STEP = 8
MLSTM_CONV_UNROLL = 4
MLSTM_SCAN_UNROLL = 8

LANES = 128
GATE_PAD = LANES
MIB = 1024 * 1024

NT_DIMS = (((1,), (1,)), ((), ()))


def _const_spec(shape):
    return pl.BlockSpec(shape, lambda *_: (0,) * len(shape), pipeline_mode=pl.Buffered(1))


def _rms_scale(x):
    return lax.rsqrt(jnp.mean(x * x, axis=-1, keepdims=True) + RMS_EPS)


def _split3(x):
    hi = x.astype(BF16)
    r1 = x - hi.astype(F32)
    mid = r1.astype(BF16)
    lo = (r1 - mid.astype(F32)).astype(BF16)
    return hi, mid, lo


def _inproj_body(x_ref, g_ref, wm_ref, wn_ref, wg_ref, gb_ref, bd_ref, qg_ref, kg_ref,
                 qk_ref, v_ref, o_ref, nq_ref, nk_ref, nv_ref, gate_ref, *, d_m, d_n):
    x = x_ref[...]
    u = (x * _rms_scale(x) * g_ref[...]).astype(BF16)

    def proj(w_ref, lo, hi):
        return jnp.dot(u, w_ref[:, lo:hi], preferred_element_type=F32)

    def head_norm(y, gain_ref):
        ss = jnp.dot((y * y).astype(BF16), bd_ref[...], preferred_element_type=F32)
        return y * lax.rsqrt(ss * (1.0 / NA_HEAD_DIM) + RMS_EPS) * gain_ref[...]

    qk_ref[...] = proj(wm_ref, 0, 2 * d_m).astype(BF16)
    v_ref[...] = proj(wm_ref, 2 * d_m, 3 * d_m).astype(BF16)
    o_ref[...] = proj(wm_ref, 3 * d_m, 4 * d_m).astype(BF16)
    nq_ref[...] = head_norm(proj(wn_ref, 0, d_n), qg_ref).astype(BF16)
    nk_ref[...] = head_norm(proj(wn_ref, d_n, 2 * d_n), kg_ref).astype(BF16)
    nv_ref[...] = proj(wn_ref, 2 * d_n, 3 * d_n).astype(BF16)
    gates_t = jnp.dot(u, wg_ref[...], preferred_element_type=F32).T
    gate_ref[...] = gates_t[:N_GATE, :] + gb_ref[...]


def _inproj(x2, g, w_all, w_n, gate_b, bd, qg, kg, *, d_m, d_n, tm):
    t, d = x2.shape
    assert (4 * d_m) % GATE_PAD == 0
    fixed = functools.partial(pl.BlockSpec, pipeline_mode=pl.Buffered(1))
    row = lambda width: pl.BlockSpec((tm, width), lambda i: (i, 0))
    out_shapes = (
        jax.ShapeDtypeStruct((t, 2 * d_m), BF16),
        jax.ShapeDtypeStruct((t, d_m), BF16),
        jax.ShapeDtypeStruct((t, d_m), BF16),
        jax.ShapeDtypeStruct((t, d_n), BF16),
        jax.ShapeDtypeStruct((t, d_n), BF16),
        jax.ShapeDtypeStruct((t, d_n), BF16),
        jax.ShapeDtypeStruct((N_GATE, t), F32),
    )
    return pl.pallas_call(
        functools.partial(_inproj_body, d_m=d_m, d_n=d_n),
        grid=(t // tm,),
        in_specs=[row(d), _const_spec((1, d)), fixed((d, 4 * d_m), lambda i: (0, 0)), _const_spec(w_n.shape),
                  fixed((d, GATE_PAD), lambda i: (0, 4 * d_m // GATE_PAD)), _const_spec((N_GATE, 1)),
                  _const_spec((d_n, d_n)), _const_spec((1, d_n)), _const_spec((1, d_n))],
        out_specs=(row(2 * d_m), row(d_m), row(d_m), row(d_n), row(d_n), row(d_n),
                   pl.BlockSpec((N_GATE, tm), lambda i: (0, i))),
        out_shape=out_shapes,
        compiler_params=pltpu.CompilerParams(dimension_semantics=("arbitrary",),
                                             vmem_limit_bytes=44 * MIB),
        name="inproj",
    )(x2, g, w_all, w_n, w_all, gate_b, bd, qg, kg)


def _log_sigmoid(x):
    return jnp.minimum(x, 0.0) - jnp.log1p(jnp.exp(-jnp.abs(x)))


def _sigmoid(x):
    return 1.0 / (1.0 + jnp.exp(-x))


def _mlstm_body(q_ref, k_ref, v_ref, o_ref, gate_ref, cw_ref, cb_ref, ng_ref, tri3_ref, eye2_ref,
                out_ref, qs_ref, kt_ref, cs_ref, ccur_ref, brow_ref, crow_ref, cmax_ref, wt_ref, bl_ref,
                ml_ref, ms_ref, s_ref, einv_ref, *, seq):
    L = CHUNK
    d = MLSTM_HEAD_DIM
    nc = seq // L

    row_id = lax.broadcasted_iota(jnp.int32, (L, d), 0)
    col_id = lax.broadcasted_iota(jnp.int32, (L, d), 1)
    pos_id = lax.broadcasted_iota(jnp.int32, (nc, L), 1)
    ones_blk = jnp.ones((L, d), BF16)

    def conv_silu(src_ref, c, s0, w, b):
        x = src_ref[pl.ds(s0, L), :].astype(F32)
        p0 = pl.multiple_of(jnp.maximum(s0 - 16, 0), 16)
        n0 = pl.multiple_of(jnp.minimum(s0 + L, seq - 16), 16)
        prev_row = src_ref[pl.ds(p0, 16), :][15:16, :].astype(F32)
        next_row = src_ref[pl.ds(n0, 16), :][0:1, :].astype(F32)
        prev_row = jnp.where(c > 0, prev_row, 0.0)
        next_row = jnp.where(c < nc - 1, next_row, 0.0)
        x_prev = jnp.where(row_id == 0, prev_row, pltpu.roll(x, 1, 0))
        x_next = jnp.where(row_id == L - 1, next_row, pltpu.roll(x, L - 1, 0))
        y = w[0:1, :] * x_prev + w[1:2, :] * x + w[2:3, :] * x_next + b
        return y * _sigmoid(y)

    def conv_step(c, carry):
        s0 = pl.multiple_of(c * L, L)
        qs_ref[pl.ds(s0, L), :] = conv_silu(q_ref, c, s0, cw_ref[0], cb_ref[0]).astype(BF16)
        kk = conv_silu(k_ref, c, s0, cw_ref[1], cb_ref[1]) * (d ** -0.5)
        kt_ref[:, pl.ds(s0, L)] = kk.T.astype(BF16)
        return carry

    lax.fori_loop(0, nc, conv_step, 0, unroll=MLSTM_CONV_UNROLL)

    for dirn in (0, 1):
        i_g = gate_ref[2 * dirn]
        f_log = _log_sigmoid(gate_ref[2 * dirn + 1])
        f_cat = jnp.concatenate(_split3(f_log), axis=1)
        brow = jnp.dot(f_cat, tri3_ref[1 - dirn], preferred_element_type=F32)
        b_last = brow[:, L - 1:L] if dirn == 0 else brow[:, 0:1]
        a_row = i_g + b_last - brow
        a_max = jnp.max(a_row, axis=1, keepdims=True)
        crow = i_g - brow
        cmax = crow
        for sh in [1 << e for e in range(L.bit_length() - 1)]:
            if dirn == 0:
                cmax = jnp.maximum(cmax, jnp.where(pos_id >= sh, pltpu.roll(cmax, sh, 1), -jnp.inf))
            else:
                cmax = jnp.maximum(cmax, jnp.where(pos_id < L - sh, pltpu.roll(cmax, L - sh, 1), -jnp.inf))
        brow_ref[dirn] = brow
        crow_ref[dirn] = crow
        cmax_ref[dirn] = cmax
        wt_ref[dirn] = jnp.exp(a_row - a_max)
        bl_ref[dirn] = jnp.broadcast_to(b_last, (nc, L))
        ml_ref[dirn] = jnp.broadcast_to(a_max, (nc, L))

    def v_aug(s0):
        return jnp.concatenate([v_ref[pl.ds(s0, L), :], ones_blk], axis=1)

    ccur_ref[...] = jnp.zeros_like(ccur_ref)

    def scan_step(i, carry):
        new = []
        for dirn, c, m in ((0, i, carry[0]), (1, nc - 1 - i, carry[1])):
            s0 = pl.multiple_of(c * L, L)
            kw = (kt_ref[:, pl.ds(s0, L)].astype(F32) * wt_ref[dirn, pl.ds(c, 1), :]).astype(BF16)
            k_loc = jnp.dot(kw, v_aug(s0), preferred_element_type=F32)
            state = ccur_ref[dirn]
            cs_ref[dirn, c] = state.astype(BF16)
            ms_ref[dirn, pl.ds(c, 1), :] = m
            a_prev = m + bl_ref[dirn, pl.ds(c, 1), :]
            a_max = ml_ref[dirn, pl.ds(c, 1), :]
            m_new = jnp.maximum(a_prev, a_max)
            w_prev = jnp.exp(a_prev - m_new)[:, 0:1]
            w_loc = jnp.exp(a_max - m_new)[:, 0:1]
            ccur_ref[dirn] = w_prev * state + w_loc * k_loc
            new.append(m_new)
        return tuple(new)

    m0 = jnp.zeros((1, L), F32)
    lax.fori_loop(0, nc, scan_step, (m0, m0), unroll=MLSTM_SCAN_UNROLL)

    for dirn in (0, 1):
        mu_all = jnp.maximum(ms_ref[dirn], cmax_ref[dirn])
        cmax_ref[dirn] = mu_all
        brow_ref[dirn] = jnp.minimum(jnp.exp(-(brow_ref[dirn] + mu_all)), F32_BIG)
    lower = col_id <= row_id
    upper = col_id >= row_id

    def weights_stage(c, slot):
        s0 = pl.multiple_of(c * L, L)
        q = qs_ref[pl.ds(s0, L), :]
        qk = jnp.dot(q, kt_ref[:, pl.ds(s0, L)], preferred_element_type=F32)
        for dirn, mask in ((0, lower), (1, upper)):
            rows = []
            for stat_ref in (cmax_ref, brow_ref):
                hi, mid, _ = _split3(stat_ref[dirn, pl.ds(c, 1), :])
                rows.append(jnp.broadcast_to(jnp.concatenate([hi, mid], axis=1), (L, 2 * L)))
            col = lax.dot_general(eye2_ref[...], jnp.concatenate(rows, axis=0), NT_DIMS,
                                  preferred_element_type=F32)
            mu = col[:, :L]
            p = jnp.exp(jnp.where(mask, crow_ref[dirn, pl.ds(c, 1), :] - mu, -jnp.inf))
            q_inter = q.astype(F32) * jnp.exp(ms_ref[dirn, pl.ds(c, 1), :] - mu)
            s_ref[slot, dirn] = jnp.concatenate([qk * p, q_inter], axis=1).astype(BF16)
            einv_ref[slot, dirn] = col[:, L:]

    def output_stage(c, slot):
        s0 = pl.multiple_of(c * L, L)
        vaug = v_aug(s0)
        hsum = None
        for dirn in (0, 1):
            nd = jnp.dot(s_ref[slot, dirn], jnp.concatenate([vaug, cs_ref[dirn, c]], axis=0),
                         preferred_element_type=F32)
            h = nd[:, :d] / jnp.maximum(jnp.abs(nd[:, d:]), einv_ref[slot, dirn])
            hsum = h if hsum is None else hsum + h
        y = hsum * _rms_scale(hsum) * ng_ref[...]
        y = y * _sigmoid(o_ref[pl.ds(s0, L), :].astype(F32))
        out_ref[pl.ds(s0, L), :] = y.astype(out_ref.dtype)

    G = MLSTM_CHUNKS_PER_STEP
    for i in range(G):
        weights_stage(i, i)

    def out_step(j, carry):
        cur = (j % 2) * G
        nxt = G - cur
        for i in range(G):
            output_stage(j * G + i, cur + i)
        for i in range(G):
            weights_stage(jnp.minimum((j + 1) * G + i, nc - 1), nxt + i)
        return carry

    lax.fori_loop(0, nc // G, out_step, 0)


def _mlstm(qk, v, o, gates, conv_w, conv_b, norm_g, tri3, eye2):
    b, seq, _ = v.shape
    H, d, L = N_MLSTM_HEADS, MLSTM_HEAD_DIM, CHUNK
    nc = seq // L
    assert nc % MLSTM_SCAN_UNROLL == 0 and nc % MLSTM_CONV_UNROLL == 0 and L == LANES
    assert nc % MLSTM_CHUNKS_PER_STEP == 0
    col = lambda off: pl.BlockSpec((None, seq, d), lambda bi, hi: (bi, 0, hi + off))
    stat = pltpu.VMEM((2, nc, L), F32)
    slots = 2 * MLSTM_CHUNKS_PER_STEP
    return pl.pallas_call(
        functools.partial(_mlstm_body, seq=seq),
        grid=(b, H),
        in_specs=[
            col(0), col(H), col(0), col(0),
            pl.BlockSpec((4, None, None, nc, L), lambda bi, hi: (0, hi, bi, 0, 0)),
            pl.BlockSpec((None, 2, 3, d), lambda bi, hi: (hi, 0, 0, 0)),
            pl.BlockSpec((None, 2, 1, d), lambda bi, hi: (hi, 0, 0, 0)),
            pl.BlockSpec((None, 1, d), lambda bi, hi: (hi, 0, 0)),
            _const_spec(tri3.shape), _const_spec(eye2.shape),
        ],
        out_specs=pl.BlockSpec((None, seq, d), lambda bi, hi: (bi, 0, hi)),
        out_shape=jax.ShapeDtypeStruct((b, seq, H * d), BF16),
        scratch_shapes=[
            pltpu.VMEM((seq, d), BF16),
            pltpu.VMEM((d, seq), BF16),
            pltpu.VMEM((2, nc, d, 2 * d), BF16),
            pltpu.VMEM((2, d, 2 * d), F32),
            stat, stat, stat,
            stat, stat, stat,
            stat,
            pltpu.VMEM((slots, 2, L, L + d), BF16),
            pltpu.VMEM((slots, 2, L, d), F32),
        ],
        compiler_params=pltpu.CompilerParams(dimension_semantics=("arbitrary", "arbitrary"),
                                             vmem_limit_bytes=52 * MIB),
        name="mlstm",
    )(qk, qk, v, o, gates, conv_w, conv_b, norm_g, tri3, eye2)


def _natten_body(q_ref, k_ref, v_ref, bias_ref, out_ref, pa_ref, pb_ref, *, rows):
    hd = NA_HEAD_DIM
    win = WIN_H * GRID_W
    lane = lax.broadcasted_iota(jnp.int32, (GRID_W, 2 * hd), 1)
    first = lane < hd
    ones_blk = jnp.ones((win, 2 * hd), BF16)

    def window_start(r):
        return jnp.clip(r - WIN_H // 2, 0, rows - WIN_H)

    def prob_stage(r, p_ref, slot):
        rs = window_start(r)
        q = q_ref[pl.ds(pl.multiple_of(r * GRID_W, GRID_W), GRID_W), :]
        zero = jnp.zeros_like(q)
        qs = jnp.concatenate([jnp.where(first, q, zero), jnp.where(first, zero, q)], axis=0)
        kwin = k_ref[pl.ds(pl.multiple_of(rs * GRID_W, GRID_W), win), :]
        off = rs - r + (WIN_H - 1)
        bias = bias_ref[off & 1, :, pl.ds(pl.multiple_of((off >> 1) * LANES, LANES), win)]
        s = lax.dot_general(qs, kwin, NT_DIMS, preferred_element_type=F32) + bias
        p_ref[slot] = jnp.exp(s - jnp.max(s, axis=1, keepdims=True)).astype(BF16)

    def output_stage(r, p_ref, slot):
        rs = window_start(r)
        vwin = v_ref[pl.ds(pl.multiple_of(rs * GRID_W, GRID_W), win), :]
        o = jnp.dot(p_ref[slot], jnp.concatenate([vwin, ones_blk], axis=1), preferred_element_type=F32)
        o = o[:, :2 * hd] / o[:, 2 * hd:]
        out = jnp.where(first, o[:GRID_W], o[GRID_W:])
        out_ref[pl.ds(pl.multiple_of(r * GRID_W, GRID_W), GRID_W), :] = out.astype(out_ref.dtype)

    R = NA_ROWS_PER_STEP
    n_groups = rows // R

    def produce(g, p_ref):
        for i in range(R):
            prob_stage(jnp.minimum(g * R + i, rows - 1), p_ref, i)

    def consume(g, p_ref):
        for i in range(R):
            output_stage(g * R + i, p_ref, i)

    produce(0, pa_ref)

    def group_pair(j, carry):
        g = 2 * j
        produce(g + 1, pb_ref)
        consume(g, pa_ref)
        produce(g + 2, pa_ref)
        consume(g + 1, pb_ref)
        return carry

    lax.fori_loop(0, n_groups // 2, group_pair, 0)


def _natten(nq, nk, nv, bias_tab):
    b, seq, d_n = nq.shape
    pairs = N_NA_HEADS // 2
    width = 2 * NA_HEAD_DIM
    rows = seq // GRID_W
    assert rows % (2 * NA_ROWS_PER_STEP) == 0
    col = pl.BlockSpec((None, seq, width), lambda bi, pi: (bi, 0, pi))
    return pl.pallas_call(
        functools.partial(_natten_body, rows=rows),
        grid=(b, pairs),
        in_specs=[col, col, col,
                  pl.BlockSpec((None, 2, 2 * GRID_W, NA_BIAS_LANES), lambda bi, pi: (pi, 0, 0, 0))],
        out_specs=col,
        out_shape=jax.ShapeDtypeStruct((b, seq, d_n), BF16),
        scratch_shapes=[pltpu.VMEM((NA_ROWS_PER_STEP, 2 * GRID_W, WIN_H * GRID_W), BF16)] * 2,
        compiler_params=pltpu.CompilerParams(dimension_semantics=("arbitrary", "arbitrary"),
                                             vmem_limit_bytes=40 * MIB),
        name="natten",
    )(nq, nk, nv, bias_tab)


def _natten_bias_table(rpb):
    c = np.arange(GRID_W)
    cs = np.clip(c - WIN_W // 2, 0, GRID_W - WIN_W)
    cp = np.arange(GRID_W)
    valid = (cp[None, :] >= cs[:, None]) & (cp[None, :] < cs[:, None] + WIN_W)
    rel = cp[None, None, :] - c[None, :, None] + (WIN_W - 1)
    onehot = (rel == np.arange(2 * WIN_W - 1)[:, None, None]).astype(np.float32)
    n_rel = 2 * WIN_H - 1
    tab = jnp.einsum('phrd,dcq->phcrq', rpb.astype(F32).reshape(N_NA_HEADS // 2, 2, n_rel, 2 * WIN_W - 1),
                     jnp.asarray(onehot), precision=lax.Precision.HIGHEST)
    tab = jnp.where(jnp.asarray(valid)[None, None, :, None, :], tab, NEG_BIG)
    tab = tab.reshape(N_NA_HEADS // 2, 2 * GRID_W, n_rel * GRID_W)
    even = tab[:, :, :NA_BIAS_LANES]
    odd = tab[:, :, GRID_W:GRID_W + NA_BIAS_LANES]
    return jnp.stack([even, odd], axis=1)


def _tail_body(x_ref, ya_ref, yb_ref, p_ref, woa_ref, wob_ref, g2_ref, w1_ref, w2_ref,
               g3_ref, wg_ref, wu_ref, out_ref, *, ff_chunk):
    d_ff = w1_ref.shape[1]
    h = (x_ref[...]
         + jnp.dot(ya_ref[...], woa_ref[...], preferred_element_type=F32)
         + jnp.dot(yb_ref[...], wob_ref[...], preferred_element_type=F32))
    u = (h * _rms_scale(h) * g2_ref[...]).astype(BF16)
    out_ref[...] = h
    for j in range(d_ff // ff_chunk):
        z = jnp.dot(u, w1_ref[:, j * ff_chunk:(j + 1) * ff_chunk], preferred_element_type=F32)
        z = jnp.maximum(z, 0.0)
        out_ref[...] += jnp.dot((z * z).astype(BF16), w2_ref[j * ff_chunk:(j + 1) * ff_chunk, :],
                                preferred_element_type=F32)
    h = out_ref[...]
    u = (h * _rms_scale(h) * g3_ref[...]).astype(BF16)
    gate = _sigmoid(jnp.dot(u, wg_ref[...], preferred_element_type=F32))
    up = jnp.dot(p_ref[...].astype(BF16), wu_ref[...], preferred_element_type=F32)
    out_ref[...] = h + gate * up


def _tail(x2, ya, yb, p2, wo, g2, w1, w2, g3, wg, wu, *, tm, ff_chunk):
    t, d = x2.shape
    d_mix = wo.shape[0]
    assert ya.shape[1] == yb.shape[1] == d_mix // 2 and t % tm == 0
    row = lambda width: pl.BlockSpec((tm, width), lambda i: (i, 0))
    half = lambda k: pl.BlockSpec((d_mix // 2, d), lambda i: (k, 0), pipeline_mode=pl.Buffered(1))
    consts = [g2, w1, w2, g3, wg, wu]
    return pl.pallas_call(
        functools.partial(_tail_body, ff_chunk=ff_chunk),
        grid=(t // tm,),
        in_specs=[row(d), row(ya.shape[1]), row(yb.shape[1]), row(p2.shape[1]), half(0), half(1)]
                 + [_const_spec(c.shape) for c in consts],
        out_specs=row(d),
        out_shape=jax.ShapeDtypeStruct((t, d), F32),
        compiler_params=pltpu.CompilerParams(dimension_semantics=("arbitrary",),
                                             vmem_limit_bytes=52 * MIB),
        name="tail",
    )(x2, ya, yb, p2, wo, wo, *consts)


def kernel(x, p, norm1_g, w_in, conv_w, conv_b, gate_b, mlstm_norm_g, q_norm_g, k_norm_g, rpb,
           w_out, norm2_g, w_ff1, w_ff2, ple_norm_g, w_ple_gate, w_ple_up):
    b, seq, d = x.shape
    depth = w_in.shape[0]
    H, hd, L = N_MLSTM_HEADS, MLSTM_HEAD_DIM, CHUNK
    d_m = H * hd
    d_n = N_NA_HEADS * NA_HEAD_DIM
    t = b * seq
    rows = seq // GRID_W
    assert rows >= WIN_H

    bd = jnp.asarray(np.kron(np.eye(N_NA_HEADS), np.ones((NA_HEAD_DIM, NA_HEAD_DIM))), BF16)
    ri, ci = np.indices((L, L))
    tri = np.stack([ci <= ri, ci >= ri])
    tri3 = jnp.asarray(np.concatenate([tri, tri, tri], axis=1), BF16)
    eye2 = jnp.asarray(np.concatenate([ci == ri, ci == ri], axis=1), BF16)

    h = x.reshape(t, d)
    for i in range(depth):
        wi = w_in[i].astype(BF16)
        w_n = wi[:, 4 * d_m + N_GATE:]
        qg = (q_norm_g[i].reshape(1, d_n) * (NA_HEAD_DIM ** -0.5)).astype(F32)
        kg = k_norm_g[i].reshape(1, d_n).astype(F32)
        qk, mv, mo, nq, nk, nv, gates = _inproj(
            h, norm1_g[i].reshape(1, d), wi, w_n, gate_b[i].reshape(N_GATE, 1), bd, qg, kg,
            d_m=d_m, d_n=d_n, tm=1024)

        cw = conv_w[i].reshape(3, 2, H, hd).transpose(2, 1, 0, 3)
        cb = conv_b[i].reshape(2, H, 1, hd).transpose(1, 0, 2, 3)
        y_a = _mlstm(qk.reshape(b, seq, 2 * d_m), mv.reshape(b, seq, d_m), mo.reshape(b, seq, d_m),
                     gates.reshape(4, H, b, seq // L, L), cw, cb,
                     mlstm_norm_g[i].reshape(H, 1, hd), tri3, eye2)

        bias_tab = _natten_bias_table(rpb[i])
        y_b = _natten(nq.reshape(b, seq, d_n), nk.reshape(b, seq, d_n), nv.reshape(b, seq, d_n), bias_tab)

        h = _tail(h, y_a.reshape(t, d_m), y_b.reshape(t, d_n), p[i].reshape(t, -1),
                  w_out[i].astype(BF16), norm2_g[i].reshape(1, d), w_ff1[i].astype(BF16), w_ff2[i].astype(BF16),
                  ple_norm_g[i].reshape(1, d), w_ple_gate[i].astype(BF16), w_ple_up[i].astype(BF16),
                  tm=1024, ff_chunk=1024)
    return h.reshape(b, seq, d)
```

```python
import functools

import jax
import jax.numpy as jnp
import numpy as np
from jax import lax
from jax.experimental import pallas as pl
from jax.experimental.pallas import tpu as pltpu

F32 = jnp.float32
BF16 = jnp.bfloat16

N_MLSTM_HEADS = 4
MLSTM_HEAD_DIM = 128
N_NA_HEADS = 8
NA_HEAD_DIM = 64
GRID_W = 64
WIN_H = 8
WIN_W = 16
CHUNK = 128
N_GATE = 4 * N_MLSTM_HEADS
RMS_EPS = 1e-6
NEG_BIG = -1e30
F32_BIG = 3e38
NA_BIAS_LANES = (2 * WIN_H - 2) * GRID_W
NA_ROWS_PER_STEP = 8
MLSTM_CHUNKS_PER_STEP = 8
MLSTM_CONV_UNROLL = 4
MLSTM_SCAN_UNROLL = 8

LANES = 128
GATE_PAD = LANES
MIB = 1024 * 1024

NT_DIMS = (((1,), (1,)), ((), ()))


def _const_spec(shape):
    return pl.BlockSpec(shape, lambda *_: (0,) * len(shape), pipeline_mode=pl.Buffered(1))


def _rms_scale(x):
    return lax.rsqrt(jnp.mean(x * x, axis=-1, keepdims=True) + RMS_EPS)


def _split3(x):
    hi = x.astype(BF16)
    r1 = x - hi.astype(F32)
    mid = r1.astype(BF16)
    lo = (r1 - mid.astype(F32)).astype(BF16)
    return hi, mid, lo


def _inproj_body(x_ref, g_ref, wm_ref, wn_ref, wg_ref, gb_ref, bd_ref, qg_ref, kg_ref,
                 qk_ref, v_ref, o_ref, nq_ref, nk_ref, nv_ref, gate_ref, *, d_m, d_n):
    x = x_ref[...]
    u = (x * _rms_scale(x) * g_ref[...]).astype(BF16)

    def proj(w_ref, lo, hi):
        return jnp.dot(u, w_ref[:, lo:hi], preferred_element_type=F32)

    def head_norm(y, gain_ref):
        ss = jnp.dot((y * y).astype(BF16), bd_ref[...], preferred_element_type=F32)
        return y * lax.rsqrt(ss * (1.0 / NA_HEAD_DIM) + RMS_EPS) * gain_ref[...]

    qk_ref[...] = proj(wm_ref, 0, 2 * d_m).astype(BF16)
    v_ref[...] = proj(wm_ref, 2 * d_m, 3 * d_m).astype(BF16)
    o_ref[...] = proj(wm_ref, 3 * d_m, 4 * d_m).astype(BF16)
    nq_ref[...] = head_norm(proj(wn_ref, 0, d_n), qg_ref).astype(BF16)
    nk_ref[...] = head_norm(proj(wn_ref, d_n, 2 * d_n), kg_ref).astype(BF16)
    nv_ref[...] = proj(wn_ref, 2 * d_n, 3 * d_n).astype(BF16)
    gate_ref[...] = lax.dot_general(wg_ref[...], u, NT_DIMS, preferred_element_type=F32) + gb_ref[...]


def _inproj(x2, g, w_all, w_n, w_gt, gate_b, bd, qg, kg, *, d_m, d_n, tm):
    t, d = x2.shape
    assert (4 * d_m) % GATE_PAD == 0
    fixed = functools.partial(pl.BlockSpec, pipeline_mode=pl.Buffered(1))
    row = lambda width: pl.BlockSpec((tm, width), lambda i: (i, 0))
    out_shapes = (
        jax.ShapeDtypeStruct((t, 2 * d_m), BF16),
        jax.ShapeDtypeStruct((t, d_m), BF16),
        jax.ShapeDtypeStruct((t, d_m), BF16),
        jax.ShapeDtypeStruct((t, d_n), BF16),
        jax.ShapeDtypeStruct((t, d_n), BF16),
        jax.ShapeDtypeStruct((t, d_n), BF16),
        jax.ShapeDtypeStruct((N_GATE, t), F32),
    )
    return pl.pallas_call(
        functools.partial(_inproj_body, d_m=d_m, d_n=d_n),
        grid=(t // tm,),
        in_specs=[row(d), _const_spec((1, d)), fixed((d, 4 * d_m), lambda i: (0, 0)), _const_spec(w_n.shape),
                  _const_spec((N_GATE, d)), _const_spec((N_GATE, 1)),
                  _const_spec((d_n, d_n)), _const_spec((1, d_n)), _const_spec((1, d_n))],
        out_specs=(row(2 * d_m), row(d_m), row(d_m), row(d_n), row(d_n), row(d_n),
                   pl.BlockSpec((N_GATE, tm), lambda i: (0, i))),
        out_shape=out_shapes,
        compiler_params=pltpu.CompilerParams(dimension_semantics=("arbitrary",),
                                             vmem_limit_bytes=44 * MIB),
        name="inproj",
    )(x2, g, w_all, w_n, w_gt, gate_b, bd, qg, kg)


def _log_sigmoid(x):
    return jnp.minimum(x, 0.0) - jnp.log1p(jnp.exp(-jnp.abs(x)))


def _sigmoid(x):
    return 1.0 / (1.0 + jnp.exp(-x))


def _mlstm_body(q_ref, k_ref, v_ref, o_ref, gate_ref, cw_ref, cb_ref, ng_ref, tri3_ref, eye2_ref,
                out_ref, qs_ref, kt_ref, cs_ref, ccur_ref, brow_ref, crow_ref, cmax_ref, wt_ref, bl_ref,
                ml_ref, ms_ref, s_ref, einv_ref, *, seq):
    L = CHUNK
    d = MLSTM_HEAD_DIM
    nc = seq // L

    row_id = lax.broadcasted_iota(jnp.int32, (L, d), 0)
    col_id = lax.broadcasted_iota(jnp.int32, (L, d), 1)
    pos_id = lax.broadcasted_iota(jnp.int32, (nc, L), 1)
    ones_blk = jnp.ones((L, d), BF16)

    def conv_silu(src_ref, c, s0, w, b):
        x = src_ref[pl.ds(s0, L), :].astype(F32)
        p0 = pl.multiple_of(jnp.maximum(s0 - 16, 0), 16)
        n0 = pl.multiple_of(jnp.minimum(s0 + L, seq - 16), 16)
        prev_row = src_ref[pl.ds(p0, 16), :][15:16, :].astype(F32)
        next_row = src_ref[pl.ds(n0, 16), :][0:1, :].astype(F32)
        prev_row = jnp.where(c > 0, prev_row, 0.0)
        next_row = jnp.where(c < nc - 1, next_row, 0.0)
        x_prev = jnp.where(row_id == 0, prev_row, pltpu.roll(x, 1, 0))
        x_next = jnp.where(row_id == L - 1, next_row, pltpu.roll(x, L - 1, 0))
        y = w[0:1, :] * x_prev + w[1:2, :] * x + w[2:3, :] * x_next + b
        return y * _sigmoid(y)

    def conv_step(c, carry):
        s0 = pl.multiple_of(c * L, L)
        qs_ref[pl.ds(s0, L), :] = conv_silu(q_ref, c, s0, cw_ref[0], cb_ref[0]).astype(BF16)
        kk = conv_silu(k_ref, c, s0, cw_ref[1], cb_ref[1]) * (d ** -0.5)
        kt_ref[:, pl.ds(s0, L)] = kk.T.astype(BF16)
        return carry

    lax.fori_loop(0, nc, conv_step, 0, unroll=MLSTM_CONV_UNROLL)

    for dirn in (0, 1):
        i_g = gate_ref[2 * dirn]
        f_log = _log_sigmoid(gate_ref[2 * dirn + 1])
        f_cat = jnp.concatenate(_split3(f_log), axis=1)
        brow = jnp.dot(f_cat, tri3_ref[1 - dirn], preferred_element_type=F32)
        b_last = brow[:, L - 1:L] if dirn == 0 else brow[:, 0:1]
        a_row = i_g + b_last - brow
        a_max = jnp.max(a_row, axis=1, keepdims=True)
        crow = i_g - brow
        cmax = crow
        for sh in [1 << e for e in range(L.bit_length() - 1)]:
            if dirn == 0:
                cmax = jnp.maximum(cmax, jnp.where(pos_id >= sh, pltpu.roll(cmax, sh, 1), -jnp.inf))
            else:
                cmax = jnp.maximum(cmax, jnp.where(pos_id < L - sh, pltpu.roll(cmax, L - sh, 1), -jnp.inf))
        brow_ref[dirn] = brow
        crow_ref[dirn] = crow
        cmax_ref[dirn] = cmax
        wt_ref[dirn] = jnp.exp(a_row - a_max)
        bl_ref[dirn] = jnp.broadcast_to(b_last, (nc, L))
        ml_ref[dirn] = jnp.broadcast_to(a_max, (nc, L))

    def v_aug(s0):
        return jnp.concatenate([v_ref[pl.ds(s0, L), :], ones_blk], axis=1)

    ccur_ref[...] = jnp.zeros_like(ccur_ref)

    def scan_step(i, carry):
        new = []
        for dirn, c, m in ((0, i, carry[0]), (1, nc - 1 - i, carry[1])):
            s0 = pl.multiple_of(c * L, L)
            kw = (kt_ref[:, pl.ds(s0, L)].astype(F32) * wt_ref[dirn, pl.ds(c, 1), :]).astype(BF16)
            k_loc = jnp.dot(kw, v_aug(s0), preferred_element_type=F32)
            state = ccur_ref[dirn]
            cs_ref[dirn, c] = state.astype(BF16)
            ms_ref[dirn, pl.ds(c, 1), :] = m
            a_prev = m + bl_ref[dirn, pl.ds(c, 1), :]
            a_max = ml_ref[dirn, pl.ds(c, 1), :]
            m_new = jnp.maximum(a_prev, a_max)
            w_prev = jnp.exp(a_prev - m_new)[:, 0:1]
            w_loc = jnp.exp(a_max - m_new)[:, 0:1]
            ccur_ref[dirn] = w_prev * state + w_loc * k_loc
            new.append(m_new)
        return tuple(new)

    m0 = jnp.zeros((1, L), F32)
    lax.fori_loop(0, nc, scan_step, (m0, m0), unroll=MLSTM_SCAN_UNROLL)

    for dirn in (0, 1):
        mu_all = jnp.maximum(ms_ref[dirn], cmax_ref[dirn])
        cmax_ref[dirn] = mu_all
        brow_ref[dirn] = jnp.minimum(jnp.exp(-(brow_ref[dirn] + mu_all)), F32_BIG)
    lower = col_id <= row_id
    upper = col_id >= row_id

    def weights_stage(c, slot):
        s0 = pl.multiple_of(c * L, L)
        q = qs_ref[pl.ds(s0, L), :]
        qk = jnp.dot(q, kt_ref[:, pl.ds(s0, L)], preferred_element_type=F32)
        for dirn, mask in ((0, lower), (1, upper)):
            rows = []
            for stat_ref in (cmax_ref, brow_ref):
                hi, mid, _ = _split3(stat_ref[dirn, pl.ds(c, 1), :])
                rows.append(jnp.broadcast_to(jnp.concatenate([hi, mid], axis=1), (L, 2 * L)))
            col = lax.dot_general(eye2_ref[...], jnp.concatenate(rows, axis=0), NT_DIMS,
                                  preferred_element_type=F32)
            mu = col[:, :L]
            p = jnp.exp(jnp.where(mask, crow_ref[dirn, pl.ds(c, 1), :] - mu, -jnp.inf))
            q_inter = q.astype(F32) * jnp.exp(ms_ref[dirn, pl.ds(c, 1), :] - mu)
            s_ref[slot, dirn] = jnp.concatenate([qk * p, q_inter], axis=1).astype(BF16)
            einv_ref[slot, dirn] = col[:, L:]

    def output_stage(c, slot):
        s0 = pl.multiple_of(c * L, L)
        vaug = v_aug(s0)
        hsum = None
        for dirn in (0, 1):
            nd = jnp.dot(s_ref[slot, dirn], jnp.concatenate([vaug, cs_ref[dirn, c]], axis=0),
                         preferred_element_type=F32)
            h = nd[:, :d] / jnp.maximum(jnp.abs(nd[:, d:]), einv_ref[slot, dirn])
            hsum = h if hsum is None else hsum + h
        y = hsum * _rms_scale(hsum) * ng_ref[...]
        y = y * _sigmoid(o_ref[pl.ds(s0, L), :].astype(F32))
        out_ref[pl.ds(s0, L), :] = y.astype(out_ref.dtype)

    G = MLSTM_CHUNKS_PER_STEP
    for i in range(G):
        weights_stage(i, i)

    def out_step(j, carry):
        cur = (j % 2) * G
        nxt = G - cur
        for i in range(G):
            output_stage(j * G + i, cur + i)
        for i in range(G):
            weights_stage(jnp.minimum((j + 1) * G + i, nc - 1), nxt + i)
        return carry

    lax.fori_loop(0, nc // G, out_step, 0)


def _mlstm(qk, v, o, gates, conv_w, conv_b, norm_g, tri3, eye2):
    b, seq, _ = v.shape
    H, d, L = N_MLSTM_HEADS, MLSTM_HEAD_DIM, CHUNK
    nc = seq // L
    assert nc % MLSTM_SCAN_UNROLL == 0 and nc % MLSTM_CONV_UNROLL == 0 and L == LANES
    assert nc % MLSTM_CHUNKS_PER_STEP == 0
    col = lambda off: pl.BlockSpec((None, seq, d), lambda bi, hi: (bi, 0, hi + off))
    stat = pltpu.VMEM((2, nc, L), F32)
    slots = 2 * MLSTM_CHUNKS_PER_STEP
    return pl.pallas_call(
        functools.partial(_mlstm_body, seq=seq),
        grid=(b, H),
        in_specs=[
            col(0), col(H), col(0), col(0),
            pl.BlockSpec((4, None, None, nc, L), lambda bi, hi: (0, hi, bi, 0, 0)),
            pl.BlockSpec((None, 2, 3, d), lambda bi, hi: (hi, 0, 0, 0)),
            pl.BlockSpec((None, 2, 1, d), lambda bi, hi: (hi, 0, 0, 0)),
            pl.BlockSpec((None, 1, d), lambda bi, hi: (hi, 0, 0)),
            _const_spec(tri3.shape), _const_spec(eye2.shape),
        ],
        out_specs=pl.BlockSpec((None, seq, d), lambda bi, hi: (bi, 0, hi)),
        out_shape=jax.ShapeDtypeStruct((b, seq, H * d), BF16),
        scratch_shapes=[
            pltpu.VMEM((seq, d), BF16),
            pltpu.VMEM((d, seq), BF16),
            pltpu.VMEM((2, nc, d, 2 * d), BF16),
            pltpu.VMEM((2, d, 2 * d), F32),
            stat, stat, stat,
            stat, stat, stat,
            stat,
            pltpu.VMEM((slots, 2, L, L + d), BF16),
            pltpu.VMEM((slots, 2, L, d), F32),
        ],
        compiler_params=pltpu.CompilerParams(dimension_semantics=("arbitrary", "arbitrary"),
                                             vmem_limit_bytes=52 * MIB),
        name="mlstm",
    )(qk, qk, v, o, gates, conv_w, conv_b, norm_g, tri3, eye2)


def _natten_body(q_ref, k_ref, v_ref, bias_ref, out_ref, pa_ref, pb_ref, *, rows):
    hd = NA_HEAD_DIM
    win = WIN_H * GRID_W
    lane = lax.broadcasted_iota(jnp.int32, (GRID_W, 2 * hd), 1)
    first = lane < hd
    ones_blk = jnp.ones((win, 2 * hd), BF16)

    def window_start(r):
        return jnp.clip(r - WIN_H // 2, 0, rows - WIN_H)

    def prob_stage(r, p_ref, slot):
        rs = window_start(r)
        q = q_ref[pl.ds(pl.multiple_of(r * GRID_W, GRID_W), GRID_W), :]
        zero = jnp.zeros_like(q)
        qs = jnp.concatenate([jnp.where(first, q, zero), jnp.where(first, zero, q)], axis=0)
        kwin = k_ref[pl.ds(pl.multiple_of(rs * GRID_W, GRID_W), win), :]
        off = rs - r + (WIN_H - 1)
        bias = bias_ref[off & 1, :, pl.ds(pl.multiple_of((off >> 1) * LANES, LANES), win)]
        s = lax.dot_general(qs, kwin, NT_DIMS, preferred_element_type=F32) + bias
        p_ref[slot] = jnp.exp(s - jnp.max(s, axis=1, keepdims=True)).astype(BF16)

    def output_stage(r, p_ref, slot):
        rs = window_start(r)
        vwin = v_ref[pl.ds(pl.multiple_of(rs * GRID_W, GRID_W), win), :]
        o = jnp.dot(p_ref[slot], jnp.concatenate([vwin, ones_blk], axis=1), preferred_element_type=F32)
        o = o[:, :2 * hd] / o[:, 2 * hd:]
        out = jnp.where(first, o[:GRID_W], o[GRID_W:])
        out_ref[pl.ds(pl.multiple_of(r * GRID_W, GRID_W), GRID_W), :] = out.astype(out_ref.dtype)

    R = NA_ROWS_PER_STEP
    n_groups = rows // R

    def produce(g, p_ref):
        for i in range(R):
            prob_stage(jnp.minimum(g * R + i, rows - 1), p_ref, i)

    def consume(g, p_ref):
        for i in range(R):
            output_stage(g * R + i, p_ref, i)

    produce(0, pa_ref)

    def group_pair(j, carry):
        g = 2 * j
        produce(g + 1, pb_ref)
        consume(g, pa_ref)
        produce(g + 2, pa_ref)
        consume(g + 1, pb_ref)
        return carry

    lax.fori_loop(0, n_groups // 2, group_pair, 0)


def _natten(nq, nk, nv, bias_tab):
    b, seq, d_n = nq.shape
    pairs = N_NA_HEADS // 2
    width = 2 * NA_HEAD_DIM
    rows = seq // GRID_W
    assert rows % (2 * NA_ROWS_PER_STEP) == 0
    col = pl.BlockSpec((None, seq, width), lambda bi, pi: (bi, 0, pi))
    return pl.pallas_call(
        functools.partial(_natten_body, rows=rows),
        grid=(b, pairs),
        in_specs=[col, col, col,
                  pl.BlockSpec((None, 2, 2 * GRID_W, NA_BIAS_LANES), lambda bi, pi: (pi, 0, 0, 0))],
        out_specs=col,
        out_shape=jax.ShapeDtypeStruct((b, seq, d_n), BF16),
        scratch_shapes=[pltpu.VMEM((NA_ROWS_PER_STEP, 2 * GRID_W, WIN_H * GRID_W), BF16)] * 2,
        compiler_params=pltpu.CompilerParams(dimension_semantics=("arbitrary", "arbitrary"),
                                             vmem_limit_bytes=40 * MIB),
        name="natten",
    )(nq, nk, nv, bias_tab)


def _natten_bias_table(rpb):
    c = np.arange(GRID_W)
    cs = np.clip(c - WIN_W // 2, 0, GRID_W - WIN_W)
    cp = np.arange(GRID_W)
    valid = (cp[None, :] >= cs[:, None]) & (cp[None, :] < cs[:, None] + WIN_W)
    rel = cp[None, None, :] - c[None, :, None] + (WIN_W - 1)
    onehot = (rel == np.arange(2 * WIN_W - 1)[:, None, None]).astype(np.float32)
    n_rel = 2 * WIN_H - 1
    tab = jnp.einsum('phrd,dcq->phcrq', rpb.astype(F32).reshape(N_NA_HEADS // 2, 2, n_rel, 2 * WIN_W - 1),
                     jnp.asarray(onehot), precision=lax.Precision.HIGHEST)
    tab = jnp.where(jnp.asarray(valid)[None, None, :, None, :], tab, NEG_BIG)
    tab = tab.reshape(N_NA_HEADS // 2, 2 * GRID_W, n_rel * GRID_W)
    even = tab[:, :, :NA_BIAS_LANES]
    odd = tab[:, :, GRID_W:GRID_W + NA_BIAS_LANES]
    return jnp.stack([even, odd], axis=1)


def _tail_body(x_ref, ya_ref, yb_ref, p_ref, woa_ref, wob_ref, g2_ref, w1_ref, w2_ref,
               g3_ref, wg_ref, wu_ref, out_ref, *, ff_chunk):
    d_ff = w1_ref.shape[1]
    h = (x_ref[...]
         + jnp.dot(ya_ref[...], woa_ref[...], preferred_element_type=F32)
         + jnp.dot(yb_ref[...], wob_ref[...], preferred_element_type=F32))
    u = (h * _rms_scale(h) * g2_ref[...]).astype(BF16)
    out_ref[...] = h
    for j in range(d_ff // ff_chunk):
        z = jnp.dot(u, w1_ref[:, j * ff_chunk:(j + 1) * ff_chunk], preferred_element_type=F32)
        z = jnp.maximum(z, 0.0)
        out_ref[...] += jnp.dot((z * z).astype(BF16), w2_ref[j * ff_chunk:(j + 1) * ff_chunk, :],
                                preferred_element_type=F32)
    h = out_ref[...]
    u = (h * _rms_scale(h) * g3_ref[...]).astype(BF16)
    gate = _sigmoid(jnp.dot(u, wg_ref[...], preferred_element_type=F32))
    up = jnp.dot(p_ref[...].astype(BF16), wu_ref[...], preferred_element_type=F32)
    out_ref[...] = h + gate * up


def _tail(x2, ya, yb, p2, wo, g2, w1, w2, g3, wg, wu, *, tm, ff_chunk):
    t, d = x2.shape
    d_mix = wo.shape[0]
    assert ya.shape[1] == yb.shape[1] == d_mix // 2 and t % tm == 0
    row = lambda width: pl.BlockSpec((tm, width), lambda i: (i, 0))
    half = lambda k: pl.BlockSpec((d_mix // 2, d), lambda i: (k, 0), pipeline_mode=pl.Buffered(1))
    consts = [g2, w1, w2, g3, wg, wu]
    return pl.pallas_call(
        functools.partial(_tail_body, ff_chunk=ff_chunk),
        grid=(t // tm,),
        in_specs=[row(d), row(ya.shape[1]), row(yb.shape[1]), row(p2.shape[1]), half(0), half(1)]
                 + [_const_spec(c.shape) for c in consts],
        out_specs=row(d),
        out_shape=jax.ShapeDtypeStruct((t, d), F32),
        compiler_params=pltpu.CompilerParams(dimension_semantics=("arbitrary",),
                                             vmem_limit_bytes=52 * MIB),
        name="tail",
    )(x2, ya, yb, p2, wo, wo, *consts)


def kernel(x, p, norm1_g, w_in, conv_w, conv_b, gate_b, mlstm_norm_g, q_norm_g, k_norm_g, rpb,
           w_out, norm2_g, w_ff1, w_ff2, ple_norm_g, w_ple_gate, w_ple_up):
    b, seq, d = x.shape
    depth = w_in.shape[0]
    H, hd, L = N_MLSTM_HEADS, MLSTM_HEAD_DIM, CHUNK
    d_m = H * hd
    d_n = N_NA_HEADS * NA_HEAD_DIM
    t = b * seq
    rows = seq // GRID_W
    assert rows >= WIN_H

    bd = jnp.asarray(np.kron(np.eye(N_NA_HEADS), np.ones((NA_HEAD_DIM, NA_HEAD_DIM))), BF16)
    ri, ci = np.indices((L, L))
    tri = np.stack([ci <= ri, ci >= ri])
    tri3 = jnp.asarray(np.concatenate([tri, tri, tri], axis=1), BF16)
    eye2 = jnp.asarray(np.concatenate([ci == ri, ci == ri], axis=1), BF16)

    h = x.reshape(t, d)
    for i in range(depth):
        wi = w_in[i].astype(BF16)
        w_n = wi[:, 4 * d_m + N_GATE:]
        qg = (q_norm_g[i].reshape(1, d_n) * (NA_HEAD_DIM ** -0.5)).astype(F32)
        kg = k_norm_g[i].reshape(1, d_n).astype(F32)
        qk, mv, mo, nq, nk, nv, gates = _inproj(
            h, norm1_g[i].reshape(1, d), wi, w_n, wi[:, 4 * d_m:4 * d_m + N_GATE].T,
            gate_b[i].reshape(N_GATE, 1), bd, qg, kg,
            d_m=d_m, d_n=d_n, tm=1024)

        cw = conv_w[i].reshape(3, 2, H, hd).transpose(2, 1, 0, 3)
        cb = conv_b[i].reshape(2, H, 1, hd).transpose(1, 0, 2, 3)
        y_a = _mlstm(qk.reshape(b, seq, 2 * d_m), mv.reshape(b, seq, d_m), mo.reshape(b, seq, d_m),
                     gates.reshape(4, H, b, seq // L, L), cw, cb,
                     mlstm_norm_g[i].reshape(H, 1, hd), tri3, eye2)

        bias_tab = _natten_bias_table(rpb[i])
        y_b = _natten(nq.reshape(b, seq, d_n), nk.reshape(b, seq, d_n), nv.reshape(b, seq, d_n), bias_tab)

        h = _tail(h, y_a.reshape(t, d_m), y_b.reshape(t, d_n), p[i].reshape(t, -1),
                  w_out[i].astype(BF16), norm2_g[i].reshape(1, d), w_ff1[i].astype(BF16), w_ff2[i].astype(BF16),
                  ple_norm_g[i].reshape(1, d), w_ple_gate[i].astype(BF16), w_ple_up[i].astype(BF16),
                  tm=1024, ff_chunk=1024)
    return h.reshape(b, seq, d)
```

```python
import functools

import jax
import jax.numpy as jnp
import numpy as np
from jax import lax
from jax.experimental import pallas as pl
from jax.experimental.pallas import tpu as pltpu

F32 = jnp.float32
BF16 = jnp.bfloat16

N_MLSTM_HEADS = 4
MLSTM_HEAD_DIM = 128
N_NA_HEADS = 8
NA_HEAD_DIM = 64
GRID_W = 64
WIN_H = 8
WIN_W = 16
CHUNK = 128
N_GATE = 4 * N_MLSTM_HEADS
RMS_EPS = 1e-6
NEG_BIG = -1e30
F32_BIG = 3e38
NA_BIAS_LANES = (2 * WIN_H - 2) * GRID_W
NA_ROWS_PER_STEP = 8
MLSTM_CHUNKS_PER_STEP = 8
MLSTM_CONV_UNROLL = 4
MLSTM_SCAN_UNROLL = 8

LANES = 128
GATE_PAD = LANES
MIB = 1024 * 1024

NT_DIMS = (((1,), (1,)), ((), ()))


def _const_spec(shape):
    return pl.BlockSpec(shape, lambda *_: (0,) * len(shape), pipeline_mode=pl.Buffered(1))


def _rms_scale(x):
    return lax.rsqrt(jnp.mean(x * x, axis=-1, keepdims=True) + RMS_EPS)


def _split3(x):
    hi = x.astype(BF16)
    r1 = x - hi.astype(F32)
    mid = r1.astype(BF16)
    lo = (r1 - mid.astype(F32)).astype(BF16)
    return hi, mid, lo


def _inproj_body(x_ref, g_ref, wm_ref, wn_ref, wg_ref, gb_ref, bd_ref, qg_ref, kg_ref,
                 qk_ref, v_ref, o_ref, nq_ref, nk_ref, nv_ref, gate_ref, *, d_m, d_n):
    x = x_ref[...]
    u = (x * _rms_scale(x) * g_ref[...]).astype(BF16)

    def proj(w_ref, lo, hi):
        return jnp.dot(u, w_ref[:, lo:hi], preferred_element_type=F32)

    def head_norm(y, gain_ref):
        ss = jnp.dot((y * y).astype(BF16), bd_ref[...], preferred_element_type=F32)
        return y * lax.rsqrt(ss * (1.0 / NA_HEAD_DIM) + RMS_EPS) * gain_ref[...]

    qk_ref[...] = proj(wm_ref, 0, 2 * d_m).astype(BF16)
    v_ref[...] = proj(wm_ref, 2 * d_m, 3 * d_m).astype(BF16)
    o_ref[...] = proj(wm_ref, 3 * d_m, 4 * d_m).astype(BF16)
    nq_ref[...] = head_norm(proj(wn_ref, 0, d_n), qg_ref).astype(BF16)
    nk_ref[...] = head_norm(proj(wn_ref, d_n, 2 * d_n), kg_ref).astype(BF16)
    nv_ref[...] = proj(wn_ref, 2 * d_n, 3 * d_n).astype(BF16)
    gate_ref[...] = lax.dot_general(wg_ref[...], u, NT_DIMS, preferred_element_type=F32) + gb_ref[...]


def _inproj(x2, g, w_all, w_n, w_gt, gate_b, bd, qg, kg, *, d_m, d_n, tm):
    t, d = x2.shape
    assert (4 * d_m) % GATE_PAD == 0
    fixed = functools.partial(pl.BlockSpec, pipeline_mode=pl.Buffered(1))
    row = lambda width: pl.BlockSpec((tm, width), lambda i: (i, 0))
    out_shapes = (
        jax.ShapeDtypeStruct((t, 2 * d_m), BF16),
        jax.ShapeDtypeStruct((t, d_m), BF16),
        jax.ShapeDtypeStruct((t, d_m), BF16),
        jax.ShapeDtypeStruct((t, d_n), BF16),
        jax.ShapeDtypeStruct((t, d_n), BF16),
        jax.ShapeDtypeStruct((t, d_n), BF16),
        jax.ShapeDtypeStruct((N_GATE, t), F32),
    )
    return pl.pallas_call(
        functools.partial(_inproj_body, d_m=d_m, d_n=d_n),
        grid=(t // tm,),
        in_specs=[row(d), _const_spec((1, d)), fixed((d, 4 * d_m), lambda i: (0, 0)), _const_spec(w_n.shape),
                  _const_spec((N_GATE, d)), _const_spec((N_GATE, 1)),
                  _const_spec((d_n, d_n)), _const_spec((1, d_n)), _const_spec((1, d_n))],
        out_specs=(row(2 * d_m), row(d_m), row(d_m), row(d_n), row(d_n), row(d_n),
                   pl.BlockSpec((N_GATE, tm), lambda i: (0, i))),
        out_shape=out_shapes,
        compiler_params=pltpu.CompilerParams(dimension_semantics=("arbitrary",),
                                             vmem_limit_bytes=44 * MIB),
        name="inproj",
    )(x2, g, w_all, w_n, w_gt, gate_b, bd, qg, kg)


def _log_sigmoid(x):
    return jnp.minimum(x, 0.0) - jnp.log1p(jnp.exp(-jnp.abs(x)))


def _sigmoid(x):
    return 1.0 / (1.0 + jnp.exp(-x))


def _mlstm_body(q_ref, k_ref, v_ref, o_ref, gate_ref, cw_ref, cb_ref, ng_ref, tri3_ref, eye2_ref,
                out_ref, qs_ref, kt_ref, cs_ref, ccur_ref, brow_ref, crow_ref, cmax_ref, wt_ref, bl_ref,
                ml_ref, ms_ref, s_ref, einv_ref, *, seq):
    L = CHUNK
    d = MLSTM_HEAD_DIM
    nc = seq // L

    row_id = lax.broadcasted_iota(jnp.int32, (L, d), 0)
    col_id = lax.broadcasted_iota(jnp.int32, (L, d), 1)
    pos_id = lax.broadcasted_iota(jnp.int32, (nc, L), 1)
    ones_blk = jnp.ones((L, d), BF16)

    def conv_silu(src_ref, c, s0, w, b):
        x = src_ref[pl.ds(s0, L), :].astype(F32)
        p0 = pl.multiple_of(jnp.maximum(s0 - 16, 0), 16)
        n0 = pl.multiple_of(jnp.minimum(s0 + L, seq - 16), 16)
        prev_row = src_ref[pl.ds(p0, 16), :][15:16, :].astype(F32)
        next_row = src_ref[pl.ds(n0, 16), :][0:1, :].astype(F32)
        prev_row = jnp.where(c > 0, prev_row, 0.0)
        next_row = jnp.where(c < nc - 1, next_row, 0.0)
        x_prev = jnp.where(row_id == 0, prev_row, pltpu.roll(x, 1, 0))
        x_next = jnp.where(row_id == L - 1, next_row, pltpu.roll(x, L - 1, 0))
        y = w[0:1, :] * x_prev + w[1:2, :] * x + w[2:3, :] * x_next + b
        return y * _sigmoid(y)

    def conv_step(c, carry):
        s0 = pl.multiple_of(c * L, L)
        qs_ref[pl.ds(s0, L), :] = conv_silu(q_ref, c, s0, cw_ref[0], cb_ref[0]).astype(BF16)
        kk = conv_silu(k_ref, c, s0, cw_ref[1], cb_ref[1]) * (d ** -0.5)
        kt_ref[:, pl.ds(s0, L)] = kk.T.astype(BF16)
        return carry

    lax.fori_loop(0, nc, conv_step, 0, unroll=MLSTM_CONV_UNROLL)

    for dirn in (0, 1):
        i_g = gate_ref[2 * dirn]
        f_log = _log_sigmoid(gate_ref[2 * dirn + 1])
        f_cat = jnp.concatenate(_split3(f_log), axis=1)
        brow = jnp.dot(f_cat, tri3_ref[1 - dirn], preferred_element_type=F32)
        b_last = brow[:, L - 1:L] if dirn == 0 else brow[:, 0:1]
        a_row = i_g + b_last - brow
        a_max = jnp.max(a_row, axis=1, keepdims=True)
        crow = i_g - brow
        cmax = crow
        for sh in [1 << e for e in range(L.bit_length() - 1)]:
            if dirn == 0:
                cmax = jnp.maximum(cmax, jnp.where(pos_id >= sh, pltpu.roll(cmax, sh, 1), -jnp.inf))
            else:
                cmax = jnp.maximum(cmax, jnp.where(pos_id < L - sh, pltpu.roll(cmax, L - sh, 1), -jnp.inf))
        brow_ref[dirn] = brow
        crow_ref[dirn] = crow
        cmax_ref[dirn] = cmax
        wt_ref[dirn] = jnp.exp(a_row - a_max)
        bl_ref[dirn] = jnp.broadcast_to(b_last, (nc, L))
        ml_ref[dirn] = jnp.broadcast_to(a_max, (nc, L))

    def v_aug(s0):
        return jnp.concatenate([v_ref[pl.ds(s0, L), :], ones_blk], axis=1)

    ccur_ref[...] = jnp.zeros_like(ccur_ref)

    def scan_step(i, carry):
        new = []
        for dirn, c, m in ((0, i, carry[0]), (1, nc - 1 - i, carry[1])):
            s0 = pl.multiple_of(c * L, L)
            kw = (kt_ref[:, pl.ds(s0, L)].astype(F32) * wt_ref[dirn, pl.ds(c, 1), :]).astype(BF16)
            k_loc = jnp.dot(kw, v_aug(s0), preferred_element_type=F32)
            state = ccur_ref[dirn]
            cs_ref[dirn, c] = state.astype(BF16)
            ms_ref[dirn, pl.ds(c, 1), :] = m
            a_prev = m + bl_ref[dirn, pl.ds(c, 1), :]
            a_max = ml_ref[dirn, pl.ds(c, 1), :]
            m_new = jnp.maximum(a_prev, a_max)
            w_prev = jnp.exp(a_prev - m_new)[:, 0:1]
            w_loc = jnp.exp(a_max - m_new)[:, 0:1]
            ccur_ref[dirn] = w_prev * state + w_loc * k_loc
            new.append(m_new)
        return tuple(new)

    m0 = jnp.zeros((1, L), F32)
    lax.fori_loop(0, nc, scan_step, (m0, m0), unroll=MLSTM_SCAN_UNROLL)

    for dirn in (0, 1):
        mu_all = jnp.maximum(ms_ref[dirn], cmax_ref[dirn])
        cmax_ref[dirn] = mu_all
        brow_ref[dirn] = jnp.minimum(jnp.exp(-(brow_ref[dirn] + mu_all)), F32_BIG)
    lower = col_id <= row_id
    upper = col_id >= row_id

    def weights_stage(c, slot):
        s0 = pl.multiple_of(c * L, L)
        q = qs_ref[pl.ds(s0, L), :]
        qk = jnp.dot(q, kt_ref[:, pl.ds(s0, L)], preferred_element_type=F32)
        for dirn, mask in ((0, lower), (1, upper)):
            rows = []
            for stat_ref in (cmax_ref, brow_ref):
                hi, mid, _ = _split3(stat_ref[dirn, pl.ds(c, 1), :])
                rows.append(jnp.broadcast_to(jnp.concatenate([hi, mid], axis=1), (L, 2 * L)))
            col = lax.dot_general(eye2_ref[...], jnp.concatenate(rows, axis=0), NT_DIMS,
                                  preferred_element_type=F32)
            mu = col[:, :L]
            p = jnp.exp(jnp.where(mask, crow_ref[dirn, pl.ds(c, 1), :] - mu, -jnp.inf))
            q_inter = q.astype(F32) * jnp.exp(ms_ref[dirn, pl.ds(c, 1), :] - mu)
            s_ref[slot, dirn] = jnp.concatenate([qk * p, q_inter], axis=1).astype(BF16)
            einv_ref[slot, dirn] = col[:, L:]

    def output_stage(c, slot):
        s0 = pl.multiple_of(c * L, L)
        vaug = v_aug(s0)
        hsum = None
        for dirn in (0, 1):
            nd = jnp.dot(s_ref[slot, dirn], jnp.concatenate([vaug, cs_ref[dirn, c]], axis=0),
                         preferred_element_type=F32)
            h = nd[:, :d] / jnp.maximum(jnp.abs(nd[:, d:]), einv_ref[slot, dirn])
            hsum = h if hsum is None else hsum + h
        y = hsum * _rms_scale(hsum) * ng_ref[...]
        y = y * _sigmoid(o_ref[pl.ds(s0, L), :].astype(F32))
        out_ref[pl.ds(s0, L), :] = y.astype(out_ref.dtype)

    G = MLSTM_CHUNKS_PER_STEP
    for i in range(G):
        weights_stage(i, i)

    def out_step(j, carry):
        cur = (j % 2) * G
        nxt = G - cur
        for i in range(G):
            output_stage(j * G + i, cur + i)
        for i in range(G):
            weights_stage(jnp.minimum((j + 1) * G + i, nc - 1), nxt + i)
        return carry

    lax.fori_loop(0, nc // G, out_step, 0)


def _mlstm(qk, v, o, gates, conv_w, conv_b, norm_g, tri3, eye2):
    b, seq, _ = v.shape
    H, d, L = N_MLSTM_HEADS, MLSTM_HEAD_DIM, CHUNK
    nc = seq // L
    assert nc % MLSTM_SCAN_UNROLL == 0 and nc % MLSTM_CONV_UNROLL == 0 and L == LANES
    assert nc % MLSTM_CHUNKS_PER_STEP == 0
    col = lambda off: pl.BlockSpec((None, seq, d), lambda bi, hi: (bi, 0, hi + off))
    stat = pltpu.VMEM((2, nc, L), F32)
    slots = 2 * MLSTM_CHUNKS_PER_STEP
    return pl.pallas_call(
        functools.partial(_mlstm_body, seq=seq),
        grid=(b, H),
        in_specs=[
            col(0), col(H), col(0), col(0),
            pl.BlockSpec((4, None, None, nc, L), lambda bi, hi: (0, hi, bi, 0, 0)),
            pl.BlockSpec((None, 2, 3, d), lambda bi, hi: (hi, 0, 0, 0)),
            pl.BlockSpec((None, 2, 1, d), lambda bi, hi: (hi, 0, 0, 0)),
            pl.BlockSpec((None, 1, d), lambda bi, hi: (hi, 0, 0)),
            _const_spec(tri3.shape), _const_spec(eye2.shape),
        ],
        out_specs=pl.BlockSpec((None, seq, d), lambda bi, hi: (bi, 0, hi)),
        out_shape=jax.ShapeDtypeStruct((b, seq, H * d), BF16),
        scratch_shapes=[
            pltpu.VMEM((seq, d), BF16),
            pltpu.VMEM((d, seq), BF16),
            pltpu.VMEM((2, nc, d, 2 * d), BF16),
            pltpu.VMEM((2, d, 2 * d), F32),
            stat, stat, stat,
            stat, stat, stat,
            stat,
            pltpu.VMEM((slots, 2, L, L + d), BF16),
            pltpu.VMEM((slots, 2, L, d), F32),
        ],
        compiler_params=pltpu.CompilerParams(dimension_semantics=("arbitrary", "arbitrary"),
                                             vmem_limit_bytes=52 * MIB),
        name="mlstm",
    )(qk, qk, v, o, gates, conv_w, conv_b, norm_g, tri3, eye2)


def _natten_body(q_ref, k_ref, v_ref, bias_ref, out_ref, pa_ref, pb_ref, *, rows):
    hd = NA_HEAD_DIM
    win = WIN_H * GRID_W
    lane = lax.broadcasted_iota(jnp.int32, (GRID_W, 2 * hd), 1)
    first = lane < hd
    ones_blk = jnp.ones((win, 2 * hd), BF16)

    def window_start(r):
        return jnp.clip(r - WIN_H // 2, 0, rows - WIN_H)

    def prob_stage(r, p_ref, slot):
        rs = window_start(r)
        q = q_ref[pl.ds(pl.multiple_of(r * GRID_W, GRID_W), GRID_W), :]
        zero = jnp.zeros_like(q)
        qs = jnp.concatenate([jnp.where(first, q, zero), jnp.where(first, zero, q)], axis=0)
        kwin = k_ref[pl.ds(pl.multiple_of(rs * GRID_W, GRID_W), win), :]
        off = rs - r + (WIN_H - 1)
        bias = bias_ref[off & 1, :, pl.ds(pl.multiple_of((off >> 1) * LANES, LANES), win)]
        s = lax.dot_general(qs, kwin, NT_DIMS, preferred_element_type=F32) + bias
        p_ref[slot] = jnp.exp(s - jnp.max(s, axis=1, keepdims=True)).astype(BF16)

    def output_stage(r, p_ref, slot):
        rs = window_start(r)
        vwin = v_ref[pl.ds(pl.multiple_of(rs * GRID_W, GRID_W), win), :]
        o = jnp.dot(p_ref[slot], jnp.concatenate([vwin, ones_blk], axis=1), preferred_element_type=F32)
        o = o[:, :2 * hd] / o[:, 2 * hd:]
        out = jnp.where(first, o[:GRID_W], o[GRID_W:])
        out_ref[pl.ds(pl.multiple_of(r * GRID_W, GRID_W), GRID_W), :] = out.astype(out_ref.dtype)

    R = NA_ROWS_PER_STEP
    n_groups = rows // R

    def produce(g, p_ref):
        for i in range(R):
            prob_stage(jnp.minimum(g * R + i, rows - 1), p_ref, i)

    def consume(g, p_ref):
        for i in range(R):
            output_stage(g * R + i, p_ref, i)

    produce(0, pa_ref)

    def group_pair(j, carry):
        g = 2 * j
        produce(g + 1, pb_ref)
        consume(g, pa_ref)
        produce(g + 2, pa_ref)
        consume(g + 1, pb_ref)
        return carry

    lax.fori_loop(0, n_groups // 2, group_pair, 0)


def _natten(nq, nk, nv, bias_tab):
    b, seq, d_n = nq.shape
    pairs = N_NA_HEADS // 2
    width = 2 * NA_HEAD_DIM
    rows = seq // GRID_W
    assert rows % (2 * NA_ROWS_PER_STEP) == 0
    col = pl.BlockSpec((None, seq, width), lambda bi, pi: (bi, 0, pi))
    return pl.pallas_call(
        functools.partial(_natten_body, rows=rows),
        grid=(b, pairs),
        in_specs=[col, col, col,
                  pl.BlockSpec((None, 2, 2 * GRID_W, NA_BIAS_LANES), lambda bi, pi: (pi, 0, 0, 0))],
        out_specs=col,
        out_shape=jax.ShapeDtypeStruct((b, seq, d_n), BF16),
        scratch_shapes=[pltpu.VMEM((NA_ROWS_PER_STEP, 2 * GRID_W, WIN_H * GRID_W), BF16)] * 2,
        compiler_params=pltpu.CompilerParams(dimension_semantics=("arbitrary", "arbitrary"),
                                             vmem_limit_bytes=40 * MIB),
        name="natten",
    )(nq, nk, nv, bias_tab)


def _natten_bias_table(rpb):
    c = np.arange(GRID_W)
    cs = np.clip(c - WIN_W // 2, 0, GRID_W - WIN_W)
    cp = np.arange(GRID_W)
    valid = (cp[None, :] >= cs[:, None]) & (cp[None, :] < cs[:, None] + WIN_W)
    rel = cp[None, None, :] - c[None, :, None] + (WIN_W - 1)
    onehot = (rel == np.arange(2 * WIN_W - 1)[:, None, None]).astype(np.float32)
    n_rel = 2 * WIN_H - 1
    tab = jnp.einsum('phrd,dcq->phcrq', rpb.astype(F32).reshape(N_NA_HEADS // 2, 2, n_rel, 2 * WIN_W - 1),
                     jnp.asarray(onehot), precision=lax.Precision.HIGHEST)
    tab = jnp.where(jnp.asarray(valid)[None, None, :, None, :], tab, NEG_BIG)
    tab = tab.reshape(N_NA_HEADS // 2, 2 * GRID_W, n_rel * GRID_W)
    even = tab[:, :, :NA_BIAS_LANES]
    odd = tab[:, :, GRID_W:GRID_W + NA_BIAS_LANES]
    return jnp.stack([even, odd], axis=1)


def _tail_body(x_ref, ya_ref, yb_ref, p_ref, woa_ref, wob_ref, g2_ref, w1_ref, w2_ref,
               g3_ref, wg_ref, wu_ref, out_ref, *, ff_chunk):
    d_ff = w1_ref.shape[1]
    h = (x_ref[...]
         + jnp.dot(ya_ref[...], woa_ref[...], preferred_element_type=F32)
         + jnp.dot(yb_ref[...], wob_ref[...], preferred_element_type=F32))
    u = (h * _rms_scale(h) * g2_ref[...]).astype(BF16)
    out_ref[...] = h
    for j in range(d_ff // ff_chunk):
        z = jnp.dot(u, w1_ref[:, j * ff_chunk:(j + 1) * ff_chunk], preferred_element_type=F32)
        z = jnp.maximum(z, 0.0)
        out_ref[...] += jnp.dot((z * z).astype(BF16), w2_ref[j * ff_chunk:(j + 1) * ff_chunk, :],
                                preferred_element_type=F32)
    h = out_ref[...]
    u = (h * _rms_scale(h) * g3_ref[...]).astype(BF16)
    gate = _sigmoid(jnp.dot(u, wg_ref[...], preferred_element_type=F32))
    up = jnp.dot(p_ref[...].astype(BF16), wu_ref[...], preferred_element_type=F32)
    out_ref[...] = h + gate * up


def _tail(x2, ya, yb, p2, wo, g2, w1, w2, g3, wg, wu, *, tm, ff_chunk):
    t, d = x2.shape
    d_mix = wo.shape[0]
    assert ya.shape[1] == yb.shape[1] == d_mix // 2 and t % tm == 0
    row = lambda width: pl.BlockSpec((tm, width), lambda i: (i, 0))
    half = lambda k: pl.BlockSpec((d_mix // 2, d), lambda i: (k, 0), pipeline_mode=pl.Buffered(1))
    consts = [g2, w1, w2, g3, wg, wu]
    return pl.pallas_call(
        functools.partial(_tail_body, ff_chunk=ff_chunk),
        grid=(t // tm,),
        in_specs=[row(d), row(ya.shape[1]), row(yb.shape[1]), row(p2.shape[1]), half(0), half(1)]
                 + [_const_spec(c.shape) for c in consts],
        out_specs=row(d),
        out_shape=jax.ShapeDtypeStruct((t, d), F32),
        compiler_params=pltpu.CompilerParams(dimension_semantics=("arbitrary",),
                                             vmem_limit_bytes=52 * MIB),
        name="tail",
    )(x2, ya, yb, p2, wo, wo, *consts)


def kernel(x, p, norm1_g, w_in, conv_w, conv_b, gate_b, mlstm_norm_g, q_norm_g, k_norm_g, rpb,
           w_out, norm2_g, w_ff1, w_ff2, ple_norm_g, w_ple_gate, w_ple_up):
    b, seq, d = x.shape
    depth = w_in.shape[0]
    H, hd, L = N_MLSTM_HEADS, MLSTM_HEAD_DIM, CHUNK
    d_m = H * hd
    d_n = N_NA_HEADS * NA_HEAD_DIM
    t = b * seq
    rows = seq // GRID_W
    assert rows >= WIN_H

    bd = jnp.asarray(np.kron(np.eye(N_NA_HEADS), np.ones((NA_HEAD_DIM, NA_HEAD_DIM))), BF16)
    ri, ci = np.indices((L, L))
    tri = np.stack([ci <= ri, ci >= ri])
    tri3 = jnp.asarray(np.concatenate([tri, tri, tri], axis=1), BF16)
    eye2 = jnp.asarray(np.concatenate([ci == ri, ci == ri], axis=1), BF16)

    h = x.reshape(t, d)
    for i in range(depth):
        wi = w_in[i].astype(BF16)
        w_n = wi[:, 4 * d_m + N_GATE:]
        qg = (q_norm_g[i].reshape(1, d_n) * (NA_HEAD_DIM ** -0.5)).astype(F32)
        kg = k_norm_g[i].reshape(1, d_n).astype(F32)
        qk, mv, mo, nq, nk, nv, gates = _inproj(
            h, norm1_g[i].reshape(1, d), wi, w_n, w_in[i][:, 4 * d_m:4 * d_m + N_GATE].T.astype(BF16),
            gate_b[i].reshape(N_GATE, 1), bd, qg, kg,
            d_m=d_m, d_n=d_n, tm=1024)

        cw = conv_w[i].reshape(3, 2, H, hd).transpose(2, 1, 0, 3)
        cb = conv_b[i].reshape(2, H, 1, hd).transpose(1, 0, 2, 3)
        y_a = _mlstm(qk.reshape(b, seq, 2 * d_m), mv.reshape(b, seq, d_m), mo.reshape(b, seq, d_m),
                     gates.reshape(4, H, b, seq // L, L), cw, cb,
                     mlstm_norm_g[i].reshape(H, 1, hd), tri3, eye2)

        bias_tab = _natten_bias_table(rpb[i])
        y_b = _natten(nq.reshape(b, seq, d_n), nk.reshape(b, seq, d_n), nv.reshape(b, seq, d_n), bias_tab)

        h = _tail(h, y_a.reshape(t, d_m), y_b.reshape(t, d_n), p[i].reshape(t, -1),
                  w_out[i].astype(BF16), norm2_g[i].reshape(1, d), w_ff1[i].astype(BF16), w_ff2[i].astype(BF16),
                  ple_norm_g[i].reshape(1, d), w_ple_gate[i].astype(BF16), w_ple_up[i].astype(BF16),
                  tm=1024, ff_chunk=1024)
    return h.reshape(b, seq, d)
```

```python
import functools

import jax
import jax.numpy as jnp
import numpy as np
from jax import lax
from jax.experimental import pallas as pl
from jax.experimental.pallas import tpu as pltpu

F32 = jnp.float32
BF16 = jnp.bfloat16

N_MLSTM_HEADS = 4
MLSTM_HEAD_DIM = 128
N_NA_HEADS = 8
NA_HEAD_DIM = 64
GRID_W = 64
WIN_H = 8
WIN_W = 16
CHUNK = 128
N_GATE = 4 * N_MLSTM_HEADS
RMS_EPS = 1e-6
NEG_BIG = -1e30
F32_BIG = 3e38
NA_BIAS_LANES = (2 * WIN_H - 2) * GRID_W
NA_ROWS_PER_STEP = 8
MLSTM_CHUNKS_PER_STEP = 8
MLSTM_CONV_UNROLL = 4
MLSTM_SCAN_UNROLL = 8

LANES = 128
GATE_PAD = LANES
MIB = 1024 * 1024

NT_DIMS = (((1,), (1,)), ((), ()))


def _const_spec(shape):
    return pl.BlockSpec(shape, lambda *_: (0,) * len(shape), pipeline_mode=pl.Buffered(1))


def _rms_scale(x):
    return lax.rsqrt(jnp.mean(x * x, axis=-1, keepdims=True) + RMS_EPS)


def _split3(x):
    hi = x.astype(BF16)
    r1 = x - hi.astype(F32)
    mid = r1.astype(BF16)
    lo = (r1 - mid.astype(F32)).astype(BF16)
    return hi, mid, lo


def _inproj_body(x_ref, g_ref, wm_ref, wn_ref, wg_ref, gb_ref, bd_ref, qg_ref, kg_ref,
                 qk_ref, v_ref, o_ref, nq_ref, nk_ref, nv_ref, gate_ref, *, d_m, d_n):
    x = x_ref[...]
    u = (x * _rms_scale(x) * g_ref[...]).astype(BF16)

    def proj(w_ref, lo, hi):
        return jnp.dot(u, w_ref[:, lo:hi], preferred_element_type=F32)

    def head_norm(y, gain_ref):
        ss = jnp.dot((y * y).astype(BF16), bd_ref[...], preferred_element_type=F32)
        return y * lax.rsqrt(ss * (1.0 / NA_HEAD_DIM) + RMS_EPS) * gain_ref[...]

    qk_ref[...] = proj(wm_ref, 0, 2 * d_m).astype(BF16)
    v_ref[...] = proj(wm_ref, 2 * d_m, 3 * d_m).astype(BF16)
    o_ref[...] = proj(wm_ref, 3 * d_m, 4 * d_m).astype(BF16)
    nq_ref[...] = head_norm(proj(wn_ref, 0, d_n), qg_ref).astype(BF16)
    nk_ref[...] = head_norm(proj(wn_ref, d_n, 2 * d_n), kg_ref).astype(BF16)
    nv_ref[...] = proj(wn_ref, 2 * d_n, 3 * d_n).astype(BF16)
    wg_t = wg_ref[...].astype(F32).T[:N_GATE, :].astype(BF16)
    gate_ref[...] = lax.dot_general(wg_t, u, NT_DIMS, preferred_element_type=F32) + gb_ref[...]


def _inproj(x2, g, w_all, w_n, gate_b, bd, qg, kg, *, d_m, d_n, tm):
    t, d = x2.shape
    assert (4 * d_m) % GATE_PAD == 0
    fixed = functools.partial(pl.BlockSpec, pipeline_mode=pl.Buffered(1))
    row = lambda width: pl.BlockSpec((tm, width), lambda i: (i, 0))
    out_shapes = (
        jax.ShapeDtypeStruct((t, 2 * d_m), BF16),
        jax.ShapeDtypeStruct((t, d_m), BF16),
        jax.ShapeDtypeStruct((t, d_m), BF16),
        jax.ShapeDtypeStruct((t, d_n), BF16),
        jax.ShapeDtypeStruct((t, d_n), BF16),
        jax.ShapeDtypeStruct((t, d_n), BF16),
        jax.ShapeDtypeStruct((N_GATE, t), F32),
    )
    return pl.pallas_call(
        functools.partial(_inproj_body, d_m=d_m, d_n=d_n),
        grid=(t // tm,),
        in_specs=[row(d), _const_spec((1, d)), fixed((d, 4 * d_m), lambda i: (0, 0)), _const_spec(w_n.shape),
                  fixed((d, GATE_PAD), lambda i: (0, 4 * d_m // GATE_PAD)), _const_spec((N_GATE, 1)),
                  _const_spec((d_n, d_n)), _const_spec((1, d_n)), _const_spec((1, d_n))],
        out_specs=(row(2 * d_m), row(d_m), row(d_m), row(d_n), row(d_n), row(d_n),
                   pl.BlockSpec((N_GATE, tm), lambda i: (0, i))),
        out_shape=out_shapes,
        compiler_params=pltpu.CompilerParams(dimension_semantics=("arbitrary",),
                                             vmem_limit_bytes=44 * MIB),
        name="inproj",
    )(x2, g, w_all, w_n, w_all, gate_b, bd, qg, kg)


def _log_sigmoid(x):
    return jnp.minimum(x, 0.0) - jnp.log1p(jnp.exp(-jnp.abs(x)))


def _sigmoid(x):
    return 1.0 / (1.0 + jnp.exp(-x))


def _mlstm_body(q_ref, k_ref, v_ref, o_ref, gate_ref, cw_ref, cb_ref, ng_ref, tri3_ref, eye2_ref,
                out_ref, qs_ref, kt_ref, cs_ref, ccur_ref, brow_ref, crow_ref, cmax_ref, wt_ref, bl_ref,
                ml_ref, ms_ref, s_ref, einv_ref, *, seq):
    L = CHUNK
    d = MLSTM_HEAD_DIM
    nc = seq // L

    row_id = lax.broadcasted_iota(jnp.int32, (L, d), 0)
    col_id = lax.broadcasted_iota(jnp.int32, (L, d), 1)
    pos_id = lax.broadcasted_iota(jnp.int32, (nc, L), 1)
    ones_blk = jnp.ones((L, d), BF16)

    def conv_silu(src_ref, c, s0, w, b):
        x = src_ref[pl.ds(s0, L), :].astype(F32)
        p0 = pl.multiple_of(jnp.maximum(s0 - 16, 0), 16)
        n0 = pl.multiple_of(jnp.minimum(s0 + L, seq - 16), 16)
        prev_row = src_ref[pl.ds(p0, 16), :][15:16, :].astype(F32)
        next_row = src_ref[pl.ds(n0, 16), :][0:1, :].astype(F32)
        prev_row = jnp.where(c > 0, prev_row, 0.0)
        next_row = jnp.where(c < nc - 1, next_row, 0.0)
        x_prev = jnp.where(row_id == 0, prev_row, pltpu.roll(x, 1, 0))
        x_next = jnp.where(row_id == L - 1, next_row, pltpu.roll(x, L - 1, 0))
        y = w[0:1, :] * x_prev + w[1:2, :] * x + w[2:3, :] * x_next + b
        return y * _sigmoid(y)

    def conv_step(c, carry):
        s0 = pl.multiple_of(c * L, L)
        qs_ref[pl.ds(s0, L), :] = conv_silu(q_ref, c, s0, cw_ref[0], cb_ref[0]).astype(BF16)
        kk = conv_silu(k_ref, c, s0, cw_ref[1], cb_ref[1]) * (d ** -0.5)
        kt_ref[:, pl.ds(s0, L)] = kk.T.astype(BF16)
        return carry

    lax.fori_loop(0, nc, conv_step, 0, unroll=MLSTM_CONV_UNROLL)

    for dirn in (0, 1):
        i_g = gate_ref[2 * dirn]
        f_log = _log_sigmoid(gate_ref[2 * dirn + 1])
        f_cat = jnp.concatenate(_split3(f_log), axis=1)
        brow = jnp.dot(f_cat, tri3_ref[1 - dirn], preferred_element_type=F32)
        b_last = brow[:, L - 1:L] if dirn == 0 else brow[:, 0:1]
        a_row = i_g + b_last - brow
        a_max = jnp.max(a_row, axis=1, keepdims=True)
        crow = i_g - brow
        cmax = crow
        for sh in [1 << e for e in range(L.bit_length() - 1)]:
            if dirn == 0:
                cmax = jnp.maximum(cmax, jnp.where(pos_id >= sh, pltpu.roll(cmax, sh, 1), -jnp.inf))
            else:
                cmax = jnp.maximum(cmax, jnp.where(pos_id < L - sh, pltpu.roll(cmax, L - sh, 1), -jnp.inf))
        brow_ref[dirn] = brow
        crow_ref[dirn] = crow
        cmax_ref[dirn] = cmax
        wt_ref[dirn] = jnp.exp(a_row - a_max)
        bl_ref[dirn] = jnp.broadcast_to(b_last, (nc, L))
        ml_ref[dirn] = jnp.broadcast_to(a_max, (nc, L))

    def v_aug(s0):
        return jnp.concatenate([v_ref[pl.ds(s0, L), :], ones_blk], axis=1)

    ccur_ref[...] = jnp.zeros_like(ccur_ref)

    def scan_step(i, carry):
        new = []
        for dirn, c, m in ((0, i, carry[0]), (1, nc - 1 - i, carry[1])):
            s0 = pl.multiple_of(c * L, L)
            kw = (kt_ref[:, pl.ds(s0, L)].astype(F32) * wt_ref[dirn, pl.ds(c, 1), :]).astype(BF16)
            k_loc = jnp.dot(kw, v_aug(s0), preferred_element_type=F32)
            state = ccur_ref[dirn]
            cs_ref[dirn, c] = state.astype(BF16)
            ms_ref[dirn, pl.ds(c, 1), :] = m
            a_prev = m + bl_ref[dirn, pl.ds(c, 1), :]
            a_max = ml_ref[dirn, pl.ds(c, 1), :]
            m_new = jnp.maximum(a_prev, a_max)
            w_prev = jnp.exp(a_prev - m_new)[:, 0:1]
            w_loc = jnp.exp(a_max - m_new)[:, 0:1]
            ccur_ref[dirn] = w_prev * state + w_loc * k_loc
            new.append(m_new)
        return tuple(new)

    m0 = jnp.zeros((1, L), F32)
    lax.fori_loop(0, nc, scan_step, (m0, m0), unroll=MLSTM_SCAN_UNROLL)

    for dirn in (0, 1):
        mu_all = jnp.maximum(ms_ref[dirn], cmax_ref[dirn])
        cmax_ref[dirn] = mu_all
        brow_ref[dirn] = jnp.minimum(jnp.exp(-(brow_ref[dirn] + mu_all)), F32_BIG)
    lower = col_id <= row_id
    upper = col_id >= row_id

    def weights_stage(c, slot):
        s0 = pl.multiple_of(c * L, L)
        q = qs_ref[pl.ds(s0, L), :]
        qk = jnp.dot(q, kt_ref[:, pl.ds(s0, L)], preferred_element_type=F32)
        for dirn, mask in ((0, lower), (1, upper)):
            rows = []
            for stat_ref in (cmax_ref, brow_ref):
                hi, mid, _ = _split3(stat_ref[dirn, pl.ds(c, 1), :])
                rows.append(jnp.broadcast_to(jnp.concatenate([hi, mid], axis=1), (L, 2 * L)))
            col = lax.dot_general(eye2_ref[...], jnp.concatenate(rows, axis=0), NT_DIMS,
                                  preferred_element_type=F32)
            mu = col[:, :L]
            p = jnp.exp(jnp.where(mask, crow_ref[dirn, pl.ds(c, 1), :] - mu, -jnp.inf))
            q_inter = q.astype(F32) * jnp.exp(ms_ref[dirn, pl.ds(c, 1), :] - mu)
            s_ref[slot, dirn] = jnp.concatenate([qk * p, q_inter], axis=1).astype(BF16)
            einv_ref[slot, dirn] = col[:, L:]

    def output_stage(c, slot):
        s0 = pl.multiple_of(c * L, L)
        vaug = v_aug(s0)
        hsum = None
        for dirn in (0, 1):
            nd = jnp.dot(s_ref[slot, dirn], jnp.concatenate([vaug, cs_ref[dirn, c]], axis=0),
                         preferred_element_type=F32)
            h = nd[:, :d] / jnp.maximum(jnp.abs(nd[:, d:]), einv_ref[slot, dirn])
            hsum = h if hsum is None else hsum + h
        y = hsum * _rms_scale(hsum) * ng_ref[...]
        y = y * _sigmoid(o_ref[pl.ds(s0, L), :].astype(F32))
        out_ref[pl.ds(s0, L), :] = y.astype(out_ref.dtype)

    G = MLSTM_CHUNKS_PER_STEP
    for i in range(G):
        weights_stage(i, i)

    def out_step(j, carry):
        cur = (j % 2) * G
        nxt = G - cur
        for i in range(G):
            output_stage(j * G + i, cur + i)
        for i in range(G):
            weights_stage(jnp.minimum((j + 1) * G + i, nc - 1), nxt + i)
        return carry

    lax.fori_loop(0, nc // G, out_step, 0)


def _mlstm(qk, v, o, gates, conv_w, conv_b, norm_g, tri3, eye2):
    b, seq, _ = v.shape
    H, d, L = N_MLSTM_HEADS, MLSTM_HEAD_DIM, CHUNK
    nc = seq // L
    assert nc % MLSTM_SCAN_UNROLL == 0 and nc % MLSTM_CONV_UNROLL == 0 and L == LANES
    assert nc % MLSTM_CHUNKS_PER_STEP == 0
    col = lambda off: pl.BlockSpec((None, seq, d), lambda bi, hi: (bi, 0, hi + off))
    stat = pltpu.VMEM((2, nc, L), F32)
    slots = 2 * MLSTM_CHUNKS_PER_STEP
    return pl.pallas_call(
        functools.partial(_mlstm_body, seq=seq),
        grid=(b, H),
        in_specs=[
            col(0), col(H), col(0), col(0),
            pl.BlockSpec((4, None, None, nc, L), lambda bi, hi: (0, hi, bi, 0, 0)),
            pl.BlockSpec((None, 2, 3, d), lambda bi, hi: (hi, 0, 0, 0)),
            pl.BlockSpec((None, 2, 1, d), lambda bi, hi: (hi, 0, 0, 0)),
            pl.BlockSpec((None, 1, d), lambda bi, hi: (hi, 0, 0)),
            _const_spec(tri3.shape), _const_spec(eye2.shape),
        ],
        out_specs=pl.BlockSpec((None, seq, d), lambda bi, hi: (bi, 0, hi)),
        out_shape=jax.ShapeDtypeStruct((b, seq, H * d), BF16),
        scratch_shapes=[
            pltpu.VMEM((seq, d), BF16),
            pltpu.VMEM((d, seq), BF16),
            pltpu.VMEM((2, nc, d, 2 * d), BF16),
            pltpu.VMEM((2, d, 2 * d), F32),
            stat, stat, stat,
            stat, stat, stat,
            stat,
            pltpu.VMEM((slots, 2, L, L + d), BF16),
            pltpu.VMEM((slots, 2, L, d), F32),
        ],
        compiler_params=pltpu.CompilerParams(dimension_semantics=("arbitrary", "arbitrary"),
                                             vmem_limit_bytes=52 * MIB),
        name="mlstm",
    )(qk, qk, v, o, gates, conv_w, conv_b, norm_g, tri3, eye2)


def _natten_body(q_ref, k_ref, v_ref, bias_ref, out_ref, pa_ref, pb_ref, *, rows):
    hd = NA_HEAD_DIM
    win = WIN_H * GRID_W
    lane = lax.broadcasted_iota(jnp.int32, (GRID_W, 2 * hd), 1)
    first = lane < hd
    ones_blk = jnp.ones((win, 2 * hd), BF16)

    def window_start(r):
        return jnp.clip(r - WIN_H // 2, 0, rows - WIN_H)

    def prob_stage(r, p_ref, slot):
        rs = window_start(r)
        q = q_ref[pl.ds(pl.multiple_of(r * GRID_W, GRID_W), GRID_W), :]
        zero = jnp.zeros_like(q)
        qs = jnp.concatenate([jnp.where(first, q, zero), jnp.where(first, zero, q)], axis=0)
        kwin = k_ref[pl.ds(pl.multiple_of(rs * GRID_W, GRID_W), win), :]
        off = rs - r + (WIN_H - 1)
        bias = bias_ref[off & 1, :, pl.ds(pl.multiple_of((off >> 1) * LANES, LANES), win)]
        s = lax.dot_general(qs, kwin, NT_DIMS, preferred_element_type=F32) + bias
        p_ref[slot] = jnp.exp(s - jnp.max(s, axis=1, keepdims=True)).astype(BF16)

    def output_stage(r, p_ref, slot):
        rs = window_start(r)
        vwin = v_ref[pl.ds(pl.multiple_of(rs * GRID_W, GRID_W), win), :]
        o = jnp.dot(p_ref[slot], jnp.concatenate([vwin, ones_blk], axis=1), preferred_element_type=F32)
        o = o[:, :2 * hd] / o[:, 2 * hd:]
        out = jnp.where(first, o[:GRID_W], o[GRID_W:])
        out_ref[pl.ds(pl.multiple_of(r * GRID_W, GRID_W), GRID_W), :] = out.astype(out_ref.dtype)

    R = NA_ROWS_PER_STEP
    n_groups = rows // R

    def produce(g, p_ref):
        for i in range(R):
            prob_stage(jnp.minimum(g * R + i, rows - 1), p_ref, i)

    def consume(g, p_ref):
        for i in range(R):
            output_stage(g * R + i, p_ref, i)

    produce(0, pa_ref)

    def group_pair(j, carry):
        g = 2 * j
        produce(g + 1, pb_ref)
        consume(g, pa_ref)
        produce(g + 2, pa_ref)
        consume(g + 1, pb_ref)
        return carry

    lax.fori_loop(0, n_groups // 2, group_pair, 0)


def _natten(nq, nk, nv, bias_tab):
    b, seq, d_n = nq.shape
    pairs = N_NA_HEADS // 2
    width = 2 * NA_HEAD_DIM
    rows = seq // GRID_W
    assert rows % (2 * NA_ROWS_PER_STEP) == 0
    col = pl.BlockSpec((None, seq, width), lambda bi, pi: (bi, 0, pi))
    return pl.pallas_call(
        functools.partial(_natten_body, rows=rows),
        grid=(b, pairs),
        in_specs=[col, col, col,
                  pl.BlockSpec((None, 2, 2 * GRID_W, NA_BIAS_LANES), lambda bi, pi: (pi, 0, 0, 0))],
        out_specs=col,
        out_shape=jax.ShapeDtypeStruct((b, seq, d_n), BF16),
        scratch_shapes=[pltpu.VMEM((NA_ROWS_PER_STEP, 2 * GRID_W, WIN_H * GRID_W), BF16)] * 2,
        compiler_params=pltpu.CompilerParams(dimension_semantics=("arbitrary", "arbitrary"),
                                             vmem_limit_bytes=40 * MIB),
        name="natten",
    )(nq, nk, nv, bias_tab)


def _natten_bias_table(rpb):
    c = np.arange(GRID_W)
    cs = np.clip(c - WIN_W // 2, 0, GRID_W - WIN_W)
    cp = np.arange(GRID_W)
    valid = (cp[None, :] >= cs[:, None]) & (cp[None, :] < cs[:, None] + WIN_W)
    rel = cp[None, None, :] - c[None, :, None] + (WIN_W - 1)
    onehot = (rel == np.arange(2 * WIN_W - 1)[:, None, None]).astype(np.float32)
    n_rel = 2 * WIN_H - 1
    tab = jnp.einsum('phrd,dcq->phcrq', rpb.astype(F32).reshape(N_NA_HEADS // 2, 2, n_rel, 2 * WIN_W - 1),
                     jnp.asarray(onehot), precision=lax.Precision.HIGHEST)
    tab = jnp.where(jnp.asarray(valid)[None, None, :, None, :], tab, NEG_BIG)
    tab = tab.reshape(N_NA_HEADS // 2, 2 * GRID_W, n_rel * GRID_W)
    even = tab[:, :, :NA_BIAS_LANES]
    odd = tab[:, :, GRID_W:GRID_W + NA_BIAS_LANES]
    return jnp.stack([even, odd], axis=1)


def _tail_body(x_ref, ya_ref, yb_ref, p_ref, woa_ref, wob_ref, g2_ref, w1_ref, w2_ref,
               g3_ref, wg_ref, wu_ref, out_ref, *, ff_chunk):
    d_ff = w1_ref.shape[1]
    h = (x_ref[...]
         + jnp.dot(ya_ref[...], woa_ref[...], preferred_element_type=F32)
         + jnp.dot(yb_ref[...], wob_ref[...], preferred_element_type=F32))
    u = (h * _rms_scale(h) * g2_ref[...]).astype(BF16)
    out_ref[...] = h
    for j in range(d_ff // ff_chunk):
        z = jnp.dot(u, w1_ref[:, j * ff_chunk:(j + 1) * ff_chunk], preferred_element_type=F32)
        z = jnp.maximum(z, 0.0)
        out_ref[...] += jnp.dot((z * z).astype(BF16), w2_ref[j * ff_chunk:(j + 1) * ff_chunk, :],
                                preferred_element_type=F32)
    h = out_ref[...]
    u = (h * _rms_scale(h) * g3_ref[...]).astype(BF16)
    gate = _sigmoid(jnp.dot(u, wg_ref[...], preferred_element_type=F32))
    up = jnp.dot(p_ref[...].astype(BF16), wu_ref[...], preferred_element_type=F32)
    out_ref[...] = h + gate * up


def _tail(x2, ya, yb, p2, wo, g2, w1, w2, g3, wg, wu, *, tm, ff_chunk):
    t, d = x2.shape
    d_mix = wo.shape[0]
    assert ya.shape[1] == yb.shape[1] == d_mix // 2 and t % tm == 0
    row = lambda width: pl.BlockSpec((tm, width), lambda i: (i, 0))
    half = lambda k: pl.BlockSpec((d_mix // 2, d), lambda i: (k, 0), pipeline_mode=pl.Buffered(1))
    consts = [g2, w1, w2, g3, wg, wu]
    return pl.pallas_call(
        functools.partial(_tail_body, ff_chunk=ff_chunk),
        grid=(t // tm,),
        in_specs=[row(d), row(ya.shape[1]), row(yb.shape[1]), row(p2.shape[1]), half(0), half(1)]
                 + [_const_spec(c.shape) for c in consts],
        out_specs=row(d),
        out_shape=jax.ShapeDtypeStruct((t, d), F32),
        compiler_params=pltpu.CompilerParams(dimension_semantics=("arbitrary",),
                                             vmem_limit_bytes=52 * MIB),
        name="tail",
    )(x2, ya, yb, p2, wo, wo, *consts)


def kernel(x, p, norm1_g, w_in, conv_w, conv_b, gate_b, mlstm_norm_g, q_norm_g, k_norm_g, rpb,
           w_out, norm2_g, w_ff1, w_ff2, ple_norm_g, w_ple_gate, w_ple_up):
    b, seq, d = x.shape
    depth = w_in.shape[0]
    H, hd, L = N_MLSTM_HEADS, MLSTM_HEAD_DIM, CHUNK
    d_m = H * hd
    d_n = N_NA_HEADS * NA_HEAD_DIM
    t = b * seq
    rows = seq // GRID_W
    assert rows >= WIN_H

    bd = jnp.asarray(np.kron(np.eye(N_NA_HEADS), np.ones((NA_HEAD_DIM, NA_HEAD_DIM))), BF16)
    ri, ci = np.indices((L, L))
    tri = np.stack([ci <= ri, ci >= ri])
    tri3 = jnp.asarray(np.concatenate([tri, tri, tri], axis=1), BF16)
    eye2 = jnp.asarray(np.concatenate([ci == ri, ci == ri], axis=1), BF16)

    h = x.reshape(t, d)
    for i in range(depth):
        wi = w_in[i].astype(BF16)
        w_n = wi[:, 4 * d_m + N_GATE:]
        qg = (q_norm_g[i].reshape(1, d_n) * (NA_HEAD_DIM ** -0.5)).astype(F32)
        kg = k_norm_g[i].reshape(1, d_n).astype(F32)
        qk, mv, mo, nq, nk, nv, gates = _inproj(
            h, norm1_g[i].reshape(1, d), wi, w_n, gate_b[i].reshape(N_GATE, 1), bd, qg, kg,
            d_m=d_m, d_n=d_n, tm=1024)

        cw = conv_w[i].reshape(3, 2, H, hd).transpose(2, 1, 0, 3)
        cb = conv_b[i].reshape(2, H, 1, hd).transpose(1, 0, 2, 3)
        y_a = _mlstm(qk.reshape(b, seq, 2 * d_m), mv.reshape(b, seq, d_m), mo.reshape(b, seq, d_m),
                     gates.reshape(4, H, b, seq // L, L), cw, cb,
                     mlstm_norm_g[i].reshape(H, 1, hd), tri3, eye2)

        bias_tab = _natten_bias_table(rpb[i])
        y_b = _natten(nq.reshape(b, seq, d_n), nk.reshape(b, seq, d_n), nv.reshape(b, seq, d_n), bias_tab)

        h = _tail(h, y_a.reshape(t, d_m), y_b.reshape(t, d_n), p[i].reshape(t, -1),
                  w_out[i].astype(BF16), norm2_g[i].reshape(1, d), w_ff1[i].astype(BF16), w_ff2[i].astype(BF16),
                  ple_norm_g[i].reshape(1, d), w_ple_gate[i].astype(BF16), w_ple_up[i].astype(BF16),
                  tm=1024, ff_chunk=1024)
    return h.reshape(b, seq, d)
```

```python
import functools

import jax
import jax.numpy as jnp
import numpy as np
from jax import lax
from jax.experimental import pallas as pl
from jax.experimental.pallas import tpu as pltpu

F32 = jnp.float32
BF16 = jnp.bfloat16

N_MLSTM_HEADS = 4
MLSTM_HEAD_DIM = 128
N_NA_HEADS = 8
NA_HEAD_DIM = 64
GRID_W = 64
WIN_H = 8
WIN_W = 16
CHUNK = 128
N_GATE = 4 * N_MLSTM_HEADS
RMS_EPS = 1e-6
NEG_BIG = -1e30
F32_BIG = 3e38
NA_BIAS_LANES = (2 * WIN_H - 2) * GRID_W
NA_ROWS_PER_STEP = 8
MLSTM_CHUNKS_PER_STEP = 8
MLSTM_CONV_UNROLL = 4
MLSTM_SCAN_UNROLL = 8

LANES = 128
GATE_PAD = LANES
MIB = 1024 * 1024

INPROJ_TM = 1024
TAIL_TM = 1024
TAIL_FF_CHUNK = 1024
VMEM_LIMIT = {"inproj": 44 * MIB, "mlstm": 52 * MIB, "natten": 40 * MIB, "tail": 52 * MIB}

NT_DIMS = (((1,), (1,)), ((), ()))


def _const_spec(shape):
    return pl.BlockSpec(shape, lambda *_: (0,) * len(shape), pipeline_mode=pl.Buffered(1))


def _rms_scale(x):
    return lax.rsqrt(jnp.mean(x * x, axis=-1, keepdims=True) + RMS_EPS)


def _split3(x):
    hi = x.astype(BF16)
    r1 = x - hi.astype(F32)
    mid = r1.astype(BF16)
    lo = (r1 - mid.astype(F32)).astype(BF16)
    return hi, mid, lo


def _inproj_body(x_ref, g_ref, wm_ref, wn_ref, wg_ref, gb_ref, bd_ref, qg_ref, kg_ref,
                 qk_ref, v_ref, o_ref, nq_ref, nk_ref, nv_ref, gate_ref, *, d_m, d_n):
    x = x_ref[...]
    u = (x * _rms_scale(x) * g_ref[...]).astype(BF16)

    def proj(w_ref, lo, hi):
        return jnp.dot(u, w_ref[:, lo:hi], preferred_element_type=F32)

    def head_norm(y, gain_ref):
        ss = jnp.dot((y * y).astype(BF16), bd_ref[...], preferred_element_type=F32)
        return y * lax.rsqrt(ss * (1.0 / NA_HEAD_DIM) + RMS_EPS) * gain_ref[...]

    qk_ref[...] = proj(wm_ref, 0, 2 * d_m).astype(BF16)
    v_ref[...] = proj(wm_ref, 2 * d_m, 3 * d_m).astype(BF16)
    o_ref[...] = proj(wm_ref, 3 * d_m, 4 * d_m).astype(BF16)
    nq_ref[...] = head_norm(proj(wn_ref, 0, d_n), qg_ref).astype(BF16)
    nk_ref[...] = head_norm(proj(wn_ref, d_n, 2 * d_n), kg_ref).astype(BF16)
    nv_ref[...] = proj(wn_ref, 2 * d_n, 3 * d_n).astype(BF16)
    wg_t = wg_ref[...].astype(F32).T[:N_GATE, :].astype(BF16)
    gate_ref[...] = lax.dot_general(wg_t, u, NT_DIMS, preferred_element_type=F32) + gb_ref[...]


def _inproj(x2, g, w_all, w_n, gate_b, bd, qg, kg, *, d_m, d_n, tm):
    t, d = x2.shape
    assert (4 * d_m) % GATE_PAD == 0
    fixed = functools.partial(pl.BlockSpec, pipeline_mode=pl.Buffered(1))
    row = lambda width: pl.BlockSpec((tm, width), lambda i: (i, 0))
    out_shapes = (
        jax.ShapeDtypeStruct((t, 2 * d_m), BF16),
        jax.ShapeDtypeStruct((t, d_m), BF16),
        jax.ShapeDtypeStruct((t, d_m), BF16),
        jax.ShapeDtypeStruct((t, d_n), BF16),
        jax.ShapeDtypeStruct((t, d_n), BF16),
        jax.ShapeDtypeStruct((t, d_n), BF16),
        jax.ShapeDtypeStruct((N_GATE, t), F32),
    )
    return pl.pallas_call(
        functools.partial(_inproj_body, d_m=d_m, d_n=d_n),
        grid=(t // tm,),
        in_specs=[row(d), _const_spec((1, d)), fixed((d, 4 * d_m), lambda i: (0, 0)), _const_spec(w_n.shape),
                  fixed((d, GATE_PAD), lambda i: (0, 4 * d_m // GATE_PAD)), _const_spec((N_GATE, 1)),
                  _const_spec((d_n, d_n)), _const_spec((1, d_n)), _const_spec((1, d_n))],
        out_specs=(row(2 * d_m), row(d_m), row(d_m), row(d_n), row(d_n), row(d_n),
                   pl.BlockSpec((N_GATE, tm), lambda i: (0, i))),
        out_shape=out_shapes,
        compiler_params=pltpu.CompilerParams(dimension_semantics=("arbitrary",),
                                             vmem_limit_bytes=VMEM_LIMIT["inproj"]),
        name="inproj",
    )(x2, g, w_all, w_n, w_all, gate_b, bd, qg, kg)


def _log_sigmoid(x):
    return jnp.minimum(x, 0.0) - jnp.log1p(jnp.exp(-jnp.abs(x)))


def _sigmoid(x):
    return 1.0 / (1.0 + jnp.exp(-x))


def _mlstm_body(q_ref, k_ref, v_ref, o_ref, gate_ref, cw_ref, cb_ref, ng_ref, tri3_ref, eye2_ref,
                out_ref, qs_ref, kt_ref, cs_ref, ccur_ref, brow_ref, crow_ref, cmax_ref, arow_ref, bl_ref,
                ml_ref, ms_ref, s_ref, einv_ref, *, seq):
    L = CHUNK
    d = MLSTM_HEAD_DIM
    nc = seq // L

    row_id = lax.broadcasted_iota(jnp.int32, (L, d), 0)
    col_id = lax.broadcasted_iota(jnp.int32, (L, d), 1)
    pos_id = lax.broadcasted_iota(jnp.int32, (nc, L), 1)
    ones_blk = jnp.ones((L, d), BF16)

    def conv_silu(src_ref, c, s0, w, b):
        x = src_ref[pl.ds(s0, L), :].astype(F32)
        p0 = pl.multiple_of(jnp.maximum(s0 - 16, 0), 16)
        n0 = pl.multiple_of(jnp.minimum(s0 + L, seq - 16), 16)
        prev_row = src_ref[pl.ds(p0, 16), :][15:16, :].astype(F32)
        next_row = src_ref[pl.ds(n0, 16), :][0:1, :].astype(F32)
        prev_row = jnp.where(c > 0, prev_row, 0.0)
        next_row = jnp.where(c < nc - 1, next_row, 0.0)
        x_prev = jnp.where(row_id == 0, prev_row, pltpu.roll(x, 1, 0))
        x_next = jnp.where(row_id == L - 1, next_row, pltpu.roll(x, L - 1, 0))
        y = w[0:1, :] * x_prev + w[1:2, :] * x + w[2:3, :] * x_next + b
        return y * _sigmoid(y)

    def conv_step(c, carry):
        s0 = pl.multiple_of(c * L, L)
        qs_ref[pl.ds(s0, L), :] = conv_silu(q_ref, c, s0, cw_ref[0], cb_ref[0]).astype(BF16)
        kk = conv_silu(k_ref, c, s0, cw_ref[1], cb_ref[1]) * (d ** -0.5)
        kt_ref[:, pl.ds(s0, L)] = kk.T.astype(BF16)
        return carry

    lax.fori_loop(0, nc, conv_step, 0, unroll=MLSTM_CONV_UNROLL)

    for dirn in (0, 1):
        i_g = gate_ref[2 * dirn]
        f_log = _log_sigmoid(gate_ref[2 * dirn + 1])
        f_cat = jnp.concatenate(_split3(f_log), axis=1)
        brow = jnp.dot(f_cat, tri3_ref[1 - dirn], preferred_element_type=F32)
        b_last = brow[:, L - 1:L] if dirn == 0 else brow[:, 0:1]
        a_row = i_g + b_last - brow
        a_max = jnp.max(a_row, axis=1, keepdims=True)
        crow = i_g - brow
        cmax = crow
        for sh in [1 << e for e in range(L.bit_length() - 1)]:
            if dirn == 0:
                cmax = jnp.maximum(cmax, jnp.where(pos_id >= sh, pltpu.roll(cmax, sh, 1), -jnp.inf))
            else:
                cmax = jnp.maximum(cmax, jnp.where(pos_id < L - sh, pltpu.roll(cmax, L - sh, 1), -jnp.inf))
        brow_ref[dirn] = brow
        crow_ref[dirn] = crow
        cmax_ref[dirn] = cmax
        arow_ref[dirn] = a_row
        bl_ref[dirn] = jnp.broadcast_to(b_last, (nc, L))
        ml_ref[dirn] = jnp.broadcast_to(a_max, (nc, L))

    def v_aug(s0):
        return jnp.concatenate([v_ref[pl.ds(s0, L), :], ones_blk], axis=1)

    ccur_ref[...] = jnp.zeros_like(ccur_ref)

    def scan_step(i, carry):
        new = []
        for dirn, c, m in ((0, i, carry[0]), (1, nc - 1 - i, carry[1])):
            s0 = pl.multiple_of(c * L, L)
            state = ccur_ref[dirn]
            cs_ref[dirn, c] = state.astype(BF16)
            ms_ref[dirn, pl.ds(c, 1), :] = m
            a_prev = m + bl_ref[dirn, pl.ds(c, 1), :]
            m_new = jnp.maximum(a_prev, ml_ref[dirn, pl.ds(c, 1), :])
            w_row = jnp.exp(arow_ref[dirn, pl.ds(c, 1), :] - m_new)
            kw = (kt_ref[:, pl.ds(s0, L)].astype(F32) * w_row).astype(BF16)
            k_loc = jnp.dot(kw, v_aug(s0), preferred_element_type=F32)
            ccur_ref[dirn] = jnp.exp(a_prev - m_new)[:, 0:1] * state + k_loc
            new.append(m_new)
        return tuple(new)

    m0 = jnp.zeros((1, L), F32)
    lax.fori_loop(0, nc, scan_step, (m0, m0), unroll=MLSTM_SCAN_UNROLL)

    for dirn in (0, 1):
        mu_all = jnp.maximum(ms_ref[dirn], cmax_ref[dirn])
        cmax_ref[dirn] = mu_all
        brow_ref[dirn] = jnp.minimum(jnp.exp(-(brow_ref[dirn] + mu_all)), F32_BIG)
    lower = col_id <= row_id
    upper = col_id >= row_id

    def weights_stage(c, slot):
        s0 = pl.multiple_of(c * L, L)
        q = qs_ref[pl.ds(s0, L), :]
        qk = jnp.dot(q, kt_ref[:, pl.ds(s0, L)], preferred_element_type=F32)
        for dirn, mask in ((0, lower), (1, upper)):
            rows = []
            for stat_ref in (cmax_ref, brow_ref):
                hi, mid, _ = _split3(stat_ref[dirn, pl.ds(c, 1), :])
                rows.append(jnp.broadcast_to(jnp.concatenate([hi, mid], axis=1), (L, 2 * L)))
            col = lax.dot_general(eye2_ref[...], jnp.concatenate(rows, axis=0), NT_DIMS,
                                  preferred_element_type=F32)
            mu = col[:, :L]
            p = jnp.exp(jnp.where(mask, crow_ref[dirn, pl.ds(c, 1), :] - mu, -jnp.inf))
            q_inter = q.astype(F32) * jnp.exp(ms_ref[dirn, pl.ds(c, 1), :] - mu)
            s_ref[slot, dirn] = jnp.concatenate([qk * p, q_inter], axis=1).astype(BF16)
            einv_ref[slot, dirn] = col[:, L:]

    def output_stage(c, slot):
        s0 = pl.multiple_of(c * L, L)
        vaug = v_aug(s0)
        hsum = None
        for dirn in (0, 1):
            nd = jnp.dot(s_ref[slot, dirn], jnp.concatenate([vaug, cs_ref[dirn, c]], axis=0),
                         preferred_element_type=F32)
            h = nd[:, :d] / jnp.maximum(jnp.abs(nd[:, d:]), einv_ref[slot, dirn])
            hsum = h if hsum is None else hsum + h
        y = hsum * _rms_scale(hsum) * ng_ref[...]
        y = y * _sigmoid(o_ref[pl.ds(s0, L), :].astype(F32))
        out_ref[pl.ds(s0, L), :] = y.astype(out_ref.dtype)

    G = MLSTM_CHUNKS_PER_STEP
    for i in range(G):
        weights_stage(i, i)

    def out_step(j, carry):
        cur = (j % 2) * G
        nxt = G - cur
        for i in range(G):
            output_stage(j * G + i, cur + i)
        for i in range(G):
            weights_stage(jnp.minimum((j + 1) * G + i, nc - 1), nxt + i)
        return carry

    lax.fori_loop(0, nc // G, out_step, 0)


def _mlstm(qk, v, o, gates, conv_w, conv_b, norm_g, tri3, eye2):
    b, seq, _ = v.shape
    H, d, L = N_MLSTM_HEADS, MLSTM_HEAD_DIM, CHUNK
    nc = seq // L
    assert nc % MLSTM_SCAN_UNROLL == 0 and nc % MLSTM_CONV_UNROLL == 0 and L == LANES
    assert nc % MLSTM_CHUNKS_PER_STEP == 0
    col = lambda off: pl.BlockSpec((None, seq, d), lambda bi, hi: (bi, 0, hi + off))
    stat = pltpu.VMEM((2, nc, L), F32)
    slots = 2 * MLSTM_CHUNKS_PER_STEP
    return pl.pallas_call(
        functools.partial(_mlstm_body, seq=seq),
        grid=(b, H),
        in_specs=[
            col(0), col(H), col(0), col(0),
            pl.BlockSpec((4, None, None, nc, L), lambda bi, hi: (0, hi, bi, 0, 0)),
            pl.BlockSpec((None, 2, 3, d), lambda bi, hi: (hi, 0, 0, 0)),
            pl.BlockSpec((None, 2, 1, d), lambda bi, hi: (hi, 0, 0, 0)),
            pl.BlockSpec((None, 1, d), lambda bi, hi: (hi, 0, 0)),
            _const_spec(tri3.shape), _const_spec(eye2.shape),
        ],
        out_specs=pl.BlockSpec((None, seq, d), lambda bi, hi: (bi, 0, hi)),
        out_shape=jax.ShapeDtypeStruct((b, seq, H * d), BF16),
        scratch_shapes=[
            pltpu.VMEM((seq, d), BF16),
            pltpu.VMEM((d, seq), BF16),
            pltpu.VMEM((2, nc, d, 2 * d), BF16),
            pltpu.VMEM((2, d, 2 * d), F32),
            stat, stat, stat,
            stat, stat, stat,
            stat,
            pltpu.VMEM((slots, 2, L, L + d), BF16),
            pltpu.VMEM((slots, 2, L, d), F32),
        ],
        compiler_params=pltpu.CompilerParams(dimension_semantics=("arbitrary", "arbitrary"),
                                             vmem_limit_bytes=VMEM_LIMIT["mlstm"]),
        name="mlstm",
    )(qk, qk, v, o, gates, conv_w, conv_b, norm_g, tri3, eye2)


def _natten_body(q_ref, k_ref, v_ref, bias_ref, out_ref, pa_ref, pb_ref, *, rows):
    hd = NA_HEAD_DIM
    win = WIN_H * GRID_W
    lane = lax.broadcasted_iota(jnp.int32, (GRID_W, 2 * hd), 1)
    first = lane < hd
    ones_blk = jnp.ones((win, 2 * hd), BF16)

    def window_start(r):
        return jnp.clip(r - WIN_H // 2, 0, rows - WIN_H)

    def prob_stage(r, p_ref, slot):
        rs = window_start(r)
        q = q_ref[pl.ds(pl.multiple_of(r * GRID_W, GRID_W), GRID_W), :]
        zero = jnp.zeros_like(q)
        qs = jnp.concatenate([jnp.where(first, q, zero), jnp.where(first, zero, q)], axis=0)
        kwin = k_ref[pl.ds(pl.multiple_of(rs * GRID_W, GRID_W), win), :]
        off = rs - r + (WIN_H - 1)
        bias = bias_ref[off & 1, :, pl.ds(pl.multiple_of((off >> 1) * LANES, LANES), win)]
        s = lax.dot_general(qs, kwin, NT_DIMS, preferred_element_type=F32) + bias
        p_ref[slot] = jnp.exp(s - jnp.max(s, axis=1, keepdims=True)).astype(BF16)

    def output_stage(r, p_ref, slot):
        rs = window_start(r)
        vwin = v_ref[pl.ds(pl.multiple_of(rs * GRID_W, GRID_W), win), :]
        o = jnp.dot(p_ref[slot], jnp.concatenate([vwin, ones_blk], axis=1), preferred_element_type=F32)
        o = o[:, :2 * hd] / o[:, 2 * hd:]
        out = jnp.where(first, o[:GRID_W], o[GRID_W:])
        out_ref[pl.ds(pl.multiple_of(r * GRID_W, GRID_W), GRID_W), :] = out.astype(out_ref.dtype)

    R = NA_ROWS_PER_STEP
    n_groups = rows // R

    def produce(g, p_ref):
        for i in range(R):
            prob_stage(jnp.minimum(g * R + i, rows - 1), p_ref, i)

    def consume(g, p_ref):
        for i in range(R):
            output_stage(g * R + i, p_ref, i)

    produce(0, pa_ref)

    def group_pair(j, carry):
        g = 2 * j
        produce(g + 1, pb_ref)
        consume(g, pa_ref)
        produce(g + 2, pa_ref)
        consume(g + 1, pb_ref)
        return carry

    lax.fori_loop(0, n_groups // 2, group_pair, 0)


def _natten(nq, nk, nv, bias_tab):
    b, seq, d_n = nq.shape
    pairs = N_NA_HEADS // 2
    width = 2 * NA_HEAD_DIM
    rows = seq // GRID_W
    assert rows % (2 * NA_ROWS_PER_STEP) == 0
    col = pl.BlockSpec((None, seq, width), lambda bi, pi: (bi, 0, pi))
    return pl.pallas_call(
        functools.partial(_natten_body, rows=rows),
        grid=(b, pairs),
        in_specs=[col, col, col,
                  pl.BlockSpec((None, 2, 2 * GRID_W, NA_BIAS_LANES), lambda bi, pi: (pi, 0, 0, 0))],
        out_specs=col,
        out_shape=jax.ShapeDtypeStruct((b, seq, d_n), BF16),
        scratch_shapes=[pltpu.VMEM((NA_ROWS_PER_STEP, 2 * GRID_W, WIN_H * GRID_W), BF16)] * 2,
        compiler_params=pltpu.CompilerParams(dimension_semantics=("arbitrary", "arbitrary"),
                                             vmem_limit_bytes=VMEM_LIMIT["natten"]),
        name="natten",
    )(nq, nk, nv, bias_tab)


def _natten_bias_table(rpb):
    c = np.arange(GRID_W)
    cs = np.clip(c - WIN_W // 2, 0, GRID_W - WIN_W)
    cp = np.arange(GRID_W)
    valid = (cp[None, :] >= cs[:, None]) & (cp[None, :] < cs[:, None] + WIN_W)
    rel = cp[None, None, :] - c[None, :, None] + (WIN_W - 1)
    onehot = (rel == np.arange(2 * WIN_W - 1)[:, None, None]).astype(np.float32)
    n_rel = 2 * WIN_H - 1
    tab = jnp.einsum('phrd,dcq->phcrq', rpb.astype(F32).reshape(N_NA_HEADS // 2, 2, n_rel, 2 * WIN_W - 1),
                     jnp.asarray(onehot), precision=lax.Precision.HIGHEST)
    tab = jnp.where(jnp.asarray(valid)[None, None, :, None, :], tab, NEG_BIG)
    tab = tab.reshape(N_NA_HEADS // 2, 2 * GRID_W, n_rel * GRID_W)
    even = tab[:, :, :NA_BIAS_LANES]
    odd = tab[:, :, GRID_W:GRID_W + NA_BIAS_LANES]
    return jnp.stack([even, odd], axis=1)


def _tail_body(x_ref, ya_ref, yb_ref, p_ref, woa_ref, wob_ref, g2_ref, w1_ref, w2_ref,
               g3_ref, wg_ref, wu_ref, out_ref, *, ff_chunk):
    d_ff = w1_ref.shape[1]
    h = (x_ref[...]
         + jnp.dot(ya_ref[...], woa_ref[...], preferred_element_type=F32)
         + jnp.dot(yb_ref[...], wob_ref[...], preferred_element_type=F32))
    u = (h * _rms_scale(h) * g2_ref[...]).astype(BF16)
    out_ref[...] = h
    for j in range(d_ff // ff_chunk):
        z = jnp.dot(u, w1_ref[:, j * ff_chunk:(j + 1) * ff_chunk], preferred_element_type=F32)
        z = jnp.maximum(z, 0.0)
        out_ref[...] += jnp.dot((z * z).astype(BF16), w2_ref[j * ff_chunk:(j + 1) * ff_chunk, :],
                                preferred_element_type=F32)
    h = out_ref[...]
    u = (h * _rms_scale(h) * g3_ref[...]).astype(BF16)
    gate = _sigmoid(jnp.dot(u, wg_ref[...], preferred_element_type=F32))
    up = jnp.dot(p_ref[...].astype(BF16), wu_ref[...], preferred_element_type=F32)
    out_ref[...] = h + gate * up


def _tail(x2, ya, yb, p2, wo, g2, w1, w2, g3, wg, wu, *, tm, ff_chunk):
    t, d = x2.shape
    d_mix = wo.shape[0]
    assert ya.shape[1] == yb.shape[1] == d_mix // 2 and t % tm == 0
    row = lambda width: pl.BlockSpec((tm, width), lambda i: (i, 0))
    half = lambda k: pl.BlockSpec((d_mix // 2, d), lambda i: (k, 0), pipeline_mode=pl.Buffered(1))
    consts = [g2, w1, w2, g3, wg, wu]
    return pl.pallas_call(
        functools.partial(_tail_body, ff_chunk=ff_chunk),
        grid=(t // tm,),
        in_specs=[row(d), row(ya.shape[1]), row(yb.shape[1]), row(p2.shape[1]), half(0), half(1)]
                 + [_const_spec(c.shape) for c in consts],
        out_specs=row(d),
        out_shape=jax.ShapeDtypeStruct((t, d), F32),
        compiler_params=pltpu.CompilerParams(dimension_semantics=("arbitrary",),
                                             vmem_limit_bytes=VMEM_LIMIT["tail"]),
        name="tail",
    )(x2, ya, yb, p2, wo, wo, *consts)


def kernel(x, p, norm1_g, w_in, conv_w, conv_b, gate_b, mlstm_norm_g, q_norm_g, k_norm_g, rpb,
           w_out, norm2_g, w_ff1, w_ff2, ple_norm_g, w_ple_gate, w_ple_up):
    b, seq, d = x.shape
    depth = w_in.shape[0]
    H, hd, L = N_MLSTM_HEADS, MLSTM_HEAD_DIM, CHUNK
    d_m = H * hd
    d_n = N_NA_HEADS * NA_HEAD_DIM
    t = b * seq
    rows = seq // GRID_W
    assert rows >= WIN_H

    bd = jnp.asarray(np.kron(np.eye(N_NA_HEADS), np.ones((NA_HEAD_DIM, NA_HEAD_DIM))), BF16)
    ri, ci = np.indices((L, L))
    tri = np.stack([ci <= ri, ci >= ri])
    tri3 = jnp.asarray(np.concatenate([tri, tri, tri], axis=1), BF16)
    eye2 = jnp.asarray(np.concatenate([ci == ri, ci == ri], axis=1), BF16)

    h = x.reshape(t, d)
    for i in range(depth):
        wi = w_in[i].astype(BF16)
        w_n = wi[:, 4 * d_m + N_GATE:]
        qg = (q_norm_g[i].reshape(1, d_n) * (NA_HEAD_DIM ** -0.5)).astype(F32)
        kg = k_norm_g[i].reshape(1, d_n).astype(F32)
        qk, mv, mo, nq, nk, nv, gates = _inproj(
            h, norm1_g[i].reshape(1, d), wi, w_n, gate_b[i].reshape(N_GATE, 1), bd, qg, kg,
            d_m=d_m, d_n=d_n, tm=INPROJ_TM)

        cw = conv_w[i].reshape(3, 2, H, hd).transpose(2, 1, 0, 3)
        cb = conv_b[i].reshape(2, H, 1, hd).transpose(1, 0, 2, 3)
        y_a = _mlstm(qk.reshape(b, seq, 2 * d_m), mv.reshape(b, seq, d_m), mo.reshape(b, seq, d_m),
                     gates.reshape(4, H, b, seq // L, L), cw, cb,
                     mlstm_norm_g[i].reshape(H, 1, hd), tri3, eye2)

        bias_tab = _natten_bias_table(rpb[i])
        y_b = _natten(nq.reshape(b, seq, d_n), nk.reshape(b, seq, d_n), nv.reshape(b, seq, d_n), bias_tab)

        h = _tail(h, y_a.reshape(t, d_m), y_b.reshape(t, d_n), p[i].reshape(t, -1),
                  w_out[i].astype(BF16), norm2_g[i].reshape(1, d), w_ff1[i].astype(BF16), w_ff2[i].astype(BF16),
                  ple_norm_g[i].reshape(1, d), w_ple_gate[i].astype(BF16), w_ple_up[i].astype(BF16),
                  tm=TAIL_TM, ff_chunk=TAIL_FF_CHUNK)
    return h.reshape(b, seq, d)
```

```python
import functools

import jax
import jax.numpy as jnp
import numpy as np
from jax import lax
from jax.experimental import pallas as pl
from jax.experimental.pallas import tpu as pltpu

F32 = jnp.float32
BF16 = jnp.bfloat16

N_MLSTM_HEADS = 4
MLSTM_HEAD_DIM = 128
N_NA_HEADS = 8
NA_HEAD_DIM = 64
GRID_W = 64
WIN_H = 8
WIN_W = 16
CHUNK = 128
N_GATE = 4 * N_MLSTM_HEADS
RMS_EPS = 1e-6
NEG_BIG = -1e30
F32_BIG = 3e38
NA_BIAS_LANES = (2 * WIN_H - 2) * GRID_W
NA_ROWS_PER_STEP = 8
MLSTM_CHUNKS_PER_STEP = 8
MLSTM_CONV_UNROLL = 8
MLSTM_SCAN_UNROLL = 16

LANES = 128
GATE_PAD = LANES
MIB = 1024 * 1024

INPROJ_TM = 1024
TAIL_TM = 1024
TAIL_FF_CHUNK = 1024
VMEM_LIMIT = {"inproj": 44 * MIB, "mlstm": 52 * MIB, "natten": 40 * MIB, "tail": 52 * MIB}

NT_DIMS = (((1,), (1,)), ((), ()))


def _const_spec(shape):
    return pl.BlockSpec(shape, lambda *_: (0,) * len(shape), pipeline_mode=pl.Buffered(1))


def _rms_scale(x):
    return lax.rsqrt(jnp.mean(x * x, axis=-1, keepdims=True) + RMS_EPS)


def _split3(x):
    hi = x.astype(BF16)
    r1 = x - hi.astype(F32)
    mid = r1.astype(BF16)
    lo = (r1 - mid.astype(F32)).astype(BF16)
    return hi, mid, lo


def _inproj_body(x_ref, g_ref, wm_ref, wn_ref, wg_ref, gb_ref, bd_ref, qg_ref, kg_ref,
                 qk_ref, v_ref, o_ref, nq_ref, nk_ref, nv_ref, gate_ref, *, d_m, d_n):
    x = x_ref[...]
    u = (x * _rms_scale(x) * g_ref[...]).astype(BF16)

    def proj(w_ref, lo, hi):
        return jnp.dot(u, w_ref[:, lo:hi], preferred_element_type=F32)

    def head_norm(y, gain_ref):
        ss = jnp.dot((y * y).astype(BF16), bd_ref[...], preferred_element_type=F32)
        return y * lax.rsqrt(ss * (1.0 / NA_HEAD_DIM) + RMS_EPS) * gain_ref[...]

    qk_ref[...] = proj(wm_ref, 0, 2 * d_m).astype(BF16)
    v_ref[...] = proj(wm_ref, 2 * d_m, 3 * d_m).astype(BF16)
    o_ref[...] = proj(wm_ref, 3 * d_m, 4 * d_m).astype(BF16)
    nq_ref[...] = head_norm(proj(wn_ref, 0, d_n), qg_ref).astype(BF16)
    nk_ref[...] = head_norm(proj(wn_ref, d_n, 2 * d_n), kg_ref).astype(BF16)
    nv_ref[...] = proj(wn_ref, 2 * d_n, 3 * d_n).astype(BF16)
    wg_t = wg_ref[...].astype(F32).T[:N_GATE, :].astype(BF16)
    gate_ref[...] = lax.dot_general(wg_t, u, NT_DIMS, preferred_element_type=F32) + gb_ref[...]


def _inproj(x2, g, w_all, w_n, gate_b, bd, qg, kg, *, d_m, d_n, tm):
    t, d = x2.shape
    assert (4 * d_m) % GATE_PAD == 0
    fixed = functools.partial(pl.BlockSpec, pipeline_mode=pl.Buffered(1))
    row = lambda width: pl.BlockSpec((tm, width), lambda i: (i, 0))
    out_shapes = (
        jax.ShapeDtypeStruct((t, 2 * d_m), BF16),
        jax.ShapeDtypeStruct((t, d_m), BF16),
        jax.ShapeDtypeStruct((t, d_m), BF16),
        jax.ShapeDtypeStruct((t, d_n), BF16),
        jax.ShapeDtypeStruct((t, d_n), BF16),
        jax.ShapeDtypeStruct((t, d_n), BF16),
        jax.ShapeDtypeStruct((N_GATE, t), F32),
    )
    return pl.pallas_call(
        functools.partial(_inproj_body, d_m=d_m, d_n=d_n),
        grid=(t // tm,),
        in_specs=[row(d), _const_spec((1, d)), fixed((d, 4 * d_m), lambda i: (0, 0)), _const_spec(w_n.shape),
                  fixed((d, GATE_PAD), lambda i: (0, 4 * d_m // GATE_PAD)), _const_spec((N_GATE, 1)),
                  _const_spec((d_n, d_n)), _const_spec((1, d_n)), _const_spec((1, d_n))],
        out_specs=(row(2 * d_m), row(d_m), row(d_m), row(d_n), row(d_n), row(d_n),
                   pl.BlockSpec((N_GATE, tm), lambda i: (0, i))),
        out_shape=out_shapes,
        compiler_params=pltpu.CompilerParams(dimension_semantics=("arbitrary",),
                                             vmem_limit_bytes=VMEM_LIMIT["inproj"]),
        name="inproj",
    )(x2, g, w_all, w_n, w_all, gate_b, bd, qg, kg)


def _log_sigmoid(x):
    return jnp.minimum(x, 0.0) - jnp.log1p(jnp.exp(-jnp.abs(x)))


def _sigmoid(x):
    return 1.0 / (1.0 + jnp.exp(-x))


def _mlstm_body(q_ref, k_ref, v_ref, o_ref, gate_ref, cw_ref, cb_ref, ng_ref, tri3_ref, eye2_ref,
                out_ref, qs_ref, kt_ref, cs_ref, ccur_ref, brow_ref, crow_ref, cmax_ref, arow_ref, bl_ref,
                ml_ref, ms_ref, s_ref, einv_ref, *, seq):
    L = CHUNK
    d = MLSTM_HEAD_DIM
    nc = seq // L

    row_id = lax.broadcasted_iota(jnp.int32, (L, d), 0)
    col_id = lax.broadcasted_iota(jnp.int32, (L, d), 1)
    pos_id = lax.broadcasted_iota(jnp.int32, (nc, L), 1)
    ones_blk = jnp.ones((L, d), BF16)

    def conv_silu(src_ref, c, s0, w, b):
        x = src_ref[pl.ds(s0, L), :].astype(F32)
        p0 = pl.multiple_of(jnp.maximum(s0 - 16, 0), 16)
        n0 = pl.multiple_of(jnp.minimum(s0 + L, seq - 16), 16)
        prev_row = src_ref[pl.ds(p0, 16), :][15:16, :].astype(F32)
        next_row = src_ref[pl.ds(n0, 16), :][0:1, :].astype(F32)
        prev_row = jnp.where(c > 0, prev_row, 0.0)
        next_row = jnp.where(c < nc - 1, next_row, 0.0)
        x_prev = jnp.where(row_id == 0, prev_row, pltpu.roll(x, 1, 0))
        x_next = jnp.where(row_id == L - 1, next_row, pltpu.roll(x, L - 1, 0))
        y = w[0:1, :] * x_prev + w[1:2, :] * x + w[2:3, :] * x_next + b
        return y * _sigmoid(y)

    def conv_step(c, carry):
        s0 = pl.multiple_of(c * L, L)
        qs_ref[pl.ds(s0, L), :] = conv_silu(q_ref, c, s0, cw_ref[0], cb_ref[0]).astype(BF16)
        kk = conv_silu(k_ref, c, s0, cw_ref[1], cb_ref[1]) * (d ** -0.5)
        kt_ref[:, pl.ds(s0, L)] = kk.T.astype(BF16)
        return carry

    lax.fori_loop(0, nc, conv_step, 0, unroll=MLSTM_CONV_UNROLL)

    for dirn in (0, 1):
        i_g = gate_ref[2 * dirn]
        f_log = _log_sigmoid(gate_ref[2 * dirn + 1])
        f_cat = jnp.concatenate(_split3(f_log), axis=1)
        brow = jnp.dot(f_cat, tri3_ref[1 - dirn], preferred_element_type=F32)
        b_last = brow[:, L - 1:L] if dirn == 0 else brow[:, 0:1]
        a_row = i_g + b_last - brow
        a_max = jnp.max(a_row, axis=1, keepdims=True)
        crow = i_g - brow
        cmax = crow
        for sh in [1 << e for e in range(L.bit_length() - 1)]:
            if dirn == 0:
                cmax = jnp.maximum(cmax, jnp.where(pos_id >= sh, pltpu.roll(cmax, sh, 1), -jnp.inf))
            else:
                cmax = jnp.maximum(cmax, jnp.where(pos_id < L - sh, pltpu.roll(cmax, L - sh, 1), -jnp.inf))
        brow_ref[dirn] = brow
        crow_ref[dirn] = crow
        cmax_ref[dirn] = cmax
        arow_ref[dirn] = a_row
        bl_ref[dirn] = jnp.broadcast_to(b_last, (nc, L))
        ml_ref[dirn] = jnp.broadcast_to(a_max, (nc, L))

    def v_aug(s0):
        return jnp.concatenate([v_ref[pl.ds(s0, L), :], ones_blk], axis=1)

    ccur_ref[...] = jnp.zeros_like(ccur_ref)

    def scan_step(i, carry):
        new = []
        for dirn, c, m in ((0, i, carry[0]), (1, nc - 1 - i, carry[1])):
            s0 = pl.multiple_of(c * L, L)
            state = ccur_ref[dirn]
            cs_ref[dirn, c] = state.astype(BF16)
            ms_ref[dirn, pl.ds(c, 1), :] = m
            a_prev = m + bl_ref[dirn, pl.ds(c, 1), :]
            m_new = jnp.maximum(a_prev, ml_ref[dirn, pl.ds(c, 1), :])
            w_row = jnp.exp(arow_ref[dirn, pl.ds(c, 1), :] - m_new)
            kw = (kt_ref[:, pl.ds(s0, L)].astype(F32) * w_row).astype(BF16)
            k_loc = jnp.dot(kw, v_aug(s0), preferred_element_type=F32)
            ccur_ref[dirn] = jnp.exp(a_prev - m_new)[:, 0:1] * state + k_loc
            new.append(m_new)
        return tuple(new)

    m0 = jnp.zeros((1, L), F32)
    lax.fori_loop(0, nc, scan_step, (m0, m0), unroll=MLSTM_SCAN_UNROLL)

    for dirn in (0, 1):
        mu_all = jnp.maximum(ms_ref[dirn], cmax_ref[dirn])
        cmax_ref[dirn] = mu_all
        brow_ref[dirn] = jnp.minimum(jnp.exp(-(brow_ref[dirn] + mu_all)), F32_BIG)
    lower = col_id <= row_id
    upper = col_id >= row_id

    def weights_stage(c, slot):
        s0 = pl.multiple_of(c * L, L)
        q = qs_ref[pl.ds(s0, L), :]
        qk = jnp.dot(q, kt_ref[:, pl.ds(s0, L)], preferred_element_type=F32)
        for dirn, mask in ((0, lower), (1, upper)):
            rows = []
            for stat_ref in (cmax_ref, brow_ref):
                hi, mid, _ = _split3(stat_ref[dirn, pl.ds(c, 1), :])
                rows.append(jnp.broadcast_to(jnp.concatenate([hi, mid], axis=1), (L, 2 * L)))
            col = lax.dot_general(eye2_ref[...], jnp.concatenate(rows, axis=0), NT_DIMS,
                                  preferred_element_type=F32)
            mu = col[:, :L]
            p = jnp.exp(jnp.where(mask, crow_ref[dirn, pl.ds(c, 1), :] - mu, -jnp.inf))
            q_inter = q.astype(F32) * jnp.exp(ms_ref[dirn, pl.ds(c, 1), :] - mu)
            s_ref[slot, dirn] = jnp.concatenate([qk * p, q_inter], axis=1).astype(BF16)
            einv_ref[slot, dirn] = col[:, L:]

    def output_stage(c, slot):
        s0 = pl.multiple_of(c * L, L)
        vaug = v_aug(s0)
        hsum = None
        for dirn in (0, 1):
            nd = jnp.dot(s_ref[slot, dirn], jnp.concatenate([vaug, cs_ref[dirn, c]], axis=0),
                         preferred_element_type=F32)
            h = nd[:, :d] / jnp.maximum(jnp.abs(nd[:, d:]), einv_ref[slot, dirn])
            hsum = h if hsum is None else hsum + h
        y = hsum * _rms_scale(hsum) * ng_ref[...]
        y = y * _sigmoid(o_ref[pl.ds(s0, L), :].astype(F32))
        out_ref[pl.ds(s0, L), :] = y.astype(out_ref.dtype)

    G = MLSTM_CHUNKS_PER_STEP
    for i in range(G):
        weights_stage(i, i)

    def out_step(j, carry):
        cur = (j % 2) * G
        nxt = G - cur
        for i in range(G):
            output_stage(j * G + i, cur + i)
        for i in range(G):
            weights_stage(jnp.minimum((j + 1) * G + i, nc - 1), nxt + i)
        return carry

    lax.fori_loop(0, nc // G, out_step, 0)


def _mlstm(qk, v, o, gates, conv_w, conv_b, norm_g, tri3, eye2):
    b, seq, _ = v.shape
    H, d, L = N_MLSTM_HEADS, MLSTM_HEAD_DIM, CHUNK
    nc = seq // L
    assert nc % MLSTM_SCAN_UNROLL == 0 and nc % MLSTM_CONV_UNROLL == 0 and L == LANES
    assert nc % MLSTM_CHUNKS_PER_STEP == 0
    col = lambda off: pl.BlockSpec((None, seq, d), lambda bi, hi: (bi, 0, hi + off))
    stat = pltpu.VMEM((2, nc, L), F32)
    slots = 2 * MLSTM_CHUNKS_PER_STEP
    return pl.pallas_call(
        functools.partial(_mlstm_body, seq=seq),
        grid=(b, H),
        in_specs=[
            col(0), col(H), col(0), col(0),
            pl.BlockSpec((4, None, None, nc, L), lambda bi, hi: (0, hi, bi, 0, 0)),
            pl.BlockSpec((None, 2, 3, d), lambda bi, hi: (hi, 0, 0, 0)),
            pl.BlockSpec((None, 2, 1, d), lambda bi, hi: (hi, 0, 0, 0)),
            pl.BlockSpec((None, 1, d), lambda bi, hi: (hi, 0, 0)),
            _const_spec(tri3.shape), _const_spec(eye2.shape),
        ],
        out_specs=pl.BlockSpec((None, seq, d), lambda bi, hi: (bi, 0, hi)),
        out_shape=jax.ShapeDtypeStruct((b, seq, H * d), BF16),
        scratch_shapes=[
            pltpu.VMEM((seq, d), BF16),
            pltpu.VMEM((d, seq), BF16),
            pltpu.VMEM((2, nc, d, 2 * d), BF16),
            pltpu.VMEM((2, d, 2 * d), F32),
            stat, stat, stat,
            stat, stat, stat,
            stat,
            pltpu.VMEM((slots, 2, L, L + d), BF16),
            pltpu.VMEM((slots, 2, L, d), F32),
        ],
        compiler_params=pltpu.CompilerParams(dimension_semantics=("arbitrary", "arbitrary"),
                                             vmem_limit_bytes=VMEM_LIMIT["mlstm"]),
        name="mlstm",
    )(qk, qk, v, o, gates, conv_w, conv_b, norm_g, tri3, eye2)


def _natten_body(q_ref, k_ref, v_ref, bias_ref, out_ref, pa_ref, pb_ref, *, rows):
    hd = NA_HEAD_DIM
    win = WIN_H * GRID_W
    lane = lax.broadcasted_iota(jnp.int32, (GRID_W, 2 * hd), 1)
    first = lane < hd
    ones_blk = jnp.ones((win, 2 * hd), BF16)

    def window_start(r):
        return jnp.clip(r - WIN_H // 2, 0, rows - WIN_H)

    def prob_stage(r, p_ref, slot):
        rs = window_start(r)
        q = q_ref[pl.ds(pl.multiple_of(r * GRID_W, GRID_W), GRID_W), :]
        zero = jnp.zeros_like(q)
        qs = jnp.concatenate([jnp.where(first, q, zero), jnp.where(first, zero, q)], axis=0)
        kwin = k_ref[pl.ds(pl.multiple_of(rs * GRID_W, GRID_W), win), :]
        off = rs - r + (WIN_H - 1)
        bias = bias_ref[off & 1, :, pl.ds(pl.multiple_of((off >> 1) * LANES, LANES), win)]
        s = lax.dot_general(qs, kwin, NT_DIMS, preferred_element_type=F32) + bias
        p_ref[slot] = jnp.exp(s - jnp.max(s, axis=1, keepdims=True)).astype(BF16)

    def output_stage(r, p_ref, slot):
        rs = window_start(r)
        vwin = v_ref[pl.ds(pl.multiple_of(rs * GRID_W, GRID_W), win), :]
        o = jnp.dot(p_ref[slot], jnp.concatenate([vwin, ones_blk], axis=1), preferred_element_type=F32)
        o = o[:, :2 * hd] / o[:, 2 * hd:]
        out = jnp.where(first, o[:GRID_W], o[GRID_W:])
        out_ref[pl.ds(pl.multiple_of(r * GRID_W, GRID_W), GRID_W), :] = out.astype(out_ref.dtype)

    R = NA_ROWS_PER_STEP
    n_groups = rows // R

    def produce(g, p_ref):
        for i in range(R):
            prob_stage(jnp.minimum(g * R + i, rows - 1), p_ref, i)

    def consume(g, p_ref):
        for i in range(R):
            output_stage(g * R + i, p_ref, i)

    produce(0, pa_ref)

    def group_pair(j, carry):
        g = 2 * j
        produce(g + 1, pb_ref)
        consume(g, pa_ref)
        produce(g + 2, pa_ref)
        consume(g + 1, pb_ref)
        return carry

    lax.fori_loop(0, n_groups // 2, group_pair, 0)


def _natten(nq, nk, nv, bias_tab):
    b, seq, d_n = nq.shape
    pairs = N_NA_HEADS // 2
    width = 2 * NA_HEAD_DIM
    rows = seq // GRID_W
    assert rows % (2 * NA_ROWS_PER_STEP) == 0
    col = pl.BlockSpec((None, seq, width), lambda bi, pi: (bi, 0, pi))
    return pl.pallas_call(
        functools.partial(_natten_body, rows=rows),
        grid=(b, pairs),
        in_specs=[col, col, col,
                  pl.BlockSpec((None, 2, 2 * GRID_W, NA_BIAS_LANES), lambda bi, pi: (pi, 0, 0, 0))],
        out_specs=col,
        out_shape=jax.ShapeDtypeStruct((b, seq, d_n), BF16),
        scratch_shapes=[pltpu.VMEM((NA_ROWS_PER_STEP, 2 * GRID_W, WIN_H * GRID_W), BF16)] * 2,
        compiler_params=pltpu.CompilerParams(dimension_semantics=("arbitrary", "arbitrary"),
                                             vmem_limit_bytes=VMEM_LIMIT["natten"]),
        name="natten",
    )(nq, nk, nv, bias_tab)


def _natten_bias_table(rpb):
    c = np.arange(GRID_W)
    cs = np.clip(c - WIN_W // 2, 0, GRID_W - WIN_W)
    cp = np.arange(GRID_W)
    valid = (cp[None, :] >= cs[:, None]) & (cp[None, :] < cs[:, None] + WIN_W)
    rel = cp[None, None, :] - c[None, :, None] + (WIN_W - 1)
    onehot = (rel == np.arange(2 * WIN_W - 1)[:, None, None]).astype(np.float32)
    n_rel = 2 * WIN_H - 1
    tab = jnp.einsum('phrd,dcq->phcrq', rpb.astype(F32).reshape(N_NA_HEADS // 2, 2, n_rel, 2 * WIN_W - 1),
                     jnp.asarray(onehot), precision=lax.Precision.HIGHEST)
    tab = jnp.where(jnp.asarray(valid)[None, None, :, None, :], tab, NEG_BIG)
    tab = tab.reshape(N_NA_HEADS // 2, 2 * GRID_W, n_rel * GRID_W)
    even = tab[:, :, :NA_BIAS_LANES]
    odd = tab[:, :, GRID_W:GRID_W + NA_BIAS_LANES]
    return jnp.stack([even, odd], axis=1)


def _tail_body(x_ref, ya_ref, yb_ref, p_ref, woa_ref, wob_ref, g2_ref, w1_ref, w2_ref,
               g3_ref, wg_ref, wu_ref, out_ref, *, ff_chunk):
    d_ff = w1_ref.shape[1]
    h = (x_ref[...]
         + jnp.dot(ya_ref[...], woa_ref[...], preferred_element_type=F32)
         + jnp.dot(yb_ref[...], wob_ref[...], preferred_element_type=F32))
    u = (h * _rms_scale(h) * g2_ref[...]).astype(BF16)
    out_ref[...] = h
    for j in range(d_ff // ff_chunk):
        z = jnp.dot(u, w1_ref[:, j * ff_chunk:(j + 1) * ff_chunk], preferred_element_type=F32)
        z = jnp.maximum(z, 0.0)
        out_ref[...] += jnp.dot((z * z).astype(BF16), w2_ref[j * ff_chunk:(j + 1) * ff_chunk, :],
                                preferred_element_type=F32)
    h = out_ref[...]
    u = (h * _rms_scale(h) * g3_ref[...]).astype(BF16)
    gate = _sigmoid(jnp.dot(u, wg_ref[...], preferred_element_type=F32))
    up = jnp.dot(p_ref[...].astype(BF16), wu_ref[...], preferred_element_type=F32)
    out_ref[...] = h + gate * up


def _tail(x2, ya, yb, p2, wo, g2, w1, w2, g3, wg, wu, *, tm, ff_chunk):
    t, d = x2.shape
    d_mix = wo.shape[0]
    assert ya.shape[1] == yb.shape[1] == d_mix // 2 and t % tm == 0
    row = lambda width: pl.BlockSpec((tm, width), lambda i: (i, 0))
    half = lambda k: pl.BlockSpec((d_mix // 2, d), lambda i: (k, 0), pipeline_mode=pl.Buffered(1))
    consts = [g2, w1, w2, g3, wg, wu]
    return pl.pallas_call(
        functools.partial(_tail_body, ff_chunk=ff_chunk),
        grid=(t // tm,),
        in_specs=[row(d), row(ya.shape[1]), row(yb.shape[1]), row(p2.shape[1]), half(0), half(1)]
                 + [_const_spec(c.shape) for c in consts],
        out_specs=row(d),
        out_shape=jax.ShapeDtypeStruct((t, d), F32),
        compiler_params=pltpu.CompilerParams(dimension_semantics=("arbitrary",),
                                             vmem_limit_bytes=VMEM_LIMIT["tail"]),
        name="tail",
    )(x2, ya, yb, p2, wo, wo, *consts)


def kernel(x, p, norm1_g, w_in, conv_w, conv_b, gate_b, mlstm_norm_g, q_norm_g, k_norm_g, rpb,
           w_out, norm2_g, w_ff1, w_ff2, ple_norm_g, w_ple_gate, w_ple_up):
    b, seq, d = x.shape
    depth = w_in.shape[0]
    H, hd, L = N_MLSTM_HEADS, MLSTM_HEAD_DIM, CHUNK
    d_m = H * hd
    d_n = N_NA_HEADS * NA_HEAD_DIM
    t = b * seq
    rows = seq // GRID_W
    assert rows >= WIN_H

    bd = jnp.asarray(np.kron(np.eye(N_NA_HEADS), np.ones((NA_HEAD_DIM, NA_HEAD_DIM))), BF16)
    ri, ci = np.indices((L, L))
    tri = np.stack([ci <= ri, ci >= ri])
    tri3 = jnp.asarray(np.concatenate([tri, tri, tri], axis=1), BF16)
    eye2 = jnp.asarray(np.concatenate([ci == ri, ci == ri], axis=1), BF16)

    h = x.reshape(t, d)
    for i in range(depth):
        wi = w_in[i].astype(BF16)
        w_n = wi[:, 4 * d_m + N_GATE:]
        qg = (q_norm_g[i].reshape(1, d_n) * (NA_HEAD_DIM ** -0.5)).astype(F32)
        kg = k_norm_g[i].reshape(1, d_n).astype(F32)
        qk, mv, mo, nq, nk, nv, gates = _inproj(
            h, norm1_g[i].reshape(1, d), wi, w_n, gate_b[i].reshape(N_GATE, 1), bd, qg, kg,
            d_m=d_m, d_n=d_n, tm=INPROJ_TM)

        cw = conv_w[i].reshape(3, 2, H, hd).transpose(2, 1, 0, 3)
        cb = conv_b[i].reshape(2, H, 1, hd).transpose(1, 0, 2, 3)
        y_a = _mlstm(qk.reshape(b, seq, 2 * d_m), mv.reshape(b, seq, d_m), mo.reshape(b, seq, d_m),
                     gates.reshape(4, H, b, seq // L, L), cw, cb,
                     mlstm_norm_g[i].reshape(H, 1, hd), tri3, eye2)

        bias_tab = _natten_bias_table(rpb[i])
        y_b = _natten(nq.reshape(b, seq, d_n), nk.reshape(b, seq, d_n), nv.reshape(b, seq, d_n), bias_tab)

        h = _tail(h, y_a.reshape(t, d_m), y_b.reshape(t, d_n), p[i].reshape(t, -1),
                  w_out[i].astype(BF16), norm2_g[i].reshape(1, d), w_ff1[i].astype(BF16), w_ff2[i].astype(BF16),
                  ple_norm_g[i].reshape(1, d), w_ple_gate[i].astype(BF16), w_ple_up[i].astype(BF16),
                  tm=TAIL_TM, ff_chunk=TAIL_FF_CHUNK)
    return h.reshape(b, seq, d)
```

```python
import functools

import jax
import jax.numpy as jnp
import numpy as np
from jax import lax
from jax.experimental import pallas as pl
from jax.experimental.pallas import tpu as pltpu

F32 = jnp.float32
BF16 = jnp.bfloat16

N_MLSTM_HEADS = 4
MLSTM_HEAD_DIM = 128
N_NA_HEADS = 8
NA_HEAD_DIM = 64
GRID_W = 64
WIN_H = 8
WIN_W = 16
CHUNK = 128
N_GATE = 4 * N_MLSTM_HEADS
RMS_EPS = 1e-6
NEG_BIG = -1e30
F32_BIG = 3e38
NA_BIAS_LANES = (2 * WIN_H - 2) * GRID_W
NA_ROWS_PER_STEP = 16
MLSTM_CHUNKS_PER_STEP = 8
MLSTM_CONV_UNROLL = 16
MLSTM_SCAN_UNROLL = 16

LANES = 128
GATE_PAD = LANES
MIB = 1024 * 1024

INPROJ_TM = 1024
TAIL_TM = 1024
TAIL_FF_CHUNK = 1024
VMEM_LIMIT = {"inproj": 44 * MIB, "mlstm": 52 * MIB, "natten": 40 * MIB, "tail": 52 * MIB}

NT_DIMS = (((1,), (1,)), ((), ()))


def _const_spec(shape):
    return pl.BlockSpec(shape, lambda *_: (0,) * len(shape), pipeline_mode=pl.Buffered(1))


def _rms_scale(x):
    return lax.rsqrt(jnp.mean(x * x, axis=-1, keepdims=True) + RMS_EPS)


def _split3(x):
    hi = x.astype(BF16)
    r1 = x - hi.astype(F32)
    mid = r1.astype(BF16)
    lo = (r1 - mid.astype(F32)).astype(BF16)
    return hi, mid, lo


def _inproj_body(x_ref, g_ref, wm_ref, wn_ref, wg_ref, gb_ref, bd_ref, qg_ref, kg_ref,
                 qk_ref, v_ref, o_ref, nq_ref, nk_ref, nv_ref, gate_ref, *, d_m, d_n):
    x = x_ref[...]
    u = (x * _rms_scale(x) * g_ref[...]).astype(BF16)

    def proj(w_ref, lo, hi):
        return jnp.dot(u, w_ref[:, lo:hi], preferred_element_type=F32)

    def head_norm(y, gain_ref):
        ss = jnp.dot((y * y).astype(BF16), bd_ref[...], preferred_element_type=F32)
        return y * lax.rsqrt(ss * (1.0 / NA_HEAD_DIM) + RMS_EPS) * gain_ref[...]

    qk_ref[...] = proj(wm_ref, 0, 2 * d_m).astype(BF16)
    v_ref[...] = proj(wm_ref, 2 * d_m, 3 * d_m).astype(BF16)
    o_ref[...] = proj(wm_ref, 3 * d_m, 4 * d_m).astype(BF16)
    nq_ref[...] = head_norm(proj(wn_ref, 0, d_n), qg_ref).astype(BF16)
    nk_ref[...] = head_norm(proj(wn_ref, d_n, 2 * d_n), kg_ref).astype(BF16)
    nv_ref[...] = proj(wn_ref, 2 * d_n, 3 * d_n).astype(BF16)
    wg_t = wg_ref[...].astype(F32).T[:N_GATE, :].astype(BF16)
    gate_ref[...] = lax.dot_general(wg_t, u, NT_DIMS, preferred_element_type=F32) + gb_ref[...]


def _inproj(x2, g, w_all, w_n, gate_b, bd, qg, kg, *, d_m, d_n, tm):
    t, d = x2.shape
    assert (4 * d_m) % GATE_PAD == 0
    fixed = functools.partial(pl.BlockSpec, pipeline_mode=pl.Buffered(1))
    row = lambda width: pl.BlockSpec((tm, width), lambda i: (i, 0))
    out_shapes = (
        jax.ShapeDtypeStruct((t, 2 * d_m), BF16),
        jax.ShapeDtypeStruct((t, d_m), BF16),
        jax.ShapeDtypeStruct((t, d_m), BF16),
        jax.ShapeDtypeStruct((t, d_n), BF16),
        jax.ShapeDtypeStruct((t, d_n), BF16),
        jax.ShapeDtypeStruct((t, d_n), BF16),
        jax.ShapeDtypeStruct((N_GATE, t), F32),
    )
    return pl.pallas_call(
        functools.partial(_inproj_body, d_m=d_m, d_n=d_n),
        grid=(t // tm,),
        in_specs=[row(d), _const_spec((1, d)), fixed((d, 4 * d_m), lambda i: (0, 0)), _const_spec(w_n.shape),
                  fixed((d, GATE_PAD), lambda i: (0, 4 * d_m // GATE_PAD)), _const_spec((N_GATE, 1)),
                  _const_spec((d_n, d_n)), _const_spec((1, d_n)), _const_spec((1, d_n))],
        out_specs=(row(2 * d_m), row(d_m), row(d_m), row(d_n), row(d_n), row(d_n),
                   pl.BlockSpec((N_GATE, tm), lambda i: (0, i))),
        out_shape=out_shapes,
        compiler_params=pltpu.CompilerParams(dimension_semantics=("arbitrary",),
                                             vmem_limit_bytes=VMEM_LIMIT["inproj"]),
        name="inproj",
    )(x2, g, w_all, w_n, w_all, gate_b, bd, qg, kg)


def _log_sigmoid(x):
    return jnp.minimum(x, 0.0) - jnp.log1p(jnp.exp(-jnp.abs(x)))


def _sigmoid(x):
    return 1.0 / (1.0 + jnp.exp(-x))


def _mlstm_body(q_ref, k_ref, v_ref, o_ref, gate_ref, cw_ref, cb_ref, ng_ref, tri3_ref, eye2_ref,
                out_ref, qs_ref, kt_ref, cs_ref, ccur_ref, brow_ref, crow_ref, cmax_ref, arow_ref, bl_ref,
                ml_ref, ms_ref, s_ref, einv_ref, *, seq):
    L = CHUNK
    d = MLSTM_HEAD_DIM
    nc = seq // L

    row_id = lax.broadcasted_iota(jnp.int32, (L, d), 0)
    col_id = lax.broadcasted_iota(jnp.int32, (L, d), 1)
    pos_id = lax.broadcasted_iota(jnp.int32, (nc, L), 1)
    ones_blk = jnp.ones((L, d), BF16)

    def conv_silu(src_ref, c, s0, w, b):
        x = src_ref[pl.ds(s0, L), :].astype(F32)
        p0 = pl.multiple_of(jnp.maximum(s0 - 16, 0), 16)
        n0 = pl.multiple_of(jnp.minimum(s0 + L, seq - 16), 16)
        prev_row = src_ref[pl.ds(p0, 16), :][15:16, :].astype(F32)
        next_row = src_ref[pl.ds(n0, 16), :][0:1, :].astype(F32)
        prev_row = jnp.where(c > 0, prev_row, 0.0)
        next_row = jnp.where(c < nc - 1, next_row, 0.0)
        x_prev = jnp.where(row_id == 0, prev_row, pltpu.roll(x, 1, 0))
        x_next = jnp.where(row_id == L - 1, next_row, pltpu.roll(x, L - 1, 0))
        y = w[0:1, :] * x_prev + w[1:2, :] * x + w[2:3, :] * x_next + b
        return y * _sigmoid(y)

    def conv_step(c, carry):
        s0 = pl.multiple_of(c * L, L)
        qs_ref[pl.ds(s0, L), :] = conv_silu(q_ref, c, s0, cw_ref[0], cb_ref[0]).astype(BF16)
        kk = conv_silu(k_ref, c, s0, cw_ref[1], cb_ref[1]) * (d ** -0.5)
        kt_ref[:, pl.ds(s0, L)] = kk.T.astype(BF16)
        return carry

    lax.fori_loop(0, nc, conv_step, 0, unroll=MLSTM_CONV_UNROLL)

    for dirn in (0, 1):
        i_g = gate_ref[2 * dirn]
        f_log = _log_sigmoid(gate_ref[2 * dirn + 1])
        f_cat = jnp.concatenate(_split3(f_log), axis=1)
        brow = jnp.dot(f_cat, tri3_ref[1 - dirn], preferred_element_type=F32)
        b_last = brow[:, L - 1:L] if dirn == 0 else brow[:, 0:1]
        a_row = i_g + b_last - brow
        a_max = jnp.max(a_row, axis=1, keepdims=True)
        crow = i_g - brow
        cmax = crow
        for sh in [1 << e for e in range(L.bit_length() - 1)]:
            if dirn == 0:
                cmax = jnp.maximum(cmax, jnp.where(pos_id >= sh, pltpu.roll(cmax, sh, 1), -jnp.inf))
            else:
                cmax = jnp.maximum(cmax, jnp.where(pos_id < L - sh, pltpu.roll(cmax, L - sh, 1), -jnp.inf))
        brow_ref[dirn] = brow
        crow_ref[dirn] = crow
        cmax_ref[dirn] = cmax
        arow_ref[dirn] = a_row
        bl_ref[dirn] = jnp.broadcast_to(b_last, (nc, L))
        ml_ref[dirn] = jnp.broadcast_to(a_max, (nc, L))

    def v_aug(s0):
        return jnp.concatenate([v_ref[pl.ds(s0, L), :], ones_blk], axis=1)

    ccur_ref[...] = jnp.zeros_like(ccur_ref)

    def scan_step(i, carry):
        new = []
        for dirn, c, m in ((0, i, carry[0]), (1, nc - 1 - i, carry[1])):
            s0 = pl.multiple_of(c * L, L)
            state = ccur_ref[dirn]
            cs_ref[dirn, c] = state.astype(BF16)
            ms_ref[dirn, pl.ds(c, 1), :] = m
            a_prev = m + bl_ref[dirn, pl.ds(c, 1), :]
            m_new = jnp.maximum(a_prev, ml_ref[dirn, pl.ds(c, 1), :])
            w_row = jnp.exp(arow_ref[dirn, pl.ds(c, 1), :] - m_new)
            kw = (kt_ref[:, pl.ds(s0, L)].astype(F32) * w_row).astype(BF16)
            k_loc = jnp.dot(kw, v_aug(s0), preferred_element_type=F32)
            ccur_ref[dirn] = jnp.exp(a_prev - m_new)[:, 0:1] * state + k_loc
            new.append(m_new)
        return tuple(new)

    m0 = jnp.zeros((1, L), F32)
    lax.fori_loop(0, nc, scan_step, (m0, m0), unroll=MLSTM_SCAN_UNROLL)

    for dirn in (0, 1):
        mu_all = jnp.maximum(ms_ref[dirn], cmax_ref[dirn])
        cmax_ref[dirn] = mu_all
        brow_ref[dirn] = jnp.minimum(jnp.exp(-(brow_ref[dirn] + mu_all)), F32_BIG)
    lower = col_id <= row_id
    upper = col_id >= row_id

    def weights_stage(c, slot):
        s0 = pl.multiple_of(c * L, L)
        q = qs_ref[pl.ds(s0, L), :]
        qk = jnp.dot(q, kt_ref[:, pl.ds(s0, L)], preferred_element_type=F32)
        for dirn, mask in ((0, lower), (1, upper)):
            rows = []
            for stat_ref in (cmax_ref, brow_ref):
                hi, mid, _ = _split3(stat_ref[dirn, pl.ds(c, 1), :])
                rows.append(jnp.broadcast_to(jnp.concatenate([hi, mid], axis=1), (L, 2 * L)))
            col = lax.dot_general(eye2_ref[...], jnp.concatenate(rows, axis=0), NT_DIMS,
                                  preferred_element_type=F32)
            mu = col[:, :L]
            p = jnp.exp(jnp.where(mask, crow_ref[dirn, pl.ds(c, 1), :] - mu, -jnp.inf))
            q_inter = q.astype(F32) * jnp.exp(ms_ref[dirn, pl.ds(c, 1), :] - mu)
            s_ref[slot, dirn] = jnp.concatenate([qk * p, q_inter], axis=1).astype(BF16)
            einv_ref[slot, dirn] = col[:, L:]

    def output_stage(c, slot):
        s0 = pl.multiple_of(c * L, L)
        vaug = v_aug(s0)
        hsum = None
        for dirn in (0, 1):
            nd = jnp.dot(s_ref[slot, dirn], jnp.concatenate([vaug, cs_ref[dirn, c]], axis=0),
                         preferred_element_type=F32)
            h = nd[:, :d] / jnp.maximum(jnp.abs(nd[:, d:]), einv_ref[slot, dirn])
            hsum = h if hsum is None else hsum + h
        y = hsum * _rms_scale(hsum) * ng_ref[...]
        y = y * _sigmoid(o_ref[pl.ds(s0, L), :].astype(F32))
        out_ref[pl.ds(s0, L), :] = y.astype(out_ref.dtype)

    G = MLSTM_CHUNKS_PER_STEP
    for i in range(G):
        weights_stage(i, i)

    def out_step(j, carry):
        cur = (j % 2) * G
        nxt = G - cur
        for i in range(G):
            output_stage(j * G + i, cur + i)
        for i in range(G):
            weights_stage(jnp.minimum((j + 1) * G + i, nc - 1), nxt + i)
        return carry

    lax.fori_loop(0, nc // G, out_step, 0)


def _mlstm(qk, v, o, gates, conv_w, conv_b, norm_g, tri3, eye2):
    b, seq, _ = v.shape
    H, d, L = N_MLSTM_HEADS, MLSTM_HEAD_DIM, CHUNK
    nc = seq // L
    assert nc % MLSTM_SCAN_UNROLL == 0 and nc % MLSTM_CONV_UNROLL == 0 and L == LANES
    assert nc % MLSTM_CHUNKS_PER_STEP == 0
    col = lambda off: pl.BlockSpec((None, seq, d), lambda bi, hi: (bi, 0, hi + off))
    stat = pltpu.VMEM((2, nc, L), F32)
    slots = 2 * MLSTM_CHUNKS_PER_STEP
    return pl.pallas_call(
        functools.partial(_mlstm_body, seq=seq),
        grid=(b, H),
        in_specs=[
            col(0), col(H), col(0), col(0),
            pl.BlockSpec((4, None, None, nc, L), lambda bi, hi: (0, hi, bi, 0, 0)),
            pl.BlockSpec((None, 2, 3, d), lambda bi, hi: (hi, 0, 0, 0)),
            pl.BlockSpec((None, 2, 1, d), lambda bi, hi: (hi, 0, 0, 0)),
            pl.BlockSpec((None, 1, d), lambda bi, hi: (hi, 0, 0)),
            _const_spec(tri3.shape), _const_spec(eye2.shape),
        ],
        out_specs=pl.BlockSpec((None, seq, d), lambda bi, hi: (bi, 0, hi)),
        out_shape=jax.ShapeDtypeStruct((b, seq, H * d), BF16),
        scratch_shapes=[
            pltpu.VMEM((seq, d), BF16),
            pltpu.VMEM((d, seq), BF16),
            pltpu.VMEM((2, nc, d, 2 * d), BF16),
            pltpu.VMEM((2, d, 2 * d), F32),
            stat, stat, stat,
            stat, stat, stat,
            stat,
            pltpu.VMEM((slots, 2, L, L + d), BF16),
            pltpu.VMEM((slots, 2, L, d), F32),
        ],
        compiler_params=pltpu.CompilerParams(dimension_semantics=("arbitrary", "arbitrary"),
                                             vmem_limit_bytes=VMEM_LIMIT["mlstm"]),
        name="mlstm",
    )(qk, qk, v, o, gates, conv_w, conv_b, norm_g, tri3, eye2)


def _natten_body(q_ref, k_ref, v_ref, bias_ref, out_ref, pa_ref, pb_ref, *, rows):
    hd = NA_HEAD_DIM
    win = WIN_H * GRID_W
    lane = lax.broadcasted_iota(jnp.int32, (GRID_W, 2 * hd), 1)
    first = lane < hd
    ones_blk = jnp.ones((win, 2 * hd), BF16)

    def window_start(r):
        return jnp.clip(r - WIN_H // 2, 0, rows - WIN_H)

    def prob_stage(r, p_ref, slot):
        rs = window_start(r)
        q = q_ref[pl.ds(pl.multiple_of(r * GRID_W, GRID_W), GRID_W), :]
        zero = jnp.zeros_like(q)
        qs = jnp.concatenate([jnp.where(first, q, zero), jnp.where(first, zero, q)], axis=0)
        kwin = k_ref[pl.ds(pl.multiple_of(rs * GRID_W, GRID_W), win), :]
        off = rs - r + (WIN_H - 1)
        bias = bias_ref[off & 1, :, pl.ds(pl.multiple_of((off >> 1) * LANES, LANES), win)]
        s = lax.dot_general(qs, kwin, NT_DIMS, preferred_element_type=F32) + bias
        p_ref[slot] = jnp.exp(s - jnp.max(s, axis=1, keepdims=True)).astype(BF16)

    def output_stage(r, p_ref, slot):
        rs = window_start(r)
        vwin = v_ref[pl.ds(pl.multiple_of(rs * GRID_W, GRID_W), win), :]
        o = jnp.dot(p_ref[slot], jnp.concatenate([vwin, ones_blk], axis=1), preferred_element_type=F32)
        o = o[:, :2 * hd] / o[:, 2 * hd:]
        out = jnp.where(first, o[:GRID_W], o[GRID_W:])
        out_ref[pl.ds(pl.multiple_of(r * GRID_W, GRID_W), GRID_W), :] = out.astype(out_ref.dtype)

    R = NA_ROWS_PER_STEP
    n_groups = rows // R

    def produce(g, p_ref):
        for i in range(R):
            prob_stage(jnp.minimum(g * R + i, rows - 1), p_ref, i)

    def consume(g, p_ref):
        for i in range(R):
            output_stage(g * R + i, p_ref, i)

    produce(0, pa_ref)

    def group_pair(j, carry):
        g = 2 * j
        produce(g + 1, pb_ref)
        consume(g, pa_ref)
        produce(g + 2, pa_ref)
        consume(g + 1, pb_ref)
        return carry

    lax.fori_loop(0, n_groups // 2, group_pair, 0)


def _natten(nq, nk, nv, bias_tab):
    b, seq, d_n = nq.shape
    pairs = N_NA_HEADS // 2
    width = 2 * NA_HEAD_DIM
    rows = seq // GRID_W
    assert rows % (2 * NA_ROWS_PER_STEP) == 0
    col = pl.BlockSpec((None, seq, width), lambda bi, pi: (bi, 0, pi))
    return pl.pallas_call(
        functools.partial(_natten_body, rows=rows),
        grid=(b, pairs),
        in_specs=[col, col, col,
                  pl.BlockSpec((None, 2, 2 * GRID_W, NA_BIAS_LANES), lambda bi, pi: (pi, 0, 0, 0))],
        out_specs=col,
        out_shape=jax.ShapeDtypeStruct((b, seq, d_n), BF16),
        scratch_shapes=[pltpu.VMEM((NA_ROWS_PER_STEP, 2 * GRID_W, WIN_H * GRID_W), BF16)] * 2,
        compiler_params=pltpu.CompilerParams(dimension_semantics=("arbitrary", "arbitrary"),
                                             vmem_limit_bytes=VMEM_LIMIT["natten"]),
        name="natten",
    )(nq, nk, nv, bias_tab)


def _natten_bias_table(rpb):
    c = np.arange(GRID_W)
    cs = np.clip(c - WIN_W // 2, 0, GRID_W - WIN_W)
    cp = np.arange(GRID_W)
    valid = (cp[None, :] >= cs[:, None]) & (cp[None, :] < cs[:, None] + WIN_W)
    rel = cp[None, None, :] - c[None, :, None] + (WIN_W - 1)
    onehot = (rel == np.arange(2 * WIN_W - 1)[:, None, None]).astype(np.float32)
    n_rel = 2 * WIN_H - 1
    tab = jnp.einsum('phrd,dcq->phcrq', rpb.astype(F32).reshape(N_NA_HEADS // 2, 2, n_rel, 2 * WIN_W - 1),
                     jnp.asarray(onehot), precision=lax.Precision.HIGHEST)
    tab = jnp.where(jnp.asarray(valid)[None, None, :, None, :], tab, NEG_BIG)
    tab = tab.reshape(N_NA_HEADS // 2, 2 * GRID_W, n_rel * GRID_W)
    even = tab[:, :, :NA_BIAS_LANES]
    odd = tab[:, :, GRID_W:GRID_W + NA_BIAS_LANES]
    return jnp.stack([even, odd], axis=1)


def _tail_body(x_ref, ya_ref, yb_ref, p_ref, woa_ref, wob_ref, g2_ref, w1_ref, w2_ref,
               g3_ref, wg_ref, wu_ref, out_ref, *, ff_chunk):
    d_ff = w1_ref.shape[1]
    h = (x_ref[...]
         + jnp.dot(ya_ref[...], woa_ref[...], preferred_element_type=F32)
         + jnp.dot(yb_ref[...], wob_ref[...], preferred_element_type=F32))
    u = (h * _rms_scale(h) * g2_ref[...]).astype(BF16)
    out_ref[...] = h
    for j in range(d_ff // ff_chunk):
        z = jnp.dot(u, w1_ref[:, j * ff_chunk:(j + 1) * ff_chunk], preferred_element_type=F32)
        z = jnp.maximum(z, 0.0)
        out_ref[...] += jnp.dot((z * z).astype(BF16), w2_ref[j * ff_chunk:(j + 1) * ff_chunk, :],
                                preferred_element_type=F32)
    h = out_ref[...]
    u = (h * _rms_scale(h) * g3_ref[...]).astype(BF16)
    gate = _sigmoid(jnp.dot(u, wg_ref[...], preferred_element_type=F32))
    up = jnp.dot(p_ref[...].astype(BF16), wu_ref[...], preferred_element_type=F32)
    out_ref[...] = h + gate * up


def _tail(x2, ya, yb, p2, wo, g2, w1, w2, g3, wg, wu, *, tm, ff_chunk):
    t, d = x2.shape
    d_mix = wo.shape[0]
    assert ya.shape[1] == yb.shape[1] == d_mix // 2 and t % tm == 0
    row = lambda width: pl.BlockSpec((tm, width), lambda i: (i, 0))
    half = lambda k: pl.BlockSpec((d_mix // 2, d), lambda i: (k, 0), pipeline_mode=pl.Buffered(1))
    consts = [g2, w1, w2, g3, wg, wu]
    return pl.pallas_call(
        functools.partial(_tail_body, ff_chunk=ff_chunk),
        grid=(t // tm,),
        in_specs=[row(d), row(ya.shape[1]), row(yb.shape[1]), row(p2.shape[1]), half(0), half(1)]
                 + [_const_spec(c.shape) for c in consts],
        out_specs=row(d),
        out_shape=jax.ShapeDtypeStruct((t, d), F32),
        compiler_params=pltpu.CompilerParams(dimension_semantics=("arbitrary",),
                                             vmem_limit_bytes=VMEM_LIMIT["tail"]),
        name="tail",
    )(x2, ya, yb, p2, wo, wo, *consts)


def kernel(x, p, norm1_g, w_in, conv_w, conv_b, gate_b, mlstm_norm_g, q_norm_g, k_norm_g, rpb,
           w_out, norm2_g, w_ff1, w_ff2, ple_norm_g, w_ple_gate, w_ple_up):
    b, seq, d = x.shape
    depth = w_in.shape[0]
    H, hd, L = N_MLSTM_HEADS, MLSTM_HEAD_DIM, CHUNK
    d_m = H * hd
    d_n = N_NA_HEADS * NA_HEAD_DIM
    t = b * seq
    rows = seq // GRID_W
    assert rows >= WIN_H

    bd = jnp.asarray(np.kron(np.eye(N_NA_HEADS), np.ones((NA_HEAD_DIM, NA_HEAD_DIM))), BF16)
    ri, ci = np.indices((L, L))
    tri = np.stack([ci <= ri, ci >= ri])
    tri3 = jnp.asarray(np.concatenate([tri, tri, tri], axis=1), BF16)
    eye2 = jnp.asarray(np.concatenate([ci == ri, ci == ri], axis=1), BF16)

    h = x.reshape(t, d)
    for i in range(depth):
        wi = w_in[i].astype(BF16)
        w_n = wi[:, 4 * d_m + N_GATE:]
        qg = (q_norm_g[i].reshape(1, d_n) * (NA_HEAD_DIM ** -0.5)).astype(F32)
        kg = k_norm_g[i].reshape(1, d_n).astype(F32)
        qk, mv, mo, nq, nk, nv, gates = _inproj(
            h, norm1_g[i].reshape(1, d), wi, w_n, gate_b[i].reshape(N_GATE, 1), bd, qg, kg,
            d_m=d_m, d_n=d_n, tm=INPROJ_TM)

        cw = conv_w[i].reshape(3, 2, H, hd).transpose(2, 1, 0, 3)
        cb = conv_b[i].reshape(2, H, 1, hd).transpose(1, 0, 2, 3)
        y_a = _mlstm(qk.reshape(b, seq, 2 * d_m), mv.reshape(b, seq, d_m), mo.reshape(b, seq, d_m),
                     gates.reshape(4, H, b, seq // L, L), cw, cb,
                     mlstm_norm_g[i].reshape(H, 1, hd), tri3, eye2)

        bias_tab = _natten_bias_table(rpb[i])
        y_b = _natten(nq.reshape(b, seq, d_n), nk.reshape(b, seq, d_n), nv.reshape(b, seq, d_n), bias_tab)

        h = _tail(h, y_a.reshape(t, d_m), y_b.reshape(t, d_n), p[i].reshape(t, -1),
                  w_out[i].astype(BF16), norm2_g[i].reshape(1, d), w_ff1[i].astype(BF16), w_ff2[i].astype(BF16),
                  ple_norm_g[i].reshape(1, d), w_ple_gate[i].astype(BF16), w_ple_up[i].astype(BF16),
                  tm=TAIL_TM, ff_chunk=TAIL_FF_CHUNK)
    return h.reshape(b, seq, d)
```

```python
import functools

import jax
import jax.numpy as jnp
import numpy as np
from jax import lax
from jax.experimental import pallas as pl
from jax.experimental.pallas import tpu as pltpu

F32 = jnp.float32
BF16 = jnp.bfloat16

N_MLSTM_HEADS = 4
MLSTM_HEAD_DIM = 128
N_NA_HEADS = 8
NA_HEAD_DIM = 64
GRID_W = 64
WIN_H = 8
WIN_W = 16
CHUNK = 128
N_GATE = 4 * N_MLSTM_HEADS
RMS_EPS = 1e-6
NEG_BIG = -1e30
F32_BIG = 3e38
NA_BIAS_LANES = (2 * WIN_H - 2) * GRID_W
NA_ROWS_PER_STEP = 16
MLSTM_CHUNKS_PER_STEP = 16
MLSTM_CONV_UNROLL = 16
MLSTM_SCAN_UNROLL = 16

LANES = 128
GATE_PAD = LANES
MIB = 1024 * 1024

INPROJ_TM = 1024
TAIL_TM = 1024
TAIL_FF_CHUNK = 1024
VMEM_LIMIT = {"inproj": 44 * MIB, "mlstm": 52 * MIB, "natten": 40 * MIB, "tail": 52 * MIB}

NT_DIMS = (((1,), (1,)), ((), ()))


def _const_spec(shape):
    return pl.BlockSpec(shape, lambda *_: (0,) * len(shape), pipeline_mode=pl.Buffered(1))


def _rms_scale(x):
    return lax.rsqrt(jnp.mean(x * x, axis=-1, keepdims=True) + RMS_EPS)


def _split3(x):
    hi = x.astype(BF16)
    r1 = x - hi.astype(F32)
    mid = r1.astype(BF16)
    lo = (r1 - mid.astype(F32)).astype(BF16)
    return hi, mid, lo


def _inproj_body(x_ref, g_ref, wm_ref, wn_ref, wg_ref, gb_ref, bd_ref, qg_ref, kg_ref,
                 qk_ref, v_ref, o_ref, nq_ref, nk_ref, nv_ref, gate_ref, *, d_m, d_n):
    x = x_ref[...]
    u = (x * _rms_scale(x) * g_ref[...]).astype(BF16)

    def proj(w_ref, lo, hi):
        return jnp.dot(u, w_ref[:, lo:hi], preferred_element_type=F32)

    def head_norm(y, gain_ref):
        ss = jnp.dot((y * y).astype(BF16), bd_ref[...], preferred_element_type=F32)
        return y * lax.rsqrt(ss * (1.0 / NA_HEAD_DIM) + RMS_EPS) * gain_ref[...]

    qk_ref[...] = proj(wm_ref, 0, 2 * d_m).astype(BF16)
    v_ref[...] = proj(wm_ref, 2 * d_m, 3 * d_m).astype(BF16)
    o_ref[...] = proj(wm_ref, 3 * d_m, 4 * d_m).astype(BF16)
    nq_ref[...] = head_norm(proj(wn_ref, 0, d_n), qg_ref).astype(BF16)
    nk_ref[...] = head_norm(proj(wn_ref, d_n, 2 * d_n), kg_ref).astype(BF16)
    nv_ref[...] = proj(wn_ref, 2 * d_n, 3 * d_n).astype(BF16)
    wg_t = wg_ref[...].astype(F32).T[:N_GATE, :].astype(BF16)
    gate_ref[...] = lax.dot_general(wg_t, u, NT_DIMS, preferred_element_type=F32) + gb_ref[...]


def _inproj(x2, g, w_all, w_n, gate_b, bd, qg, kg, *, d_m, d_n, tm):
    t, d = x2.shape
    assert (4 * d_m) % GATE_PAD == 0
    fixed = functools.partial(pl.BlockSpec, pipeline_mode=pl.Buffered(1))
    row = lambda width: pl.BlockSpec((tm, width), lambda i: (i, 0))
    out_shapes = (
        jax.ShapeDtypeStruct((t, 2 * d_m), BF16),
        jax.ShapeDtypeStruct((t, d_m), BF16),
        jax.ShapeDtypeStruct((t, d_m), BF16),
        jax.ShapeDtypeStruct((t, d_n), BF16),
        jax.ShapeDtypeStruct((t, d_n), BF16),
        jax.ShapeDtypeStruct((t, d_n), BF16),
        jax.ShapeDtypeStruct((N_GATE, t), F32),
    )
    return pl.pallas_call(
        functools.partial(_inproj_body, d_m=d_m, d_n=d_n),
        grid=(t // tm,),
        in_specs=[row(d), _const_spec((1, d)), fixed((d, 4 * d_m), lambda i: (0, 0)), _const_spec(w_n.shape),
                  fixed((d, GATE_PAD), lambda i: (0, 4 * d_m // GATE_PAD)), _const_spec((N_GATE, 1)),
                  _const_spec((d_n, d_n)), _const_spec((1, d_n)), _const_spec((1, d_n))],
        out_specs=(row(2 * d_m), row(d_m), row(d_m), row(d_n), row(d_n), row(d_n),
                   pl.BlockSpec((N_GATE, tm), lambda i: (0, i))),
        out_shape=out_shapes,
        compiler_params=pltpu.CompilerParams(dimension_semantics=("arbitrary",),
                                             vmem_limit_bytes=VMEM_LIMIT["inproj"]),
        name="inproj",
    )(x2, g, w_all, w_n, w_all, gate_b, bd, qg, kg)


def _log_sigmoid(x):
    return jnp.minimum(x, 0.0) - jnp.log1p(jnp.exp(-jnp.abs(x)))


def _sigmoid(x):
    return 1.0 / (1.0 + jnp.exp(-x))


def _mlstm_body(q_ref, k_ref, v_ref, o_ref, gate_ref, cw_ref, cb_ref, ng_ref, tri3_ref, eye2_ref,
                out_ref, qs_ref, kt_ref, cs_ref, ccur_ref, brow_ref, crow_ref, cmax_ref, arow_ref, bl_ref,
                ml_ref, ms_ref, s_ref, einv_ref, *, seq):
    L = CHUNK
    d = MLSTM_HEAD_DIM
    nc = seq // L

    row_id = lax.broadcasted_iota(jnp.int32, (L, d), 0)
    col_id = lax.broadcasted_iota(jnp.int32, (L, d), 1)
    pos_id = lax.broadcasted_iota(jnp.int32, (nc, L), 1)
    ones_blk = jnp.ones((L, d), BF16)

    def conv_silu(src_ref, c, s0, w, b):
        x = src_ref[pl.ds(s0, L), :].astype(F32)
        p0 = pl.multiple_of(jnp.maximum(s0 - 16, 0), 16)
        n0 = pl.multiple_of(jnp.minimum(s0 + L, seq - 16), 16)
        prev_row = src_ref[pl.ds(p0, 16), :][15:16, :].astype(F32)
        next_row = src_ref[pl.ds(n0, 16), :][0:1, :].astype(F32)
        prev_row = jnp.where(c > 0, prev_row, 0.0)
        next_row = jnp.where(c < nc - 1, next_row, 0.0)
        x_prev = jnp.where(row_id == 0, prev_row, pltpu.roll(x, 1, 0))
        x_next = jnp.where(row_id == L - 1, next_row, pltpu.roll(x, L - 1, 0))
        y = w[0:1, :] * x_prev + w[1:2, :] * x + w[2:3, :] * x_next + b
        return y * _sigmoid(y)

    def conv_step(c, carry):
        s0 = pl.multiple_of(c * L, L)
        qs_ref[pl.ds(s0, L), :] = conv_silu(q_ref, c, s0, cw_ref[0], cb_ref[0]).astype(BF16)
        kk = conv_silu(k_ref, c, s0, cw_ref[1], cb_ref[1]) * (d ** -0.5)
        kt_ref[:, pl.ds(s0, L)] = kk.T.astype(BF16)
        return carry

    lax.fori_loop(0, nc, conv_step, 0, unroll=MLSTM_CONV_UNROLL)

    for dirn in (0, 1):
        i_g = gate_ref[2 * dirn]
        f_log = _log_sigmoid(gate_ref[2 * dirn + 1])
        f_cat = jnp.concatenate(_split3(f_log), axis=1)
        brow = jnp.dot(f_cat, tri3_ref[1 - dirn], preferred_element_type=F32)
        b_last = brow[:, L - 1:L] if dirn == 0 else brow[:, 0:1]
        a_row = i_g + b_last - brow
        a_max = jnp.max(a_row, axis=1, keepdims=True)
        crow = i_g - brow
        cmax = crow
        for sh in [1 << e for e in range(L.bit_length() - 1)]:
            if dirn == 0:
                cmax = jnp.maximum(cmax, jnp.where(pos_id >= sh, pltpu.roll(cmax, sh, 1), -jnp.inf))
            else:
                cmax = jnp.maximum(cmax, jnp.where(pos_id < L - sh, pltpu.roll(cmax, L - sh, 1), -jnp.inf))
        brow_ref[dirn] = brow
        crow_ref[dirn] = crow
        cmax_ref[dirn] = cmax
        arow_ref[dirn] = a_row
        bl_ref[dirn] = jnp.broadcast_to(b_last, (nc, L))
        ml_ref[dirn] = jnp.broadcast_to(a_max, (nc, L))

    def v_aug(s0):
        return jnp.concatenate([v_ref[pl.ds(s0, L), :], ones_blk], axis=1)

    ccur_ref[...] = jnp.zeros_like(ccur_ref)

    def scan_step(i, carry):
        new = []
        for dirn, c, m in ((0, i, carry[0]), (1, nc - 1 - i, carry[1])):
            s0 = pl.multiple_of(c * L, L)
            state = ccur_ref[dirn]
            cs_ref[dirn, c] = state.astype(BF16)
            ms_ref[dirn, pl.ds(c, 1), :] = m
            a_prev = m + bl_ref[dirn, pl.ds(c, 1), :]
            m_new = jnp.maximum(a_prev, ml_ref[dirn, pl.ds(c, 1), :])
            w_row = jnp.exp(arow_ref[dirn, pl.ds(c, 1), :] - m_new)
            kw = (kt_ref[:, pl.ds(s0, L)].astype(F32) * w_row).astype(BF16)
            k_loc = jnp.dot(kw, v_aug(s0), preferred_element_type=F32)
            ccur_ref[dirn] = jnp.exp(a_prev - m_new)[:, 0:1] * state + k_loc
            new.append(m_new)
        return tuple(new)

    m0 = jnp.zeros((1, L), F32)
    lax.fori_loop(0, nc, scan_step, (m0, m0), unroll=MLSTM_SCAN_UNROLL)

    for dirn in (0, 1):
        mu_all = jnp.maximum(ms_ref[dirn], cmax_ref[dirn])
        cmax_ref[dirn] = mu_all
        brow_ref[dirn] = jnp.minimum(jnp.exp(-(brow_ref[dirn] + mu_all)), F32_BIG)
    lower = col_id <= row_id
    upper = col_id >= row_id

    def weights_stage(c, slot):
        s0 = pl.multiple_of(c * L, L)
        q = qs_ref[pl.ds(s0, L), :]
        qk = jnp.dot(q, kt_ref[:, pl.ds(s0, L)], preferred_element_type=F32)
        for dirn, mask in ((0, lower), (1, upper)):
            rows = []
            for stat_ref in (cmax_ref, brow_ref):
                hi, mid, _ = _split3(stat_ref[dirn, pl.ds(c, 1), :])
                rows.append(jnp.broadcast_to(jnp.concatenate([hi, mid], axis=1), (L, 2 * L)))
            col = lax.dot_general(eye2_ref[...], jnp.concatenate(rows, axis=0), NT_DIMS,
                                  preferred_element_type=F32)
            mu = col[:, :L]
            p = jnp.exp(jnp.where(mask, crow_ref[dirn, pl.ds(c, 1), :] - mu, -jnp.inf))
            q_inter = q.astype(F32) * jnp.exp(ms_ref[dirn, pl.ds(c, 1), :] - mu)
            s_ref[slot, dirn] = jnp.concatenate([qk * p, q_inter], axis=1).astype(BF16)
            einv_ref[slot, dirn] = col[:, L:]

    def output_stage(c, slot):
        s0 = pl.multiple_of(c * L, L)
        vaug = v_aug(s0)
        hsum = None
        for dirn in (0, 1):
            nd = jnp.dot(s_ref[slot, dirn], jnp.concatenate([vaug, cs_ref[dirn, c]], axis=0),
                         preferred_element_type=F32)
            h = nd[:, :d] / jnp.maximum(jnp.abs(nd[:, d:]), einv_ref[slot, dirn])
            hsum = h if hsum is None else hsum + h
        y = hsum * _rms_scale(hsum) * ng_ref[...]
        y = y * _sigmoid(o_ref[pl.ds(s0, L), :].astype(F32))
        out_ref[pl.ds(s0, L), :] = y.astype(out_ref.dtype)

    G = MLSTM_CHUNKS_PER_STEP
    for i in range(G):
        weights_stage(i, i)

    def out_step(j, carry):
        cur = (j % 2) * G
        nxt = G - cur
        for i in range(G):
            output_stage(j * G + i, cur + i)
        for i in range(G):
            weights_stage(jnp.minimum((j + 1) * G + i, nc - 1), nxt + i)
        return carry

    lax.fori_loop(0, nc // G, out_step, 0)


def _mlstm(qk, v, o, gates, conv_w, conv_b, norm_g, tri3, eye2):
    b, seq, _ = v.shape
    H, d, L = N_MLSTM_HEADS, MLSTM_HEAD_DIM, CHUNK
    nc = seq // L
    assert nc % MLSTM_SCAN_UNROLL == 0 and nc % MLSTM_CONV_UNROLL == 0 and L == LANES
    assert nc % MLSTM_CHUNKS_PER_STEP == 0
    col = lambda off: pl.BlockSpec((None, seq, d), lambda bi, hi: (bi, 0, hi + off))
    stat = pltpu.VMEM((2, nc, L), F32)
    slots = 2 * MLSTM_CHUNKS_PER_STEP
    return pl.pallas_call(
        functools.partial(_mlstm_body, seq=seq),
        grid=(b, H),
        in_specs=[
            col(0), col(H), col(0), col(0),
            pl.BlockSpec((4, None, None, nc, L), lambda bi, hi: (0, hi, bi, 0, 0)),
            pl.BlockSpec((None, 2, 3, d), lambda bi, hi: (hi, 0, 0, 0)),
            pl.BlockSpec((None, 2, 1, d), lambda bi, hi: (hi, 0, 0, 0)),
            pl.BlockSpec((None, 1, d), lambda bi, hi: (hi, 0, 0)),
            _const_spec(tri3.shape), _const_spec(eye2.shape),
        ],
        out_specs=pl.BlockSpec((None, seq, d), lambda bi, hi: (bi, 0, hi)),
        out_shape=jax.ShapeDtypeStruct((b, seq, H * d), BF16),
        scratch_shapes=[
            pltpu.VMEM((seq, d), BF16),
            pltpu.VMEM((d, seq), BF16),
            pltpu.VMEM((2, nc, d, 2 * d), BF16),
            pltpu.VMEM((2, d, 2 * d), F32),
            stat, stat, stat,
            stat, stat, stat,
            stat,
            pltpu.VMEM((slots, 2, L, L + d), BF16),
            pltpu.VMEM((slots, 2, L, d), F32),
        ],
        compiler_params=pltpu.CompilerParams(dimension_semantics=("arbitrary", "arbitrary"),
                                             vmem_limit_bytes=VMEM_LIMIT["mlstm"]),
        name="mlstm",
    )(qk, qk, v, o, gates, conv_w, conv_b, norm_g, tri3, eye2)


def _natten_body(q_ref, k_ref, v_ref, bias_ref, out_ref, pa_ref, pb_ref, *, rows):
    hd = NA_HEAD_DIM
    win = WIN_H * GRID_W
    lane = lax.broadcasted_iota(jnp.int32, (GRID_W, 2 * hd), 1)
    first = lane < hd
    ones_blk = jnp.ones((win, 2 * hd), BF16)

    def window_start(r):
        return jnp.clip(r - WIN_H // 2, 0, rows - WIN_H)

    def prob_stage(r, p_ref, slot):
        rs = window_start(r)
        q = q_ref[pl.ds(pl.multiple_of(r * GRID_W, GRID_W), GRID_W), :]
        zero = jnp.zeros_like(q)
        qs = jnp.concatenate([jnp.where(first, q, zero), jnp.where(first, zero, q)], axis=0)
        kwin = k_ref[pl.ds(pl.multiple_of(rs * GRID_W, GRID_W), win), :]
        off = rs - r + (WIN_H - 1)
        bias = bias_ref[off & 1, :, pl.ds(pl.multiple_of((off >> 1) * LANES, LANES), win)]
        s = lax.dot_general(qs, kwin, NT_DIMS, preferred_element_type=F32) + bias
        p_ref[slot] = jnp.exp(s - jnp.max(s, axis=1, keepdims=True)).astype(BF16)

    def output_stage(r, p_ref, slot):
        rs = window_start(r)
        vwin = v_ref[pl.ds(pl.multiple_of(rs * GRID_W, GRID_W), win), :]
        o = jnp.dot(p_ref[slot], jnp.concatenate([vwin, ones_blk], axis=1), preferred_element_type=F32)
        o = o[:, :2 * hd] / o[:, 2 * hd:]
        out = jnp.where(first, o[:GRID_W], o[GRID_W:])
        out_ref[pl.ds(pl.multiple_of(r * GRID_W, GRID_W), GRID_W), :] = out.astype(out_ref.dtype)

    R = NA_ROWS_PER_STEP
    n_groups = rows // R

    def produce(g, p_ref):
        for i in range(R):
            prob_stage(jnp.minimum(g * R + i, rows - 1), p_ref, i)

    def consume(g, p_ref):
        for i in range(R):
            output_stage(g * R + i, p_ref, i)

    produce(0, pa_ref)

    def group_pair(j, carry):
        g = 2 * j
        produce(g + 1, pb_ref)
        consume(g, pa_ref)
        produce(g + 2, pa_ref)
        consume(g + 1, pb_ref)
        return carry

    lax.fori_loop(0, n_groups // 2, group_pair, 0)


def _natten(nq, nk, nv, bias_tab):
    b, seq, d_n = nq.shape
    pairs = N_NA_HEADS // 2
    width = 2 * NA_HEAD_DIM
    rows = seq // GRID_W
    assert rows % (2 * NA_ROWS_PER_STEP) == 0
    col = pl.BlockSpec((None, seq, width), lambda bi, pi: (bi, 0, pi))
    return pl.pallas_call(
        functools.partial(_natten_body, rows=rows),
        grid=(b, pairs),
        in_specs=[col, col, col,
                  pl.BlockSpec((None, 2, 2 * GRID_W, NA_BIAS_LANES), lambda bi, pi: (pi, 0, 0, 0))],
        out_specs=col,
        out_shape=jax.ShapeDtypeStruct((b, seq, d_n), BF16),
        scratch_shapes=[pltpu.VMEM((NA_ROWS_PER_STEP, 2 * GRID_W, WIN_H * GRID_W), BF16)] * 2,
        compiler_params=pltpu.CompilerParams(dimension_semantics=("arbitrary", "arbitrary"),
                                             vmem_limit_bytes=VMEM_LIMIT["natten"]),
        name="natten",
    )(nq, nk, nv, bias_tab)


def _natten_bias_table(rpb):
    c = np.arange(GRID_W)
    cs = np.clip(c - WIN_W // 2, 0, GRID_W - WIN_W)
    cp = np.arange(GRID_W)
    valid = (cp[None, :] >= cs[:, None]) & (cp[None, :] < cs[:, None] + WIN_W)
    rel = cp[None, None, :] - c[None, :, None] + (WIN_W - 1)
    onehot = (rel == np.arange(2 * WIN_W - 1)[:, None, None]).astype(np.float32)
    n_rel = 2 * WIN_H - 1
    tab = jnp.einsum('phrd,dcq->phcrq', rpb.astype(F32).reshape(N_NA_HEADS // 2, 2, n_rel, 2 * WIN_W - 1),
                     jnp.asarray(onehot), precision=lax.Precision.HIGHEST)
    tab = jnp.where(jnp.asarray(valid)[None, None, :, None, :], tab, NEG_BIG)
    tab = tab.reshape(N_NA_HEADS // 2, 2 * GRID_W, n_rel * GRID_W)
    even = tab[:, :, :NA_BIAS_LANES]
    odd = tab[:, :, GRID_W:GRID_W + NA_BIAS_LANES]
    return jnp.stack([even, odd], axis=1)


def _tail_body(x_ref, ya_ref, yb_ref, p_ref, woa_ref, wob_ref, g2_ref, w1_ref, w2_ref,
               g3_ref, wg_ref, wu_ref, out_ref, *, ff_chunk):
    d_ff = w1_ref.shape[1]
    h = (x_ref[...]
         + jnp.dot(ya_ref[...], woa_ref[...], preferred_element_type=F32)
         + jnp.dot(yb_ref[...], wob_ref[...], preferred_element_type=F32))
    u = (h * _rms_scale(h) * g2_ref[...]).astype(BF16)
    out_ref[...] = h
    for j in range(d_ff // ff_chunk):
        z = jnp.dot(u, w1_ref[:, j * ff_chunk:(j + 1) * ff_chunk], preferred_element_type=F32)
        z = jnp.maximum(z, 0.0)
        out_ref[...] += jnp.dot((z * z).astype(BF16), w2_ref[j * ff_chunk:(j + 1) * ff_chunk, :],
                                preferred_element_type=F32)
    h = out_ref[...]
    u = (h * _rms_scale(h) * g3_ref[...]).astype(BF16)
    gate = _sigmoid(jnp.dot(u, wg_ref[...], preferred_element_type=F32))
    up = jnp.dot(p_ref[...].astype(BF16), wu_ref[...], preferred_element_type=F32)
    out_ref[...] = h + gate * up


def _tail(x2, ya, yb, p2, wo, g2, w1, w2, g3, wg, wu, *, tm, ff_chunk):
    t, d = x2.shape
    d_mix = wo.shape[0]
    assert ya.shape[1] == yb.shape[1] == d_mix // 2 and t % tm == 0
    row = lambda width: pl.BlockSpec((tm, width), lambda i: (i, 0))
    half = lambda k: pl.BlockSpec((d_mix // 2, d), lambda i: (k, 0), pipeline_mode=pl.Buffered(1))
    consts = [g2, w1, w2, g3, wg, wu]
    return pl.pallas_call(
        functools.partial(_tail_body, ff_chunk=ff_chunk),
        grid=(t // tm,),
        in_specs=[row(d), row(ya.shape[1]), row(yb.shape[1]), row(p2.shape[1]), half(0), half(1)]
                 + [_const_spec(c.shape) for c in consts],
        out_specs=row(d),
        out_shape=jax.ShapeDtypeStruct((t, d), F32),
        compiler_params=pltpu.CompilerParams(dimension_semantics=("arbitrary",),
                                             vmem_limit_bytes=VMEM_LIMIT["tail"]),
        name="tail",
    )(x2, ya, yb, p2, wo, wo, *consts)


def kernel(x, p, norm1_g, w_in, conv_w, conv_b, gate_b, mlstm_norm_g, q_norm_g, k_norm_g, rpb,
           w_out, norm2_g, w_ff1, w_ff2, ple_norm_g, w_ple_gate, w_ple_up):
    b, seq, d = x.shape
    depth = w_in.shape[0]
    H, hd, L = N_MLSTM_HEADS, MLSTM_HEAD_DIM, CHUNK
    d_m = H * hd
    d_n = N_NA_HEADS * NA_HEAD_DIM
    t = b * seq
    rows = seq // GRID_W
    assert rows >= WIN_H

    bd = jnp.asarray(np.kron(np.eye(N_NA_HEADS), np.ones((NA_HEAD_DIM, NA_HEAD_DIM))), BF16)
    ri, ci = np.indices((L, L))
    tri = np.stack([ci <= ri, ci >= ri])
    tri3 = jnp.asarray(np.concatenate([tri, tri, tri], axis=1), BF16)
    eye2 = jnp.asarray(np.concatenate([ci == ri, ci == ri], axis=1), BF16)

    h = x.reshape(t, d)
    for i in range(depth):
        wi = w_in[i].astype(BF16)
        w_n = wi[:, 4 * d_m + N_GATE:]
        qg = (q_norm_g[i].reshape(1, d_n) * (NA_HEAD_DIM ** -0.5)).astype(F32)
        kg = k_norm_g[i].reshape(1, d_n).astype(F32)
        qk, mv, mo, nq, nk, nv, gates = _inproj(
            h, norm1_g[i].reshape(1, d), wi, w_n, gate_b[i].reshape(N_GATE, 1), bd, qg, kg,
            d_m=d_m, d_n=d_n, tm=INPROJ_TM)

        cw = conv_w[i].reshape(3, 2, H, hd).transpose(2, 1, 0, 3)
        cb = conv_b[i].reshape(2, H, 1, hd).transpose(1, 0, 2, 3)
        y_a = _mlstm(qk.reshape(b, seq, 2 * d_m), mv.reshape(b, seq, d_m), mo.reshape(b, seq, d_m),
                     gates.reshape(4, H, b, seq // L, L), cw, cb,
                     mlstm_norm_g[i].reshape(H, 1, hd), tri3, eye2)

        bias_tab = _natten_bias_table(rpb[i])
        y_b = _natten(nq.reshape(b, seq, d_n), nk.reshape(b, seq, d_n), nv.reshape(b, seq, d_n), bias_tab)

        h = _tail(h, y_a.reshape(t, d_m), y_b.reshape(t, d_n), p[i].reshape(t, -1),
                  w_out[i].astype(BF16), norm2_g[i].reshape(1, d), w_ff1[i].astype(BF16), w_ff2[i].astype(BF16),
                  ple_norm_g[i].reshape(1, d), w_ple_gate[i].astype(BF16), w_ple_up[i].astype(BF16),
                  tm=TAIL_TM, ff_chunk=TAIL_FF_CHUNK)
    return h.reshape(b, seq, d)
```

```python
import functools

import jax
import jax.numpy as jnp
import numpy as np
from jax import lax
from jax.experimental import pallas as pl
from jax.experimental.pallas import tpu as pltpu

F32 = jnp.float32
BF16 = jnp.bfloat16

N_MLSTM_HEADS = 4
MLSTM_HEAD_DIM = 128
N_NA_HEADS = 8
NA_HEAD_DIM = 64
GRID_W = 64
WIN_H = 8
WIN_W = 16
CHUNK = 128
N_GATE = 4 * N_MLSTM_HEADS
RMS_EPS = 1e-6
NEG_BIG = -1e30
F32_BIG = 3e38
NA_BIAS_LANES = (2 * WIN_H - 2) * GRID_W
NA_ROWS_PER_STEP = 16
MLSTM_CHUNKS_PER_STEP = 8
MLSTM_CONV_UNROLL = 16
MLSTM_SCAN_UNROLL = 16

LANES = 128
GATE_PAD = LANES
MIB = 1024 * 1024

INPROJ_TM = 1024
TAIL_TM = 1024
TAIL_FF_CHUNK = 1024
VMEM_LIMIT = {"inproj": 44 * MIB, "mlstm": 52 * MIB, "natten": 40 * MIB, "tail": 52 * MIB}

NT_DIMS = (((1,), (1,)), ((), ()))


def _const_spec(shape):
    return pl.BlockSpec(shape, lambda *_: (0,) * len(shape), pipeline_mode=pl.Buffered(1))


def _rms_scale(x):
    return lax.rsqrt(jnp.mean(x * x, axis=-1, keepdims=True) + RMS_EPS)


def _split3(x):
    hi = x.astype(BF16)
    r1 = x - hi.astype(F32)
    mid = r1.astype(BF16)
    lo = (r1 - mid.astype(F32)).astype(BF16)
    return hi, mid, lo


def _inproj_body(x_ref, g_ref, wm_ref, wn_ref, wg_ref, gb_ref, bd_ref, qg_ref, kg_ref,
                 qk_ref, v_ref, o_ref, nq_ref, nk_ref, nv_ref, gate_ref, *, d_m, d_n):
    x = x_ref[...]
    u = (x * _rms_scale(x) * g_ref[...]).astype(BF16)

    def proj(w_ref, lo, hi):
        return jnp.dot(u, w_ref[:, lo:hi], preferred_element_type=F32)

    def head_norm(y, gain_ref):
        ss = jnp.dot((y * y).astype(BF16), bd_ref[...], preferred_element_type=F32)
        return y * lax.rsqrt(ss * (1.0 / NA_HEAD_DIM) + RMS_EPS) * gain_ref[...]

    qk_ref[...] = proj(wm_ref, 0, 2 * d_m).astype(BF16)
    v_ref[...] = proj(wm_ref, 2 * d_m, 3 * d_m).astype(BF16)
    o_ref[...] = proj(wm_ref, 3 * d_m, 4 * d_m).astype(BF16)
    nq_ref[...] = head_norm(proj(wn_ref, 0, d_n), qg_ref).astype(BF16)
    nk_ref[...] = head_norm(proj(wn_ref, d_n, 2 * d_n), kg_ref).astype(BF16)
    nv_ref[...] = proj(wn_ref, 2 * d_n, 3 * d_n).astype(BF16)
    wg_t = wg_ref[...].astype(F32).T[:N_GATE, :].astype(BF16)
    gate_ref[...] = lax.dot_general(wg_t, u, NT_DIMS, preferred_element_type=F32) + gb_ref[...]


def _inproj(x2, g, w_all, w_n, gate_b, bd, qg, kg, *, d_m, d_n, tm):
    t, d = x2.shape
    assert (4 * d_m) % GATE_PAD == 0
    fixed = functools.partial(pl.BlockSpec, pipeline_mode=pl.Buffered(1))
    row = lambda width: pl.BlockSpec((tm, width), lambda i: (i, 0))
    out_shapes = (
        jax.ShapeDtypeStruct((t, 2 * d_m), BF16),
        jax.ShapeDtypeStruct((t, d_m), BF16),
        jax.ShapeDtypeStruct((t, d_m), BF16),
        jax.ShapeDtypeStruct((t, d_n), BF16),
        jax.ShapeDtypeStruct((t, d_n), BF16),
        jax.ShapeDtypeStruct((t, d_n), BF16),
        jax.ShapeDtypeStruct((N_GATE, t), F32),
    )
    return pl.pallas_call(
        functools.partial(_inproj_body, d_m=d_m, d_n=d_n),
        grid=(t // tm,),
        in_specs=[row(d), _const_spec((1, d)), fixed((d, 4 * d_m), lambda i: (0, 0)), _const_spec(w_n.shape),
                  fixed((d, GATE_PAD), lambda i: (0, 4 * d_m // GATE_PAD)), _const_spec((N_GATE, 1)),
                  _const_spec((d_n, d_n)), _const_spec((1, d_n)), _const_spec((1, d_n))],
        out_specs=(row(2 * d_m), row(d_m), row(d_m), row(d_n), row(d_n), row(d_n),
                   pl.BlockSpec((N_GATE, tm), lambda i: (0, i))),
        out_shape=out_shapes,
        compiler_params=pltpu.CompilerParams(dimension_semantics=("arbitrary",),
                                             vmem_limit_bytes=VMEM_LIMIT["inproj"]),
        name="inproj",
    )(x2, g, w_all, w_n, w_all, gate_b, bd, qg, kg)


def _log_sigmoid(x):
    return jnp.minimum(x, 0.0) - jnp.log1p(jnp.exp(-jnp.abs(x)))


def _sigmoid(x):
    return 1.0 / (1.0 + jnp.exp(-x))


def _mlstm_body(q_ref, k_ref, v_ref, o_ref, gate_ref, cw_ref, cb_ref, ng_ref, tri3_ref, eye2_ref,
                out_ref, qs_ref, kt_ref, cs_ref, ccur_ref, brow_ref, crow_ref, cmax_ref, arow_ref, bl_ref,
                ml_ref, ms_ref, s_ref, einv_ref, *, seq):
    L = CHUNK
    d = MLSTM_HEAD_DIM
    nc = seq // L

    row_id = lax.broadcasted_iota(jnp.int32, (L, d), 0)
    col_id = lax.broadcasted_iota(jnp.int32, (L, d), 1)
    pos_id = lax.broadcasted_iota(jnp.int32, (nc, L), 1)
    ones_blk = jnp.ones((L, d), BF16)

    def conv_silu(src_ref, c, s0, w, b):
        x = src_ref[pl.ds(s0, L), :].astype(F32)
        p0 = pl.multiple_of(jnp.maximum(s0 - 16, 0), 16)
        n0 = pl.multiple_of(jnp.minimum(s0 + L, seq - 16), 16)
        prev_row = src_ref[pl.ds(p0, 16), :][15:16, :].astype(F32)
        next_row = src_ref[pl.ds(n0, 16), :][0:1, :].astype(F32)
        prev_row = jnp.where(c > 0, prev_row, 0.0)
        next_row = jnp.where(c < nc - 1, next_row, 0.0)
        x_prev = jnp.where(row_id == 0, prev_row, pltpu.roll(x, 1, 0))
        x_next = jnp.where(row_id == L - 1, next_row, pltpu.roll(x, L - 1, 0))
        y = w[0:1, :] * x_prev + w[1:2, :] * x + w[2:3, :] * x_next + b
        return y * _sigmoid(y)

    def conv_step(c, carry):
        s0 = pl.multiple_of(c * L, L)
        qs_ref[pl.ds(s0, L), :] = conv_silu(q_ref, c, s0, cw_ref[0], cb_ref[0]).astype(BF16)
        kk = conv_silu(k_ref, c, s0, cw_ref[1], cb_ref[1]) * (d ** -0.5)
        kt_ref[:, pl.ds(s0, L)] = kk.T.astype(BF16)
        return carry

    lax.fori_loop(0, nc, conv_step, 0, unroll=MLSTM_CONV_UNROLL)

    for dirn in (0, 1):
        i_g = gate_ref[2 * dirn]
        f_log = _log_sigmoid(gate_ref[2 * dirn + 1])
        f_cat = jnp.concatenate(_split3(f_log), axis=1)
        brow = jnp.dot(f_cat, tri3_ref[1 - dirn], preferred_element_type=F32)
        b_last = brow[:, L - 1:L] if dirn == 0 else brow[:, 0:1]
        a_row = i_g + b_last - brow
        a_max = jnp.max(a_row, axis=1, keepdims=True)
        crow = i_g - brow
        cmax = crow
        for sh in [1 << e for e in range(L.bit_length() - 1)]:
            if dirn == 0:
                cmax = jnp.maximum(cmax, jnp.where(pos_id >= sh, pltpu.roll(cmax, sh, 1), -jnp.inf))
            else:
                cmax = jnp.maximum(cmax, jnp.where(pos_id < L - sh, pltpu.roll(cmax, L - sh, 1), -jnp.inf))
        brow_ref[dirn] = brow
        crow_ref[dirn] = crow
        cmax_ref[dirn] = cmax
        arow_ref[dirn] = a_row
        bl_ref[dirn] = jnp.broadcast_to(b_last, (nc, L))
        ml_ref[dirn] = jnp.broadcast_to(a_max, (nc, L))

    def v_aug(s0):
        return jnp.concatenate([v_ref[pl.ds(s0, L), :], ones_blk], axis=1)

    ccur_ref[...] = jnp.zeros_like(ccur_ref)

    def scan_step(i, carry):
        new = []
        for dirn, c, m in ((0, i, carry[0]), (1, nc - 1 - i, carry[1])):
            s0 = pl.multiple_of(c * L, L)
            state = ccur_ref[dirn]
            cs_ref[dirn, c] = state.astype(BF16)
            ms_ref[dirn, pl.ds(c, 1), :] = m
            a_prev = m + bl_ref[dirn, pl.ds(c, 1), :]
            m_new = jnp.maximum(a_prev, ml_ref[dirn, pl.ds(c, 1), :])
            w_row = jnp.exp(arow_ref[dirn, pl.ds(c, 1), :] - m_new)
            kw = (kt_ref[:, pl.ds(s0, L)].astype(F32) * w_row).astype(BF16)
            k_loc = jnp.dot(kw, v_aug(s0), preferred_element_type=F32)
            ccur_ref[dirn] = jnp.exp(a_prev - m_new)[:, 0:1] * state + k_loc
            new.append(m_new)
        return tuple(new)

    m0 = jnp.zeros((1, L), F32)
    lax.fori_loop(0, nc, scan_step, (m0, m0), unroll=MLSTM_SCAN_UNROLL)

    for dirn in (0, 1):
        mu_all = jnp.maximum(ms_ref[dirn], cmax_ref[dirn])
        cmax_ref[dirn] = mu_all
        brow_ref[dirn] = jnp.minimum(jnp.exp(-(brow_ref[dirn] + mu_all)), F32_BIG)
    lower = col_id <= row_id
    upper = col_id >= row_id

    def weights_stage(c, slot):
        s0 = pl.multiple_of(c * L, L)
        q = qs_ref[pl.ds(s0, L), :]
        qk = jnp.dot(q, kt_ref[:, pl.ds(s0, L)], preferred_element_type=F32)
        for dirn, mask in ((0, lower), (1, upper)):
            rows = []
            for stat_ref in (cmax_ref, brow_ref):
                hi, mid, _ = _split3(stat_ref[dirn, pl.ds(c, 1), :])
                rows.append(jnp.broadcast_to(jnp.concatenate([hi, mid], axis=1), (L, 2 * L)))
            col = lax.dot_general(eye2_ref[...], jnp.concatenate(rows, axis=0), NT_DIMS,
                                  preferred_element_type=F32)
            mu = col[:, :L]
            p = jnp.exp(jnp.where(mask, crow_ref[dirn, pl.ds(c, 1), :] - mu, -jnp.inf))
            q_inter = q.astype(F32) * jnp.exp(ms_ref[dirn, pl.ds(c, 1), :] - mu)
            s_ref[slot, dirn] = jnp.concatenate([qk * p, q_inter], axis=1).astype(BF16)
            einv_ref[slot, dirn] = col[:, L:]

    def output_stage(c, slot):
        s0 = pl.multiple_of(c * L, L)
        vaug = v_aug(s0)
        hsum = None
        for dirn in (0, 1):
            nd = jnp.dot(s_ref[slot, dirn], jnp.concatenate([vaug, cs_ref[dirn, c]], axis=0),
                         preferred_element_type=F32)
            h = nd[:, :d] / jnp.maximum(jnp.abs(nd[:, d:]), einv_ref[slot, dirn])
            hsum = h if hsum is None else hsum + h
        y = hsum * _rms_scale(hsum) * ng_ref[...]
        y = y * _sigmoid(o_ref[pl.ds(s0, L), :].astype(F32))
        out_ref[pl.ds(s0, L), :] = y.astype(out_ref.dtype)

    G = MLSTM_CHUNKS_PER_STEP
    for i in range(G):
        weights_stage(i, i)

    def out_step(j, carry):
        cur = (j % 2) * G
        nxt = G - cur
        for i in range(G):
            output_stage(j * G + i, cur + i)
        for i in range(G):
            weights_stage((j + 1) * G + i, nxt + i)
        return carry

    n_steps = nc // G
    lax.fori_loop(0, n_steps - 1, out_step, 0)
    for i in range(G):
        output_stage((n_steps - 1) * G + i, ((n_steps - 1) % 2) * G + i)


def _mlstm(qk, v, o, gates, conv_w, conv_b, norm_g, tri3, eye2):
    b, seq, _ = v.shape
    H, d, L = N_MLSTM_HEADS, MLSTM_HEAD_DIM, CHUNK
    nc = seq // L
    assert nc % MLSTM_SCAN_UNROLL == 0 and nc % MLSTM_CONV_UNROLL == 0 and L == LANES
    assert nc % MLSTM_CHUNKS_PER_STEP == 0
    col = lambda off: pl.BlockSpec((None, seq, d), lambda bi, hi: (bi, 0, hi + off))
    stat = pltpu.VMEM((2, nc, L), F32)
    slots = 2 * MLSTM_CHUNKS_PER_STEP
    return pl.pallas_call(
        functools.partial(_mlstm_body, seq=seq),
        grid=(b, H),
        in_specs=[
            col(0), col(H), col(0), col(0),
            pl.BlockSpec((4, None, None, nc, L), lambda bi, hi: (0, hi, bi, 0, 0)),
            pl.BlockSpec((None, 2, 3, d), lambda bi, hi: (hi, 0, 0, 0)),
            pl.BlockSpec((None, 2, 1, d), lambda bi, hi: (hi, 0, 0, 0)),
            pl.BlockSpec((None, 1, d), lambda bi, hi: (hi, 0, 0)),
            _const_spec(tri3.shape), _const_spec(eye2.shape),
        ],
        out_specs=pl.BlockSpec((None, seq, d), lambda bi, hi: (bi, 0, hi)),
        out_shape=jax.ShapeDtypeStruct((b, seq, H * d), BF16),
        scratch_shapes=[
            pltpu.VMEM((seq, d), BF16),
            pltpu.VMEM((d, seq), BF16),
            pltpu.VMEM((2, nc, d, 2 * d), BF16),
            pltpu.VMEM((2, d, 2 * d), F32),
            stat, stat, stat,
            stat, stat, stat,
            stat,
            pltpu.VMEM((slots, 2, L, L + d), BF16),
            pltpu.VMEM((slots, 2, L, d), F32),
        ],
        compiler_params=pltpu.CompilerParams(dimension_semantics=("arbitrary", "arbitrary"),
                                             vmem_limit_bytes=VMEM_LIMIT["mlstm"]),
        name="mlstm",
    )(qk, qk, v, o, gates, conv_w, conv_b, norm_g, tri3, eye2)


def _natten_body(q_ref, k_ref, v_ref, bias_ref, out_ref, pa_ref, pb_ref, *, rows):
    hd = NA_HEAD_DIM
    win = WIN_H * GRID_W
    lane = lax.broadcasted_iota(jnp.int32, (GRID_W, 2 * hd), 1)
    first = lane < hd
    ones_blk = jnp.ones((win, 2 * hd), BF16)

    def window_start(r):
        return jnp.clip(r - WIN_H // 2, 0, rows - WIN_H)

    def prob_stage(r, p_ref, slot):
        rs = window_start(r)
        q = q_ref[pl.ds(pl.multiple_of(r * GRID_W, GRID_W), GRID_W), :]
        zero = jnp.zeros_like(q)
        qs = jnp.concatenate([jnp.where(first, q, zero), jnp.where(first, zero, q)], axis=0)
        kwin = k_ref[pl.ds(pl.multiple_of(rs * GRID_W, GRID_W), win), :]
        off = rs - r + (WIN_H - 1)
        bias = bias_ref[off & 1, :, pl.ds(pl.multiple_of((off >> 1) * LANES, LANES), win)]
        s = lax.dot_general(qs, kwin, NT_DIMS, preferred_element_type=F32) + bias
        p_ref[slot] = jnp.exp(s - jnp.max(s, axis=1, keepdims=True)).astype(BF16)

    def output_stage(r, p_ref, slot):
        rs = window_start(r)
        vwin = v_ref[pl.ds(pl.multiple_of(rs * GRID_W, GRID_W), win), :]
        o = jnp.dot(p_ref[slot], jnp.concatenate([vwin, ones_blk], axis=1), preferred_element_type=F32)
        o = o[:, :2 * hd] / o[:, 2 * hd:]
        out = jnp.where(first, o[:GRID_W], o[GRID_W:])
        out_ref[pl.ds(pl.multiple_of(r * GRID_W, GRID_W), GRID_W), :] = out.astype(out_ref.dtype)

    R = NA_ROWS_PER_STEP
    n_groups = rows // R

    def produce(g, p_ref):
        for i in range(R):
            prob_stage(g * R + i, p_ref, i)

    def consume(g, p_ref):
        for i in range(R):
            output_stage(g * R + i, p_ref, i)

    produce(0, pa_ref)

    def group_pair(j, carry):
        g = 2 * j
        produce(g + 1, pb_ref)
        consume(g, pa_ref)
        produce(g + 2, pa_ref)
        consume(g + 1, pb_ref)
        return carry

    lax.fori_loop(0, n_groups // 2 - 1, group_pair, 0)
    produce(n_groups - 1, pb_ref)
    consume(n_groups - 2, pa_ref)
    consume(n_groups - 1, pb_ref)


def _natten(nq, nk, nv, bias_tab):
    b, seq, d_n = nq.shape
    pairs = N_NA_HEADS // 2
    width = 2 * NA_HEAD_DIM
    rows = seq // GRID_W
    assert rows % (2 * NA_ROWS_PER_STEP) == 0 and rows >= 4 * NA_ROWS_PER_STEP
    col = pl.BlockSpec((None, seq, width), lambda bi, pi: (bi, 0, pi))
    return pl.pallas_call(
        functools.partial(_natten_body, rows=rows),
        grid=(b, pairs),
        in_specs=[col, col, col,
                  pl.BlockSpec((None, 2, 2 * GRID_W, NA_BIAS_LANES), lambda bi, pi: (pi, 0, 0, 0))],
        out_specs=col,
        out_shape=jax.ShapeDtypeStruct((b, seq, d_n), BF16),
        scratch_shapes=[pltpu.VMEM((NA_ROWS_PER_STEP, 2 * GRID_W, WIN_H * GRID_W), BF16)] * 2,
        compiler_params=pltpu.CompilerParams(dimension_semantics=("arbitrary", "arbitrary"),
                                             vmem_limit_bytes=VMEM_LIMIT["natten"]),
        name="natten",
    )(nq, nk, nv, bias_tab)


def _natten_bias_table(rpb):
    c = np.arange(GRID_W)
    cs = np.clip(c - WIN_W // 2, 0, GRID_W - WIN_W)
    cp = np.arange(GRID_W)
    valid = (cp[None, :] >= cs[:, None]) & (cp[None, :] < cs[:, None] + WIN_W)
    rel = cp[None, None, :] - c[None, :, None] + (WIN_W - 1)
    onehot = (rel == np.arange(2 * WIN_W - 1)[:, None, None]).astype(np.float32)
    n_rel = 2 * WIN_H - 1
    tab = jnp.einsum('phrd,dcq->phcrq', rpb.astype(F32).reshape(N_NA_HEADS // 2, 2, n_rel, 2 * WIN_W - 1),
                     jnp.asarray(onehot), precision=lax.Precision.HIGHEST)
    tab = jnp.where(jnp.asarray(valid)[None, None, :, None, :], tab, NEG_BIG)
    tab = tab.reshape(N_NA_HEADS // 2, 2 * GRID_W, n_rel * GRID_W)
    even = tab[:, :, :NA_BIAS_LANES]
    odd = tab[:, :, GRID_W:GRID_W + NA_BIAS_LANES]
    return jnp.stack([even, odd], axis=1)


def _tail_body(x_ref, ya_ref, yb_ref, p_ref, woa_ref, wob_ref, g2_ref, w1_ref, w2_ref,
               g3_ref, wg_ref, wu_ref, out_ref, *, ff_chunk):
    d_ff = w1_ref.shape[1]
    h = (x_ref[...]
         + jnp.dot(ya_ref[...], woa_ref[...], preferred_element_type=F32)
         + jnp.dot(yb_ref[...], wob_ref[...], preferred_element_type=F32))
    u = (h * _rms_scale(h) * g2_ref[...]).astype(BF16)
    out_ref[...] = h
    for j in range(d_ff // ff_chunk):
        z = jnp.dot(u, w1_ref[:, j * ff_chunk:(j + 1) * ff_chunk], preferred_element_type=F32)
        z = jnp.maximum(z, 0.0)
        out_ref[...] += jnp.dot((z * z).astype(BF16), w2_ref[j * ff_chunk:(j + 1) * ff_chunk, :],
                                preferred_element_type=F32)
    h = out_ref[...]
    u = (h * _rms_scale(h) * g3_ref[...]).astype(BF16)
    gate = _sigmoid(jnp.dot(u, wg_ref[...], preferred_element_type=F32))
    up = jnp.dot(p_ref[...].astype(BF16), wu_ref[...], preferred_element_type=F32)
    out_ref[...] = h + gate * up


def _tail(x2, ya, yb, p2, wo, g2, w1, w2, g3, wg, wu, *, tm, ff_chunk):
    t, d = x2.shape
    d_mix = wo.shape[0]
    assert ya.shape[1] == yb.shape[1] == d_mix // 2 and t % tm == 0
    row = lambda width: pl.BlockSpec((tm, width), lambda i: (i, 0))
    half = lambda k: pl.BlockSpec((d_mix // 2, d), lambda i: (k, 0), pipeline_mode=pl.Buffered(1))
    consts = [g2, w1, w2, g3, wg, wu]
    return pl.pallas_call(
        functools.partial(_tail_body, ff_chunk=ff_chunk),
        grid=(t // tm,),
        in_specs=[row(d), row(ya.shape[1]), row(yb.shape[1]), row(p2.shape[1]), half(0), half(1)]
                 + [_const_spec(c.shape) for c in consts],
        out_specs=row(d),
        out_shape=jax.ShapeDtypeStruct((t, d), F32),
        compiler_params=pltpu.CompilerParams(dimension_semantics=("arbitrary",),
                                             vmem_limit_bytes=VMEM_LIMIT["tail"]),
        name="tail",
    )(x2, ya, yb, p2, wo, wo, *consts)


def kernel(x, p, norm1_g, w_in, conv_w, conv_b, gate_b, mlstm_norm_g, q_norm_g, k_norm_g, rpb,
           w_out, norm2_g, w_ff1, w_ff2, ple_norm_g, w_ple_gate, w_ple_up):
    b, seq, d = x.shape
    depth = w_in.shape[0]
    H, hd, L = N_MLSTM_HEADS, MLSTM_HEAD_DIM, CHUNK
    d_m = H * hd
    d_n = N_NA_HEADS * NA_HEAD_DIM
    t = b * seq
    rows = seq // GRID_W
    assert rows >= WIN_H

    bd = jnp.asarray(np.kron(np.eye(N_NA_HEADS), np.ones((NA_HEAD_DIM, NA_HEAD_DIM))), BF16)
    ri, ci = np.indices((L, L))
    tri = np.stack([ci <= ri, ci >= ri])
    tri3 = jnp.asarray(np.concatenate([tri, tri, tri], axis=1), BF16)
    eye2 = jnp.asarray(np.concatenate([ci == ri, ci == ri], axis=1), BF16)

    h = x.reshape(t, d)
    for i in range(depth):
        wi = w_in[i].astype(BF16)
        w_n = wi[:, 4 * d_m + N_GATE:]
        qg = (q_norm_g[i].reshape(1, d_n) * (NA_HEAD_DIM ** -0.5)).astype(F32)
        kg = k_norm_g[i].reshape(1, d_n).astype(F32)
        qk, mv, mo, nq, nk, nv, gates = _inproj(
            h, norm1_g[i].reshape(1, d), wi, w_n, gate_b[i].reshape(N_GATE, 1), bd, qg, kg,
            d_m=d_m, d_n=d_n, tm=INPROJ_TM)

        cw = conv_w[i].reshape(3, 2, H, hd).transpose(2, 1, 0, 3)
        cb = conv_b[i].reshape(2, H, 1, hd).transpose(1, 0, 2, 3)
        y_a = _mlstm(qk.reshape(b, seq, 2 * d_m), mv.reshape(b, seq, d_m), mo.reshape(b, seq, d_m),
                     gates.reshape(4, H, b, seq // L, L), cw, cb,
                     mlstm_norm_g[i].reshape(H, 1, hd), tri3, eye2)

        bias_tab = _natten_bias_table(rpb[i])
        y_b = _natten(nq.reshape(b, seq, d_n), nk.reshape(b, seq, d_n), nv.reshape(b, seq, d_n), bias_tab)

        h = _tail(h, y_a.reshape(t, d_m), y_b.reshape(t, d_n), p[i].reshape(t, -1),
                  w_out[i].astype(BF16), norm2_g[i].reshape(1, d), w_ff1[i].astype(BF16), w_ff2[i].astype(BF16),
                  ple_norm_g[i].reshape(1, d), w_ple_gate[i].astype(BF16), w_ple_up[i].astype(BF16),
                  tm=TAIL_TM, ff_chunk=TAIL_FF_CHUNK)
    return h.reshape(b, seq, d)
```

```python
import functools

import jax
import jax.numpy as jnp
import numpy as np
from jax import lax
from jax.experimental import pallas as pl
from jax.experimental.pallas import tpu as pltpu

F32 = jnp.float32
BF16 = jnp.bfloat16

N_MLSTM_HEADS = 4
MLSTM_HEAD_DIM = 128
N_NA_HEADS = 8
NA_HEAD_DIM = 64
GRID_W = 64
WIN_H = 8
WIN_W = 16
CHUNK = 128
N_GATE = 4 * N_MLSTM_HEADS
RMS_EPS = 1e-6
NEG_BIG = -1e30
F32_BIG = 3e38
NA_BIAS_LANES = (2 * WIN_H - 2) * GRID_W
NA_ROWS_PER_STEP = 16
MLSTM_CHUNKS_PER_STEP = 8
MLSTM_CONV_UNROLL = 16
MLSTM_SCAN_UNROLL = 16

LANES = 128
GATE_PAD = LANES
MIB = 1024 * 1024

INPROJ_TM = 1024
TAIL_TM = 1024
TAIL_FF_CHUNK = 1024
VMEM_LIMIT = {"inproj": 44 * MIB, "mlstm": 52 * MIB, "natten": 40 * MIB, "tail": 52 * MIB}

NT_DIMS = (((1,), (1,)), ((), ()))


def _const_spec(shape):
    return pl.BlockSpec(shape, lambda *_: (0,) * len(shape), pipeline_mode=pl.Buffered(1))


def _rms_scale(x):
    return lax.rsqrt(jnp.mean(x * x, axis=-1, keepdims=True) + RMS_EPS)


def _split3(x):
    hi = x.astype(BF16)
    r1 = x - hi.astype(F32)
    mid = r1.astype(BF16)
    lo = (r1 - mid.astype(F32)).astype(BF16)
    return hi, mid, lo


def _inproj_body(x_ref, g_ref, wm_ref, wn_ref, wg_ref, gb_ref, bd_ref, qg_ref, kg_ref,
                 qk_ref, v_ref, o_ref, nq_ref, nk_ref, nv_ref, gate_ref, *, d_m, d_n):
    x = x_ref[...]
    u = (x * _rms_scale(x) * g_ref[...]).astype(BF16)

    def proj(w_ref, lo, hi):
        return jnp.dot(u, w_ref[:, lo:hi], preferred_element_type=F32)

    def head_norm(y, gain_ref):
        ss = jnp.dot((y * y).astype(BF16), bd_ref[...], preferred_element_type=F32)
        return y * lax.rsqrt(ss * (1.0 / NA_HEAD_DIM) + RMS_EPS) * gain_ref[...]

    qk_ref[...] = proj(wm_ref, 0, 2 * d_m).astype(BF16)
    v_ref[...] = proj(wm_ref, 2 * d_m, 3 * d_m).astype(BF16)
    o_ref[...] = proj(wm_ref, 3 * d_m, 4 * d_m).astype(BF16)
    nq_ref[...] = head_norm(proj(wn_ref, 0, d_n), qg_ref).astype(BF16)
    nk_ref[...] = head_norm(proj(wn_ref, d_n, 2 * d_n), kg_ref).astype(BF16)
    nv_ref[...] = proj(wn_ref, 2 * d_n, 3 * d_n).astype(BF16)
    wg_t = wg_ref[...].astype(F32).T[:N_GATE, :].astype(BF16)
    gate_ref[...] = lax.dot_general(wg_t, u, NT_DIMS, preferred_element_type=F32) + gb_ref[...]


def _inproj(x2, g, w_all, w_n, gate_b, bd, qg, kg, *, d_m, d_n, tm):
    t, d = x2.shape
    assert (4 * d_m) % GATE_PAD == 0
    fixed = functools.partial(pl.BlockSpec, pipeline_mode=pl.Buffered(1))
    row = lambda width: pl.BlockSpec((tm, width), lambda i: (i, 0))
    out_shapes = (
        jax.ShapeDtypeStruct((t, 2 * d_m), BF16),
        jax.ShapeDtypeStruct((t, d_m), BF16),
        jax.ShapeDtypeStruct((t, d_m), BF16),
        jax.ShapeDtypeStruct((t, d_n), BF16),
        jax.ShapeDtypeStruct((t, d_n), BF16),
        jax.ShapeDtypeStruct((t, d_n), BF16),
        jax.ShapeDtypeStruct((N_GATE, t), F32),
    )
    return pl.pallas_call(
        functools.partial(_inproj_body, d_m=d_m, d_n=d_n),
        grid=(t // tm,),
        in_specs=[row(d), _const_spec((1, d)), fixed((d, 4 * d_m), lambda i: (0, 0)), _const_spec(w_n.shape),
                  fixed((d, GATE_PAD), lambda i: (0, 4 * d_m // GATE_PAD)), _const_spec((N_GATE, 1)),
                  _const_spec((d_n, d_n)), _const_spec((1, d_n)), _const_spec((1, d_n))],
        out_specs=(row(2 * d_m), row(d_m), row(d_m), row(d_n), row(d_n), row(d_n),
                   pl.BlockSpec((N_GATE, tm), lambda i: (0, i))),
        out_shape=out_shapes,
        compiler_params=pltpu.CompilerParams(dimension_semantics=("arbitrary",),
                                             vmem_limit_bytes=VMEM_LIMIT["inproj"]),
        name="inproj",
    )(x2, g, w_all, w_n, w_all, gate_b, bd, qg, kg)


def _log_sigmoid(x):
    return jnp.minimum(x, 0.0) - jnp.log1p(jnp.exp(-jnp.abs(x)))


def _sigmoid(x):
    return 1.0 / (1.0 + jnp.exp(-x))


def _mlstm_body(q_ref, k_ref, v_ref, o_ref, gate_ref, cw_ref, cb_ref, ng_ref, tri3_ref, eye2_ref,
                out_ref, qs_ref, kt_ref, cs_ref, ccur_ref, brow_ref, crow_ref, cmax_ref, arow_ref, bl_ref,
                ml_ref, ms_ref, sa_ref, ea_ref, sb_ref, eb_ref, *, seq):
    L = CHUNK
    d = MLSTM_HEAD_DIM
    nc = seq // L

    row_id = lax.broadcasted_iota(jnp.int32, (L, d), 0)
    col_id = lax.broadcasted_iota(jnp.int32, (L, d), 1)
    pos_id = lax.broadcasted_iota(jnp.int32, (nc, L), 1)
    ones_blk = jnp.ones((L, d), BF16)

    def conv_silu(src_ref, c, s0, w, b):
        x = src_ref[pl.ds(s0, L), :].astype(F32)
        p0 = pl.multiple_of(jnp.maximum(s0 - 16, 0), 16)
        n0 = pl.multiple_of(jnp.minimum(s0 + L, seq - 16), 16)
        prev_row = src_ref[pl.ds(p0, 16), :][15:16, :].astype(F32)
        next_row = src_ref[pl.ds(n0, 16), :][0:1, :].astype(F32)
        prev_row = jnp.where(c > 0, prev_row, 0.0)
        next_row = jnp.where(c < nc - 1, next_row, 0.0)
        x_prev = jnp.where(row_id == 0, prev_row, pltpu.roll(x, 1, 0))
        x_next = jnp.where(row_id == L - 1, next_row, pltpu.roll(x, L - 1, 0))
        y = w[0:1, :] * x_prev + w[1:2, :] * x + w[2:3, :] * x_next + b
        return y * _sigmoid(y)

    def conv_step(c, carry):
        s0 = pl.multiple_of(c * L, L)
        qs_ref[pl.ds(s0, L), :] = conv_silu(q_ref, c, s0, cw_ref[0], cb_ref[0]).astype(BF16)
        kk = conv_silu(k_ref, c, s0, cw_ref[1], cb_ref[1]) * (d ** -0.5)
        kt_ref[:, pl.ds(s0, L)] = kk.T.astype(BF16)
        return carry

    lax.fori_loop(0, nc, conv_step, 0, unroll=MLSTM_CONV_UNROLL)

    for dirn in (0, 1):
        i_g = gate_ref[2 * dirn]
        f_log = _log_sigmoid(gate_ref[2 * dirn + 1])
        f_cat = jnp.concatenate(_split3(f_log), axis=1)
        brow = jnp.dot(f_cat, tri3_ref[1 - dirn], preferred_element_type=F32)
        b_last = brow[:, L - 1:L] if dirn == 0 else brow[:, 0:1]
        a_row = i_g + b_last - brow
        a_max = jnp.max(a_row, axis=1, keepdims=True)
        crow = i_g - brow
        cmax = crow
        for sh in [1 << e for e in range(L.bit_length() - 1)]:
            if dirn == 0:
                cmax = jnp.maximum(cmax, jnp.where(pos_id >= sh, pltpu.roll(cmax, sh, 1), -jnp.inf))
            else:
                cmax = jnp.maximum(cmax, jnp.where(pos_id < L - sh, pltpu.roll(cmax, L - sh, 1), -jnp.inf))
        brow_ref[dirn] = brow
        crow_ref[dirn] = crow
        cmax_ref[dirn] = cmax
        arow_ref[dirn] = a_row
        bl_ref[dirn] = jnp.broadcast_to(b_last, (nc, L))
        ml_ref[dirn] = jnp.broadcast_to(a_max, (nc, L))

    def v_aug(s0):
        return jnp.concatenate([v_ref[pl.ds(s0, L), :], ones_blk], axis=1)

    ccur_ref[...] = jnp.zeros_like(ccur_ref)

    def scan_step(i, carry):
        new = []
        for dirn, c, m in ((0, i, carry[0]), (1, nc - 1 - i, carry[1])):
            s0 = pl.multiple_of(c * L, L)
            state = ccur_ref[dirn]
            cs_ref[dirn, c] = state.astype(BF16)
            ms_ref[dirn, pl.ds(c, 1), :] = m
            a_prev = m + bl_ref[dirn, pl.ds(c, 1), :]
            m_new = jnp.maximum(a_prev, ml_ref[dirn, pl.ds(c, 1), :])
            w_row = jnp.exp(arow_ref[dirn, pl.ds(c, 1), :] - m_new)
            kw = (kt_ref[:, pl.ds(s0, L)].astype(F32) * w_row).astype(BF16)
            k_loc = jnp.dot(kw, v_aug(s0), preferred_element_type=F32)
            ccur_ref[dirn] = jnp.exp(a_prev - m_new)[:, 0:1] * state + k_loc
            new.append(m_new)
        return tuple(new)

    m0 = jnp.zeros((1, L), F32)
    lax.fori_loop(0, nc, scan_step, (m0, m0), unroll=MLSTM_SCAN_UNROLL)

    for dirn in (0, 1):
        mu_all = jnp.maximum(ms_ref[dirn], cmax_ref[dirn])
        cmax_ref[dirn] = mu_all
        brow_ref[dirn] = jnp.minimum(jnp.exp(-(brow_ref[dirn] + mu_all)), F32_BIG)
    lower = col_id <= row_id
    upper = col_id >= row_id

    def weights_stage(c, s_ref, einv_ref, slot):
        s0 = pl.multiple_of(c * L, L)
        q = qs_ref[pl.ds(s0, L), :]
        qk = jnp.dot(q, kt_ref[:, pl.ds(s0, L)], preferred_element_type=F32)
        for dirn, mask in ((0, lower), (1, upper)):
            rows = []
            for stat_ref in (cmax_ref, brow_ref):
                hi, mid, _ = _split3(stat_ref[dirn, pl.ds(c, 1), :])
                rows.append(jnp.broadcast_to(jnp.concatenate([hi, mid], axis=1), (L, 2 * L)))
            col = lax.dot_general(eye2_ref[...], jnp.concatenate(rows, axis=0), NT_DIMS,
                                  preferred_element_type=F32)
            mu = col[:, :L]
            p = jnp.exp(jnp.where(mask, crow_ref[dirn, pl.ds(c, 1), :] - mu, -jnp.inf))
            q_inter = q.astype(F32) * jnp.exp(ms_ref[dirn, pl.ds(c, 1), :] - mu)
            s_ref[slot, dirn] = jnp.concatenate([qk * p, q_inter], axis=1).astype(BF16)
            einv_ref[slot, dirn] = col[:, L:]

    def output_stage(c, s_ref, einv_ref, slot):
        s0 = pl.multiple_of(c * L, L)
        vaug = v_aug(s0)
        hsum = None
        for dirn in (0, 1):
            nd = jnp.dot(s_ref[slot, dirn], jnp.concatenate([vaug, cs_ref[dirn, c]], axis=0),
                         preferred_element_type=F32)
            h = nd[:, :d] / jnp.maximum(jnp.abs(nd[:, d:]), einv_ref[slot, dirn])
            hsum = h if hsum is None else hsum + h
        y = hsum * _rms_scale(hsum) * ng_ref[...]
        y = y * _sigmoid(o_ref[pl.ds(s0, L), :].astype(F32))
        out_ref[pl.ds(s0, L), :] = y.astype(out_ref.dtype)

    G = MLSTM_CHUNKS_PER_STEP
    n_groups = nc // G

    def produce(g, s_ref, einv_ref):
        for i in range(G):
            weights_stage(g * G + i, s_ref, einv_ref, i)

    def consume(g, s_ref, einv_ref):
        for i in range(G):
            output_stage(g * G + i, s_ref, einv_ref, i)

    produce(0, sa_ref, ea_ref)

    def out_step(j, carry):
        g = 2 * j
        produce(g + 1, sb_ref, eb_ref)
        consume(g, sa_ref, ea_ref)
        produce(g + 2, sa_ref, ea_ref)
        consume(g + 1, sb_ref, eb_ref)
        return carry

    lax.fori_loop(0, n_groups // 2 - 1, out_step, 0)
    produce(n_groups - 1, sb_ref, eb_ref)
    consume(n_groups - 2, sa_ref, ea_ref)
    consume(n_groups - 1, sb_ref, eb_ref)


def _mlstm(qk, v, o, gates, conv_w, conv_b, norm_g, tri3, eye2):
    b, seq, _ = v.shape
    H, d, L = N_MLSTM_HEADS, MLSTM_HEAD_DIM, CHUNK
    nc = seq // L
    assert nc % MLSTM_SCAN_UNROLL == 0 and nc % MLSTM_CONV_UNROLL == 0 and L == LANES
    assert nc % (2 * MLSTM_CHUNKS_PER_STEP) == 0 and nc >= 4 * MLSTM_CHUNKS_PER_STEP
    col = lambda off: pl.BlockSpec((None, seq, d), lambda bi, hi: (bi, 0, hi + off))
    stat = pltpu.VMEM((2, nc, L), F32)
    slots = MLSTM_CHUNKS_PER_STEP
    return pl.pallas_call(
        functools.partial(_mlstm_body, seq=seq),
        grid=(b, H),
        in_specs=[
            col(0), col(H), col(0), col(0),
            pl.BlockSpec((4, None, None, nc, L), lambda bi, hi: (0, hi, bi, 0, 0)),
            pl.BlockSpec((None, 2, 3, d), lambda bi, hi: (hi, 0, 0, 0)),
            pl.BlockSpec((None, 2, 1, d), lambda bi, hi: (hi, 0, 0, 0)),
            pl.BlockSpec((None, 1, d), lambda bi, hi: (hi, 0, 0)),
            _const_spec(tri3.shape), _const_spec(eye2.shape),
        ],
        out_specs=pl.BlockSpec((None, seq, d), lambda bi, hi: (bi, 0, hi)),
        out_shape=jax.ShapeDtypeStruct((b, seq, H * d), BF16),
        scratch_shapes=[
            pltpu.VMEM((seq, d), BF16),
            pltpu.VMEM((d, seq), BF16),
            pltpu.VMEM((2, nc, d, 2 * d), BF16),
            pltpu.VMEM((2, d, 2 * d), F32),
            stat, stat, stat,
            stat, stat, stat,
            stat,
            pltpu.VMEM((slots, 2, L, L + d), BF16),
            pltpu.VMEM((slots, 2, L, d), F32),
            pltpu.VMEM((slots, 2, L, L + d), BF16),
            pltpu.VMEM((slots, 2, L, d), F32),
        ],
        compiler_params=pltpu.CompilerParams(dimension_semantics=("arbitrary", "arbitrary"),
                                             vmem_limit_bytes=VMEM_LIMIT["mlstm"]),
        name="mlstm",
    )(qk, qk, v, o, gates, conv_w, conv_b, norm_g, tri3, eye2)


def _natten_body(q_ref, k_ref, v_ref, bias_ref, out_ref, pa_ref, pb_ref, *, rows):
    hd = NA_HEAD_DIM
    win = WIN_H * GRID_W
    lane = lax.broadcasted_iota(jnp.int32, (GRID_W, 2 * hd), 1)
    first = lane < hd
    ones_blk = jnp.ones((win, 2 * hd), BF16)

    def window_start(r):
        return jnp.clip(r - WIN_H // 2, 0, rows - WIN_H)

    def prob_stage(r, p_ref, slot):
        rs = window_start(r)
        q = q_ref[pl.ds(pl.multiple_of(r * GRID_W, GRID_W), GRID_W), :]
        zero = jnp.zeros_like(q)
        qs = jnp.concatenate([jnp.where(first, q, zero), jnp.where(first, zero, q)], axis=0)
        kwin = k_ref[pl.ds(pl.multiple_of(rs * GRID_W, GRID_W), win), :]
        off = rs - r + (WIN_H - 1)
        bias = bias_ref[off & 1, :, pl.ds(pl.multiple_of((off >> 1) * LANES, LANES), win)]
        s = lax.dot_general(qs, kwin, NT_DIMS, preferred_element_type=F32) + bias
        p_ref[slot] = jnp.exp(s - jnp.max(s, axis=1, keepdims=True)).astype(BF16)

    def output_stage(r, p_ref, slot):
        rs = window_start(r)
        vwin = v_ref[pl.ds(pl.multiple_of(rs * GRID_W, GRID_W), win), :]
        o = jnp.dot(p_ref[slot], jnp.concatenate([vwin, ones_blk], axis=1), preferred_element_type=F32)
        o = o[:, :2 * hd] / o[:, 2 * hd:]
        out = jnp.where(first, o[:GRID_W], o[GRID_W:])
        out_ref[pl.ds(pl.multiple_of(r * GRID_W, GRID_W), GRID_W), :] = out.astype(out_ref.dtype)

    R = NA_ROWS_PER_STEP
    n_groups = rows // R

    def produce(g, p_ref):
        for i in range(R):
            prob_stage(g * R + i, p_ref, i)

    def consume(g, p_ref):
        for i in range(R):
            output_stage(g * R + i, p_ref, i)

    produce(0, pa_ref)

    def group_pair(j, carry):
        g = 2 * j
        produce(g + 1, pb_ref)
        consume(g, pa_ref)
        produce(g + 2, pa_ref)
        consume(g + 1, pb_ref)
        return carry

    lax.fori_loop(0, n_groups // 2 - 1, group_pair, 0)
    produce(n_groups - 1, pb_ref)
    consume(n_groups - 2, pa_ref)
    consume(n_groups - 1, pb_ref)


def _natten(nq, nk, nv, bias_tab):
    b, seq, d_n = nq.shape
    pairs = N_NA_HEADS // 2
    width = 2 * NA_HEAD_DIM
    rows = seq // GRID_W
    assert rows % (2 * NA_ROWS_PER_STEP) == 0 and rows >= 4 * NA_ROWS_PER_STEP
    col = pl.BlockSpec((None, seq, width), lambda bi, pi: (bi, 0, pi))
    return pl.pallas_call(
        functools.partial(_natten_body, rows=rows),
        grid=(b, pairs),
        in_specs=[col, col, col,
                  pl.BlockSpec((None, 2, 2 * GRID_W, NA_BIAS_LANES), lambda bi, pi: (pi, 0, 0, 0))],
        out_specs=col,
        out_shape=jax.ShapeDtypeStruct((b, seq, d_n), BF16),
        scratch_shapes=[pltpu.VMEM((NA_ROWS_PER_STEP, 2 * GRID_W, WIN_H * GRID_W), BF16)] * 2,
        compiler_params=pltpu.CompilerParams(dimension_semantics=("arbitrary", "arbitrary"),
                                             vmem_limit_bytes=VMEM_LIMIT["natten"]),
        name="natten",
    )(nq, nk, nv, bias_tab)


def _natten_bias_table(rpb):
    c = np.arange(GRID_W)
    cs = np.clip(c - WIN_W // 2, 0, GRID_W - WIN_W)
    cp = np.arange(GRID_W)
    valid = (cp[None, :] >= cs[:, None]) & (cp[None, :] < cs[:, None] + WIN_W)
    rel = cp[None, None, :] - c[None, :, None] + (WIN_W - 1)
    onehot = (rel == np.arange(2 * WIN_W - 1)[:, None, None]).astype(np.float32)
    n_rel = 2 * WIN_H - 1
    tab = jnp.einsum('phrd,dcq->phcrq', rpb.astype(F32).reshape(N_NA_HEADS // 2, 2, n_rel, 2 * WIN_W - 1),
                     jnp.asarray(onehot), precision=lax.Precision.HIGHEST)
    tab = jnp.where(jnp.asarray(valid)[None, None, :, None, :], tab, NEG_BIG)
    tab = tab.reshape(N_NA_HEADS // 2, 2 * GRID_W, n_rel * GRID_W)
    even = tab[:, :, :NA_BIAS_LANES]
    odd = tab[:, :, GRID_W:GRID_W + NA_BIAS_LANES]
    return jnp.stack([even, odd], axis=1)


def _tail_body(x_ref, ya_ref, yb_ref, p_ref, woa_ref, wob_ref, g2_ref, w1_ref, w2_ref,
               g3_ref, wg_ref, wu_ref, out_ref, *, ff_chunk):
    d_ff = w1_ref.shape[1]
    h = (x_ref[...]
         + jnp.dot(ya_ref[...], woa_ref[...], preferred_element_type=F32)
         + jnp.dot(yb_ref[...], wob_ref[...], preferred_element_type=F32))
    u = (h * _rms_scale(h) * g2_ref[...]).astype(BF16)
    out_ref[...] = h
    for j in range(d_ff // ff_chunk):
        z = jnp.dot(u, w1_ref[:, j * ff_chunk:(j + 1) * ff_chunk], preferred_element_type=F32)
        z = jnp.maximum(z, 0.0)
        out_ref[...] += jnp.dot((z * z).astype(BF16), w2_ref[j * ff_chunk:(j + 1) * ff_chunk, :],
                                preferred_element_type=F32)
    h = out_ref[...]
    u = (h * _rms_scale(h) * g3_ref[...]).astype(BF16)
    gate = _sigmoid(jnp.dot(u, wg_ref[...], preferred_element_type=F32))
    up = jnp.dot(p_ref[...].astype(BF16), wu_ref[...], preferred_element_type=F32)
    out_ref[...] = h + gate * up


def _tail(x2, ya, yb, p2, wo, g2, w1, w2, g3, wg, wu, *, tm, ff_chunk):
    t, d = x2.shape
    d_mix = wo.shape[0]
    assert ya.shape[1] == yb.shape[1] == d_mix // 2 and t % tm == 0
    row = lambda width: pl.BlockSpec((tm, width), lambda i: (i, 0))
    half = lambda k: pl.BlockSpec((d_mix // 2, d), lambda i: (k, 0), pipeline_mode=pl.Buffered(1))
    consts = [g2, w1, w2, g3, wg, wu]
    return pl.pallas_call(
        functools.partial(_tail_body, ff_chunk=ff_chunk),
        grid=(t // tm,),
        in_specs=[row(d), row(ya.shape[1]), row(yb.shape[1]), row(p2.shape[1]), half(0), half(1)]
                 + [_const_spec(c.shape) for c in consts],
        out_specs=row(d),
        out_shape=jax.ShapeDtypeStruct((t, d), F32),
        compiler_params=pltpu.CompilerParams(dimension_semantics=("arbitrary",),
                                             vmem_limit_bytes=VMEM_LIMIT["tail"]),
        name="tail",
    )(x2, ya, yb, p2, wo, wo, *consts)


def kernel(x, p, norm1_g, w_in, conv_w, conv_b, gate_b, mlstm_norm_g, q_norm_g, k_norm_g, rpb,
           w_out, norm2_g, w_ff1, w_ff2, ple_norm_g, w_ple_gate, w_ple_up):
    b, seq, d = x.shape
    depth = w_in.shape[0]
    H, hd, L = N_MLSTM_HEADS, MLSTM_HEAD_DIM, CHUNK
    d_m = H * hd
    d_n = N_NA_HEADS * NA_HEAD_DIM
    t = b * seq
    rows = seq // GRID_W
    assert rows >= WIN_H

    bd = jnp.asarray(np.kron(np.eye(N_NA_HEADS), np.ones((NA_HEAD_DIM, NA_HEAD_DIM))), BF16)
    ri, ci = np.indices((L, L))
    tri = np.stack([ci <= ri, ci >= ri])
    tri3 = jnp.asarray(np.concatenate([tri, tri, tri], axis=1), BF16)
    eye2 = jnp.asarray(np.concatenate([ci == ri, ci == ri], axis=1), BF16)

    h = x.reshape(t, d)
    for i in range(depth):
        wi = w_in[i].astype(BF16)
        w_n = wi[:, 4 * d_m + N_GATE:]
        qg = (q_norm_g[i].reshape(1, d_n) * (NA_HEAD_DIM ** -0.5)).astype(F32)
        kg = k_norm_g[i].reshape(1, d_n).astype(F32)
        qk, mv, mo, nq, nk, nv, gates = _inproj(
            h, norm1_g[i].reshape(1, d), wi, w_n, gate_b[i].reshape(N_GATE, 1), bd, qg, kg,
            d_m=d_m, d_n=d_n, tm=INPROJ_TM)

        cw = conv_w[i].reshape(3, 2, H, hd).transpose(2, 1, 0, 3)
        cb = conv_b[i].reshape(2, H, 1, hd).transpose(1, 0, 2, 3)
        y_a = _mlstm(qk.reshape(b, seq, 2 * d_m), mv.reshape(b, seq, d_m), mo.reshape(b, seq, d_m),
                     gates.reshape(4, H, b, seq // L, L), cw, cb,
                     mlstm_norm_g[i].reshape(H, 1, hd), tri3, eye2)

        bias_tab = _natten_bias_table(rpb[i])
        y_b = _natten(nq.reshape(b, seq, d_n), nk.reshape(b, seq, d_n), nv.reshape(b, seq, d_n), bias_tab)

        h = _tail(h, y_a.reshape(t, d_m), y_b.reshape(t, d_n), p[i].reshape(t, -1),
                  w_out[i].astype(BF16), norm2_g[i].reshape(1, d), w_ff1[i].astype(BF16), w_ff2[i].astype(BF16),
                  ple_norm_g[i].reshape(1, d), w_ple_gate[i].astype(BF16), w_ple_up[i].astype(BF16),
                  tm=TAIL_TM, ff_chunk=TAIL_FF_CHUNK)
    return h.reshape(b, seq, d)
```

```python
import functools

import jax
import jax.numpy as jnp
import numpy as np
from jax import lax
from jax.experimental import pallas as pl
from jax.experimental.pallas import tpu as pltpu

F32 = jnp.float32
BF16 = jnp.bfloat16

N_MLSTM_HEADS = 4
MLSTM_HEAD_DIM = 128
N_NA_HEADS = 8
NA_HEAD_DIM = 64
GRID_W = 64
WIN_H = 8
WIN_W = 16
CHUNK = 128
N_GATE = 4 * N_MLSTM_HEADS
RMS_EPS = 1e-6
NEG_BIG = -1e30
F32_BIG = 3e38
NA_BIAS_LANES = (2 * WIN_H - 2) * GRID_W
NA_ROWS_PER_STEP = 16
MLSTM_CHUNKS_PER_STEP = 8
MLSTM_CONV_UNROLL = 16
MLSTM_SCAN_UNROLL = 16

LANES = 128
BF16_ROWS = 16
GATE_PAD = LANES
MIB = 1024 * 1024

INPROJ_TM = 1024
TAIL_TM = 1024
TAIL_FF_CHUNK = 1024
VMEM_LIMIT = {"inproj": 44 * MIB, "mlstm": 52 * MIB, "natten": 40 * MIB, "tail": 52 * MIB}

NT_DIMS = (((1,), (1,)), ((), ()))


def _const_spec(shape):
    return pl.BlockSpec(shape, lambda *_: (0,) * len(shape), pipeline_mode=pl.Buffered(1))


def _rms_scale(x):
    return lax.rsqrt(jnp.mean(x * x, axis=-1, keepdims=True) + RMS_EPS)


def _split3(x):
    hi = x.astype(BF16)
    r1 = x - hi.astype(F32)
    mid = r1.astype(BF16)
    lo = (r1 - mid.astype(F32)).astype(BF16)
    return hi, mid, lo


def _inproj_body(x_ref, g_ref, wm_ref, wn_ref, wg_ref, gb_ref, bd_ref, qg_ref, kg_ref,
                 qk_ref, v_ref, o_ref, nq_ref, nk_ref, nv_ref, gate_ref, *, d_m, d_n):
    x = x_ref[...]
    u = (x * _rms_scale(x) * g_ref[...]).astype(BF16)

    def proj(w_ref, lo, hi):
        return jnp.dot(u, w_ref[:, lo:hi], preferred_element_type=F32)

    def head_norm(y, gain_ref):
        ss = jnp.dot((y * y).astype(BF16), bd_ref[...], preferred_element_type=F32)
        return y * lax.rsqrt(ss * (1.0 / NA_HEAD_DIM) + RMS_EPS) * gain_ref[...]

    qk_ref[...] = proj(wm_ref, 0, 2 * d_m).astype(BF16)
    v_ref[...] = proj(wm_ref, 2 * d_m, 3 * d_m).astype(BF16)
    o_ref[...] = proj(wm_ref, 3 * d_m, 4 * d_m).astype(BF16)
    nq_ref[...] = head_norm(proj(wn_ref, 0, d_n), qg_ref).astype(BF16)
    nk_ref[...] = head_norm(proj(wn_ref, d_n, 2 * d_n), kg_ref).astype(BF16)
    nv_ref[...] = proj(wn_ref, 2 * d_n, 3 * d_n).astype(BF16)
    wg_t = wg_ref[...].astype(F32).T[:N_GATE, :].astype(BF16)
    gate_ref[...] = lax.dot_general(wg_t, u, NT_DIMS, preferred_element_type=F32) + gb_ref[...]


def _inproj(x2, g, w_all, w_n, gate_b, bd, qg, kg, *, d_m, d_n, tm):
    t, d = x2.shape
    assert (4 * d_m) % GATE_PAD == 0
    fixed = functools.partial(pl.BlockSpec, pipeline_mode=pl.Buffered(1))
    row = lambda width: pl.BlockSpec((tm, width), lambda i: (i, 0))
    out_shapes = (
        jax.ShapeDtypeStruct((t, 2 * d_m), BF16),
        jax.ShapeDtypeStruct((t, d_m), BF16),
        jax.ShapeDtypeStruct((t, d_m), BF16),
        jax.ShapeDtypeStruct((t, d_n), BF16),
        jax.ShapeDtypeStruct((t, d_n), BF16),
        jax.ShapeDtypeStruct((t, d_n), BF16),
        jax.ShapeDtypeStruct((N_GATE, t), F32),
    )
    return pl.pallas_call(
        functools.partial(_inproj_body, d_m=d_m, d_n=d_n),
        grid=(t // tm,),
        in_specs=[row(d), _const_spec((1, d)), fixed((d, 4 * d_m), lambda i: (0, 0)), _const_spec(w_n.shape),
                  fixed((d, GATE_PAD), lambda i: (0, 4 * d_m // GATE_PAD)), _const_spec((N_GATE, 1)),
                  _const_spec((d_n, d_n)), _const_spec((1, d_n)), _const_spec((1, d_n))],
        out_specs=(row(2 * d_m), row(d_m), row(d_m), row(d_n), row(d_n), row(d_n),
                   pl.BlockSpec((N_GATE, tm), lambda i: (0, i))),
        out_shape=out_shapes,
        compiler_params=pltpu.CompilerParams(dimension_semantics=("arbitrary",),
                                             vmem_limit_bytes=VMEM_LIMIT["inproj"]),
        name="inproj",
    )(x2, g, w_all, w_n, w_all, gate_b, bd, qg, kg)


def _log_sigmoid(x):
    return jnp.minimum(x, 0.0) - jnp.log1p(jnp.exp(-jnp.abs(x)))


def _sigmoid(x):
    return 1.0 / (1.0 + jnp.exp(-x))


def _mlstm_body(q_ref, k_ref, v_ref, o_ref, gate_ref, cw_ref, cb_ref, ng_ref, tri3_ref, eye2_ref,
                out_ref, qs_ref, kt_ref, cs_ref, ccur_ref, brow_ref, crow_ref, cmax_ref, arow_ref, bl_ref,
                ml_ref, ms_ref, sa_ref, ea_ref, sb_ref, eb_ref, *, seq):
    L = CHUNK
    d = MLSTM_HEAD_DIM
    nc = seq // L

    row_id = lax.broadcasted_iota(jnp.int32, (L, d), 0)
    col_id = lax.broadcasted_iota(jnp.int32, (L, d), 1)
    pos_id = lax.broadcasted_iota(jnp.int32, (nc, L), 1)
    ones_blk = jnp.ones((L, d), BF16)

    def conv_silu(src_ref, c, s0, w, b):
        x = src_ref[pl.ds(s0, L), :].astype(F32)
        p0 = pl.multiple_of(jnp.maximum(s0 - BF16_ROWS, 0), BF16_ROWS)
        n0 = pl.multiple_of(jnp.minimum(s0 + L, seq - BF16_ROWS), BF16_ROWS)
        prev_row = src_ref[pl.ds(p0, BF16_ROWS), :][BF16_ROWS - 1:BF16_ROWS, :].astype(F32)
        next_row = src_ref[pl.ds(n0, BF16_ROWS), :][0:1, :].astype(F32)
        prev_row = jnp.where(c > 0, prev_row, 0.0)
        next_row = jnp.where(c < nc - 1, next_row, 0.0)
        x_prev = jnp.where(row_id == 0, prev_row, pltpu.roll(x, 1, 0))
        x_next = jnp.where(row_id == L - 1, next_row, pltpu.roll(x, L - 1, 0))
        y = w[0:1, :] * x_prev + w[1:2, :] * x + w[2:3, :] * x_next + b
        return y * _sigmoid(y)

    def conv_step(c, carry):
        s0 = pl.multiple_of(c * L, L)
        qs_ref[pl.ds(s0, L), :] = conv_silu(q_ref, c, s0, cw_ref[0], cb_ref[0]).astype(BF16)
        kk = conv_silu(k_ref, c, s0, cw_ref[1], cb_ref[1]) * (d ** -0.5)
        kt_ref[:, pl.ds(s0, L)] = kk.T.astype(BF16)
        return carry

    lax.fori_loop(0, nc, conv_step, 0, unroll=MLSTM_CONV_UNROLL)

    for dirn in (0, 1):
        i_g = gate_ref[2 * dirn]
        f_log = _log_sigmoid(gate_ref[2 * dirn + 1])
        f_cat = jnp.concatenate(_split3(f_log), axis=1)
        brow = jnp.dot(f_cat, tri3_ref[1 - dirn], preferred_element_type=F32)
        b_last = brow[:, L - 1:L] if dirn == 0 else brow[:, 0:1]
        a_row = i_g + b_last - brow
        a_max = jnp.max(a_row, axis=1, keepdims=True)
        crow = i_g - brow
        cmax = crow
        for sh in [1 << e for e in range(L.bit_length() - 1)]:
            if dirn == 0:
                cmax = jnp.maximum(cmax, jnp.where(pos_id >= sh, pltpu.roll(cmax, sh, 1), -jnp.inf))
            else:
                cmax = jnp.maximum(cmax, jnp.where(pos_id < L - sh, pltpu.roll(cmax, L - sh, 1), -jnp.inf))
        brow_ref[dirn] = brow
        crow_ref[dirn] = crow
        cmax_ref[dirn] = cmax
        arow_ref[dirn] = a_row
        bl_ref[dirn] = jnp.broadcast_to(b_last, (nc, L))
        ml_ref[dirn] = jnp.broadcast_to(a_max, (nc, L))

    def v_aug(s0):
        return jnp.concatenate([v_ref[pl.ds(s0, L), :], ones_blk], axis=1)

    ccur_ref[...] = jnp.zeros_like(ccur_ref)

    def scan_step(i, carry):
        new = []
        for dirn, c, m in ((0, i, carry[0]), (1, nc - 1 - i, carry[1])):
            s0 = pl.multiple_of(c * L, L)
            state = ccur_ref[dirn]
            cs_ref[dirn, c] = state.astype(BF16)
            ms_ref[dirn, pl.ds(c, 1), :] = m
            a_prev = m + bl_ref[dirn, pl.ds(c, 1), :]
            m_new = jnp.maximum(a_prev, ml_ref[dirn, pl.ds(c, 1), :])
            w_row = jnp.exp(arow_ref[dirn, pl.ds(c, 1), :] - m_new)
            kw = (kt_ref[:, pl.ds(s0, L)].astype(F32) * w_row).astype(BF16)
            k_loc = jnp.dot(kw, v_aug(s0), preferred_element_type=F32)
            ccur_ref[dirn] = jnp.exp(a_prev - m_new)[:, 0:1] * state + k_loc
            new.append(m_new)
        return tuple(new)

    m0 = jnp.zeros((1, L), F32)
    lax.fori_loop(0, nc, scan_step, (m0, m0), unroll=MLSTM_SCAN_UNROLL)

    for dirn in (0, 1):
        mu_all = jnp.maximum(ms_ref[dirn], cmax_ref[dirn])
        cmax_ref[dirn] = mu_all
        brow_ref[dirn] = jnp.minimum(jnp.exp(-(brow_ref[dirn] + mu_all)), F32_BIG)
    lower = col_id <= row_id
    upper = col_id >= row_id

    def weights_stage(c, s_ref, einv_ref, slot):
        s0 = pl.multiple_of(c * L, L)
        q = qs_ref[pl.ds(s0, L), :]
        qk = jnp.dot(q, kt_ref[:, pl.ds(s0, L)], preferred_element_type=F32)
        for dirn, mask in ((0, lower), (1, upper)):
            rows = []
            for stat_ref in (cmax_ref, brow_ref):
                hi, mid, _ = _split3(stat_ref[dirn, pl.ds(c, 1), :])
                rows.append(jnp.broadcast_to(jnp.concatenate([hi, mid], axis=1), (L, 2 * L)))
            col = lax.dot_general(eye2_ref[...], jnp.concatenate(rows, axis=0), NT_DIMS,
                                  preferred_element_type=F32)
            mu = col[:, :L]
            p = jnp.exp(jnp.where(mask, crow_ref[dirn, pl.ds(c, 1), :] - mu, -jnp.inf))
            q_inter = q.astype(F32) * jnp.exp(ms_ref[dirn, pl.ds(c, 1), :] - mu)
            s_ref[slot, dirn] = jnp.concatenate([qk * p, q_inter], axis=1).astype(BF16)
            einv_ref[slot, dirn] = col[:, L:]

    def output_stage(c, s_ref, einv_ref, slot):
        s0 = pl.multiple_of(c * L, L)
        vaug = v_aug(s0)
        hsum = None
        for dirn in (0, 1):
            nd = jnp.dot(s_ref[slot, dirn], jnp.concatenate([vaug, cs_ref[dirn, c]], axis=0),
                         preferred_element_type=F32)
            h = nd[:, :d] / jnp.maximum(jnp.abs(nd[:, d:]), einv_ref[slot, dirn])
            hsum = h if hsum is None else hsum + h
        y = hsum * _rms_scale(hsum) * ng_ref[...]
        y = y * _sigmoid(o_ref[pl.ds(s0, L), :].astype(F32))
        out_ref[pl.ds(s0, L), :] = y.astype(out_ref.dtype)

    G = MLSTM_CHUNKS_PER_STEP
    n_groups = nc // G

    def produce(g, s_ref, einv_ref):
        for i in range(G):
            weights_stage(g * G + i, s_ref, einv_ref, i)

    def consume(g, s_ref, einv_ref):
        for i in range(G):
            output_stage(g * G + i, s_ref, einv_ref, i)

    produce(0, sa_ref, ea_ref)

    def out_step(j, carry):
        g = 2 * j
        produce(g + 1, sb_ref, eb_ref)
        consume(g, sa_ref, ea_ref)
        produce(g + 2, sa_ref, ea_ref)
        consume(g + 1, sb_ref, eb_ref)
        return carry

    lax.fori_loop(0, n_groups // 2 - 1, out_step, 0)
    produce(n_groups - 1, sb_ref, eb_ref)
    consume(n_groups - 2, sa_ref, ea_ref)
    consume(n_groups - 1, sb_ref, eb_ref)


def _mlstm(qk, v, o, gates, conv_w, conv_b, norm_g, tri3, eye2):
    b, seq, _ = v.shape
    H, d, L = N_MLSTM_HEADS, MLSTM_HEAD_DIM, CHUNK
    nc = seq // L
    assert nc % MLSTM_SCAN_UNROLL == 0 and nc % MLSTM_CONV_UNROLL == 0 and L == LANES
    assert nc % (2 * MLSTM_CHUNKS_PER_STEP) == 0 and nc >= 4 * MLSTM_CHUNKS_PER_STEP
    col = lambda off: pl.BlockSpec((None, seq, d), lambda bi, hi: (bi, 0, hi + off))
    stat = pltpu.VMEM((2, nc, L), F32)
    slots = MLSTM_CHUNKS_PER_STEP
    return pl.pallas_call(
        functools.partial(_mlstm_body, seq=seq),
        grid=(b, H),
        in_specs=[
            col(0), col(H), col(0), col(0),
            pl.BlockSpec((4, None, None, nc, L), lambda bi, hi: (0, hi, bi, 0, 0)),
            pl.BlockSpec((None, 2, 3, d), lambda bi, hi: (hi, 0, 0, 0)),
            pl.BlockSpec((None, 2, 1, d), lambda bi, hi: (hi, 0, 0, 0)),
            pl.BlockSpec((None, 1, d), lambda bi, hi: (hi, 0, 0)),
            _const_spec(tri3.shape), _const_spec(eye2.shape),
        ],
        out_specs=pl.BlockSpec((None, seq, d), lambda bi, hi: (bi, 0, hi)),
        out_shape=jax.ShapeDtypeStruct((b, seq, H * d), BF16),
        scratch_shapes=[
            pltpu.VMEM((seq, d), BF16),
            pltpu.VMEM((d, seq), BF16),
            pltpu.VMEM((2, nc, d, 2 * d), BF16),
            pltpu.VMEM((2, d, 2 * d), F32),
            stat, stat, stat,
            stat, stat, stat,
            stat,
            pltpu.VMEM((slots, 2, L, L + d), BF16),
            pltpu.VMEM((slots, 2, L, d), F32),
            pltpu.VMEM((slots, 2, L, L + d), BF16),
            pltpu.VMEM((slots, 2, L, d), F32),
        ],
        compiler_params=pltpu.CompilerParams(dimension_semantics=("arbitrary", "arbitrary"),
                                             vmem_limit_bytes=VMEM_LIMIT["mlstm"]),
        name="mlstm",
    )(qk, qk, v, o, gates, conv_w, conv_b, norm_g, tri3, eye2)


def _natten_body(q_ref, k_ref, v_ref, bias_ref, out_ref, pa_ref, pb_ref, *, rows):
    hd = NA_HEAD_DIM
    win = WIN_H * GRID_W
    lane = lax.broadcasted_iota(jnp.int32, (GRID_W, 2 * hd), 1)
    first = lane < hd
    ones_blk = jnp.ones((win, 2 * hd), BF16)

    def window_start(r):
        return jnp.clip(r - WIN_H // 2, 0, rows - WIN_H)

    def prob_stage(r, p_ref, slot):
        rs = window_start(r)
        q = q_ref[pl.ds(pl.multiple_of(r * GRID_W, GRID_W), GRID_W), :]
        zero = jnp.zeros_like(q)
        qs = jnp.concatenate([jnp.where(first, q, zero), jnp.where(first, zero, q)], axis=0)
        kwin = k_ref[pl.ds(pl.multiple_of(rs * GRID_W, GRID_W), win), :]
        off = rs - r + (WIN_H - 1)
        bias = bias_ref[off & 1, :, pl.ds(pl.multiple_of((off >> 1) * LANES, LANES), win)]
        s = lax.dot_general(qs, kwin, NT_DIMS, preferred_element_type=F32) + bias
        p_ref[slot] = jnp.exp(s - jnp.max(s, axis=1, keepdims=True)).astype(BF16)

    def output_stage(r, p_ref, slot):
        rs = window_start(r)
        vwin = v_ref[pl.ds(pl.multiple_of(rs * GRID_W, GRID_W), win), :]
        o = jnp.dot(p_ref[slot], jnp.concatenate([vwin, ones_blk], axis=1), preferred_element_type=F32)
        o = o[:, :2 * hd] / o[:, 2 * hd:]
        out = jnp.where(first, o[:GRID_W], o[GRID_W:])
        out_ref[pl.ds(pl.multiple_of(r * GRID_W, GRID_W), GRID_W), :] = out.astype(out_ref.dtype)

    R = NA_ROWS_PER_STEP
    n_groups = rows // R

    def produce(g, p_ref):
        for i in range(R):
            prob_stage(g * R + i, p_ref, i)

    def consume(g, p_ref):
        for i in range(R):
            output_stage(g * R + i, p_ref, i)

    produce(0, pa_ref)

    def group_pair(j, carry):
        g = 2 * j
        produce(g + 1, pb_ref)
        consume(g, pa_ref)
        produce(g + 2, pa_ref)
        consume(g + 1, pb_ref)
        return carry

    lax.fori_loop(0, n_groups // 2 - 1, group_pair, 0)
    produce(n_groups - 1, pb_ref)
    consume(n_groups - 2, pa_ref)
    consume(n_groups - 1, pb_ref)


def _natten(nq, nk, nv, bias_tab):
    b, seq, d_n = nq.shape
    pairs = N_NA_HEADS // 2
    width = 2 * NA_HEAD_DIM
    rows = seq // GRID_W
    assert rows % (2 * NA_ROWS_PER_STEP) == 0 and rows >= 4 * NA_ROWS_PER_STEP
    col = pl.BlockSpec((None, seq, width), lambda bi, pi: (bi, 0, pi))
    return pl.pallas_call(
        functools.partial(_natten_body, rows=rows),
        grid=(b, pairs),
        in_specs=[col, col, col,
                  pl.BlockSpec((None, 2, 2 * GRID_W, NA_BIAS_LANES), lambda bi, pi: (pi, 0, 0, 0))],
        out_specs=col,
        out_shape=jax.ShapeDtypeStruct((b, seq, d_n), BF16),
        scratch_shapes=[pltpu.VMEM((NA_ROWS_PER_STEP, 2 * GRID_W, WIN_H * GRID_W), BF16)] * 2,
        compiler_params=pltpu.CompilerParams(dimension_semantics=("arbitrary", "arbitrary"),
                                             vmem_limit_bytes=VMEM_LIMIT["natten"]),
        name="natten",
    )(nq, nk, nv, bias_tab)


def _natten_bias_table(rpb):
    c = np.arange(GRID_W)
    cs = np.clip(c - WIN_W // 2, 0, GRID_W - WIN_W)
    cp = np.arange(GRID_W)
    valid = (cp[None, :] >= cs[:, None]) & (cp[None, :] < cs[:, None] + WIN_W)
    rel = cp[None, None, :] - c[None, :, None] + (WIN_W - 1)
    onehot = (rel == np.arange(2 * WIN_W - 1)[:, None, None]).astype(np.float32)
    n_rel = 2 * WIN_H - 1
    tab = jnp.einsum('phrd,dcq->phcrq', rpb.astype(F32).reshape(N_NA_HEADS // 2, 2, n_rel, 2 * WIN_W - 1),
                     jnp.asarray(onehot), precision=lax.Precision.HIGHEST)
    tab = jnp.where(jnp.asarray(valid)[None, None, :, None, :], tab, NEG_BIG)
    tab = tab.reshape(N_NA_HEADS // 2, 2 * GRID_W, n_rel * GRID_W)
    even = tab[:, :, :NA_BIAS_LANES]
    odd = tab[:, :, GRID_W:GRID_W + NA_BIAS_LANES]
    return jnp.stack([even, odd], axis=1)


def _tail_body(x_ref, ya_ref, yb_ref, p_ref, woa_ref, wob_ref, g2_ref, w1_ref, w2_ref,
               g3_ref, wg_ref, wu_ref, out_ref, *, ff_chunk):
    d_ff = w1_ref.shape[1]
    h = (x_ref[...]
         + jnp.dot(ya_ref[...], woa_ref[...], preferred_element_type=F32)
         + jnp.dot(yb_ref[...], wob_ref[...], preferred_element_type=F32))
    u = (h * _rms_scale(h) * g2_ref[...]).astype(BF16)
    out_ref[...] = h
    for j in range(d_ff // ff_chunk):
        z = jnp.dot(u, w1_ref[:, j * ff_chunk:(j + 1) * ff_chunk], preferred_element_type=F32)
        z = jnp.maximum(z, 0.0)
        out_ref[...] += jnp.dot((z * z).astype(BF16), w2_ref[j * ff_chunk:(j + 1) * ff_chunk, :],
                                preferred_element_type=F32)
    h = out_ref[...]
    u = (h * _rms_scale(h) * g3_ref[...]).astype(BF16)
    gate = _sigmoid(jnp.dot(u, wg_ref[...], preferred_element_type=F32))
    up = jnp.dot(p_ref[...].astype(BF16), wu_ref[...], preferred_element_type=F32)
    out_ref[...] = h + gate * up


def _tail(x2, ya, yb, p2, wo, g2, w1, w2, g3, wg, wu, *, tm, ff_chunk):
    t, d = x2.shape
    d_mix = wo.shape[0]
    assert ya.shape[1] == yb.shape[1] == d_mix // 2 and t % tm == 0
    row = lambda width: pl.BlockSpec((tm, width), lambda i: (i, 0))
    half = lambda k: pl.BlockSpec((d_mix // 2, d), lambda i: (k, 0), pipeline_mode=pl.Buffered(1))
    consts = [g2, w1, w2, g3, wg, wu]
    return pl.pallas_call(
        functools.partial(_tail_body, ff_chunk=ff_chunk),
        grid=(t // tm,),
        in_specs=[row(d), row(ya.shape[1]), row(yb.shape[1]), row(p2.shape[1]), half(0), half(1)]
                 + [_const_spec(c.shape) for c in consts],
        out_specs=row(d),
        out_shape=jax.ShapeDtypeStruct((t, d), F32),
        compiler_params=pltpu.CompilerParams(dimension_semantics=("arbitrary",),
                                             vmem_limit_bytes=VMEM_LIMIT["tail"]),
        name="tail",
    )(x2, ya, yb, p2, wo, wo, *consts)


def kernel(x, p, norm1_g, w_in, conv_w, conv_b, gate_b, mlstm_norm_g, q_norm_g, k_norm_g, rpb,
           w_out, norm2_g, w_ff1, w_ff2, ple_norm_g, w_ple_gate, w_ple_up):
    b, seq, d = x.shape
    depth = w_in.shape[0]
    H, hd, L = N_MLSTM_HEADS, MLSTM_HEAD_DIM, CHUNK
    d_m = H * hd
    d_n = N_NA_HEADS * NA_HEAD_DIM
    t = b * seq
    rows = seq // GRID_W
    assert rows >= WIN_H

    bd = jnp.asarray(np.kron(np.eye(N_NA_HEADS), np.ones((NA_HEAD_DIM, NA_HEAD_DIM))), BF16)
    ri, ci = np.indices((L, L))
    tri = np.stack([ci <= ri, ci >= ri])
    tri3 = jnp.asarray(np.concatenate([tri, tri, tri], axis=1), BF16)
    eye2 = jnp.asarray(np.concatenate([ci == ri, ci == ri], axis=1), BF16)

    h = x.reshape(t, d)
    for i in range(depth):
        wi = w_in[i].astype(BF16)
        w_n = wi[:, 4 * d_m + N_GATE:]
        qg = (q_norm_g[i].reshape(1, d_n) * (NA_HEAD_DIM ** -0.5)).astype(F32)
        kg = k_norm_g[i].reshape(1, d_n).astype(F32)
        qk, mv, mo, nq, nk, nv, gates = _inproj(
            h, norm1_g[i].reshape(1, d), wi, w_n, gate_b[i].reshape(N_GATE, 1), bd, qg, kg,
            d_m=d_m, d_n=d_n, tm=INPROJ_TM)

        cw = conv_w[i].reshape(3, 2, H, hd).transpose(2, 1, 0, 3)
        cb = conv_b[i].reshape(2, H, 1, hd).transpose(1, 0, 2, 3)
        y_a = _mlstm(qk.reshape(b, seq, 2 * d_m), mv.reshape(b, seq, d_m), mo.reshape(b, seq, d_m),
                     gates.reshape(4, H, b, seq // L, L), cw, cb,
                     mlstm_norm_g[i].reshape(H, 1, hd), tri3, eye2)

        bias_tab = _natten_bias_table(rpb[i])
        y_b = _natten(nq.reshape(b, seq, d_n), nk.reshape(b, seq, d_n), nv.reshape(b, seq, d_n), bias_tab)

        h = _tail(h, y_a.reshape(t, d_m), y_b.reshape(t, d_n), p[i].reshape(t, -1),
                  w_out[i].astype(BF16), norm2_g[i].reshape(1, d), w_ff1[i].astype(BF16), w_ff2[i].astype(BF16),
                  ple_norm_g[i].reshape(1, d), w_ple_gate[i].astype(BF16), w_ple_up[i].astype(BF16),
                  tm=TAIL_TM, ff_chunk=TAIL_FF_CHUNK)
    return h.reshape(b, seq, d)
```

```python
import functools

import jax
import jax.numpy as jnp
import numpy as np
from jax import lax
from jax.experimental import pallas as pl
from jax.experimental.pallas import tpu as pltpu

F32 = jnp.float32
BF16 = jnp.bfloat16

N_MLSTM_HEADS = 4
MLSTM_HEAD_DIM = 128
N_NA_HEADS = 8
NA_HEAD_DIM = 64
GRID_W = 64
WIN_H = 8
WIN_W = 16
CHUNK = 128
N_GATE = 4 * N_MLSTM_HEADS
RMS_EPS = 1e-6
NEG_BIG = -1e30
F32_BIG = 3e38
NA_BIAS_LANES = (2 * WIN_H - 2) * GRID_W
NA_ROWS_PER_STEP = 16
MLSTM_CHUNKS_PER_STEP = 8
MLSTM_SCAN_UNROLL = 16

LANES = 128
BF16_ROWS = 16
GATE_PAD = LANES
HALO = 8
MIB = 1024 * 1024

INPROJ_TM = 1024
TAIL_TM = 1024
TAIL_FF_CHUNK = 1024
VMEM_LIMIT = {"inproj": 52 * MIB, "mlstm": 52 * MIB, "natten": 40 * MIB, "tail": 52 * MIB}

NT_DIMS = (((1,), (1,)), ((), ()))


def _const_spec(shape):
    return pl.BlockSpec(shape, lambda *_: (0,) * len(shape), pipeline_mode=pl.Buffered(1))


def _rms_scale(x):
    return lax.rsqrt(jnp.mean(x * x, axis=-1, keepdims=True) + RMS_EPS)


def _sigmoid(x):
    return 1.0 / (1.0 + jnp.exp(-x))


def _split3(x):
    hi = x.astype(BF16)
    r1 = x - hi.astype(F32)
    mid = r1.astype(BF16)
    lo = (r1 - mid.astype(F32)).astype(BF16)
    return hi, mid, lo


def _inproj_body(x_ref, xp_ref, xn_ref, g_ref, wm_ref, wn_ref, wg_ref, gb_ref, bd_ref, qg_ref, kg_ref,
                 cw_ref, cb_ref, q_ref, kt_ref, v_ref, o_ref, nq_ref, nk_ref, nv_ref, gate_ref,
                 *, d_m, d_n, steps_per_seq):
    tm = x_ref.shape[0]
    step = pl.program_id(0) % steps_per_seq

    def normed(x):
        return (x * _rms_scale(x) * g_ref[...]).astype(BF16)

    u = normed(x_ref[...])

    def proj(w_ref, lo, hi):
        return jnp.dot(u, w_ref[:, lo:hi], preferred_element_type=F32)

    def head_norm(y, gain_ref):
        ss = jnp.dot((y * y).astype(BF16), bd_ref[...], preferred_element_type=F32)
        return y * lax.rsqrt(ss * (1.0 / NA_HEAD_DIM) + RMS_EPS) * gain_ref[...]

    qk = proj(wm_ref, 0, 2 * d_m)
    u_halo = normed(jnp.concatenate([xp_ref[...], xn_ref[...]], axis=0))
    qk_halo = jnp.dot(u_halo, wm_ref[:, 0:2 * d_m], preferred_element_type=F32)
    prev_row = jnp.where(step > 0, qk_halo[HALO - 1:HALO, :], 0.0)
    next_row = jnp.where(step < steps_per_seq - 1, qk_halo[HALO:HALO + 1, :], 0.0)
    row_id = lax.broadcasted_iota(jnp.int32, qk.shape, 0)
    qk_prev = jnp.where(row_id == 0, prev_row, pltpu.roll(qk, 1, 0))
    qk_next = jnp.where(row_id == tm - 1, next_row, pltpu.roll(qk, tm - 1, 0))
    y = cw_ref[0:1, :] * qk_prev + cw_ref[1:2, :] * qk + cw_ref[2:3, :] * qk_next + cb_ref[...]
    y = y * _sigmoid(y)
    q_ref[...] = y[:, :d_m].astype(BF16)
    kt_ref[...] = (y[:, d_m:] * (MLSTM_HEAD_DIM ** -0.5)).T.astype(BF16)
    v_ref[...] = proj(wm_ref, 2 * d_m, 3 * d_m).astype(BF16)
    o_ref[...] = proj(wm_ref, 3 * d_m, 4 * d_m).astype(BF16)
    nq_ref[...] = head_norm(proj(wn_ref, 0, d_n), qg_ref).astype(BF16)
    nk_ref[...] = head_norm(proj(wn_ref, d_n, 2 * d_n), kg_ref).astype(BF16)
    nv_ref[...] = proj(wn_ref, 2 * d_n, 3 * d_n).astype(BF16)
    wg_t = wg_ref[...].astype(F32).T[:N_GATE, :].astype(BF16)
    gate_ref[...] = lax.dot_general(wg_t, u, NT_DIMS, preferred_element_type=F32) + gb_ref[...]


def _inproj(x2, g, w_all, w_n, gate_b, bd, qg, kg, conv_w, conv_b, *, d_m, d_n, tm, seq):
    t, d = x2.shape
    assert (4 * d_m) % GATE_PAD == 0 and seq % tm == 0 and tm % HALO == 0
    halo_blocks = tm // HALO
    prev_halo = pl.BlockSpec((HALO, d), lambda i: (jnp.maximum(i * halo_blocks - 1, 0), 0))
    next_halo = pl.BlockSpec((HALO, d), lambda i: (jnp.minimum((i + 1) * halo_blocks, t // HALO - 1), 0))
    fixed = functools.partial(pl.BlockSpec, pipeline_mode=pl.Buffered(1))
    row = lambda width: pl.BlockSpec((tm, width), lambda i: (i, 0))
    out_shapes = (
        jax.ShapeDtypeStruct((t, d_m), BF16),
        jax.ShapeDtypeStruct((d_m, t), BF16),
        jax.ShapeDtypeStruct((t, d_m), BF16),
        jax.ShapeDtypeStruct((t, d_m), BF16),
        jax.ShapeDtypeStruct((t, d_n), BF16),
        jax.ShapeDtypeStruct((t, d_n), BF16),
        jax.ShapeDtypeStruct((t, d_n), BF16),
        jax.ShapeDtypeStruct((N_GATE, t), F32),
    )
    return pl.pallas_call(
        functools.partial(_inproj_body, d_m=d_m, d_n=d_n, steps_per_seq=seq // tm),
        grid=(t // tm,),
        in_specs=[row(d), prev_halo, next_halo, _const_spec((1, d)), fixed((d, 4 * d_m), lambda i: (0, 0)),
                  _const_spec(w_n.shape),
                  fixed((d, GATE_PAD), lambda i: (0, 4 * d_m // GATE_PAD)), _const_spec((N_GATE, 1)),
                  _const_spec((d_n, d_n)), _const_spec((1, d_n)), _const_spec((1, d_n)),
                  _const_spec(conv_w.shape), _const_spec(conv_b.shape)],
        out_specs=(row(d_m), pl.BlockSpec((d_m, tm), lambda i: (0, i)), row(d_m), row(d_m), row(d_n), row(d_n), row(d_n),
                   pl.BlockSpec((N_GATE, tm), lambda i: (0, i))),
        out_shape=out_shapes,
        compiler_params=pltpu.CompilerParams(dimension_semantics=("arbitrary",),
                                             vmem_limit_bytes=VMEM_LIMIT["inproj"]),
        name="inproj",
    )(x2, x2, x2, g, w_all, w_n, w_all, gate_b, bd, qg, kg, conv_w, conv_b)


def _log_sigmoid(x):
    return jnp.minimum(x, 0.0) - jnp.log1p(jnp.exp(-jnp.abs(x)))


def _mlstm_body(qs_ref, kt_ref, v_ref, o_ref, gate_ref, ng_ref, tri3_ref, eye2_ref,
                out_ref, cs_ref, ccur_ref, brow_ref, crow_ref, cmax_ref, arow_ref, bl_ref,
                ml_ref, ms_ref, sa_ref, ea_ref, sb_ref, eb_ref, *, seq):
    L = CHUNK
    d = MLSTM_HEAD_DIM
    nc = seq // L

    row_id = lax.broadcasted_iota(jnp.int32, (L, d), 0)
    col_id = lax.broadcasted_iota(jnp.int32, (L, d), 1)
    pos_id = lax.broadcasted_iota(jnp.int32, (nc, L), 1)
    ones_blk = jnp.ones((L, d), BF16)

    for dirn in (0, 1):
        i_g = gate_ref[2 * dirn]
        f_log = _log_sigmoid(gate_ref[2 * dirn + 1])
        f_cat = jnp.concatenate(_split3(f_log), axis=1)
        brow = jnp.dot(f_cat, tri3_ref[1 - dirn], preferred_element_type=F32)
        b_last = brow[:, L - 1:L] if dirn == 0 else brow[:, 0:1]
        a_row = i_g + b_last - brow
        a_max = jnp.max(a_row, axis=1, keepdims=True)
        crow = i_g - brow
        cmax = crow
        for sh in [1 << e for e in range(L.bit_length() - 1)]:
            if dirn == 0:
                cmax = jnp.maximum(cmax, jnp.where(pos_id >= sh, pltpu.roll(cmax, sh, 1), -jnp.inf))
            else:
                cmax = jnp.maximum(cmax, jnp.where(pos_id < L - sh, pltpu.roll(cmax, L - sh, 1), -jnp.inf))
        brow_ref[dirn] = brow
        crow_ref[dirn] = crow
        cmax_ref[dirn] = cmax
        arow_ref[dirn] = a_row
        bl_ref[dirn] = jnp.broadcast_to(b_last, (nc, L))
        ml_ref[dirn] = jnp.broadcast_to(a_max, (nc, L))

    def v_aug(s0):
        return jnp.concatenate([v_ref[pl.ds(s0, L), :], ones_blk], axis=1)

    ccur_ref[...] = jnp.zeros_like(ccur_ref)

    def scan_step(i, carry):
        new = []
        for dirn, c, m in ((0, i, carry[0]), (1, nc - 1 - i, carry[1])):
            s0 = pl.multiple_of(c * L, L)
            state = ccur_ref[dirn]
            cs_ref[dirn, c] = state.astype(BF16)
            ms_ref[dirn, pl.ds(c, 1), :] = m
            a_prev = m + bl_ref[dirn, pl.ds(c, 1), :]
            m_new = jnp.maximum(a_prev, ml_ref[dirn, pl.ds(c, 1), :])
            w_row = jnp.exp(arow_ref[dirn, pl.ds(c, 1), :] - m_new)
            kw = (kt_ref[:, pl.ds(s0, L)].astype(F32) * w_row).astype(BF16)
            k_loc = jnp.dot(kw, v_aug(s0), preferred_element_type=F32)
            ccur_ref[dirn] = jnp.exp(a_prev - m_new)[:, 0:1] * state + k_loc
            new.append(m_new)
        return tuple(new)

    m0 = jnp.zeros((1, L), F32)
    lax.fori_loop(0, nc, scan_step, (m0, m0), unroll=MLSTM_SCAN_UNROLL)

    for dirn in (0, 1):
        mu_all = jnp.maximum(ms_ref[dirn], cmax_ref[dirn])
        cmax_ref[dirn] = mu_all
        brow_ref[dirn] = jnp.minimum(jnp.exp(-(brow_ref[dirn] + mu_all)), F32_BIG)
    lower = col_id <= row_id
    upper = col_id >= row_id

    def weights_stage(c, s_ref, einv_ref, slot):
        s0 = pl.multiple_of(c * L, L)
        q = qs_ref[pl.ds(s0, L), :]
        qk = jnp.dot(q, kt_ref[:, pl.ds(s0, L)], preferred_element_type=F32)
        for dirn, mask in ((0, lower), (1, upper)):
            rows = []
            for stat_ref in (cmax_ref, brow_ref):
                hi, mid, _ = _split3(stat_ref[dirn, pl.ds(c, 1), :])
                rows.append(jnp.broadcast_to(jnp.concatenate([hi, mid], axis=1), (L, 2 * L)))
            col = lax.dot_general(eye2_ref[...], jnp.concatenate(rows, axis=0), NT_DIMS,
                                  preferred_element_type=F32)
            mu = col[:, :L]
            p = jnp.exp(jnp.where(mask, crow_ref[dirn, pl.ds(c, 1), :] - mu, -jnp.inf))
            q_inter = q.astype(F32) * jnp.exp(ms_ref[dirn, pl.ds(c, 1), :] - mu)
            s_ref[slot, dirn] = jnp.concatenate([qk * p, q_inter], axis=1).astype(BF16)
            einv_ref[slot, dirn] = col[:, L:]

    def output_stage(c, s_ref, einv_ref, slot):
        s0 = pl.multiple_of(c * L, L)
        vaug = v_aug(s0)
        hsum = None
        for dirn in (0, 1):
            nd = jnp.dot(s_ref[slot, dirn], jnp.concatenate([vaug, cs_ref[dirn, c]], axis=0),
                         preferred_element_type=F32)
            h = nd[:, :d] / jnp.maximum(jnp.abs(nd[:, d:]), einv_ref[slot, dirn])
            hsum = h if hsum is None else hsum + h
        y = hsum * _rms_scale(hsum) * ng_ref[...]
        y = y * _sigmoid(o_ref[pl.ds(s0, L), :].astype(F32))
        out_ref[pl.ds(s0, L), :] = y.astype(out_ref.dtype)

    G = MLSTM_CHUNKS_PER_STEP
    n_groups = nc // G

    def produce(g, s_ref, einv_ref):
        for i in range(G):
            weights_stage(g * G + i, s_ref, einv_ref, i)

    def consume(g, s_ref, einv_ref):
        for i in range(G):
            output_stage(g * G + i, s_ref, einv_ref, i)

    produce(0, sa_ref, ea_ref)

    def out_step(j, carry):
        g = 2 * j
        produce(g + 1, sb_ref, eb_ref)
        consume(g, sa_ref, ea_ref)
        produce(g + 2, sa_ref, ea_ref)
        consume(g + 1, sb_ref, eb_ref)
        return carry

    lax.fori_loop(0, n_groups // 2 - 1, out_step, 0)
    produce(n_groups - 1, sb_ref, eb_ref)
    consume(n_groups - 2, sa_ref, ea_ref)
    consume(n_groups - 1, sb_ref, eb_ref)


def _mlstm(q, kt, v, o, gates, norm_g, tri3, eye2):
    b, seq, _ = v.shape
    H, d, L = N_MLSTM_HEADS, MLSTM_HEAD_DIM, CHUNK
    nc = seq // L
    assert nc % MLSTM_SCAN_UNROLL == 0 and L == LANES
    assert nc % (2 * MLSTM_CHUNKS_PER_STEP) == 0 and nc >= 4 * MLSTM_CHUNKS_PER_STEP
    col = pl.BlockSpec((None, seq, d), lambda bi, hi: (bi, 0, hi))
    stat = pltpu.VMEM((2, nc, L), F32)
    slots = MLSTM_CHUNKS_PER_STEP
    return pl.pallas_call(
        functools.partial(_mlstm_body, seq=seq),
        grid=(b, H),
        in_specs=[
            col, pl.BlockSpec((d, seq), lambda bi, hi: (hi, bi)), col, col,
            pl.BlockSpec((4, None, None, nc, L), lambda bi, hi: (0, hi, bi, 0, 0)),
            pl.BlockSpec((None, 1, d), lambda bi, hi: (hi, 0, 0)),
            _const_spec(tri3.shape), _const_spec(eye2.shape),
        ],
        out_specs=col,
        out_shape=jax.ShapeDtypeStruct((b, seq, H * d), BF16),
        scratch_shapes=[
            pltpu.VMEM((2, nc, d, 2 * d), BF16),
            pltpu.VMEM((2, d, 2 * d), F32),
            stat, stat, stat,
            stat, stat, stat,
            stat,
            pltpu.VMEM((slots, 2, L, L + d), BF16),
            pltpu.VMEM((slots, 2, L, d), F32),
            pltpu.VMEM((slots, 2, L, L + d), BF16),
            pltpu.VMEM((slots, 2, L, d), F32),
        ],
        compiler_params=pltpu.CompilerParams(dimension_semantics=("arbitrary", "arbitrary"),
                                             vmem_limit_bytes=VMEM_LIMIT["mlstm"]),
        name="mlstm",
    )(q, kt, v, o, gates, norm_g, tri3, eye2)


def _natten_body(q_ref, k_ref, v_ref, bias_ref, out_ref, pa_ref, pb_ref, *, rows):
    hd = NA_HEAD_DIM
    win = WIN_H * GRID_W
    lane = lax.broadcasted_iota(jnp.int32, (GRID_W, 2 * hd), 1)
    first = lane < hd
    ones_blk = jnp.ones((win, 2 * hd), BF16)

    def window_start(r):
        return jnp.clip(r - WIN_H // 2, 0, rows - WIN_H)

    def prob_stage(r, p_ref, slot):
        rs = window_start(r)
        q = q_ref[pl.ds(pl.multiple_of(r * GRID_W, GRID_W), GRID_W), :]
        zero = jnp.zeros_like(q)
        qs = jnp.concatenate([jnp.where(first, q, zero), jnp.where(first, zero, q)], axis=0)
        kwin = k_ref[pl.ds(pl.multiple_of(rs * GRID_W, GRID_W), win), :]
        off = rs - r + (WIN_H - 1)
        bias = bias_ref[off & 1, :, pl.ds(pl.multiple_of((off >> 1) * LANES, LANES), win)]
        s = lax.dot_general(qs, kwin, NT_DIMS, preferred_element_type=F32) + bias
        p_ref[slot] = jnp.exp(s - jnp.max(s, axis=1, keepdims=True)).astype(BF16)

    def output_stage(r, p_ref, slot):
        rs = window_start(r)
        vwin = v_ref[pl.ds(pl.multiple_of(rs * GRID_W, GRID_W), win), :]
        o = jnp.dot(p_ref[slot], jnp.concatenate([vwin, ones_blk], axis=1), preferred_element_type=F32)
        o = o[:, :2 * hd] / o[:, 2 * hd:]
        out = jnp.where(first, o[:GRID_W], o[GRID_W:])
        out_ref[pl.ds(pl.multiple_of(r * GRID_W, GRID_W), GRID_W), :] = out.astype(out_ref.dtype)

    R = NA_ROWS_PER_STEP
    n_groups = rows // R

    def produce(g, p_ref):
        for i in range(R):
            prob_stage(g * R + i, p_ref, i)

    def consume(g, p_ref):
        for i in range(R):
            output_stage(g * R + i, p_ref, i)

    produce(0, pa_ref)

    def group_pair(j, carry):
        g = 2 * j
        produce(g + 1, pb_ref)
        consume(g, pa_ref)
        produce(g + 2, pa_ref)
        consume(g + 1, pb_ref)
        return carry

    lax.fori_loop(0, n_groups // 2 - 1, group_pair, 0)
    produce(n_groups - 1, pb_ref)
    consume(n_groups - 2, pa_ref)
    consume(n_groups - 1, pb_ref)


def _natten(nq, nk, nv, bias_tab):
    b, seq, d_n = nq.shape
    pairs = N_NA_HEADS // 2
    width = 2 * NA_HEAD_DIM
    rows = seq // GRID_W
    assert rows % (2 * NA_ROWS_PER_STEP) == 0 and rows >= 4 * NA_ROWS_PER_STEP
    col = pl.BlockSpec((None, seq, width), lambda bi, pi: (bi, 0, pi))
    return pl.pallas_call(
        functools.partial(_natten_body, rows=rows),
        grid=(b, pairs),
        in_specs=[col, col, col,
                  pl.BlockSpec((None, 2, 2 * GRID_W, NA_BIAS_LANES), lambda bi, pi: (pi, 0, 0, 0))],
        out_specs=col,
        out_shape=jax.ShapeDtypeStruct((b, seq, d_n), BF16),
        scratch_shapes=[pltpu.VMEM((NA_ROWS_PER_STEP, 2 * GRID_W, WIN_H * GRID_W), BF16)] * 2,
        compiler_params=pltpu.CompilerParams(dimension_semantics=("arbitrary", "arbitrary"),
                                             vmem_limit_bytes=VMEM_LIMIT["natten"]),
        name="natten",
    )(nq, nk, nv, bias_tab)


def _natten_bias_table(rpb):
    c = np.arange(GRID_W)
    cs = np.clip(c - WIN_W // 2, 0, GRID_W - WIN_W)
    cp = np.arange(GRID_W)
    valid = (cp[None, :] >= cs[:, None]) & (cp[None, :] < cs[:, None] + WIN_W)
    rel = cp[None, None, :] - c[None, :, None] + (WIN_W - 1)
    onehot = (rel == np.arange(2 * WIN_W - 1)[:, None, None]).astype(np.float32)
    n_rel = 2 * WIN_H - 1
    tab = jnp.einsum('phrd,dcq->phcrq', rpb.astype(F32).reshape(N_NA_HEADS // 2, 2, n_rel, 2 * WIN_W - 1),
                     jnp.asarray(onehot), precision=lax.Precision.HIGHEST)
    tab = jnp.where(jnp.asarray(valid)[None, None, :, None, :], tab, NEG_BIG)
    tab = tab.reshape(N_NA_HEADS // 2, 2 * GRID_W, n_rel * GRID_W)
    even = tab[:, :, :NA_BIAS_LANES]
    odd = tab[:, :, GRID_W:GRID_W + NA_BIAS_LANES]
    return jnp.stack([even, odd], axis=1)


def _tail_body(x_ref, ya_ref, yb_ref, p_ref, woa_ref, wob_ref, g2_ref, w1_ref, w2_ref,
               g3_ref, wg_ref, wu_ref, out_ref, *, ff_chunk):
    d_ff = w1_ref.shape[1]
    h = (x_ref[...]
         + jnp.dot(ya_ref[...], woa_ref[...], preferred_element_type=F32)
         + jnp.dot(yb_ref[...], wob_ref[...], preferred_element_type=F32))
    u = (h * _rms_scale(h) * g2_ref[...]).astype(BF16)
    out_ref[...] = h
    for j in range(d_ff // ff_chunk):
        z = jnp.dot(u, w1_ref[:, j * ff_chunk:(j + 1) * ff_chunk], preferred_element_type=F32)
        z = jnp.maximum(z, 0.0)
        out_ref[...] += jnp.dot((z * z).astype(BF16), w2_ref[j * ff_chunk:(j + 1) * ff_chunk, :],
                                preferred_element_type=F32)
    h = out_ref[...]
    u = (h * _rms_scale(h) * g3_ref[...]).astype(BF16)
    gate = _sigmoid(jnp.dot(u, wg_ref[...], preferred_element_type=F32))
    up = jnp.dot(p_ref[...].astype(BF16), wu_ref[...], preferred_element_type=F32)
    out_ref[...] = h + gate * up


def _tail(x2, ya, yb, p2, wo, g2, w1, w2, g3, wg, wu, *, tm, ff_chunk):
    t, d = x2.shape
    d_mix = wo.shape[0]
    assert ya.shape[1] == yb.shape[1] == d_mix // 2 and t % tm == 0
    row = lambda width: pl.BlockSpec((tm, width), lambda i: (i, 0))
    half = lambda k: pl.BlockSpec((d_mix // 2, d), lambda i: (k, 0), pipeline_mode=pl.Buffered(1))
    consts = [g2, w1, w2, g3, wg, wu]
    return pl.pallas_call(
        functools.partial(_tail_body, ff_chunk=ff_chunk),
        grid=(t // tm,),
        in_specs=[row(d), row(ya.shape[1]), row(yb.shape[1]), row(p2.shape[1]), half(0), half(1)]
                 + [_const_spec(c.shape) for c in consts],
        out_specs=row(d),
        out_shape=jax.ShapeDtypeStruct((t, d), F32),
        compiler_params=pltpu.CompilerParams(dimension_semantics=("arbitrary",),
                                             vmem_limit_bytes=VMEM_LIMIT["tail"]),
        name="tail",
    )(x2, ya, yb, p2, wo, wo, *consts)


def kernel(x, p, norm1_g, w_in, conv_w, conv_b, gate_b, mlstm_norm_g, q_norm_g, k_norm_g, rpb,
           w_out, norm2_g, w_ff1, w_ff2, ple_norm_g, w_ple_gate, w_ple_up):
    b, seq, d = x.shape
    depth = w_in.shape[0]
    H, hd, L = N_MLSTM_HEADS, MLSTM_HEAD_DIM, CHUNK
    d_m = H * hd
    d_n = N_NA_HEADS * NA_HEAD_DIM
    t = b * seq
    rows = seq // GRID_W
    assert rows >= WIN_H

    bd = jnp.asarray(np.kron(np.eye(N_NA_HEADS), np.ones((NA_HEAD_DIM, NA_HEAD_DIM))), BF16)
    ri, ci = np.indices((L, L))
    tri = np.stack([ci <= ri, ci >= ri])
    tri3 = jnp.asarray(np.concatenate([tri, tri, tri], axis=1), BF16)
    eye2 = jnp.asarray(np.concatenate([ci == ri, ci == ri], axis=1), BF16)

    h = x.reshape(t, d)
    for i in range(depth):
        wi = w_in[i].astype(BF16)
        w_n = wi[:, 4 * d_m + N_GATE:]
        qg = (q_norm_g[i].reshape(1, d_n) * (NA_HEAD_DIM ** -0.5)).astype(F32)
        kg = k_norm_g[i].reshape(1, d_n).astype(F32)
        mq, mkt, mv, mo, nq, nk, nv, gates = _inproj(
            h, norm1_g[i].reshape(1, d), wi, w_n, gate_b[i].reshape(N_GATE, 1), bd, qg, kg,
            conv_w[i].astype(F32), conv_b[i].reshape(1, 2 * d_m).astype(F32),
            d_m=d_m, d_n=d_n, tm=INPROJ_TM, seq=seq)

        y_a = _mlstm(mq.reshape(b, seq, d_m), mkt, mv.reshape(b, seq, d_m), mo.reshape(b, seq, d_m),
                     gates.reshape(4, H, b, seq // L, L), mlstm_norm_g[i].reshape(H, 1, hd), tri3, eye2)

        bias_tab = _natten_bias_table(rpb[i])
        y_b = _natten(nq.reshape(b, seq, d_n), nk.reshape(b, seq, d_n), nv.reshape(b, seq, d_n), bias_tab)

        h = _tail(h, y_a.reshape(t, d_m), y_b.reshape(t, d_n), p[i].reshape(t, -1),
                  w_out[i].astype(BF16), norm2_g[i].reshape(1, d), w_ff1[i].astype(BF16), w_ff2[i].astype(BF16),
                  ple_norm_g[i].reshape(1, d), w_ple_gate[i].astype(BF16), w_ple_up[i].astype(BF16),
                  tm=TAIL_TM, ff_chunk=TAIL_FF_CHUNK)
    return h.reshape(b, seq, d)
```

```python
import functools

import jax
import jax.numpy as jnp
import numpy as np
from jax import lax
from jax.experimental import pallas as pl
from jax.experimental.pallas import tpu as pltpu

F32 = jnp.float32
BF16 = jnp.bfloat16

N_MLSTM_HEADS = 4
MLSTM_HEAD_DIM = 128
N_NA_HEADS = 8
NA_HEAD_DIM = 64
GRID_W = 64
WIN_H = 8
WIN_W = 16
CHUNK = 128
N_GATE = 4 * N_MLSTM_HEADS
RMS_EPS = 1e-6
NEG_BIG = -1e30
F32_BIG = 3e38
NA_BIAS_LANES = (2 * WIN_H - 2) * GRID_W
NA_ROWS_PER_STEP = 16
MLSTM_CHUNKS_PER_STEP = 8
MLSTM_CONV_UNROLL = 16
MLSTM_SCAN_UNROLL = 16

LANES = 128
BF16_ROWS = 16
GATE_PAD = LANES
MIB = 1024 * 1024

INPROJ_TM = 1024
TAIL_TM = 1024
TAIL_FF_CHUNK = 1024
VMEM_LIMIT = {"inproj": 44 * MIB, "mlstm": 52 * MIB, "natten": 40 * MIB, "tail": 52 * MIB}

NT_DIMS = (((1,), (1,)), ((), ()))


def _const_spec(shape):
    return pl.BlockSpec(shape, lambda *_: (0,) * len(shape), pipeline_mode=pl.Buffered(1))


def _rms_scale(x):
    return lax.rsqrt(jnp.mean(x * x, axis=-1, keepdims=True) + RMS_EPS)


def _split3(x):
    hi = x.astype(BF16)
    r1 = x - hi.astype(F32)
    mid = r1.astype(BF16)
    lo = (r1 - mid.astype(F32)).astype(BF16)
    return hi, mid, lo


def _inproj_body(x_ref, g_ref, wm_ref, wn_ref, wg_ref, gb_ref, bd_ref, qg_ref, kg_ref,
                 qk_ref, v_ref, o_ref, nq_ref, nk_ref, nv_ref, gate_ref, *, d_m, d_n):
    x = x_ref[...]
    u = (x * _rms_scale(x) * g_ref[...]).astype(BF16)

    def proj(w_ref, lo, hi):
        return jnp.dot(u, w_ref[:, lo:hi], preferred_element_type=F32)

    def head_norm(y, gain_ref):
        ss = jnp.dot((y * y).astype(BF16), bd_ref[...], preferred_element_type=F32)
        return y * lax.rsqrt(ss * (1.0 / NA_HEAD_DIM) + RMS_EPS) * gain_ref[...]

    qk_ref[...] = proj(wm_ref, 0, 2 * d_m).astype(BF16)
    v_ref[...] = proj(wm_ref, 2 * d_m, 3 * d_m).astype(BF16)
    o_ref[...] = proj(wm_ref, 3 * d_m, 4 * d_m).astype(BF16)
    nq_ref[...] = head_norm(proj(wn_ref, 0, d_n), qg_ref).astype(BF16)
    nk_ref[...] = head_norm(proj(wn_ref, d_n, 2 * d_n), kg_ref).astype(BF16)
    nv_ref[...] = proj(wn_ref, 2 * d_n, 3 * d_n).astype(BF16)
    wg_t = wg_ref[...].astype(F32).T[:N_GATE, :].astype(BF16)
    gate_ref[...] = lax.dot_general(wg_t, u, NT_DIMS, preferred_element_type=F32) + gb_ref[...]


def _inproj(x2, g, w_all, w_n, gate_b, bd, qg, kg, *, d_m, d_n, tm):
    t, d = x2.shape
    assert (4 * d_m) % GATE_PAD == 0
    fixed = functools.partial(pl.BlockSpec, pipeline_mode=pl.Buffered(1))
    row = lambda width: pl.BlockSpec((tm, width), lambda i: (i, 0))
    out_shapes = (
        jax.ShapeDtypeStruct((t, 2 * d_m), BF16),
        jax.ShapeDtypeStruct((t, d_m), BF16),
        jax.ShapeDtypeStruct((t, d_m), BF16),
        jax.ShapeDtypeStruct((t, d_n), BF16),
        jax.ShapeDtypeStruct((t, d_n), BF16),
        jax.ShapeDtypeStruct((t, d_n), BF16),
        jax.ShapeDtypeStruct((N_GATE, t), F32),
    )
    return pl.pallas_call(
        functools.partial(_inproj_body, d_m=d_m, d_n=d_n),
        grid=(t // tm,),
        in_specs=[row(d), _const_spec((1, d)), fixed((d, 4 * d_m), lambda i: (0, 0)), _const_spec(w_n.shape),
                  fixed((d, GATE_PAD), lambda i: (0, 4 * d_m // GATE_PAD)), _const_spec((N_GATE, 1)),
                  _const_spec((d_n, d_n)), _const_spec((1, d_n)), _const_spec((1, d_n))],
        out_specs=(row(2 * d_m), row(d_m), row(d_m), row(d_n), row(d_n), row(d_n),
                   pl.BlockSpec((N_GATE, tm), lambda i: (0, i))),
        out_shape=out_shapes,
        compiler_params=pltpu.CompilerParams(dimension_semantics=("arbitrary",),
                                             vmem_limit_bytes=VMEM_LIMIT["inproj"]),
        name="inproj",
    )(x2, g, w_all, w_n, w_all, gate_b, bd, qg, kg)


def _log_sigmoid(x):
    return jnp.minimum(x, 0.0) - jnp.log1p(jnp.exp(-jnp.abs(x)))


def _sigmoid(x):
    return 1.0 / (1.0 + jnp.exp(-x))


def _mlstm_body(q_ref, k_ref, v_ref, o_ref, gate_ref, cw_ref, cb_ref, ng_ref, tri3_ref, eye2_ref,
                out_ref, qs_ref, kt_ref, cs_ref, ccur_ref, brow_ref, crow_ref, cmax_ref, arow_ref, bl_ref,
                ml_ref, ms_ref, sa_ref, ea_ref, sb_ref, eb_ref, *, seq):
    L = CHUNK
    d = MLSTM_HEAD_DIM
    nc = seq // L

    row_id = lax.broadcasted_iota(jnp.int32, (L, d), 0)
    col_id = lax.broadcasted_iota(jnp.int32, (L, d), 1)
    pos_id = lax.broadcasted_iota(jnp.int32, (nc, L), 1)
    ones_blk = jnp.ones((L, d), BF16)

    def conv_silu(src_ref, c, s0, w, b):
        x = src_ref[pl.ds(s0, L), :].astype(F32)
        p0 = pl.multiple_of(jnp.maximum(s0 - BF16_ROWS, 0), BF16_ROWS)
        n0 = pl.multiple_of(jnp.minimum(s0 + L, seq - BF16_ROWS), BF16_ROWS)
        prev_row = src_ref[pl.ds(p0, BF16_ROWS), :][BF16_ROWS - 1:BF16_ROWS, :].astype(F32)
        next_row = src_ref[pl.ds(n0, BF16_ROWS), :][0:1, :].astype(F32)
        prev_row = jnp.where(c > 0, prev_row, 0.0)
        next_row = jnp.where(c < nc - 1, next_row, 0.0)
        x_prev = jnp.where(row_id == 0, prev_row, pltpu.roll(x, 1, 0))
        x_next = jnp.where(row_id == L - 1, next_row, pltpu.roll(x, L - 1, 0))
        y = w[0:1, :] * x_prev + w[1:2, :] * x + w[2:3, :] * x_next + b
        return y * _sigmoid(y)

    def conv_step(c, carry):
        s0 = pl.multiple_of(c * L, L)
        qs_ref[pl.ds(s0, L), :] = conv_silu(q_ref, c, s0, cw_ref[0], cb_ref[0]).astype(BF16)
        kk = conv_silu(k_ref, c, s0, cw_ref[1], cb_ref[1]) * (d ** -0.5)
        kt_ref[:, pl.ds(s0, L)] = kk.T.astype(BF16)
        return carry

    lax.fori_loop(0, nc, conv_step, 0, unroll=MLSTM_CONV_UNROLL)

    for dirn in (0, 1):
        i_g = gate_ref[2 * dirn]
        f_log = _log_sigmoid(gate_ref[2 * dirn + 1])
        f_cat = jnp.concatenate(_split3(f_log), axis=1)
        brow = jnp.dot(f_cat, tri3_ref[1 - dirn], preferred_element_type=F32)
        b_last = brow[:, L - 1:L] if dirn == 0 else brow[:, 0:1]
        a_row = i_g + b_last - brow
        a_max = jnp.max(a_row, axis=1, keepdims=True)
        crow = i_g - brow
        cmax = crow
        for sh in [1 << e for e in range(L.bit_length() - 1)]:
            if dirn == 0:
                cmax = jnp.maximum(cmax, jnp.where(pos_id >= sh, pltpu.roll(cmax, sh, 1), -jnp.inf))
            else:
                cmax = jnp.maximum(cmax, jnp.where(pos_id < L - sh, pltpu.roll(cmax, L - sh, 1), -jnp.inf))
        brow_ref[dirn] = brow
        crow_ref[dirn] = crow
        cmax_ref[dirn] = cmax
        arow_ref[dirn] = a_row
        bl_ref[dirn] = jnp.broadcast_to(b_last, (nc, L))
        ml_ref[dirn] = jnp.broadcast_to(a_max, (nc, L))

    def v_aug(s0):
        return jnp.concatenate([v_ref[pl.ds(s0, L), :], ones_blk], axis=1)

    ccur_ref[...] = jnp.zeros_like(ccur_ref)

    def scan_direction(dirn):
        def scan_step(i, m):
            c = i if dirn == 0 else nc - 1 - i
            s0 = pl.multiple_of(c * L, L)
            state = ccur_ref[dirn]
            cs_ref[dirn, c] = state.astype(BF16)
            ms_ref[dirn, pl.ds(c, 1), :] = m
            a_prev = m + bl_ref[dirn, pl.ds(c, 1), :]
            m_new = jnp.maximum(a_prev, ml_ref[dirn, pl.ds(c, 1), :])
            w_row = jnp.exp(arow_ref[dirn, pl.ds(c, 1), :] - m_new)
            kw = (kt_ref[:, pl.ds(s0, L)].astype(F32) * w_row).astype(BF16)
            k_loc = jnp.dot(kw, v_aug(s0), preferred_element_type=F32)
            ccur_ref[dirn] = jnp.exp(a_prev - m_new)[:, 0:1] * state + k_loc
            return m_new

        lax.fori_loop(0, nc, scan_step, jnp.zeros((1, L), F32), unroll=MLSTM_SCAN_UNROLL)

    for dirn in (0, 1):
        scan_direction(dirn)

    for dirn in (0, 1):
        mu_all = jnp.maximum(ms_ref[dirn], cmax_ref[dirn])
        cmax_ref[dirn] = mu_all
        brow_ref[dirn] = jnp.minimum(jnp.exp(-(brow_ref[dirn] + mu_all)), F32_BIG)
    lower = col_id <= row_id
    upper = col_id >= row_id

    def weights_stage(c, s_ref, einv_ref, slot):
        s0 = pl.multiple_of(c * L, L)
        q = qs_ref[pl.ds(s0, L), :]
        qk = jnp.dot(q, kt_ref[:, pl.ds(s0, L)], preferred_element_type=F32)
        for dirn, mask in ((0, lower), (1, upper)):
            rows = []
            for stat_ref in (cmax_ref, brow_ref):
                hi, mid, _ = _split3(stat_ref[dirn, pl.ds(c, 1), :])
                rows.append(jnp.broadcast_to(jnp.concatenate([hi, mid], axis=1), (L, 2 * L)))
            col = lax.dot_general(eye2_ref[...], jnp.concatenate(rows, axis=0), NT_DIMS,
                                  preferred_element_type=F32)
            mu = col[:, :L]
            p = jnp.exp(jnp.where(mask, crow_ref[dirn, pl.ds(c, 1), :] - mu, -jnp.inf))
            q_inter = q.astype(F32) * jnp.exp(ms_ref[dirn, pl.ds(c, 1), :] - mu)
            s_ref[slot, dirn] = jnp.concatenate([qk * p, q_inter], axis=1).astype(BF16)
            einv_ref[slot, dirn] = col[:, L:]

    def output_stage(c, s_ref, einv_ref, slot):
        s0 = pl.multiple_of(c * L, L)
        vaug = v_aug(s0)
        hsum = None
        for dirn in (0, 1):
            nd = jnp.dot(s_ref[slot, dirn], jnp.concatenate([vaug, cs_ref[dirn, c]], axis=0),
                         preferred_element_type=F32)
            h = nd[:, :d] / jnp.maximum(jnp.abs(nd[:, d:]), einv_ref[slot, dirn])
            hsum = h if hsum is None else hsum + h
        y = hsum * _rms_scale(hsum) * ng_ref[...]
        y = y * _sigmoid(o_ref[pl.ds(s0, L), :].astype(F32))
        out_ref[pl.ds(s0, L), :] = y.astype(out_ref.dtype)

    G = MLSTM_CHUNKS_PER_STEP
    n_groups = nc // G

    def produce(g, s_ref, einv_ref):
        for i in range(G):
            weights_stage(g * G + i, s_ref, einv_ref, i)

    def consume(g, s_ref, einv_ref):
        for i in range(G):
            output_stage(g * G + i, s_ref, einv_ref, i)

    produce(0, sa_ref, ea_ref)

    def out_step(j, carry):
        g = 2 * j
        produce(g + 1, sb_ref, eb_ref)
        consume(g, sa_ref, ea_ref)
        produce(g + 2, sa_ref, ea_ref)
        consume(g + 1, sb_ref, eb_ref)
        return carry

    lax.fori_loop(0, n_groups // 2 - 1, out_step, 0)
    produce(n_groups - 1, sb_ref, eb_ref)
    consume(n_groups - 2, sa_ref, ea_ref)
    consume(n_groups - 1, sb_ref, eb_ref)


def _mlstm(qk, v, o, gates, conv_w, conv_b, norm_g, tri3, eye2):
    b, seq, _ = v.shape
    H, d, L = N_MLSTM_HEADS, MLSTM_HEAD_DIM, CHUNK
    nc = seq // L
    assert nc % MLSTM_SCAN_UNROLL == 0 and nc % MLSTM_CONV_UNROLL == 0 and L == LANES
    assert nc % (2 * MLSTM_CHUNKS_PER_STEP) == 0 and nc >= 4 * MLSTM_CHUNKS_PER_STEP
    col = lambda off: pl.BlockSpec((None, seq, d), lambda bi, hi: (bi, 0, hi + off))
    stat = pltpu.VMEM((2, nc, L), F32)
    slots = MLSTM_CHUNKS_PER_STEP
    return pl.pallas_call(
        functools.partial(_mlstm_body, seq=seq),
        grid=(b, H),
        in_specs=[
            col(0), col(H), col(0), col(0),
            pl.BlockSpec((4, None, None, nc, L), lambda bi, hi: (0, hi, bi, 0, 0)),
            pl.BlockSpec((None, 2, 3, d), lambda bi, hi: (hi, 0, 0, 0)),
            pl.BlockSpec((None, 2, 1, d), lambda bi, hi: (hi, 0, 0, 0)),
            pl.BlockSpec((None, 1, d), lambda bi, hi: (hi, 0, 0)),
            _const_spec(tri3.shape), _const_spec(eye2.shape),
        ],
        out_specs=pl.BlockSpec((None, seq, d), lambda bi, hi: (bi, 0, hi)),
        out_shape=jax.ShapeDtypeStruct((b, seq, H * d), BF16),
        scratch_shapes=[
            pltpu.VMEM((seq, d), BF16),
            pltpu.VMEM((d, seq), BF16),
            pltpu.VMEM((2, nc, d, 2 * d), BF16),
            pltpu.VMEM((2, d, 2 * d), F32),
            stat, stat, stat,
            stat, stat, stat,
            stat,
            pltpu.VMEM((slots, 2, L, L + d), BF16),
            pltpu.VMEM((slots, 2, L, d), F32),
            pltpu.VMEM((slots, 2, L, L + d), BF16),
            pltpu.VMEM((slots, 2, L, d), F32),
        ],
        compiler_params=pltpu.CompilerParams(dimension_semantics=("arbitrary", "arbitrary"),
                                             vmem_limit_bytes=VMEM_LIMIT["mlstm"]),
        name="mlstm",
    )(qk, qk, v, o, gates, conv_w, conv_b, norm_g, tri3, eye2)


def _natten_body(q_ref, k_ref, v_ref, bias_ref, out_ref, pa_ref, pb_ref, *, rows):
    hd = NA_HEAD_DIM
    win = WIN_H * GRID_W
    lane = lax.broadcasted_iota(jnp.int32, (GRID_W, 2 * hd), 1)
    first = lane < hd
    ones_blk = jnp.ones((win, 2 * hd), BF16)

    def window_start(r):
        return jnp.clip(r - WIN_H // 2, 0, rows - WIN_H)

    def prob_stage(r, p_ref, slot):
        rs = window_start(r)
        q = q_ref[pl.ds(pl.multiple_of(r * GRID_W, GRID_W), GRID_W), :]
        zero = jnp.zeros_like(q)
        qs = jnp.concatenate([jnp.where(first, q, zero), jnp.where(first, zero, q)], axis=0)
        kwin = k_ref[pl.ds(pl.multiple_of(rs * GRID_W, GRID_W), win), :]
        off = rs - r + (WIN_H - 1)
        bias = bias_ref[off & 1, :, pl.ds(pl.multiple_of((off >> 1) * LANES, LANES), win)]
        s = lax.dot_general(qs, kwin, NT_DIMS, preferred_element_type=F32) + bias
        p_ref[slot] = jnp.exp(s - jnp.max(s, axis=1, keepdims=True)).astype(BF16)

    def output_stage(r, p_ref, slot):
        rs = window_start(r)
        vwin = v_ref[pl.ds(pl.multiple_of(rs * GRID_W, GRID_W), win), :]
        o = jnp.dot(p_ref[slot], jnp.concatenate([vwin, ones_blk], axis=1), preferred_element_type=F32)
        o = o[:, :2 * hd] / o[:, 2 * hd:]
        out = jnp.where(first, o[:GRID_W], o[GRID_W:])
        out_ref[pl.ds(pl.multiple_of(r * GRID_W, GRID_W), GRID_W), :] = out.astype(out_ref.dtype)

    R = NA_ROWS_PER_STEP
    n_groups = rows // R

    def produce(g, p_ref):
        for i in range(R):
            prob_stage(g * R + i, p_ref, i)

    def consume(g, p_ref):
        for i in range(R):
            output_stage(g * R + i, p_ref, i)

    produce(0, pa_ref)

    def group_pair(j, carry):
        g = 2 * j
        produce(g + 1, pb_ref)
        consume(g, pa_ref)
        produce(g + 2, pa_ref)
        consume(g + 1, pb_ref)
        return carry

    lax.fori_loop(0, n_groups // 2 - 1, group_pair, 0)
    produce(n_groups - 1, pb_ref)
    consume(n_groups - 2, pa_ref)
    consume(n_groups - 1, pb_ref)


def _natten(nq, nk, nv, bias_tab):
    b, seq, d_n = nq.shape
    pairs = N_NA_HEADS // 2
    width = 2 * NA_HEAD_DIM
    rows = seq // GRID_W
    assert rows % (2 * NA_ROWS_PER_STEP) == 0 and rows >= 4 * NA_ROWS_PER_STEP
    col = pl.BlockSpec((None, seq, width), lambda bi, pi: (bi, 0, pi))
    return pl.pallas_call(
        functools.partial(_natten_body, rows=rows),
        grid=(b, pairs),
        in_specs=[col, col, col,
                  pl.BlockSpec((None, 2, 2 * GRID_W, NA_BIAS_LANES), lambda bi, pi: (pi, 0, 0, 0))],
        out_specs=col,
        out_shape=jax.ShapeDtypeStruct((b, seq, d_n), BF16),
        scratch_shapes=[pltpu.VMEM((NA_ROWS_PER_STEP, 2 * GRID_W, WIN_H * GRID_W), BF16)] * 2,
        compiler_params=pltpu.CompilerParams(dimension_semantics=("arbitrary", "arbitrary"),
                                             vmem_limit_bytes=VMEM_LIMIT["natten"]),
        name="natten",
    )(nq, nk, nv, bias_tab)


def _natten_bias_table(rpb):
    c = np.arange(GRID_W)
    cs = np.clip(c - WIN_W // 2, 0, GRID_W - WIN_W)
    cp = np.arange(GRID_W)
    valid = (cp[None, :] >= cs[:, None]) & (cp[None, :] < cs[:, None] + WIN_W)
    rel = cp[None, None, :] - c[None, :, None] + (WIN_W - 1)
    onehot = (rel == np.arange(2 * WIN_W - 1)[:, None, None]).astype(np.float32)
    n_rel = 2 * WIN_H - 1
    tab = jnp.einsum('phrd,dcq->phcrq', rpb.astype(F32).reshape(N_NA_HEADS // 2, 2, n_rel, 2 * WIN_W - 1),
                     jnp.asarray(onehot), precision=lax.Precision.HIGHEST)
    tab = jnp.where(jnp.asarray(valid)[None, None, :, None, :], tab, NEG_BIG)
    tab = tab.reshape(N_NA_HEADS // 2, 2 * GRID_W, n_rel * GRID_W)
    even = tab[:, :, :NA_BIAS_LANES]
    odd = tab[:, :, GRID_W:GRID_W + NA_BIAS_LANES]
    return jnp.stack([even, odd], axis=1)


def _tail_body(x_ref, ya_ref, yb_ref, p_ref, woa_ref, wob_ref, g2_ref, w1_ref, w2_ref,
               g3_ref, wg_ref, wu_ref, out_ref, *, ff_chunk):
    d_ff = w1_ref.shape[1]
    h = (x_ref[...]
         + jnp.dot(ya_ref[...], woa_ref[...], preferred_element_type=F32)
         + jnp.dot(yb_ref[...], wob_ref[...], preferred_element_type=F32))
    u = (h * _rms_scale(h) * g2_ref[...]).astype(BF16)
    out_ref[...] = h
    for j in range(d_ff // ff_chunk):
        z = jnp.dot(u, w1_ref[:, j * ff_chunk:(j + 1) * ff_chunk], preferred_element_type=F32)
        z = jnp.maximum(z, 0.0)
        out_ref[...] += jnp.dot((z * z).astype(BF16), w2_ref[j * ff_chunk:(j + 1) * ff_chunk, :],
                                preferred_element_type=F32)
    h = out_ref[...]
    u = (h * _rms_scale(h) * g3_ref[...]).astype(BF16)
    gate = _sigmoid(jnp.dot(u, wg_ref[...], preferred_element_type=F32))
    up = jnp.dot(p_ref[...].astype(BF16), wu_ref[...], preferred_element_type=F32)
    out_ref[...] = h + gate * up


def _tail(x2, ya, yb, p2, wo, g2, w1, w2, g3, wg, wu, *, tm, ff_chunk):
    t, d = x2.shape
    d_mix = wo.shape[0]
    assert ya.shape[1] == yb.shape[1] == d_mix // 2 and t % tm == 0
    row = lambda width: pl.BlockSpec((tm, width), lambda i: (i, 0))
    half = lambda k: pl.BlockSpec((d_mix // 2, d), lambda i: (k, 0), pipeline_mode=pl.Buffered(1))
    consts = [g2, w1, w2, g3, wg, wu]
    return pl.pallas_call(
        functools.partial(_tail_body, ff_chunk=ff_chunk),
        grid=(t // tm,),
        in_specs=[row(d), row(ya.shape[1]), row(yb.shape[1]), row(p2.shape[1]), half(0), half(1)]
                 + [_const_spec(c.shape) for c in consts],
        out_specs=row(d),
        out_shape=jax.ShapeDtypeStruct((t, d), F32),
        compiler_params=pltpu.CompilerParams(dimension_semantics=("arbitrary",),
                                             vmem_limit_bytes=VMEM_LIMIT["tail"]),
        name="tail",
    )(x2, ya, yb, p2, wo, wo, *consts)


def kernel(x, p, norm1_g, w_in, conv_w, conv_b, gate_b, mlstm_norm_g, q_norm_g, k_norm_g, rpb,
           w_out, norm2_g, w_ff1, w_ff2, ple_norm_g, w_ple_gate, w_ple_up):
    b, seq, d = x.shape
    depth = w_in.shape[0]
    H, hd, L = N_MLSTM_HEADS, MLSTM_HEAD_DIM, CHUNK
    d_m = H * hd
    d_n = N_NA_HEADS * NA_HEAD_DIM
    t = b * seq
    rows = seq // GRID_W
    assert rows >= WIN_H

    bd = jnp.asarray(np.kron(np.eye(N_NA_HEADS), np.ones((NA_HEAD_DIM, NA_HEAD_DIM))), BF16)
    ri, ci = np.indices((L, L))
    tri = np.stack([ci <= ri, ci >= ri])
    tri3 = jnp.asarray(np.concatenate([tri, tri, tri], axis=1), BF16)
    eye2 = jnp.asarray(np.concatenate([ci == ri, ci == ri], axis=1), BF16)

    h = x.reshape(t, d)
    for i in range(depth):
        wi = w_in[i].astype(BF16)
        w_n = wi[:, 4 * d_m + N_GATE:]
        qg = (q_norm_g[i].reshape(1, d_n) * (NA_HEAD_DIM ** -0.5)).astype(F32)
        kg = k_norm_g[i].reshape(1, d_n).astype(F32)
        qk, mv, mo, nq, nk, nv, gates = _inproj(
            h, norm1_g[i].reshape(1, d), wi, w_n, gate_b[i].reshape(N_GATE, 1), bd, qg, kg,
            d_m=d_m, d_n=d_n, tm=INPROJ_TM)

        cw = conv_w[i].reshape(3, 2, H, hd).transpose(2, 1, 0, 3)
        cb = conv_b[i].reshape(2, H, 1, hd).transpose(1, 0, 2, 3)
        y_a = _mlstm(qk.reshape(b, seq, 2 * d_m), mv.reshape(b, seq, d_m), mo.reshape(b, seq, d_m),
                     gates.reshape(4, H, b, seq // L, L), cw, cb,
                     mlstm_norm_g[i].reshape(H, 1, hd), tri3, eye2)

        bias_tab = _natten_bias_table(rpb[i])
        y_b = _natten(nq.reshape(b, seq, d_n), nk.reshape(b, seq, d_n), nv.reshape(b, seq, d_n), bias_tab)

        h = _tail(h, y_a.reshape(t, d_m), y_b.reshape(t, d_n), p[i].reshape(t, -1),
                  w_out[i].astype(BF16), norm2_g[i].reshape(1, d), w_ff1[i].astype(BF16), w_ff2[i].astype(BF16),
                  ple_norm_g[i].reshape(1, d), w_ple_gate[i].astype(BF16), w_ple_up[i].astype(BF16),
                  tm=TAIL_TM, ff_chunk=TAIL_FF_CHUNK)
    return h.reshape(b, seq, d)
```

```python
import functools

import jax
import jax.numpy as jnp
import numpy as np
from jax import lax
from jax.experimental import pallas as pl
from jax.experimental.pallas import tpu as pltpu

F32 = jnp.float32
BF16 = jnp.bfloat16

N_MLSTM_HEADS = 4
MLSTM_HEAD_DIM = 128
N_NA_HEADS = 8
NA_HEAD_DIM = 64
GRID_W = 64
WIN_H = 8
WIN_W = 16
CHUNK = 128
N_GATE = 4 * N_MLSTM_HEADS
RMS_EPS = 1e-6
NEG_BIG = -1e30
F32_BIG = 3e38
NA_BIAS_LANES = (2 * WIN_H - 2) * GRID_W
NA_ROWS_PER_STEP = 16
MLSTM_CHUNKS_PER_STEP = 8
MLSTM_CONV_UNROLL = 16
MLSTM_SCAN_UNROLL = 16

LANES = 128
BF16_ROWS = 16
GATE_PAD = LANES
MIB = 1024 * 1024

INPROJ_TM = 1024
TAIL_TM = 1024
TAIL_FF_CHUNK = 1024
VMEM_LIMIT = {"inproj": 44 * MIB, "mlstm": 52 * MIB, "natten": 40 * MIB, "tail": 52 * MIB}

NT_DIMS = (((1,), (1,)), ((), ()))


def _const_spec(shape):
    return pl.BlockSpec(shape, lambda *_: (0,) * len(shape), pipeline_mode=pl.Buffered(1))


def _rms_scale(x):
    return lax.rsqrt(jnp.mean(x * x, axis=-1, keepdims=True) + RMS_EPS)


def _split3(x):
    hi = x.astype(BF16)
    r1 = x - hi.astype(F32)
    mid = r1.astype(BF16)
    lo = (r1 - mid.astype(F32)).astype(BF16)
    return hi, mid, lo


def _inproj_body(x_ref, g_ref, wm_ref, wn_ref, wg_ref, gb_ref, bd_ref, qg_ref, kg_ref,
                 qk_ref, v_ref, o_ref, nq_ref, nk_ref, nv_ref, gate_ref, *, d_m, d_n):
    x = x_ref[...]
    u = (x * _rms_scale(x) * g_ref[...]).astype(BF16)

    def proj(w_ref, lo, hi):
        return jnp.dot(u, w_ref[:, lo:hi], preferred_element_type=F32)

    def head_norm(y, gain_ref):
        ss = jnp.dot((y * y).astype(BF16), bd_ref[...], preferred_element_type=F32)
        return y * lax.rsqrt(ss * (1.0 / NA_HEAD_DIM) + RMS_EPS) * gain_ref[...]

    qk_ref[...] = proj(wm_ref, 0, 2 * d_m).astype(BF16)
    v_ref[...] = proj(wm_ref, 2 * d_m, 3 * d_m).astype(BF16)
    o_ref[...] = proj(wm_ref, 3 * d_m, 4 * d_m).astype(BF16)
    nq_ref[...] = head_norm(proj(wn_ref, 0, d_n), qg_ref).astype(BF16)
    nk_ref[...] = head_norm(proj(wn_ref, d_n, 2 * d_n), kg_ref).astype(BF16)
    nv_ref[...] = proj(wn_ref, 2 * d_n, 3 * d_n).astype(BF16)
    wg_t = wg_ref[...].astype(F32).T[:N_GATE, :].astype(BF16)
    gate_ref[...] = lax.dot_general(wg_t, u, NT_DIMS, preferred_element_type=F32) + gb_ref[...]


def _inproj(x2, g, w_all, w_n, gate_b, bd, qg, kg, *, d_m, d_n, tm):
    t, d = x2.shape
    assert (4 * d_m) % GATE_PAD == 0
    fixed = functools.partial(pl.BlockSpec, pipeline_mode=pl.Buffered(1))
    row = lambda width: pl.BlockSpec((tm, width), lambda i: (i, 0))
    out_shapes = (
        jax.ShapeDtypeStruct((t, 2 * d_m), BF16),
        jax.ShapeDtypeStruct((t, d_m), BF16),
        jax.ShapeDtypeStruct((t, d_m), BF16),
        jax.ShapeDtypeStruct((t, d_n), BF16),
        jax.ShapeDtypeStruct((t, d_n), BF16),
        jax.ShapeDtypeStruct((t, d_n), BF16),
        jax.ShapeDtypeStruct((N_GATE, t), F32),
    )
    return pl.pallas_call(
        functools.partial(_inproj_body, d_m=d_m, d_n=d_n),
        grid=(t // tm,),
        in_specs=[row(d), _const_spec((1, d)), fixed((d, 4 * d_m), lambda i: (0, 0)), _const_spec(w_n.shape),
                  fixed((d, GATE_PAD), lambda i: (0, 4 * d_m // GATE_PAD)), _const_spec((N_GATE, 1)),
                  _const_spec((d_n, d_n)), _const_spec((1, d_n)), _const_spec((1, d_n))],
        out_specs=(row(2 * d_m), row(d_m), row(d_m), row(d_n), row(d_n), row(d_n),
                   pl.BlockSpec((N_GATE, tm), lambda i: (0, i))),
        out_shape=out_shapes,
        compiler_params=pltpu.CompilerParams(dimension_semantics=("arbitrary",),
                                             vmem_limit_bytes=VMEM_LIMIT["inproj"]),
        name="inproj",
    )(x2, g, w_all, w_n, w_all, gate_b, bd, qg, kg)


def _log_sigmoid(x):
    return jnp.minimum(x, 0.0) - jnp.log1p(jnp.exp(-jnp.abs(x)))


def _sigmoid(x):
    return 1.0 / (1.0 + jnp.exp(-x))


def _mlstm_body(q_ref, k_ref, v_ref, o_ref, gate_ref, cw_ref, cb_ref, ng_ref, tri3_ref, eye2_ref,
                out_ref, qs_ref, kt_ref, cs_ref, ccur_ref, brow_ref, crow_ref, cmax_ref, arow_ref, bl_ref,
                ml_ref, ms_ref, sa_ref, ea_ref, sb_ref, eb_ref, *, seq):
    L = CHUNK
    d = MLSTM_HEAD_DIM
    nc = seq // L

    row_id = lax.broadcasted_iota(jnp.int32, (L, d), 0)
    col_id = lax.broadcasted_iota(jnp.int32, (L, d), 1)
    pos_id = lax.broadcasted_iota(jnp.int32, (nc, L), 1)
    ones_blk = jnp.ones((L, d), BF16)

    def conv_silu(src_ref, c, s0, w, b):
        x = src_ref[pl.ds(s0, L), :].astype(F32)
        p0 = pl.multiple_of(jnp.maximum(s0 - BF16_ROWS, 0), BF16_ROWS)
        n0 = pl.multiple_of(jnp.minimum(s0 + L, seq - BF16_ROWS), BF16_ROWS)
        prev_row = src_ref[pl.ds(p0, BF16_ROWS), :][BF16_ROWS - 1:BF16_ROWS, :].astype(F32)
        next_row = src_ref[pl.ds(n0, BF16_ROWS), :][0:1, :].astype(F32)
        prev_row = jnp.where(c > 0, prev_row, 0.0)
        next_row = jnp.where(c < nc - 1, next_row, 0.0)
        x_prev = jnp.where(row_id == 0, prev_row, pltpu.roll(x, 1, 0))
        x_next = jnp.where(row_id == L - 1, next_row, pltpu.roll(x, L - 1, 0))
        y = w[0:1, :] * x_prev + w[1:2, :] * x + w[2:3, :] * x_next + b
        return y * _sigmoid(y)

    def conv_step(c, carry):
        s0 = pl.multiple_of(c * L, L)
        qs_ref[pl.ds(s0, L), :] = conv_silu(q_ref, c, s0, cw_ref[0], cb_ref[0]).astype(BF16)
        kk = conv_silu(k_ref, c, s0, cw_ref[1], cb_ref[1]) * (d ** -0.5)
        kt_ref[:, pl.ds(s0, L)] = kk.T.astype(BF16)
        return carry

    lax.fori_loop(0, nc, conv_step, 0, unroll=MLSTM_CONV_UNROLL)

    for dirn in (0, 1):
        i_g = gate_ref[2 * dirn]
        f_log = _log_sigmoid(gate_ref[2 * dirn + 1])
        f_cat = jnp.concatenate(_split3(f_log), axis=1)
        brow = jnp.dot(f_cat, tri3_ref[1 - dirn], preferred_element_type=F32)
        b_last = brow[:, L - 1:L] if dirn == 0 else brow[:, 0:1]
        a_row = i_g + b_last - brow
        a_max = jnp.max(a_row, axis=1, keepdims=True)
        crow = i_g - brow
        cmax = crow
        for sh in [1 << e for e in range(L.bit_length() - 1)]:
            if dirn == 0:
                cmax = jnp.maximum(cmax, jnp.where(pos_id >= sh, pltpu.roll(cmax, sh, 1), -jnp.inf))
            else:
                cmax = jnp.maximum(cmax, jnp.where(pos_id < L - sh, pltpu.roll(cmax, L - sh, 1), -jnp.inf))
        brow_ref[dirn] = brow
        crow_ref[dirn] = crow
        cmax_ref[dirn] = cmax
        arow_ref[dirn] = a_row
        bl_ref[dirn] = jnp.broadcast_to(b_last, (nc, L))
        ml_ref[dirn] = jnp.broadcast_to(a_max, (nc, L))

    def v_aug(s0):
        return jnp.concatenate([v_ref[pl.ds(s0, L), :], ones_blk], axis=1)

    ccur_ref[...] = jnp.zeros_like(ccur_ref)

    def scan_step(i, carry):
        new = []
        for dirn, c, m in ((0, i, carry[0]), (1, nc - 1 - i, carry[1])):
            s0 = pl.multiple_of(c * L, L)
            state = ccur_ref[dirn]
            cs_ref[dirn, c] = state.astype(BF16)
            ms_ref[dirn, pl.ds(c, 1), :] = m
            a_prev = m + bl_ref[dirn, pl.ds(c, 1), :]
            m_new = jnp.maximum(a_prev, ml_ref[dirn, pl.ds(c, 1), :])
            w_row = jnp.exp(arow_ref[dirn, pl.ds(c, 1), :] - m_new)
            kw = (kt_ref[:, pl.ds(s0, L)].astype(F32) * w_row).astype(BF16)
            k_loc = jnp.dot(kw, v_aug(s0), preferred_element_type=F32)
            ccur_ref[dirn] = jnp.exp(a_prev - m_new)[:, 0:1] * state + k_loc
            new.append(m_new)
        return tuple(new)

    m0 = jnp.zeros((1, L), F32)
    lax.fori_loop(0, nc, scan_step, (m0, m0), unroll=MLSTM_SCAN_UNROLL)

    for dirn in (0, 1):
        mu_all = jnp.maximum(ms_ref[dirn], cmax_ref[dirn])
        cmax_ref[dirn] = mu_all
        brow_ref[dirn] = jnp.minimum(jnp.exp(-(brow_ref[dirn] + mu_all)), F32_BIG)
    lower = col_id <= row_id
    upper = col_id >= row_id

    def weights_stage(c, s_ref, einv_ref, slot):
        s0 = pl.multiple_of(c * L, L)
        q = qs_ref[pl.ds(s0, L), :]
        qk = jnp.dot(q, kt_ref[:, pl.ds(s0, L)], preferred_element_type=F32)
        for dirn, mask in ((0, lower), (1, upper)):
            rows = []
            for stat_ref in (cmax_ref, brow_ref):
                hi, mid, _ = _split3(stat_ref[dirn, pl.ds(c, 1), :])
                rows.append(jnp.broadcast_to(jnp.concatenate([hi, mid], axis=1), (L, 2 * L)))
            col = lax.dot_general(eye2_ref[...], jnp.concatenate(rows, axis=0), NT_DIMS,
                                  preferred_element_type=F32)
            mu = col[:, :L]
            p = jnp.exp(jnp.where(mask, crow_ref[dirn, pl.ds(c, 1), :] - mu, -jnp.inf))
            q_inter = q.astype(F32) * jnp.exp(ms_ref[dirn, pl.ds(c, 1), :] - mu)
            s_ref[slot, dirn] = jnp.concatenate([qk * p, q_inter], axis=1).astype(BF16)
            einv_ref[slot, dirn] = col[:, L:]

    def output_stage(c, s_ref, einv_ref, slot):
        s0 = pl.multiple_of(c * L, L)
        vaug = v_aug(s0)
        hsum = None
        for dirn in (0, 1):
            nd = jnp.dot(s_ref[slot, dirn], jnp.concatenate([vaug, cs_ref[dirn, c]], axis=0),
                         preferred_element_type=F32)
            h = nd[:, :d] / jnp.maximum(jnp.abs(nd[:, d:]), einv_ref[slot, dirn])
            hsum = h if hsum is None else hsum + h
        y = hsum * _rms_scale(hsum) * ng_ref[...]
        y = y * _sigmoid(o_ref[pl.ds(s0, L), :].astype(F32))
        out_ref[pl.ds(s0, L), :] = y.astype(out_ref.dtype)

    G = MLSTM_CHUNKS_PER_STEP
    n_groups = nc // G

    def produce(g, s_ref, einv_ref):
        for i in range(G):
            weights_stage(g * G + i, s_ref, einv_ref, i)

    def consume(g, s_ref, einv_ref):
        for i in range(G):
            output_stage(g * G + i, s_ref, einv_ref, i)

    produce(0, sa_ref, ea_ref)

    def out_step(j, carry):
        g = 2 * j
        produce(g + 1, sb_ref, eb_ref)
        consume(g, sa_ref, ea_ref)
        produce(g + 2, sa_ref, ea_ref)
        consume(g + 1, sb_ref, eb_ref)
        return carry

    lax.fori_loop(0, n_groups // 2 - 1, out_step, 0)
    produce(n_groups - 1, sb_ref, eb_ref)
    consume(n_groups - 2, sa_ref, ea_ref)
    consume(n_groups - 1, sb_ref, eb_ref)


def _mlstm(qk, v, o, gates, conv_w, conv_b, norm_g, tri3, eye2):
    b, seq, _ = v.shape
    H, d, L = N_MLSTM_HEADS, MLSTM_HEAD_DIM, CHUNK
    nc = seq // L
    assert nc % MLSTM_SCAN_UNROLL == 0 and nc % MLSTM_CONV_UNROLL == 0 and L == LANES
    assert nc % (2 * MLSTM_CHUNKS_PER_STEP) == 0 and nc >= 4 * MLSTM_CHUNKS_PER_STEP
    col = lambda off: pl.BlockSpec((None, seq, d), lambda bi, hi: (bi, 0, hi + off))
    stat = pltpu.VMEM((2, nc, L), F32)
    slots = MLSTM_CHUNKS_PER_STEP
    return pl.pallas_call(
        functools.partial(_mlstm_body, seq=seq),
        grid=(b, H),
        in_specs=[
            col(0), col(H), col(0), col(0),
            pl.BlockSpec((4, None, None, nc, L), lambda bi, hi: (0, hi, bi, 0, 0)),
            pl.BlockSpec((None, 2, 3, d), lambda bi, hi: (hi, 0, 0, 0)),
            pl.BlockSpec((None, 2, 1, d), lambda bi, hi: (hi, 0, 0, 0)),
            pl.BlockSpec((None, 1, d), lambda bi, hi: (hi, 0, 0)),
            _const_spec(tri3.shape), _const_spec(eye2.shape),
        ],
        out_specs=pl.BlockSpec((None, seq, d), lambda bi, hi: (bi, 0, hi)),
        out_shape=jax.ShapeDtypeStruct((b, seq, H * d), BF16),
        scratch_shapes=[
            pltpu.VMEM((seq, d), BF16),
            pltpu.VMEM((d, seq), BF16),
            pltpu.VMEM((2, nc, d, 2 * d), BF16),
            pltpu.VMEM((2, d, 2 * d), F32),
            stat, stat, stat,
            stat, stat, stat,
            stat,
            pltpu.VMEM((slots, 2, L, L + d), BF16),
            pltpu.VMEM((slots, 2, L, d), F32),
            pltpu.VMEM((slots, 2, L, L + d), BF16),
            pltpu.VMEM((slots, 2, L, d), F32),
        ],
        compiler_params=pltpu.CompilerParams(dimension_semantics=("arbitrary", "arbitrary"),
                                             vmem_limit_bytes=VMEM_LIMIT["mlstm"]),
        name="mlstm",
    )(qk, qk, v, o, gates, conv_w, conv_b, norm_g, tri3, eye2)


def _natten_body(q_ref, k_ref, v_ref, bias_ref, out_ref, pa_ref, pb_ref, *, rows):
    hd = NA_HEAD_DIM
    win = WIN_H * GRID_W
    lane = lax.broadcasted_iota(jnp.int32, (GRID_W, 2 * hd), 1)
    first = lane < hd
    ones_blk = jnp.ones((win, 2 * hd), BF16)

    def window_start(r):
        return jnp.clip(r - WIN_H // 2, 0, rows - WIN_H)

    def prob_stage(r, p_ref, slot):
        rs = window_start(r)
        q = q_ref[pl.ds(pl.multiple_of(r * GRID_W, GRID_W), GRID_W), :]
        zero = jnp.zeros_like(q)
        qs = jnp.concatenate([jnp.where(first, q, zero), jnp.where(first, zero, q)], axis=0)
        kwin = k_ref[pl.ds(pl.multiple_of(rs * GRID_W, GRID_W), win), :]
        off = rs - r + (WIN_H - 1)
        bias = bias_ref[off & 1, :, pl.ds(pl.multiple_of((off >> 1) * LANES, LANES), win)]
        s = lax.dot_general(qs, kwin, NT_DIMS, preferred_element_type=F32) + bias
        p_ref[slot] = jnp.exp((s - jnp.max(s, axis=1, keepdims=True)).astype(BF16))

    def output_stage(r, p_ref, slot):
        rs = window_start(r)
        vwin = v_ref[pl.ds(pl.multiple_of(rs * GRID_W, GRID_W), win), :]
        o = jnp.dot(p_ref[slot], jnp.concatenate([vwin, ones_blk], axis=1), preferred_element_type=F32)
        o = o[:, :2 * hd] / o[:, 2 * hd:]
        out = jnp.where(first, o[:GRID_W], o[GRID_W:])
        out_ref[pl.ds(pl.multiple_of(r * GRID_W, GRID_W), GRID_W), :] = out.astype(out_ref.dtype)

    R = NA_ROWS_PER_STEP
    n_groups = rows // R

    def produce(g, p_ref):
        for i in range(R):
            prob_stage(g * R + i, p_ref, i)

    def consume(g, p_ref):
        for i in range(R):
            output_stage(g * R + i, p_ref, i)

    produce(0, pa_ref)

    def group_pair(j, carry):
        g = 2 * j
        produce(g + 1, pb_ref)
        consume(g, pa_ref)
        produce(g + 2, pa_ref)
        consume(g + 1, pb_ref)
        return carry

    lax.fori_loop(0, n_groups // 2 - 1, group_pair, 0)
    produce(n_groups - 1, pb_ref)
    consume(n_groups - 2, pa_ref)
    consume(n_groups - 1, pb_ref)


def _natten(nq, nk, nv, bias_tab):
    b, seq, d_n = nq.shape
    pairs = N_NA_HEADS // 2
    width = 2 * NA_HEAD_DIM
    rows = seq // GRID_W
    assert rows % (2 * NA_ROWS_PER_STEP) == 0 and rows >= 4 * NA_ROWS_PER_STEP
    col = pl.BlockSpec((None, seq, width), lambda bi, pi: (bi, 0, pi))
    return pl.pallas_call(
        functools.partial(_natten_body, rows=rows),
        grid=(b, pairs),
        in_specs=[col, col, col,
                  pl.BlockSpec((None, 2, 2 * GRID_W, NA_BIAS_LANES), lambda bi, pi: (pi, 0, 0, 0))],
        out_specs=col,
        out_shape=jax.ShapeDtypeStruct((b, seq, d_n), BF16),
        scratch_shapes=[pltpu.VMEM((NA_ROWS_PER_STEP, 2 * GRID_W, WIN_H * GRID_W), BF16)] * 2,
        compiler_params=pltpu.CompilerParams(dimension_semantics=("arbitrary", "arbitrary"),
                                             vmem_limit_bytes=VMEM_LIMIT["natten"]),
        name="natten",
    )(nq, nk, nv, bias_tab)


def _natten_bias_table(rpb):
    c = np.arange(GRID_W)
    cs = np.clip(c - WIN_W // 2, 0, GRID_W - WIN_W)
    cp = np.arange(GRID_W)
    valid = (cp[None, :] >= cs[:, None]) & (cp[None, :] < cs[:, None] + WIN_W)
    rel = cp[None, None, :] - c[None, :, None] + (WIN_W - 1)
    onehot = (rel == np.arange(2 * WIN_W - 1)[:, None, None]).astype(np.float32)
    n_rel = 2 * WIN_H - 1
    tab = jnp.einsum('phrd,dcq->phcrq', rpb.astype(F32).reshape(N_NA_HEADS // 2, 2, n_rel, 2 * WIN_W - 1),
                     jnp.asarray(onehot), precision=lax.Precision.HIGHEST)
    tab = jnp.where(jnp.asarray(valid)[None, None, :, None, :], tab, NEG_BIG)
    tab = tab.reshape(N_NA_HEADS // 2, 2 * GRID_W, n_rel * GRID_W)
    even = tab[:, :, :NA_BIAS_LANES]
    odd = tab[:, :, GRID_W:GRID_W + NA_BIAS_LANES]
    return jnp.stack([even, odd], axis=1)


def _tail_body(x_ref, ya_ref, yb_ref, p_ref, woa_ref, wob_ref, g2_ref, w1_ref, w2_ref,
               g3_ref, wg_ref, wu_ref, out_ref, *, ff_chunk):
    d_ff = w1_ref.shape[1]
    h = (x_ref[...]
         + jnp.dot(ya_ref[...], woa_ref[...], preferred_element_type=F32)
         + jnp.dot(yb_ref[...], wob_ref[...], preferred_element_type=F32))
    u = (h * _rms_scale(h) * g2_ref[...]).astype(BF16)
    out_ref[...] = h
    for j in range(d_ff // ff_chunk):
        z = jnp.dot(u, w1_ref[:, j * ff_chunk:(j + 1) * ff_chunk], preferred_element_type=F32)
        z = jnp.maximum(z, 0.0)
        out_ref[...] += jnp.dot((z * z).astype(BF16), w2_ref[j * ff_chunk:(j + 1) * ff_chunk, :],
                                preferred_element_type=F32)
    h = out_ref[...]
    u = (h * _rms_scale(h) * g3_ref[...]).astype(BF16)
    gate = _sigmoid(jnp.dot(u, wg_ref[...], preferred_element_type=F32))
    up = jnp.dot(p_ref[...].astype(BF16), wu_ref[...], preferred_element_type=F32)
    out_ref[...] = h + gate * up


def _tail(x2, ya, yb, p2, wo, g2, w1, w2, g3, wg, wu, *, tm, ff_chunk):
    t, d = x2.shape
    d_mix = wo.shape[0]
    assert ya.shape[1] == yb.shape[1] == d_mix // 2 and t % tm == 0
    row = lambda width: pl.BlockSpec((tm, width), lambda i: (i, 0))
    half = lambda k: pl.BlockSpec((d_mix // 2, d), lambda i: (k, 0), pipeline_mode=pl.Buffered(1))
    consts = [g2, w1, w2, g3, wg, wu]
    return pl.pallas_call(
        functools.partial(_tail_body, ff_chunk=ff_chunk),
        grid=(t // tm,),
        in_specs=[row(d), row(ya.shape[1]), row(yb.shape[1]), row(p2.shape[1]), half(0), half(1)]
                 + [_const_spec(c.shape) for c in consts],
        out_specs=row(d),
        out_shape=jax.ShapeDtypeStruct((t, d), F32),
        compiler_params=pltpu.CompilerParams(dimension_semantics=("arbitrary",),
                                             vmem_limit_bytes=VMEM_LIMIT["tail"]),
        name="tail",
    )(x2, ya, yb, p2, wo, wo, *consts)


def kernel(x, p, norm1_g, w_in, conv_w, conv_b, gate_b, mlstm_norm_g, q_norm_g, k_norm_g, rpb,
           w_out, norm2_g, w_ff1, w_ff2, ple_norm_g, w_ple_gate, w_ple_up):
    b, seq, d = x.shape
    depth = w_in.shape[0]
    H, hd, L = N_MLSTM_HEADS, MLSTM_HEAD_DIM, CHUNK
    d_m = H * hd
    d_n = N_NA_HEADS * NA_HEAD_DIM
    t = b * seq
    rows = seq // GRID_W
    assert rows >= WIN_H

    bd = jnp.asarray(np.kron(np.eye(N_NA_HEADS), np.ones((NA_HEAD_DIM, NA_HEAD_DIM))), BF16)
    ri, ci = np.indices((L, L))
    tri = np.stack([ci <= ri, ci >= ri])
    tri3 = jnp.asarray(np.concatenate([tri, tri, tri], axis=1), BF16)
    eye2 = jnp.asarray(np.concatenate([ci == ri, ci == ri], axis=1), BF16)

    h = x.reshape(t, d)
    for i in range(depth):
        wi = w_in[i].astype(BF16)
        w_n = wi[:, 4 * d_m + N_GATE:]
        qg = (q_norm_g[i].reshape(1, d_n) * (NA_HEAD_DIM ** -0.5)).astype(F32)
        kg = k_norm_g[i].reshape(1, d_n).astype(F32)
        qk, mv, mo, nq, nk, nv, gates = _inproj(
            h, norm1_g[i].reshape(1, d), wi, w_n, gate_b[i].reshape(N_GATE, 1), bd, qg, kg,
            d_m=d_m, d_n=d_n, tm=INPROJ_TM)

        cw = conv_w[i].reshape(3, 2, H, hd).transpose(2, 1, 0, 3)
        cb = conv_b[i].reshape(2, H, 1, hd).transpose(1, 0, 2, 3)
        y_a = _mlstm(qk.reshape(b, seq, 2 * d_m), mv.reshape(b, seq, d_m), mo.reshape(b, seq, d_m),
                     gates.reshape(4, H, b, seq // L, L), cw, cb,
                     mlstm_norm_g[i].reshape(H, 1, hd), tri3, eye2)

        bias_tab = _natten_bias_table(rpb[i])
        y_b = _natten(nq.reshape(b, seq, d_n), nk.reshape(b, seq, d_n), nv.reshape(b, seq, d_n), bias_tab)

        h = _tail(h, y_a.reshape(t, d_m), y_b.reshape(t, d_n), p[i].reshape(t, -1),
                  w_out[i].astype(BF16), norm2_g[i].reshape(1, d), w_ff1[i].astype(BF16), w_ff2[i].astype(BF16),
                  ple_norm_g[i].reshape(1, d), w_ple_gate[i].astype(BF16), w_ple_up[i].astype(BF16),
                  tm=TAIL_TM, ff_chunk=TAIL_FF_CHUNK)
    return h.reshape(b, seq, d)
```
